```python
import jax, jax.numpy as jnp
from jax import lax
import numpy as np

D_MODEL = 2048
BATCH = 2
SEQ = 4096
DEPTH = 2
DEC_BATCH = 8
DEC_SEQ = 8
PAST_LEN = 16384
PAGE_SIZE = 128

HEAD_DIM = 128
SCALE = HEAD_DIM ** -0.5
NSA_HEADS = 8
NSA_KV_HEADS = 2
NSA_GROUP = NSA_HEADS // NSA_KV_HEADS
CMP_LEN = 32
CMP_STRIDE = 16
SEL_BLOCK = 64
SEL_TOPK = 16
WINDOW = 512
WIN_BLOCK = 128
NSA_Q_BLOCK = 64
SB_HEADS = 8
SB_Q_BLOCK = 128
HG_HEADS = 8
HG_DK = 128
HG_DV = 128
HG_CHUNK = 64
D_FF = 5632
N_EXPERTS = 8
TOP_K = 2
MOE_BLOCK = 128
N_DENSE = (DEPTH + 1) // 2
N_MOE = DEPTH // 2
NORM_EPS = 1e-6

NSA_W = NSA_HEADS * HEAD_DIM
KV_W = 2 * NSA_KV_HEADS * HEAD_DIM
SB_W = SB_HEADS * HEAD_DIM
HG_KW = HG_HEADS * HG_DK
HG_VW = HG_HEADS * HG_DV
OFF_NSA_Q = 0
OFF_CMP = OFF_NSA_Q + NSA_W
OFF_SEL = OFF_CMP + KV_W
OFF_WIN = OFF_SEL + KV_W
OFF_NSA_GATE = OFF_WIN + KV_W
OFF_SB_Q = OFF_NSA_GATE + 3 * NSA_HEADS
OFF_SB_KV = OFF_SB_Q + SB_W
OFF_HG_Q = OFF_SB_KV + 2 * SB_W
OFF_HG_F = OFF_HG_Q + HG_KW
OFF_HG_I = OFF_HG_F + HG_KW
OFF_HG_G = OFF_HG_I + HG_VW
OFF_MERGE = OFF_HG_G + HG_VW
IN_COLS = OFF_MERGE + 3 * D_MODEL

kernel_name = 'nsa_stickbreak_hgrn2_hybrid_step'

F32 = jnp.float32


def rmsnorm(x, g):
    xf = x.astype(F32)
    y = xf * lax.rsqrt(jnp.mean(xf * xf, -1, keepdims=True) + NORM_EPS)
    return (y * g.astype(F32)).astype(x.dtype)


def masked_softmax(s, mask):
    s = jnp.where(mask, s, -jnp.inf)
    m = jnp.max(s, axis=-1, keepdims=True)
    m = jnp.where(jnp.isfinite(m), m, 0.0)
    e = jnp.where(mask, jnp.exp(s - m), 0.0)
    den = jnp.sum(e, axis=-1, keepdims=True)
    return e / jnp.where(den > 0, den, 1.0)


def gqa_attend(q, k, v, mask):
    lead = q.shape[:-3]
    nq, nh, hd = q.shape[-3:]
    kvh = k.shape[-2]
    qg = q.reshape(lead + (nq, kvh, nh // kvh, hd))
    s = jnp.einsum('...qgnd,...kgd->...gnqk', qg, k, preferred_element_type=F32) * SCALE
    pr = masked_softmax(s, mask[..., None, None, :, :])
    o = jnp.einsum('...gnqk,...kgd->...qgnd', pr.astype(v.dtype), v)
    return o.reshape(lead + (nq, nh, hd))


def sweep_queries(fn, q, q_pos, block):
    B, L = q.shape[:2]
    qb = min(block, L)
    nb = -(-L // qb)
    pad = nb * qb - L
    q = jnp.pad(q, ((0, 0), (0, pad)) + ((0, 0),) * (q.ndim - 2))
    q_pos = jnp.pad(q_pos, (0, pad), mode='edge')
    qs = jnp.moveaxis(q.reshape((B, nb, qb) + q.shape[2:]), 1, 0)
    ps = q_pos.reshape(nb, qb)
    outs = lax.map(lambda a: fn(a[0], a[1]), (qs, ps))

    def unblock(o):
        o = jnp.moveaxis(o, 0, 1)
        return o.reshape((B, nb * qb) + o.shape[3:])[:, :L]

    return jax.tree_util.tree_map(unblock, outs)


def compress(rows, pe, w1, w2):
    B, T = rows.shape[:2]
    n_seg = T // CMP_STRIDE
    r = CMP_LEN // CMP_STRIDE
    n_cmp = n_seg - r + 1
    segs = rows[:, :n_seg * CMP_STRIDE].reshape(B, n_seg, CMP_STRIDE, NSA_KV_HEADS, HEAD_DIM)
    blocks = jnp.concatenate([segs[:, i:i + n_cmp] for i in range(r)], axis=2)
    z = blocks + pe[None, None, :, None, :]
    hid = jax.nn.silu(jnp.einsum('bnlgd,ldh->bngh', z, w1.reshape(CMP_LEN, HEAD_DIM, -1)))
    return hid @ w2


def nsa_global(q, q_pos, cmp_all, sel_all, pe_k, w1_k, w2_k, pe_v, w1_v, w2_v):
    B, T = cmp_all.shape[:2]
    kc = compress(cmp_all[:, :, 0], pe_k, w1_k, w2_k)
    vc = compress(cmp_all[:, :, 1], pe_v, w1_v, w2_v)
    n_cmp = kc.shape[1]
    c_first = jnp.arange(n_cmp) * CMP_STRIDE
    c_last = c_first + CMP_LEN - 1
    n_slc = -(-T // SEL_BLOCK)
    sel_pad = jnp.pad(sel_all, ((0, 0), (0, n_slc * SEL_BLOCK - T), (0, 0), (0, 0), (0, 0)))
    sblk = sel_pad.reshape(B, n_slc, SEL_BLOCK, 2, NSA_KV_HEADS, HEAD_DIM)
    ks = jnp.transpose(sblk[:, :, :, 0], (0, 3, 1, 2, 4))
    vs = jnp.transpose(sblk[:, :, :, 1], (0, 3, 1, 2, 4))
    s_first = jnp.arange(n_slc) * SEL_BLOCK
    cover = ((c_first[:, None] < s_first[None, :] + SEL_BLOCK) & (c_last[:, None] >= s_first[None, :])).astype(F32)
    k_eff = min(SEL_TOPK, n_slc)
    bi = jnp.arange(B)[:, None, None, None]
    gi = jnp.arange(NSA_KV_HEADS)[None, :, None, None]
    jj = jnp.arange(n_slc)

    def block(qb, pb):
        nq = qb.shape[1]
        qg = qb.reshape(B, nq, NSA_KV_HEADS, NSA_GROUP, HEAD_DIM)
        sc = jnp.einsum('bqgnd,bcgd->bgnqc', qg, kc, preferred_element_type=F32) * SCALE
        p_cmp = masked_softmax(sc, c_last[None, :] <= pb[:, None])
        o_cmp = jnp.einsum('bgnqc,bcgd->bqgnd', p_cmp.astype(vc.dtype), vc)
        imp = jnp.einsum('bgnqc,cj->bgqj', p_cmp, cover)
        cur = pb // SEL_BLOCK
        forced = (jj[None, :] == 0) | (jj[None, :] == cur[:, None]) | (jj[None, :] == cur[:, None] - 1)
        valid = s_first[None, :] <= pb[:, None]
        score = jnp.where(valid, jnp.where(forced, NSA_GROUP + 1.0, imp), -1.0)
        top_s, top_j = lax.top_k(score, k_eff)
        kg = ks[bi, gi, top_j]
        vg = vs[bi, gi, top_j]
        kpos = top_j[..., None] * SEL_BLOCK + jnp.arange(SEL_BLOCK)
        smask = (top_s >= 0)[..., None] & (kpos <= pb[None, None, :, None, None])
        m = k_eff * SEL_BLOCK
        ss = jnp.einsum('bqgnd,bgqkld->bgnqkl', qg, kg, preferred_element_type=F32) * SCALE
        p_sel = masked_softmax(ss.reshape(B, NSA_KV_HEADS, NSA_GROUP, nq, m),
                               smask.reshape(B, NSA_KV_HEADS, 1, nq, m))
        o_sel = jnp.einsum('bgnqm,bgqmd->bqgnd', p_sel.astype(vg.dtype),
                           vg.reshape(B, NSA_KV_HEADS, nq, m, HEAD_DIM))
        shp = (B, nq, NSA_HEADS, HEAD_DIM)
        return o_cmp.reshape(shp), o_sel.reshape(shp)

    return sweep_queries(block, q, q_pos, NSA_Q_BLOCK)


def window_prompt(q, kv):
    B, L = q.shape[:2]
    nq = L // WIN_BLOCK
    r = WINDOW // WIN_BLOCK
    kvp = jnp.pad(kv, ((0, 0), (WINDOW, 0), (0, 0), (0, 0), (0, 0)))
    kvb = kvp.reshape(B, nq + r, WIN_BLOCK, 2, NSA_KV_HEADS, HEAD_DIM)
    band = jnp.concatenate([kvb[:, i:i + nq] for i in range(r + 1)], axis=2)
    qpos = jnp.arange(nq)[:, None] * WIN_BLOCK + jnp.arange(WIN_BLOCK)
    kpos = jnp.arange(nq)[:, None] * WIN_BLOCK - WINDOW + jnp.arange(WINDOW + WIN_BLOCK)
    d = qpos[:, :, None] - kpos[:, None, :]
    mask = (d >= 0) & (d < WINDOW) & (kpos[:, None, :] >= 0)
    o = gqa_attend(q.reshape(B, nq, WIN_BLOCK, NSA_HEADS, HEAD_DIM), band[:, :, :, 0], band[:, :, :, 1], mask)
    return o.reshape(B, L, NSA_HEADS, HEAD_DIM)


def window_sample(q, buf, kv):
    L = q.shape[1]
    nbuf = buf.shape[1]
    rows = jnp.concatenate([buf, kv], axis=1)
    qpos = PAST_LEN + jnp.arange(L)
    kpos = PAST_LEN - nbuf + jnp.arange(nbuf + L)
    d = qpos[:, None] - kpos[None, :]
    mask = (d >= 0) & (d < WINDOW)
    return gqa_attend(q, rows[:, :, 0], rows[:, :, 1], mask), rows[:, L:]


def stick_breaking(q, kv, q_pos):
    T = kv.shape[1]
    k = kv[:, :, 0]
    v = kv[:, :, 1]
    k_pos = jnp.arange(T)

    def block(qb, pb):
        z = jnp.einsum('bqhd,bkhd->bhqk', qb, k, preferred_element_type=F32) * SCALE
        mask = k_pos[None, :] < pb[:, None]
        c = jnp.where(mask, jax.nn.log_sigmoid(-z), 0.0)
        after = lax.cumsum(c, axis=3, reverse=True) - c
        a = jnp.where(mask, jnp.exp(jax.nn.log_sigmoid(z) + after), 0.0)
        return jnp.einsum('bhqk,bkhd->bqhd', a.astype(v.dtype), v)

    return sweep_queries(block, q, q_pos, SB_Q_BLOCK)


def hgrn2(q_raw, f_raw, i_raw, g_raw, lb, norm_w, S0):
    B, L, H, _ = q_raw.shape
    q = jax.nn.silu(q_raw.astype(F32))
    fr = f_raw.astype(F32)
    k = (1.0 - lb) * jax.nn.sigmoid(-fr)
    logf = jnp.logaddexp(jnp.log(lb), jnp.log1p(-lb) + jax.nn.log_sigmoid(fr))
    v = i_raw.astype(F32)
    C = min(HG_CHUNK, L)
    nc = -(-L // C)
    pad = nc * C - L

    def chunks(a):
        a = jnp.pad(a, ((0, 0), (0, pad), (0, 0), (0, 0)))
        return a.reshape(B, nc, C, H, a.shape[-1]).transpose(1, 0, 3, 2, 4)

    tri = jnp.arange(C)[:, None] >= jnp.arange(C)[None, :]

    def step(S, xs):
        qc, kc, vc, gc = xs
        b = jnp.cumsum(gc, axis=2)
        o_inter = jnp.einsum('bhtk,bhkv->bhtv', qc * jnp.exp(b), S)
        decay = jnp.exp(jnp.where(tri[:, :, None], b[:, :, :, None, :] - b[:, :, None, :, :], -jnp.inf))
        att = jnp.einsum('bhtk,bhsk,bhtsk->bhts', qc, kc, decay)
        o = o_inter + jnp.einsum('bhts,bhsv->bhtv', att, vc)
        b_end = b[:, :, -1:, :]
        S = jnp.exp(b_end[:, :, 0, :])[..., None] * S + jnp.einsum('bhsk,bhsv->bhkv', kc * jnp.exp(b_end - b), vc)
        return S, o

    S, o = lax.scan(step, S0.astype(F32), (chunks(q), chunks(k), chunks(v), chunks(logf)))
    o = o.transpose(1, 0, 3, 2, 4).reshape(B, nc * C, H, HG_DV)[:, :L]
    o = o * lax.rsqrt(jnp.mean(o * o, -1, keepdims=True) + NORM_EPS) * norm_w.astype(F32)
    o = o * jax.nn.silu(g_raw.astype(F32))
    return o.reshape(B, L, HG_VW).astype(q_raw.dtype), S.astype(S0.dtype)


def swiglu(x, w1, w3, w2):
    return (jax.nn.silu(x @ w1) * (x @ w3)) @ w2


def moe_swiglu(x2d, rw, rb, w1, w3, w2):
    N, D = x2d.shape
    logits = (x2d @ rw).astype(F32) + rb.astype(F32)
    top_v, top_i = lax.top_k(logits, TOP_K)
    gates = jax.nn.softmax(top_v, axis=-1)
    A = N * TOP_K
    e_flat = top_i.reshape(A)
    tok_flat = jnp.arange(A) // TOP_K
    g_flat = gates.reshape(A)
    order = jnp.argsort(e_flat)
    e_s, tok_s, g_s = e_flat[order], tok_flat[order], g_flat[order]
    counts = jnp.bincount(e_flat, length=N_EXPERTS)
    start = jnp.cumsum(counts) - counts
    padded = (counts + MOE_BLOCK - 1) // MOE_BLOCK * MOE_BLOCK
    pend = jnp.cumsum(padded)
    pstart = pend - padded
    dest = pstart[e_s] + jnp.arange(A) - start[e_s]
    nb = -(-A // MOE_BLOCK) + N_EXPERTS
    P = nb * MOE_BLOCK
    tok_buf = jnp.full((P,), N, jnp.int32).at[dest].set(tok_s.astype(jnp.int32))
    g_buf = jnp.zeros((P,), F32).at[dest].set(g_s)
    blk_e = jnp.minimum(jnp.searchsorted(pend, jnp.arange(nb) * MOE_BLOCK, side='right'), N_EXPERTS - 1)
    x_pad = jnp.concatenate([x2d, jnp.zeros((1, D), x2d.dtype)], axis=0)

    def run(args):
        idx, e = args
        return swiglu(x_pad[idx], w1[e], w3[e], w2[e])

    yb = lax.map(run, (tok_buf.reshape(nb, MOE_BLOCK), blk_e))
    y = jnp.zeros((N + 1, D), F32).at[tok_buf].add(yb.reshape(P, D).astype(F32) * g_buf[:, None])
    return y[:N].astype(x2d.dtype)


def mixer_layer(x, l, p, past):
    B, L, _ = x.shape
    h = rmsnorm(x, p['attn_norm'][l])
    u = h @ p['w_in'][l]

    def cols(off, width, shape):
        return u[..., off:off + width].reshape((B, L) + shape)

    kvs = (2, NSA_KV_HEADS, HEAD_DIM)
    nsa_q = cols(OFF_NSA_Q, NSA_W, (NSA_HEADS, HEAD_DIM))
    cmp_kv = cols(OFF_CMP, KV_W, kvs)
    sel_kv = cols(OFF_SEL, KV_W, kvs)
    win_kv = cols(OFF_WIN, KV_W, kvs)
    nsa_gate = jax.nn.sigmoid(cols(OFF_NSA_GATE, 3 * NSA_HEADS, (3, NSA_HEADS, 1)))
    sb_q = cols(OFF_SB_Q, SB_W, (SB_HEADS, HEAD_DIM))
    sb_kv = cols(OFF_SB_KV, 2 * SB_W, (2, SB_HEADS, HEAD_DIM))
    hg_q = cols(OFF_HG_Q, HG_KW, (HG_HEADS, HG_DK))
    hg_f = cols(OFF_HG_F, HG_KW, (HG_HEADS, HG_DK))
    hg_i = cols(OFF_HG_I, HG_VW, (HG_HEADS, HG_DV))
    hg_g = cols(OFF_HG_G, HG_VW, (HG_HEADS, HG_DV))
    merge_gate = jax.nn.sigmoid(cols(OFF_MERGE, 3 * D_MODEL, (3, D_MODEL)))

    if past is None:
        q_pos = jnp.arange(L)
        cmp_all, sel_all, sb_all = cmp_kv, sel_kv, sb_kv
        o_win = window_prompt(nsa_q, win_kv)
        win_state = win_kv[:, L - min(WINDOW, L):]
        S0 = jnp.zeros((B, HG_HEADS, HG_DK, HG_DV), x.dtype)
    else:
        pt = past['page_table']

        def paged(pool):
            rows = pool[l][pt]
            return rows.reshape((B, -1) + pool.shape[3:])

        q_pos = PAST_LEN + jnp.arange(L)
        cmp_all = jnp.concatenate([paged(past['cmp']), cmp_kv], axis=1)
        sel_all = jnp.concatenate([paged(past['sel']), sel_kv], axis=1)
        sb_all = jnp.concatenate([paged(past['sb']), sb_kv], axis=1)
        o_win, win_state = window_sample(nsa_q, past['win'][l], win_kv)
        S0 = past['hgrn'][l]

    o_cmp, o_sel = nsa_global(nsa_q, q_pos, cmp_all, sel_all,
                              p['cmp_pe_k'][l], p['cmp_w1_k'][l], p['cmp_w2_k'][l],
                              p['cmp_pe_v'][l], p['cmp_w1_v'][l], p['cmp_w2_v'][l])
    o_nsa = (nsa_gate[:, :, 0] * o_cmp + nsa_gate[:, :, 1] * o_sel + nsa_gate[:, :, 2] * o_win).reshape(B, L, NSA_W)
    o_sb = stick_breaking(sb_q, sb_all, q_pos).reshape(B, L, SB_W)
    lb_all = jnp.cumsum(jax.nn.softmax(p['hg_lb_logits'].astype(F32), axis=0), axis=0)
    lb = (lb_all[l] - lb_all[0]).reshape(HG_HEADS, HG_DK)
    o_hg, S = hgrn2(hg_q, hg_f, hg_i, hg_g, lb, p['hg_norm'][l], S0)
    merged = (merge_gate[:, :, 0] * (o_nsa @ p['w_br_nsa'][l])
              + merge_gate[:, :, 1] * (o_sb @ p['w_br_sb'][l])
              + merge_gate[:, :, 2] * (o_hg @ p['w_br_hg'][l]))
    x = x + merged @ p['w_out'][l]
    return x, (cmp_kv, sel_kv, win_state, sb_kv, S)


def channel_layer(x, l, p):
    B, L, D = x.shape
    h = rmsnorm(x, p['ffn_norm'][l])
    i = l // 2
    if l % 2 == 0:
        y = swiglu(h, p['ffn_w1'][i], p['ffn_w3'][i], p['ffn_w2'][i])
    else:
        y = moe_swiglu(h.reshape(B * L, D), p['router_w'][i], p['router_b'][i],
                       p['moe_w1'][i], p['moe_w3'][i], p['moe_w2'][i]).reshape(B, L, D)
    return x + y


def trunk(x, p, past):
    states = []
    for l in range(DEPTH):
        x, st = mixer_layer(x, l, p, past)
        x = channel_layer(x, l, p)
        states.append(st)
    y = rmsnorm(x, p['final_norm'])
    stacked = [jnp.stack([st[i] for st in states]) for i in range(5)]
    return y, stacked


def setup_inputs(seed: int = 0) -> dict:
    key = jax.random.key(seed)
    keys = iter(jax.random.split(key, 48))

    def nrm(shape, scale=1.0):
        return jax.random.normal(next(keys), shape, jnp.float32) * scale

    def gain(shape):
        return 1.0 + nrm(shape, 0.02)

    D = D_MODEL
    n_pages = PAST_LEN // PAGE_SIZE
    n_used = DEC_BATCH * n_pages
    n_pool = n_used + max(1, n_used // 4)
    wbuf = min(WINDOW, PAST_LEN)
    page_table = jax.random.permutation(next(keys), n_pool)[:n_used].reshape(DEC_BATCH, n_pages).astype(jnp.int32)
    return {
        'x_prompt': nrm((BATCH, SEQ, D)),
        'x_sample': nrm((DEC_BATCH, DEC_SEQ, D)),
        'cache_nsa_cmp_kv': nrm((DEPTH, n_pool, PAGE_SIZE, 2, NSA_KV_HEADS, HEAD_DIM)),
        'cache_nsa_sel_kv': nrm((DEPTH, n_pool, PAGE_SIZE, 2, NSA_KV_HEADS, HEAD_DIM)),
        'cache_sb_kv': nrm((DEPTH, n_pool, PAGE_SIZE, 2, SB_HEADS, HEAD_DIM)),
        'cache_nsa_win_kv': nrm((DEPTH, DEC_BATCH, wbuf, 2, NSA_KV_HEADS, HEAD_DIM)),
        'state_hgrn': nrm((DEPTH, DEC_BATCH, HG_HEADS, HG_DK, HG_DV), 0.5),
        'page_table': page_table,
        'attn_norm': gain((DEPTH, D)),
        'w_in': nrm((DEPTH, D, IN_COLS), D ** -0.5),
        'cmp_pe_k': nrm((DEPTH, CMP_LEN, HEAD_DIM), 0.1),
        'cmp_w1_k': nrm((DEPTH, CMP_LEN * HEAD_DIM, HEAD_DIM), (CMP_LEN * HEAD_DIM) ** -0.5),
        'cmp_w2_k': nrm((DEPTH, HEAD_DIM, HEAD_DIM), HEAD_DIM ** -0.5),
        'cmp_pe_v': nrm((DEPTH, CMP_LEN, HEAD_DIM), 0.1),
        'cmp_w1_v': nrm((DEPTH, CMP_LEN * HEAD_DIM, HEAD_DIM), (CMP_LEN * HEAD_DIM) ** -0.5),
        'cmp_w2_v': nrm((DEPTH, HEAD_DIM, HEAD_DIM), HEAD_DIM ** -0.5),
        'hg_lb_logits': nrm((DEPTH, HG_KW), 0.5),
        'hg_norm': gain((DEPTH, HG_DV)),
        'w_br_nsa': nrm((DEPTH, NSA_W, D), NSA_W ** -0.5),
        'w_br_sb': nrm((DEPTH, SB_W, D), SB_W ** -0.5),
        'w_br_hg': nrm((DEPTH, HG_VW, D), HG_VW ** -0.5),
        'w_out': nrm((DEPTH, D, D), D ** -0.5),
        'ffn_norm': gain((DEPTH, D)),
        'ffn_w1': nrm((N_DENSE, D, D_FF), D ** -0.5),
        'ffn_w3': nrm((N_DENSE, D, D_FF), D ** -0.5),
        'ffn_w2': nrm((N_DENSE, D_FF, D), D_FF ** -0.5),
        'router_w': nrm((N_MOE, D, N_EXPERTS), D ** -0.5),
        'router_b': nrm((N_MOE, N_EXPERTS), 0.01),
        'moe_w1': nrm((N_MOE, N_EXPERTS, D, D_FF), D ** -0.5),
        'moe_w3': nrm((N_MOE, N_EXPERTS, D, D_FF), D ** -0.5),
        'moe_w2': nrm((N_MOE, N_EXPERTS, D_FF, D), D_FF ** -0.5),
        'final_norm': gain((D,)),
    }


def reference(x_prompt, x_sample, cache_nsa_cmp_kv, cache_nsa_sel_kv, cache_sb_kv, cache_nsa_win_kv,
              state_hgrn, page_table, attn_norm, w_in, cmp_pe_k, cmp_w1_k, cmp_w2_k, cmp_pe_v, cmp_w1_v,
              cmp_w2_v, hg_lb_logits, hg_norm, w_br_nsa, w_br_sb, w_br_hg, w_out, ffn_norm, ffn_w1, ffn_w3,
              ffn_w2, router_w, router_b, moe_w1, moe_w3, moe_w2, final_norm):
    p = dict(attn_norm=attn_norm, w_in=w_in, cmp_pe_k=cmp_pe_k, cmp_w1_k=cmp_w1_k, cmp_w2_k=cmp_w2_k,
             cmp_pe_v=cmp_pe_v, cmp_w1_v=cmp_w1_v, cmp_w2_v=cmp_w2_v, hg_lb_logits=hg_lb_logits,
             hg_norm=hg_norm, w_br_nsa=w_br_nsa, w_br_sb=w_br_sb, w_br_hg=w_br_hg, w_out=w_out,
             ffn_norm=ffn_norm, ffn_w1=ffn_w1, ffn_w3=ffn_w3, ffn_w2=ffn_w2, router_w=router_w,
             router_b=router_b, moe_w1=moe_w1, moe_w3=moe_w3, moe_w2=moe_w2, final_norm=final_norm)
    past = dict(page_table=page_table, cmp=cache_nsa_cmp_kv, sel=cache_nsa_sel_kv, sb=cache_sb_kv,
                win=cache_nsa_win_kv, hgrn=state_hgrn)
    y_prompt, (p_cmp, p_sel, p_win, p_sb, p_hg) = trunk(x_prompt, p, None)
    y_sample, (s_cmp, s_sel, s_win, s_sb, s_hg) = trunk(x_sample, p, past)
    return (y_prompt, y_sample, p_cmp, p_sel, p_win, p_sb, p_hg, s_cmp, s_sel, s_win, s_sb, s_hg)
```

```python
import functools

import jax
import jax.numpy as jnp
import numpy as np
from jax import lax
from jax.experimental import pallas as pl
from jax.experimental.pallas import tpu as pltpu

F32 = jnp.float32
BF16 = jnp.bfloat16

HEAD_DIM = 128
SCALE = HEAD_DIM ** -0.5
NSA_HEADS = 8
NSA_KV_HEADS = 2
NSA_GROUP = NSA_HEADS // NSA_KV_HEADS
CMP_LEN = 32
CMP_STRIDE = 16
SEL_BLOCK = 64
SEL_TOPK = 16
WINDOW = 512
SB_HEADS = 8
HG_HEADS = 8
HG_CHUNK = 64
HG_SUB = 16
N_EXPERTS = 8
TOP_K = 2
MOE_BLOCK = 128
NORM_EPS = 1e-6
NEG_BIG = -1e30

LANES = 128
SUBLANES = 8
VMEM_LIMIT = 56 * 1024 * 1024

NSA_W = NSA_HEADS * HEAD_DIM
KV_W = 2 * NSA_KV_HEADS * HEAD_DIM
OFF_CMP = NSA_W
OFF_SEL = OFF_CMP + KV_W
OFF_WIN = OFF_SEL + KV_W
OFF_NSA_GATE = OFF_WIN + KV_W
N_GATE = 3 * NSA_HEADS
HEAD_COLS = OFF_NSA_GATE + LANES
REST_START = OFF_NSA_GATE + N_GATE
SB_W = SB_HEADS * HEAD_DIM
HG_W = HG_HEADS * HEAD_DIM
R_SB_Q = 0
R_SB_K = R_SB_Q + SB_W
R_SB_V = R_SB_K + SB_W
R_HG_Q = R_SB_V + SB_W
R_HG_F = R_HG_Q + HG_W
R_HG_I = R_HG_F + HG_W
R_HG_G = R_HG_I + HG_W
R_MERGE = R_HG_G + HG_W


def _cp(n_axes, vmem=VMEM_LIMIT):
    return pltpu.CompilerParams(dimension_semantics=("arbitrary",) * n_axes, vmem_limit_bytes=vmem)


def _tile(n, pref, quantum=LANES):
    if n <= pref:
        return n
    t = (pref // quantum) * quantum
    while t > quantum and n % t:
        t -= quantum
    assert n % t == 0, (n, pref)
    return t


def _dot(a, b):
    return jnp.dot(a, b, preferred_element_type=F32)


def _dot_nt(a, b):
    return lax.dot_general(a, b, (((1,), (1,)), ((), ())), preferred_element_type=F32)


def _dot_tn(a, b):
    return lax.dot_general(a, b, (((0,), (0,)), ((), ())), preferred_element_type=F32)


def _split3(x):
    hi = x.astype(BF16)
    r = x - hi.astype(F32)
    mid = r.astype(BF16)
    lo = (r - mid.astype(F32)).astype(BF16)
    return hi, mid, lo


def _sigmoid(x):
    return 1.0 / (1.0 + jnp.exp(-x))


def _silu(x):
    return x * _sigmoid(x)


def _log_sigmoid(x):
    return jnp.minimum(x, 0.0) - jnp.log1p(jnp.exp(-jnp.abs(x)))


def _rmsnorm_kernel(x_ref, g_ref, o_ref):
    x = x_ref[...]
    y = x * lax.rsqrt(jnp.mean(x * x, axis=-1, keepdims=True) + NORM_EPS)
    o_ref[...] = (y * g_ref[...]).astype(o_ref.dtype)


def rmsnorm(x, g, out_dtype, tm):
    m, d = x.shape
    return pl.pallas_call(
        _rmsnorm_kernel,
        grid=(m // tm,),
        in_specs=[pl.BlockSpec((tm, d), lambda i: (i, 0)), pl.BlockSpec((1, d), lambda i: (0, 0))],
        out_specs=pl.BlockSpec((tm, d), lambda i: (i, 0)),
        out_shape=jax.ShapeDtypeStruct((m, d), out_dtype),
        compiler_params=_cp(1),
        name="rmsnorm",
    )(x, g.reshape(1, d))


def _mm_epilogue(mode, prods, x_refs):
    if mode == "plain":
        return prods[0]
    if mode == "merge":
        out = _sigmoid(x_refs[0][...]) * prods[0]
        for x_ref, p in zip(x_refs[1:], prods[1:]):
            out = out + _sigmoid(x_ref[...]) * p
        return out
    if mode == "residual":
        return x_refs[0][...] + prods[0]
    return _silu(prods[0]) * prods[1]


def _mm_kernel(*refs, n_a, n_w, n_extra, mode, cast):
    a_refs = refs[:n_a]
    w_refs = refs[n_a:n_a + n_w]
    x_refs = refs[n_a + n_w:n_a + n_w + n_extra]
    o_ref = refs[n_a + n_w + n_extra]
    wbf_refs = refs[n_a + n_w + n_extra + 1:]

    if cast:
        @pl.when(pl.program_id(1) == 0)
        def _():
            for w_ref, wbf_ref in zip(w_refs, wbf_refs):
                wbf_ref[...] = w_ref[...].astype(BF16)
        ws = [r[...] for r in wbf_refs]
    else:
        ws = [r[...] for r in w_refs]

    prods = [_dot(a_refs[min(i, n_a - 1)][...], w) for i, w in enumerate(ws)]
    o_ref[...] = _mm_epilogue(mode, prods, x_refs).astype(o_ref.dtype)


def matmul(a_list, w_list, w_index, col0, n_out, mode, out_dtype, extras=(), tm=512, tn=512):
    m = a_list[0].shape[0]
    tn = _tile(n_out, tn)
    assert col0 % tn == 0 and m % tm == 0
    cast = w_list[0].dtype != BF16
    in_specs, args, scratch = [], [], []
    for a in a_list:
        in_specs.append(pl.BlockSpec((tm, a.shape[1]), lambda j, i: (i, 0)))
        args.append(a)
    for w in w_list:
        k = w.shape[-2]
        if w.ndim == 3:
            in_specs.append(pl.BlockSpec((None, k, tn), lambda j, i: (w_index, 0, j + col0 // tn)))
        else:
            in_specs.append(pl.BlockSpec((k, tn), lambda j, i: (0, j + col0 // tn)))
        args.append(w)
        if cast:
            scratch.append(pltpu.VMEM((k, tn), BF16))
    for x, off in extras:
        assert off % tn == 0
        in_specs.append(pl.BlockSpec((tm, tn), lambda j, i, off=off: (i, j + off // tn)))
        args.append(x)
    return pl.pallas_call(
        functools.partial(_mm_kernel, n_a=len(a_list), n_w=len(w_list), n_extra=len(extras), mode=mode, cast=cast),
        grid=(n_out // tn, m // tm),
        in_specs=in_specs,
        out_specs=pl.BlockSpec((tm, tn), lambda j, i: (i, j)),
        out_shape=jax.ShapeDtypeStruct((m, n_out), out_dtype),
        scratch_shapes=scratch,
        compiler_params=_cp(2),
        name="mm_" + mode,
    )(*args)


def _suffix_matrix():
    j = lax.broadcasted_iota(jnp.int32, (LANES, 2 * LANES), 0)
    s = lax.broadcasted_iota(jnp.int32, (LANES, 2 * LANES), 1)
    return jnp.where((j > s) | (s >= LANES), 1.0, 0.0).astype(BF16)


def _sb_weights(z, mask, carry, umat):
    ls_pos = _log_sigmoid(z)
    c = jnp.where(mask, ls_pos - z, 0.0)
    n_sub = z.shape[1] // LANES
    pieces = [None] * n_sub
    for sb in reversed(range(n_sub)):
        sl = slice(sb * LANES, (sb + 1) * LANES)
        hi, mid, lo = _split3(c[:, sl])
        r = _dot(hi, umat) + _dot(mid, umat) + _dot(lo, umat)
        after = carry + r[:, :LANES]
        pieces[sb] = jnp.where(mask[:, sl], jnp.exp(ls_pos[:, sl] + after), 0.0)
        carry = carry + r[:, LANES:]
    a = pieces[0] if n_sub == 1 else jnp.concatenate(pieces, axis=1)
    return a, carry


def _sb_prompt_kernel(qi_tab, kb_tab, q_ref, k_ref, v_ref, o_ref, acc_ref, carry_ref, *, tq, tk):
    p = pl.program_id(2)
    qi = qi_tab[p]
    kb = kb_tab[p]

    @pl.when(kb == qi)
    def _():
        acc_ref[...] = jnp.zeros_like(acc_ref)
        carry_ref[...] = jnp.zeros_like(carry_ref)

    z = _dot_nt(q_ref[...].astype(BF16), k_ref[...].astype(BF16)) * SCALE
    qpos = qi * tq + lax.broadcasted_iota(jnp.int32, (tq, tk), 0)
    kpos = kb * tk + lax.broadcasted_iota(jnp.int32, (tq, tk), 1)
    a, carry = _sb_weights(z, kpos < qpos, carry_ref[...], _suffix_matrix())
    acc_ref[...] += _dot(a.astype(BF16), v_ref[...].astype(BF16))
    carry_ref[...] = carry

    @pl.when(kb == 0)
    def _():
        o_ref[...] = acc_ref[...].astype(o_ref.dtype)


def sb_prompt(u_rest, batch, seq, t=512):
    t = min(t, seq)
    nq = seq // t
    pairs = [(qi, kb) for qi in range(nq) for kb in range(qi, -1, -1)]
    qi_tab = jnp.asarray([p[0] for p in pairs], jnp.int32)
    kb_tab = jnp.asarray([p[1] for p in pairs], jnp.int32)
    cq, ck, cv = R_SB_Q // LANES, R_SB_K // LANES, R_SB_V // LANES
    grid_spec = pltpu.PrefetchScalarGridSpec(
        num_scalar_prefetch=2,
        grid=(batch, SB_HEADS, len(pairs)),
        in_specs=[
            pl.BlockSpec((t, LANES), lambda b, h, p, qt, kt: (b * nq + qt[p], cq + h)),
            pl.BlockSpec((t, LANES), lambda b, h, p, qt, kt: (b * nq + kt[p], ck + h)),
            pl.BlockSpec((t, LANES), lambda b, h, p, qt, kt: (b * nq + kt[p], cv + h)),
        ],
        out_specs=pl.BlockSpec((t, LANES), lambda b, h, p, qt, kt: (b * nq + qt[p], h)),
        scratch_shapes=[pltpu.VMEM((t, LANES), F32), pltpu.VMEM((t, LANES), F32)],
    )
    return pl.pallas_call(
        functools.partial(_sb_prompt_kernel, tq=t, tk=t),
        grid_spec=grid_spec,
        out_shape=jax.ShapeDtypeStruct((batch * seq, SB_W), BF16),
        compiler_params=_cp(3),
        name="sb_prompt",
    )(qi_tab, kb_tab, u_rest, u_rest, u_rest)


def _hgrn_chunk(qr, fr, v, gr, lbp, norm_w, st, c, sub, c_real):
    log_lb, log_1m_lb, one_m_lb = lbp[0:1], lbp[1:2], lbp[2:3]
    q = _silu(qr)
    k = one_m_lb * _sigmoid(-fr)
    bb = log_1m_lb + _log_sigmoid(fr)
    mx = jnp.maximum(log_lb, bb)
    logf = mx + jnp.log1p(jnp.exp(-jnp.abs(log_lb - bb)))
    row = lax.broadcasted_iota(jnp.int32, (c, c), 0)
    col = lax.broadcasted_iota(jnp.int32, (c, c), 1)
    tri = jnp.where(row >= col, 1.0, 0.0).astype(BF16)
    hi, mid, lo = _split3(logf)
    b = _dot(tri, hi) + _dot(tri, mid) + _dot(tri, lo)
    o = _dot_nt((q * jnp.exp(b)).astype(BF16), st.astype(BF16))
    ridx = lax.broadcasted_iota(jnp.int32, (c, HEAD_DIM), 0)
    lane = lax.broadcasted_iota(jnp.int32, (sub, c), 1)
    att_rows = []
    for i in range(c // sub):
        r0 = i * sub
        b_i = b[r0:r0 + sub]
        q_i = q[r0:r0 + sub]
        att_i = jnp.zeros((sub, c), F32)
        if i > 0:
            rho = b_i[0:1]
            earlier = ridx < r0
            k_dec = jnp.where(earlier, k * jnp.exp(jnp.where(earlier, rho - b, 0.0)), 0.0)
            att_i = _dot_nt((q_i * jnp.exp(b_i - rho)).astype(BF16), k_dec.astype(BF16))
        trow = lax.broadcasted_iota(jnp.int32, (sub, 1), 0)
        for s in range(sub):
            d = q_i * jnp.exp(jnp.where(trow >= s, b_i - b_i[s:s + 1], 0.0)) * k[r0 + s:r0 + s + 1]
            colsum = jnp.sum(d, axis=1, keepdims=True)
            att_i = att_i + jnp.where((lane == r0 + s) & (trow >= s), colsum, 0.0)
        att_rows.append(att_i)
    att = att_rows[0] if len(att_rows) == 1 else jnp.concatenate(att_rows, axis=0)
    o = o + _dot(att.astype(BF16), v.astype(BF16))
    b_end = b[c_real - 1:c_real]
    real = ridx < c_real
    k_end = jnp.where(real, k * jnp.exp(jnp.where(real, b_end - b, 0.0)), 0.0)
    st = st * jnp.exp(b_end) + _dot_tn(v.astype(BF16), k_end.astype(BF16))
    o = o * lax.rsqrt(jnp.mean(o * o, axis=-1, keepdims=True) + NORM_EPS) * norm_w
    return o * _silu(gr), st


def _hgrn_kernel(q_ref, f_ref, i_ref, g_ref, lbp_ref, nw_ref, s0_ref, o_ref, s_out_ref, st_ref, *, c, sub, n_chunks):
    t = pl.program_id(2)

    @pl.when(t == 0)
    def _():
        st_ref[...] = s0_ref[...].T

    lbp = lbp_ref[...]
    nw = nw_ref[...]

    if c < HG_SUB:
        pad = jnp.zeros((HG_SUB - c, HEAD_DIM), F32)
        ins = [jnp.concatenate([r[...], pad], axis=0) for r in (q_ref, f_ref, i_ref, g_ref)]
        o, st = _hgrn_chunk(*ins, lbp, nw, st_ref[...], HG_SUB, HG_SUB, c)
        st_ref[...] = st
        o_ref[...] = o[:c].astype(o_ref.dtype)
    else:
        def body(ci, carry):
            r = pl.multiple_of(ci * c, c)
            rows = pl.ds(r, c)
            o, st = _hgrn_chunk(q_ref[rows, :], f_ref[rows, :], i_ref[rows, :], g_ref[rows, :], lbp, nw,
                                st_ref[...], c, sub, c)
            st_ref[...] = st
            o_ref[rows, :] = o.astype(o_ref.dtype)
            return carry

        lax.fori_loop(0, n_chunks, body, 0)

    @pl.when(t == pl.num_programs(2) - 1)
    def _():
        s_out_ref[...] = st_ref[...].T


def hgrn_lb_params(lb):
    rows = jnp.stack([jnp.log(lb), jnp.log1p(-lb), 1.0 - lb], axis=1)
    return jnp.concatenate([rows, jnp.zeros((lb.shape[0], SUBLANES - 3, lb.shape[1]), F32)], axis=1)


def hgrn(u_rest, row0, batch, seq, lbp, norm_w, s0, tl=512):
    c = min(HG_CHUNK, seq)
    sub = min(HG_SUB, c)
    tl = min(tl, seq)
    nt = seq // tl
    rb0 = row0 // tl
    assert row0 % tl == 0 and seq % tl == 0 and tl % c == 0 and (c % HG_SUB == 0 or nt == 1)

    def col(off):
        return pl.BlockSpec((tl, LANES), lambda b, h, t, off=off: (rb0 + b * nt + t, off // LANES + h))

    o, s_out = pl.pallas_call(
        functools.partial(_hgrn_kernel, c=c, sub=sub, n_chunks=tl // c),
        grid=(batch, HG_HEADS, nt),
        in_specs=[col(R_HG_Q), col(R_HG_F), col(R_HG_I), col(R_HG_G),
                  pl.BlockSpec((None, SUBLANES, LANES), lambda b, h, t: (h, 0, 0)),
                  pl.BlockSpec((1, LANES), lambda b, h, t: (0, 0)),
                  pl.BlockSpec((None, None, HEAD_DIM, HEAD_DIM), lambda b, h, t: (b, h, 0, 0))],
        out_specs=[pl.BlockSpec((tl, LANES), lambda b, h, t: (b * nt + t, h)),
                   pl.BlockSpec((None, None, HEAD_DIM, HEAD_DIM), lambda b, h, t: (b, h, 0, 0))],
        out_shape=[jax.ShapeDtypeStruct((batch * seq, HG_W), F32),
                   jax.ShapeDtypeStruct((batch, HG_HEADS, HEAD_DIM, HEAD_DIM), F32)],
        scratch_shapes=[pltpu.VMEM((HEAD_DIM, HEAD_DIM), F32)],
        compiler_params=_cp(3),
        name="hgrn",
    )(u_rest, u_rest, u_rest, u_rest, lbp, norm_w.reshape(1, LANES), s0)
    return o, s_out


SEG_W = CMP_STRIDE * HEAD_DIM


def compress_params(pe, w1):
    w = jnp.concatenate([w1[:SEG_W], w1[SEG_W:]], axis=1).astype(BF16)
    rows = jnp.stack([pe[:CMP_STRIDE].reshape(SEG_W), pe[CMP_STRIDE:].reshape(SEG_W)])
    return w, jnp.concatenate([rows, jnp.zeros((SUBLANES - 2, SEG_W), F32)]).astype(BF16)


def _segment_products(r, w, pe):
    bias = _dot(pe, w)
    bias = jnp.concatenate([bias[0:1, :LANES], bias[1:2, LANES:]], axis=1)
    return _dot(r.astype(BF16), w) + bias


def _cmp1_prompt_kernel(x_ref, w_ref, pe_ref, o_ref, r_ref, *, n_seg):
    for l in range(CMP_STRIDE):
        r_ref[:, l * LANES:(l + 1) * LANES] = x_ref[pl.ds(l, n_seg, stride=CMP_STRIDE), :]
    o_ref[...] = _segment_products(r_ref[...], w_ref[...], pe_ref[...])


def compress_segments_prompt(u_head, batch, seq, wcat, pecat, tr=512):
    tr = min(tr, seq)
    n_seg = tr // CMP_STRIDE
    nt = seq // tr
    c0 = OFF_CMP // LANES
    return pl.pallas_call(
        functools.partial(_cmp1_prompt_kernel, n_seg=n_seg),
        grid=(batch, 2, NSA_KV_HEADS, nt),
        in_specs=[pl.BlockSpec((tr, LANES), lambda b, kv, g, t: (b * nt + t, c0 + kv * NSA_KV_HEADS + g)),
                  pl.BlockSpec((None, SEG_W, 2 * LANES), lambda b, kv, g, t: (kv, 0, 0)),
                  pl.BlockSpec((None, SUBLANES, SEG_W), lambda b, kv, g, t: (kv, 0, 0))],
        out_specs=pl.BlockSpec((None, None, None, n_seg, 2 * LANES), lambda b, kv, g, t: (b, kv, g, t, 0)),
        out_shape=jax.ShapeDtypeStruct((batch, 2, NSA_KV_HEADS, seq // CMP_STRIDE, 2 * LANES), F32),
        scratch_shapes=[pltpu.VMEM((n_seg, SEG_W), F32)],
        compiler_params=_cp(4),
        name="cmp_segments_prompt",
    )(u_head, wcat, pecat)


def _cmp2_kernel(pq_ref, w2_ref, o_ref, *, nc):
    pq = pq_ref[...]
    q_next = pltpu.roll(pq[:, LANES:], shift=nc - 1, axis=0)
    hid = _silu(pq[:, :LANES] + q_next)
    out = _dot(hid.astype(BF16), w2_ref[...].astype(BF16))
    row = lax.broadcasted_iota(jnp.int32, out.shape, 0)
    o_ref[...] = jnp.where(row < nc - 1, out, 0.0)


def compress_finish(pq, w2):
    batch, _, _, nc, _ = pq.shape
    return pl.pallas_call(
        functools.partial(_cmp2_kernel, nc=nc),
        grid=(batch, 2, NSA_KV_HEADS),
        in_specs=[pl.BlockSpec((None, None, None, nc, 2 * LANES), lambda b, kv, g: (b, kv, g, 0, 0)),
                  pl.BlockSpec((None, HEAD_DIM, HEAD_DIM), lambda b, kv, g: (kv, 0, 0))],
        out_specs=pl.BlockSpec((None, None, None, nc, LANES), lambda b, kv, g: (b, kv, g, 0, 0)),
        out_shape=jax.ShapeDtypeStruct((batch, 2, NSA_KV_HEADS, nc, LANES), F32),
        compiler_params=_cp(3),
        name="cmp_finish",
    )(pq, w2)


def _cmp_select_kernel(q_ref, kc_ref, vc_ref, o_ref, sel_ref, score_ref, *, tq, tqp, nc, n_cmp, n_slc, nsp, pos0):
    qi = pl.program_id(2)
    qb = q_ref[...]
    parts = []
    for h in range(NSA_GROUP):
        qh = qb[:, h * HEAD_DIM:(h + 1) * HEAD_DIM]
        if tqp > tq:
            qh = jnp.concatenate([qh, jnp.zeros((tqp - tq, HEAD_DIM), F32)], axis=0)
        parts.append(qh)
    q4 = jnp.concatenate(parts, axis=0).astype(BF16)
    rows = NSA_GROUP * tqp
    st = _dot_nt(kc_ref[...].astype(BF16), q4) * SCALE
    ci = lax.broadcasted_iota(jnp.int32, (nc, rows), 0)
    tok = lax.broadcasted_iota(jnp.int32, (nc, rows), 1) & (tqp - 1)
    tpos = pos0 + qi * tq + tok
    valid = (ci < n_cmp) & (ci * CMP_STRIDE + CMP_LEN - 1 <= tpos)
    st = jnp.where(valid, st, NEG_BIG)
    m = jnp.max(st, axis=0, keepdims=True)
    e = jnp.where(valid, jnp.exp(st - m), 0.0)
    den = jnp.sum(e, axis=0, keepdims=True)
    pt = e / jnp.where(den > 0, den, 1.0)
    o = _dot_tn(pt.astype(BF16), vc_ref[...].astype(BF16))
    for h in range(NSA_GROUP):
        o_ref[:, h * HEAD_DIM:(h + 1) * HEAD_DIM] = o[h * tqp:h * tqp + tq]
    psum = pt[:, 0:tqp]
    for h in range(1, NSA_GROUP):
        psum = psum + pt[:, h * tqp:(h + 1) * tqp]
    jj = lax.broadcasted_iota(jnp.int32, (nsp, nc), 0)
    ii = lax.broadcasted_iota(jnp.int32, (nsp, nc), 1)
    cover = ((ii * CMP_STRIDE < jj * SEL_BLOCK + SEL_BLOCK) & (ii * CMP_STRIDE + CMP_LEN - 1 >= jj * SEL_BLOCK)
             & (ii < n_cmp))
    cover = jnp.where(cover, 1.0, 0.0).astype(BF16)
    hi, mid, lo = _split3(psum)
    imp = _dot(cover, hi) + _dot(cover, mid) + _dot(cover, lo)
    j = lax.broadcasted_iota(jnp.int32, (nsp, tqp), 0)
    tpos2 = pos0 + qi * tq + lax.broadcasted_iota(jnp.int32, (nsp, tqp), 1)
    cur = tpos2 // SEL_BLOCK
    forced = (j == 0) | (j == cur) | (j == cur - 1)
    ok = (j * SEL_BLOCK <= tpos2) & (j < n_slc)
    score = jnp.where(ok, jnp.where(forced, NSA_GROUP + 1.0, imp), -1.0)
    score_ref[...] = score

    def body(jp, rank):
        other = score_ref[pl.ds(jp, 1), :]
        ahead = jnp.where(other > score, 1.0, jnp.where((other == score) & (jp < j), 1.0, 0.0))
        return rank + ahead

    rank = lax.fori_loop(0, n_slc, body, jnp.zeros((nsp, tqp), F32))
    sel_t = jnp.where((rank < SEL_TOPK) & (score >= 0), 1.0, 0.0)
    sel_ref[...] = sel_t.T[:tq]


def cmp_select(u_head, row0, batch, seq, kv_cmp, n_cmp, n_slc, pos0, tq=256):
    tq = min(tq, seq)
    tqp = max(tq, LANES)
    nt = seq // tq
    rb0 = row0 // tq
    nc = kv_cmp.shape[3]
    nsp = -(-n_slc // LANES) * LANES
    gw = NSA_GROUP * HEAD_DIM
    assert row0 % tq == 0 and seq % tq == 0 and tqp & (tqp - 1) == 0
    return pl.pallas_call(
        functools.partial(_cmp_select_kernel, tq=tq, tqp=tqp, nc=nc, n_cmp=n_cmp, n_slc=n_slc, nsp=nsp, pos0=pos0),
        grid=(batch, NSA_KV_HEADS, nt),
        in_specs=[pl.BlockSpec((tq, gw), lambda b, g, t: (rb0 + b * nt + t, g)),
                  pl.BlockSpec((None, None, None, nc, LANES), lambda b, g, t: (b, 0, g, 0, 0)),
                  pl.BlockSpec((None, None, None, nc, LANES), lambda b, g, t: (b, 1, g, 0, 0))],
        out_specs=[pl.BlockSpec((tq, gw), lambda b, g, t: (b * nt + t, g)),
                   pl.BlockSpec((None, None, tq, nsp), lambda b, g, t: (b, g, t, 0))],
        out_shape=[jax.ShapeDtypeStruct((batch * seq, NSA_W), F32),
                   jax.ShapeDtypeStruct((batch, NSA_KV_HEADS, seq, nsp), F32)],
        scratch_shapes=[pltpu.VMEM((nsp, tqp), F32)],
        compiler_params=_cp(3),
        name="cmp_select",
    )(u_head, kv_cmp, kv_cmp)


def _flash_step(s, mask, v, m_ref, l_ref, acc_ref):
    s = jnp.where(mask, s, NEG_BIG)
    m_prev = m_ref[...]
    m_new = jnp.maximum(m_prev, jnp.max(s, axis=1, keepdims=True))
    e = jnp.where(mask, jnp.exp(s - m_new), 0.0)
    alpha = jnp.exp(m_prev - m_new)
    l_ref[...] = alpha * l_ref[...] + jnp.sum(e, axis=1, keepdims=True)
    acc_ref[...] = alpha * acc_ref[...] + _dot(e.astype(BF16), v)
    m_ref[...] = m_new


def _flash_init(m_ref, l_ref, acc_ref):
    m_ref[...] = jnp.full_like(m_ref, NEG_BIG)
    l_ref[...] = jnp.zeros_like(l_ref)
    acc_ref[...] = jnp.zeros_like(acc_ref)


def _flash_result(l_ref, acc_ref):
    l = l_ref[...]
    return acc_ref[...] / jnp.where(l > 0, l, 1.0)


def _stack_heads(qb, pad_to=None):
    parts = []
    for h in range(NSA_GROUP):
        qh = qb[:, h * HEAD_DIM:(h + 1) * HEAD_DIM]
        if pad_to is not None and pad_to > qh.shape[0]:
            qh = jnp.concatenate([qh, jnp.zeros((pad_to - qh.shape[0], HEAD_DIM), qh.dtype)], axis=0)
        parts.append(qh)
    return jnp.concatenate(parts, axis=0)


def _flash_prompt_kernel(qi_tab, kb_tab, first_tab, last_tab, *refs, tq, tk, mode, nsp):
    if mode == "sel":
        q_ref, k_ref, v_ref, sel_ref, o_ref, m_ref, l_ref, acc_ref = refs
    else:
        q_ref, k_ref, v_ref, o_ref, m_ref, l_ref, acc_ref = refs
    p = pl.program_id(2)
    qi = qi_tab[p]
    kb = kb_tab[p]

    @pl.when(first_tab[p] == 1)
    def _():
        _flash_init(m_ref, l_ref, acc_ref)

    q4 = _stack_heads(q_ref[...]).astype(BF16)
    s = _dot_nt(q4, k_ref[...].astype(BF16)) * SCALE
    qpos = qi * tq + lax.broadcasted_iota(jnp.int32, (tq, tk), 0)
    kpos = kb * tk + lax.broadcasted_iota(jnp.int32, (tq, tk), 1)
    if mode == "win":
        d = qpos - kpos
        mask = (d >= 0) & (d < WINDOW)
    else:
        jj = lax.broadcasted_iota(jnp.int32, (nsp, tk), 0)
        kk = lax.broadcasted_iota(jnp.int32, (nsp, tk), 1)
        expand = jnp.where(jj == kb * (tk // SEL_BLOCK) + kk // SEL_BLOCK, 1.0, 0.0).astype(BF16)
        chosen = _dot(sel_ref[...].astype(BF16), expand)
        mask = (chosen > 0.5) & (kpos <= qpos)
    mask4 = jnp.concatenate([mask] * NSA_GROUP, axis=0)
    _flash_step(s, mask4, v_ref[...].astype(BF16), m_ref, l_ref, acc_ref)

    @pl.when(last_tab[p] == 1)
    def _():
        o = _flash_result(l_ref, acc_ref)
        for h in range(NSA_GROUP):
            o_ref[:, h * HEAD_DIM:(h + 1) * HEAD_DIM] = o[h * tq:(h + 1) * tq]


def flash_prompt(u_head, batch, seq, mode, sel=None, tq=256, tk=None):
    tq = min(tq, seq)
    tk = min(tk or (256 if mode == "win" else 512), seq)
    nq, nk = seq // tq, seq // tk
    pairs = []
    for qi in range(nq):
        if mode == "win":
            lo = max(0, (qi * tq - WINDOW + 1) // tk)
        else:
            lo = 0
        hi = (qi * tq + tq - 1) // tk
        kbs = list(range(lo, hi + 1))
        pairs += [(qi, kb, int(kb == kbs[0]), int(kb == kbs[-1])) for kb in kbs]
    tabs = [jnp.asarray([p[i] for p in pairs], jnp.int32) for i in range(4)]
    off = OFF_WIN if mode == "win" else OFF_SEL
    ck = off // LANES
    cv = ck + NSA_KV_HEADS
    gw = NSA_GROUP * HEAD_DIM
    in_specs = [
        pl.BlockSpec((tq, gw), lambda b, g, p, qt, kt, ft, lt: (b * nq + qt[p], g)),
        pl.BlockSpec((tk, LANES), lambda b, g, p, qt, kt, ft, lt: (b * nk + kt[p], ck + g)),
        pl.BlockSpec((tk, LANES), lambda b, g, p, qt, kt, ft, lt: (b * nk + kt[p], cv + g)),
    ]
    args = [u_head, u_head, u_head]
    nsp = 0
    if mode == "sel":
        nsp = sel.shape[-1]
        in_specs.append(pl.BlockSpec((None, None, tq, nsp), lambda b, g, p, qt, kt, ft, lt: (b, g, qt[p], 0)))
        args.append(sel)
    rows = NSA_GROUP * tq
    grid_spec = pltpu.PrefetchScalarGridSpec(
        num_scalar_prefetch=4,
        grid=(batch, NSA_KV_HEADS, len(pairs)),
        in_specs=in_specs,
        out_specs=pl.BlockSpec((tq, gw), lambda b, g, p, qt, kt, ft, lt: (b * nq + qt[p], g)),
        scratch_shapes=[pltpu.VMEM((rows, 1), F32), pltpu.VMEM((rows, 1), F32), pltpu.VMEM((rows, HEAD_DIM), F32)],
    )
    return pl.pallas_call(
        functools.partial(_flash_prompt_kernel, tq=tq, tk=tk, mode=mode, nsp=nsp),
        grid_spec=grid_spec,
        out_shape=jax.ShapeDtypeStruct((batch * seq, NSA_W), F32),
        compiler_params=_cp(3),
        name="flash_" + mode,
    )(*tabs, *args)


def _nsa_combine_kernel(gate_ref, a_ref, b_ref, c_ref, o_ref):
    gate = _sigmoid(gate_ref[...])
    for h in range(NSA_HEADS):
        sl = slice(h * HEAD_DIM, (h + 1) * HEAD_DIM)
        out = (gate[:, h:h + 1] * a_ref[:, sl] + gate[:, NSA_HEADS + h:NSA_HEADS + h + 1] * b_ref[:, sl]
               + gate[:, 2 * NSA_HEADS + h:2 * NSA_HEADS + h + 1] * c_ref[:, sl])
        o_ref[:, sl] = out.astype(o_ref.dtype)


def nsa_combine(u_head, row0, o_cmp, o_sel, o_win, tm=256):
    n = o_cmp.shape[0]
    tm = min(tm, n)
    rb0 = row0 // tm
    assert row0 % tm == 0 and n % tm == 0
    spec = pl.BlockSpec((tm, NSA_W), lambda i: (i, 0))
    return pl.pallas_call(
        _nsa_combine_kernel,
        grid=(n // tm,),
        in_specs=[pl.BlockSpec((tm, LANES), lambda i: (rb0 + i, OFF_NSA_GATE // LANES)), spec, spec, spec],
        out_specs=spec,
        out_shape=jax.ShapeDtypeStruct((n, NSA_W), F32),
        compiler_params=_cp(1),
        name="nsa_combine",
    )(u_head, o_cmp, o_sel, o_win)


def _page_rows(page_ref, first, n_rows, stride):
    return page_ref[pl.ds(first, n_rows, stride=stride), :]


def _pad_rows(x, n):
    if x.shape[0] >= n:
        return x
    return jnp.concatenate([x, jnp.zeros((n - x.shape[0], x.shape[1]), x.dtype)], axis=0)


def _cmp1_sample_kernel(pt_ref, *refs, pps, page):
    page_refs = refs[:pps]
    w_ref, pe_ref, o_ref, r_ref = refs[pps:]
    segs = page // CMP_STRIDE
    stride = CMP_STRIDE * 2 * NSA_KV_HEADS
    for kv in range(2):
        for g in range(NSA_KV_HEADS):
            for k in range(pps):
                for l in range(CMP_STRIDE):
                    r_ref[k * segs:(k + 1) * segs, l * LANES:(l + 1) * LANES] = _page_rows(
                        page_refs[k], l * 2 * NSA_KV_HEADS + kv * NSA_KV_HEADS + g, segs, stride)
            o_ref[kv, g] = _segment_products(r_ref[...], w_ref[kv], pe_ref[kv])


def compress_segments_sample(pool, layer, pt_flat, batch, n_pages, wcat, pecat, pps=16):
    page = pool.shape[2] // (2 * NSA_KV_HEADS)
    pps = min(pps, n_pages)
    assert n_pages % pps == 0 and page % CMP_STRIDE == 0
    segs = page // CMP_STRIDE
    in_specs = [pl.BlockSpec((None, None, pool.shape[2], LANES),
                             lambda b, s, pt, k=k: (layer, pt[b * n_pages + s * pps + k], 0, 0)) for k in range(pps)]
    in_specs += [pl.BlockSpec((2, SEG_W, 2 * LANES), lambda b, s, pt: (0, 0, 0)),
                 pl.BlockSpec((2, SUBLANES, SEG_W), lambda b, s, pt: (0, 0, 0))]
    grid_spec = pltpu.PrefetchScalarGridSpec(
        num_scalar_prefetch=1,
        grid=(batch, n_pages // pps),
        in_specs=in_specs,
        out_specs=pl.BlockSpec((None, 2, NSA_KV_HEADS, pps * segs, 2 * LANES), lambda b, s, pt: (b, 0, 0, s, 0)),
        scratch_shapes=[pltpu.VMEM((pps * segs, SEG_W), F32)],
    )
    return pl.pallas_call(
        functools.partial(_cmp1_sample_kernel, pps=pps, page=page),
        grid_spec=grid_spec,
        out_shape=jax.ShapeDtypeStruct((batch, 2, NSA_KV_HEADS, n_pages * segs, 2 * LANES), F32),
        compiler_params=_cp(2),
        name="cmp_segments_sample",
    )(pt_flat, *([pool] * pps), wcat, pecat)


def _sel_sample_kernel(pt_ref, *refs, pps, page, ds, past, nsp):
    q_ref, new_ref, sel_ref = refs[:3]
    page_refs = refs[3:3 + pps]
    o_ref, m_ref, l_ref, acc_ref = refs[3 + pps:]
    s = pl.program_id(1)
    rows = NSA_GROUP * ds
    keys = pps * page
    stride = 2 * NSA_KV_HEADS

    @pl.when(s == 0)
    def _():
        _flash_init(m_ref, l_ref, acc_ref)

    for g in range(NSA_KV_HEADS):
        q4 = _stack_heads(q_ref[:, g * NSA_GROUP * HEAD_DIM:(g + 1) * NSA_GROUP * HEAD_DIM]).astype(BF16)
        sel4 = jnp.concatenate([sel_ref[g]] * NSA_GROUP, axis=0)
        k_all = jnp.concatenate([_page_rows(r, g, page, stride) for r in page_refs], axis=0).astype(BF16)
        v_all = jnp.concatenate([_page_rows(r, NSA_KV_HEADS + g, page, stride) for r in page_refs],
                                axis=0).astype(BF16)
        sc = _dot_nt(q4, k_all) * SCALE
        jj = lax.broadcasted_iota(jnp.int32, (nsp, keys), 0)
        kk = lax.broadcasted_iota(jnp.int32, (nsp, keys), 1)
        expand = jnp.where(jj == s * (keys // SEL_BLOCK) + kk // SEL_BLOCK, 1.0, 0.0).astype(BF16)
        chosen = _dot(sel4.astype(BF16), expand)
        _flash_step(sc, chosen > 0.5, v_all, m_ref.at[g], l_ref.at[g], acc_ref.at[g])

        @pl.when(s == pl.num_programs(1) - 1)
        def _():
            k_new = _pad_rows(new_ref[:, g * HEAD_DIM:(g + 1) * HEAD_DIM], LANES).astype(BF16)
            v_new = _pad_rows(new_ref[:, (NSA_KV_HEADS + g) * HEAD_DIM:(NSA_KV_HEADS + g + 1) * HEAD_DIM],
                              LANES).astype(BF16)
            sn = _dot_nt(q4, k_new) * SCALE
            blk = past // SEL_BLOCK
            j = lax.broadcasted_iota(jnp.int32, (rows, LANES), 1)
            t = lax.broadcasted_iota(jnp.int32, (rows, LANES), 0) % ds
            mask = (sel4[:, blk:blk + 1] > 0.5) & (j <= t)
            _flash_step(sn, mask, v_new, m_ref.at[g], l_ref.at[g], acc_ref.at[g])
            o = _flash_result(l_ref.at[g], acc_ref.at[g])
            for h in range(NSA_GROUP):
                c0 = (g * NSA_GROUP + h) * HEAD_DIM
                o_ref[:, c0:c0 + HEAD_DIM] = o[h * ds:(h + 1) * ds]


def sel_sample(u_head, row0, pool, layer, pt_flat, sel, batch, ds, n_pages, pps=8):
    page = pool.shape[2] // (2 * NSA_KV_HEADS)
    pps = min(pps, n_pages)
    past = n_pages * page
    nsp = sel.shape[-1]
    rows = NSA_GROUP * ds
    assert n_pages % pps == 0 and row0 % ds == 0 and past % SEL_BLOCK == 0 and ds <= SEL_BLOCK
    assert page % SEL_BLOCK == 0
    in_specs = [pl.BlockSpec((ds, NSA_W), lambda b, s, pt: (row0 // ds + b, 0)),
                pl.BlockSpec((ds, KV_W), lambda b, s, pt: (row0 // ds + b, OFF_SEL // KV_W)),
                pl.BlockSpec((None, NSA_KV_HEADS, ds, nsp), lambda b, s, pt: (b, 0, 0, 0))]
    in_specs += [pl.BlockSpec((None, None, pool.shape[2], LANES),
                              lambda b, s, pt, k=k: (layer, pt[b * n_pages + s * pps + k], 0, 0)) for k in range(pps)]
    grid_spec = pltpu.PrefetchScalarGridSpec(
        num_scalar_prefetch=1,
        grid=(batch, n_pages // pps),
        in_specs=in_specs,
        out_specs=pl.BlockSpec((ds, NSA_W), lambda b, s, pt: (b, 0)),
        scratch_shapes=[pltpu.VMEM((NSA_KV_HEADS, rows, 1), F32), pltpu.VMEM((NSA_KV_HEADS, rows, 1), F32),
                        pltpu.VMEM((NSA_KV_HEADS, rows, HEAD_DIM), F32)],
    )
    return pl.pallas_call(
        functools.partial(_sel_sample_kernel, pps=pps, page=page, ds=ds, past=past, nsp=nsp),
        grid_spec=grid_spec,
        out_shape=jax.ShapeDtypeStruct((batch * ds, NSA_W), F32),
        compiler_params=_cp(2),
        name="sel_sample",
    )(pt_flat, u_head, u_head, sel, *([pool] * pps))


def _win_sample_kernel(q_ref, new_ref, buf_ref, o_ref, m_ref, l_ref, acc_ref, *, ds, nbuf):
    rows = NSA_GROUP * ds
    stride = 2 * NSA_KV_HEADS
    keys = nbuf + LANES
    i = lax.broadcasted_iota(jnp.int32, (rows, keys), 1)
    t = lax.broadcasted_iota(jnp.int32, (rows, keys), 0) % ds
    d = jnp.where(i < nbuf, t + nbuf - i, t - (i - nbuf))
    mask = (d >= 0) & (d < WINDOW) & (i < nbuf + ds)
    for g in range(NSA_KV_HEADS):
        _flash_init(m_ref, l_ref, acc_ref)
        q4 = _stack_heads(q_ref[:, g * NSA_GROUP * HEAD_DIM:(g + 1) * NSA_GROUP * HEAD_DIM]).astype(BF16)
        k_new = _pad_rows(new_ref[:, g * HEAD_DIM:(g + 1) * HEAD_DIM], LANES)
        v_new = _pad_rows(new_ref[:, (NSA_KV_HEADS + g) * HEAD_DIM:(NSA_KV_HEADS + g + 1) * HEAD_DIM], LANES)
        k_all = jnp.concatenate([_page_rows(buf_ref, g, nbuf, stride), k_new], axis=0).astype(BF16)
        v_all = jnp.concatenate([_page_rows(buf_ref, NSA_KV_HEADS + g, nbuf, stride), v_new], axis=0).astype(BF16)
        _flash_step(_dot_nt(q4, k_all) * SCALE, mask, v_all, m_ref, l_ref, acc_ref)
        o = _flash_result(l_ref, acc_ref)
        for h in range(NSA_GROUP):
            c0 = (g * NSA_GROUP + h) * HEAD_DIM
            o_ref[:, c0:c0 + HEAD_DIM] = o[h * ds:(h + 1) * ds]


def win_sample(u_head, row0, buf, layer, batch, ds):
    nbuf = buf.shape[2] // (2 * NSA_KV_HEADS)
    rows = NSA_GROUP * ds
    return pl.pallas_call(
        functools.partial(_win_sample_kernel, ds=ds, nbuf=nbuf),
        grid=(batch,),
        in_specs=[pl.BlockSpec((ds, NSA_W), lambda b: (row0 // ds + b, 0)),
                  pl.BlockSpec((ds, KV_W), lambda b: (row0 // ds + b, OFF_WIN // KV_W)),
                  pl.BlockSpec((None, None, buf.shape[2], LANES), lambda b: (layer, b, 0, 0))],
        out_specs=pl.BlockSpec((ds, NSA_W), lambda b: (b, 0)),
        out_shape=jax.ShapeDtypeStruct((batch * ds, NSA_W), F32),
        scratch_shapes=[pltpu.VMEM((rows, 1), F32), pltpu.VMEM((rows, 1), F32), pltpu.VMEM((rows, HEAD_DIM), F32)],
        compiler_params=_cp(1),
        name="win_sample",
    )(u_head, u_head, buf)


SB_ROWS = 16


def _sb_sample_kernel(pt_ref, *refs, pps, page, ds):
    q_ref, kn_ref, vn_ref = refs[:3]
    page_refs = refs[3:3 + pps]
    o_ref, acc_ref, carry_ref = refs[3 + pps:]
    s = pl.program_id(1)
    stride = 2 * SB_HEADS
    umat = _suffix_matrix()
    q_heads = [_pad_rows(q_ref[:, h * HEAD_DIM:(h + 1) * HEAD_DIM], SB_ROWS).astype(BF16) for h in range(SB_HEADS)]
    rows = SB_HEADS * SB_ROWS

    def accumulate(k_heads, v_heads, mask):
        z = jnp.concatenate([_dot_nt(q_heads[h], k_heads[h]) for h in range(SB_HEADS)], axis=0) * SCALE
        a, carry = _sb_weights(z, mask, carry_ref[...], umat)
        carry_ref[...] = carry
        a = a.astype(BF16)
        for h in range(SB_HEADS):
            sl = slice(h * SB_ROWS, (h + 1) * SB_ROWS)
            acc_ref[sl, :] += _dot(a[sl], v_heads[h])

    @pl.when(s == 0)
    def _():
        acc_ref[...] = jnp.zeros_like(acc_ref)
        carry_ref[...] = jnp.zeros_like(carry_ref)
        k_heads = [_pad_rows(kn_ref[:, h * HEAD_DIM:(h + 1) * HEAD_DIM], LANES).astype(BF16) for h in range(SB_HEADS)]
        v_heads = [_pad_rows(vn_ref[:, h * HEAD_DIM:(h + 1) * HEAD_DIM], LANES).astype(BF16) for h in range(SB_HEADS)]
        j = lax.broadcasted_iota(jnp.int32, (rows, LANES), 1)
        t = lax.broadcasted_iota(jnp.int32, (rows, LANES), 0) % SB_ROWS
        accumulate(k_heads, v_heads, j < t)

    k_heads = [jnp.concatenate([_page_rows(r, h, page, stride) for r in page_refs], axis=0).astype(BF16)
               for h in range(SB_HEADS)]
    v_heads = [jnp.concatenate([_page_rows(r, SB_HEADS + h, page, stride) for r in page_refs], axis=0).astype(BF16)
               for h in range(SB_HEADS)]
    accumulate(k_heads, v_heads, jnp.full((rows, pps * page), True))

    @pl.when(s == pl.num_programs(1) - 1)
    def _():
        for h in range(SB_HEADS):
            o_ref[:, h * HEAD_DIM:(h + 1) * HEAD_DIM] = acc_ref[h * SB_ROWS:h * SB_ROWS + ds, :]


def sb_sample(u_rest, row0, pool, layer, pt_flat, batch, ds, n_pages, pps=4):
    page = pool.shape[2] // (2 * SB_HEADS)
    pps = min(pps, n_pages)
    ns = n_pages // pps
    assert n_pages % pps == 0 and row0 % ds == 0 and ds <= SB_ROWS
    rb = row0 // ds
    in_specs = [pl.BlockSpec((ds, SB_W), lambda b, s, pt: (rb + b, R_SB_Q // SB_W)),
                pl.BlockSpec((ds, SB_W), lambda b, s, pt: (rb + b, R_SB_K // SB_W)),
                pl.BlockSpec((ds, SB_W), lambda b, s, pt: (rb + b, R_SB_V // SB_W))]
    in_specs += [pl.BlockSpec((None, None, pool.shape[2], LANES),
                              lambda b, s, pt, k=k: (layer, pt[b * n_pages + (ns - 1 - s) * pps + k], 0, 0))
                 for k in range(pps)]
    grid_spec = pltpu.PrefetchScalarGridSpec(
        num_scalar_prefetch=1,
        grid=(batch, ns),
        in_specs=in_specs,
        out_specs=pl.BlockSpec((ds, SB_W), lambda b, s, pt: (b, 0)),
        scratch_shapes=[pltpu.VMEM((SB_HEADS * SB_ROWS, HEAD_DIM), F32), pltpu.VMEM((SB_HEADS * SB_ROWS, LANES), F32)],
    )
    return pl.pallas_call(
        functools.partial(_sb_sample_kernel, pps=pps, page=page, ds=ds),
        grid_spec=grid_spec,
        out_shape=jax.ShapeDtypeStruct((batch * ds, SB_W), F32),
        compiler_params=_cp(2),
        name="sb_sample",
    )(pt_flat, u_rest, u_rest, u_rest, *([pool] * pps))


def _router_kernel(x_ref, g_ref, w_ref, b_ref, h_ref, o_ref):
    x = x_ref[...]
    h = x * lax.rsqrt(jnp.mean(x * x, axis=-1, keepdims=True) + NORM_EPS) * g_ref[...]
    h_ref[...] = h
    hh, hm, hl = _split3(h)
    wh, wm, wl = _split3(w_ref[...])
    logits = (_dot(hh, wh) + _dot(hh, wm) + _dot(hm, wh) + _dot(hh, wl) + _dot(hl, wh) + _dot(hm, wm)) + b_ref[...]
    lane = lax.broadcasted_iota(jnp.int32, logits.shape, 1)
    logits = jnp.where(lane < N_EXPERTS, logits, NEG_BIG)
    m1 = jnp.max(logits, axis=1, keepdims=True)
    i1 = jnp.min(jnp.where(logits == m1, lane, LANES), axis=1, keepdims=True)
    rest = jnp.where(lane == i1, NEG_BIG, logits)
    m2 = jnp.max(rest, axis=1, keepdims=True)
    i2 = jnp.min(jnp.where(rest == m2, lane, LANES), axis=1, keepdims=True)
    e = jnp.exp(m2 - m1)
    g1 = 1.0 / (1.0 + e)
    g2 = e / (1.0 + e)
    o_ref[...] = jnp.where(lane == 0, i1.astype(F32), jnp.where(lane == 1, i2.astype(F32),
                           jnp.where(lane == 2, g1, jnp.where(lane == 3, g2, 0.0))))


def moe_router(x, g, rw, rb, tm):
    m, d = x.shape
    w = jnp.zeros((d, LANES), F32).at[:, :N_EXPERTS].set(rw)
    b = jnp.zeros((1, LANES), F32).at[0, :N_EXPERTS].set(rb.astype(F32))
    return pl.pallas_call(
        _router_kernel,
        grid=(m // tm,),
        in_specs=[pl.BlockSpec((tm, d), lambda i: (i, 0)), pl.BlockSpec((1, d), lambda i: (0, 0)),
                  pl.BlockSpec((d, LANES), lambda i: (0, 0)), pl.BlockSpec((1, LANES), lambda i: (0, 0))],
        out_specs=[pl.BlockSpec((tm, d), lambda i: (i, 0)), pl.BlockSpec((tm, LANES), lambda i: (i, 0))],
        out_shape=[jax.ShapeDtypeStruct((m, d), F32), jax.ShapeDtypeStruct((m, LANES), F32)],
        compiler_params=_cp(1),
        name="moe_router",
    )(x, g.reshape(1, d), w, b)


def _row_copy(src_hbm, row, dst_ref, r, sem):
    return pltpu.make_async_copy(src_hbm.at[pl.ds(row, 1), :], dst_ref.at[pl.ds(r, 1), :], sem)


def _gather_kernel(tok_ref, h_hbm, o_ref, buf_ref, sem):
    blk = pl.program_id(0)
    n = buf_ref.shape[0]

    def issue(r, c):
        _row_copy(h_hbm, tok_ref[blk * n + r], buf_ref, r, sem).start()
        return c

    lax.fori_loop(0, n, issue, 0)

    def wait(r, c):
        _row_copy(h_hbm, 0, buf_ref, r, sem).wait()
        return c

    lax.fori_loop(0, n, wait, 0)
    o_ref[...] = buf_ref[...].astype(o_ref.dtype)


def moe_gather(h, tok_buf):
    p = tok_buf.shape[0]
    d = h.shape[1]
    grid_spec = pltpu.PrefetchScalarGridSpec(
        num_scalar_prefetch=1,
        grid=(p // MOE_BLOCK,),
        in_specs=[pl.BlockSpec(memory_space=pl.ANY)],
        out_specs=pl.BlockSpec((MOE_BLOCK, d), lambda i, tok: (i, 0)),
        scratch_shapes=[pltpu.VMEM((MOE_BLOCK, d), F32), pltpu.SemaphoreType.DMA],
    )
    return pl.pallas_call(
        _gather_kernel,
        grid_spec=grid_spec,
        out_shape=jax.ShapeDtypeStruct((p, d), BF16),
        compiler_params=_cp(1),
        name="moe_gather",
    )(tok_buf, h)


def _moe_mm_kernel(be_ref, *refs, n_w, mode):
    a_ref = refs[0]
    w_refs = refs[1:1 + n_w]
    o_ref = refs[1 + n_w]
    wbf_refs = refs[2 + n_w:]
    blk = pl.program_id(1)
    changed = jnp.logical_or(blk == 0, be_ref[blk] != be_ref[jnp.maximum(blk - 1, 0)])

    @pl.when(changed)
    def _():
        for w_ref, wbf_ref in zip(w_refs, wbf_refs):
            wbf_ref[...] = w_ref[...].astype(BF16)

    prods = [_dot(a_ref[...], r[...]) for r in wbf_refs]
    o_ref[...] = _mm_epilogue(mode, prods, ()).astype(o_ref.dtype)


def moe_matmul(a, w_list, moe_index, blk_e, mode, out_dtype, tn=512):
    p, k = a.shape
    n_out = w_list[0].shape[-1]
    tn = _tile(n_out, tn)
    grid_spec = pltpu.PrefetchScalarGridSpec(
        num_scalar_prefetch=1,
        grid=(n_out // tn, p // MOE_BLOCK),
        in_specs=[pl.BlockSpec((MOE_BLOCK, k), lambda j, i, be: (i, 0))]
        + [pl.BlockSpec((None, None, k, tn), lambda j, i, be: (moe_index, be[i], 0, j)) for _ in w_list],
        out_specs=pl.BlockSpec((MOE_BLOCK, tn), lambda j, i, be: (i, j)),
        scratch_shapes=[pltpu.VMEM((k, tn), BF16) for _ in w_list],
    )
    return pl.pallas_call(
        functools.partial(_moe_mm_kernel, n_w=len(w_list), mode=mode),
        grid_spec=grid_spec,
        out_shape=jax.ShapeDtypeStruct((p, n_out), out_dtype),
        compiler_params=_cp(2),
        name="moe_mm_" + mode,
    )(blk_e, a, *w_list)


def _moe_combine_kernel(pos_ref, y_hbm, x_ref, gate_ref, o_ref, buf_ref, sem, *, tm):
    i = pl.program_id(0)

    def issue(r, c):
        for k in range(TOP_K):
            _row_copy(y_hbm, pos_ref[(i * tm + r) * TOP_K + k], buf_ref.at[k], r, sem).start()
        return c

    lax.fori_loop(0, tm, issue, 0)

    def wait(r, c):
        for k in range(TOP_K):
            _row_copy(y_hbm, 0, buf_ref.at[k], r, sem).wait()
        return c

    lax.fori_loop(0, tm, wait, 0)
    gate = gate_ref[...]
    out = x_ref[...]
    y = gate[:, 2:3] * buf_ref[0]
    for k in range(1, TOP_K):
        y = y + gate[:, 2 + k:3 + k] * buf_ref[k]
    o_ref[...] = out + y


def moe_combine(x, yb, pos, gates, tm=256):
    m, d = x.shape
    grid_spec = pltpu.PrefetchScalarGridSpec(
        num_scalar_prefetch=1,
        grid=(m // tm,),
        in_specs=[pl.BlockSpec(memory_space=pl.ANY),
                  pl.BlockSpec((tm, d), lambda i, pos: (i, 0)),
                  pl.BlockSpec((tm, LANES), lambda i, pos: (i, 0))],
        out_specs=pl.BlockSpec((tm, d), lambda i, pos: (i, 0)),
        scratch_shapes=[pltpu.VMEM((TOP_K, tm, d), F32), pltpu.SemaphoreType.DMA],
    )
    return pl.pallas_call(
        functools.partial(_moe_combine_kernel, tm=tm),
        grid_spec=grid_spec,
        out_shape=jax.ShapeDtypeStruct((m, d), F32),
        compiler_params=_cp(1),
        name="moe_combine",
    )(pos, yb, x, gates)


def moe_layer(x, g, rw, rb, w1, w3, w2, moe_index, n_tok, tm):
    m, d = x.shape
    h, route = moe_router(x, g, rw, rb, tm)
    top_i = route[:n_tok, :TOP_K].astype(jnp.int32)
    a = n_tok * TOP_K
    e_flat = top_i.reshape(a)
    order = jnp.argsort(e_flat)
    e_s = e_flat[order]
    tok_s = (order // TOP_K).astype(jnp.int32)
    counts = jnp.bincount(e_flat, length=N_EXPERTS)
    start = jnp.cumsum(counts) - counts
    padded = (counts + MOE_BLOCK - 1) // MOE_BLOCK * MOE_BLOCK
    pend = jnp.cumsum(padded)
    pstart = pend - padded
    dest = (pstart[e_s] + jnp.arange(a) - start[e_s]).astype(jnp.int32)
    nb = -(-a // MOE_BLOCK) + N_EXPERTS
    p = nb * MOE_BLOCK
    tok_buf = jnp.full((p,), n_tok, jnp.int32).at[dest].set(tok_s)
    blk_e = jnp.minimum(jnp.searchsorted(pend, jnp.arange(nb) * MOE_BLOCK, side="right"),
                        N_EXPERTS - 1).astype(jnp.int32)
    pos = jnp.zeros((m * TOP_K,), jnp.int32).at[order].set(dest)
    row_ok = (jnp.arange(m) < n_tok)[:, None]
    gates = jnp.where(row_ok, route, 0.0)
    xs = moe_gather(h, tok_buf)
    act = moe_matmul(xs, [w1, w3], moe_index, blk_e, "swiglu", BF16)
    yb = moe_matmul(act, [w2], moe_index, blk_e, "plain", F32)
    return moe_combine(x, yb, pos, gates)


TM = 512


def _slab(parts, m, dtype):
    rows = sum(p.shape[0] for p in parts)
    parts = [p.astype(dtype) for p in parts]
    return jnp.concatenate(parts + [jnp.zeros((m - rows, parts[0].shape[1]), dtype)], axis=0)


def kernel(x_prompt, x_sample, cache_nsa_cmp_kv, cache_nsa_sel_kv, cache_sb_kv, cache_nsa_win_kv, state_hgrn,
           page_table, attn_norm, w_in, cmp_pe_k, cmp_w1_k, cmp_w2_k, cmp_pe_v, cmp_w1_v, cmp_w2_v, hg_lb_logits,
           hg_norm, w_br_nsa, w_br_sb, w_br_hg, w_out, ffn_norm, ffn_w1, ffn_w3, ffn_w2, router_w, router_b,
           moe_w1, moe_w3, moe_w2, final_norm):
    bsz, seq, d = x_prompt.shape
    db, ds, _ = x_sample.shape
    depth = attn_norm.shape[0]
    n_p, n_s = bsz * seq, db * ds
    n_tok = n_p + n_s
    m = -(-(n_tok + 1) // TM) * TM
    n_pool, page = cache_nsa_cmp_kv.shape[1:3]
    n_pages = page_table.shape[1]
    past = n_pages * page
    assert past % CMP_STRIDE == 0 and ds < CMP_STRIDE and seq % CMP_STRIDE == 0

    x = _slab([x_prompt.reshape(n_p, d), x_sample.reshape(n_s, d)], m, F32)
    pt_flat = page_table.reshape(-1).astype(jnp.int32)
    cmp_pool = cache_nsa_cmp_kv.reshape(depth, n_pool, page * 2 * NSA_KV_HEADS, HEAD_DIM)
    sel_pool = cache_nsa_sel_kv.reshape(depth, n_pool, page * 2 * NSA_KV_HEADS, HEAD_DIM)
    sb_pool = cache_sb_kv.reshape(depth, n_pool, page * 2 * SB_HEADS, HEAD_DIM)
    nbuf = cache_nsa_win_kv.shape[2]
    win_buf = cache_nsa_win_kv.reshape(depth, db, nbuf * 2 * NSA_KV_HEADS, HEAD_DIM)
    w_rest = w_in[:, :, REST_START:].astype(BF16)
    rest_cols = w_rest.shape[-1]
    lb_all = jnp.cumsum(jax.nn.softmax(hg_lb_logits.astype(F32), axis=0), axis=0)
    kvs = (2, NSA_KV_HEADS, HEAD_DIM)

    states = []
    for l in range(depth):
        h = rmsnorm(x, attn_norm[l], BF16, TM)
        u_head = matmul([h], [w_in], l, 0, HEAD_COLS, "plain", F32, tn=HEAD_COLS // 3)
        u_rest = matmul([h], [w_rest], l, 0, rest_cols, "plain", F32, tn=1024)

        wk, pk = compress_params(cmp_pe_k[l], cmp_w1_k[l])
        wv, pv = compress_params(cmp_pe_v[l], cmp_w1_v[l])
        wcat, pecat = jnp.stack([wk, wv]), jnp.stack([pk, pv])
        w2s = jnp.stack([cmp_w2_k[l], cmp_w2_v[l]])
        lb = (lb_all[l] - lb_all[0]).reshape(HG_HEADS, HEAD_DIM)
        lbp = hgrn_lb_params(lb)

        kvc = compress_finish(compress_segments_prompt(u_head, bsz, seq, wcat, pecat), w2s)
        o_cmp, sel = cmp_select(u_head, 0, bsz, seq, kvc, seq // CMP_STRIDE - 1, -(-seq // SEL_BLOCK), 0)
        o_sel = flash_prompt(u_head, bsz, seq, "sel", sel=sel)
        o_win = flash_prompt(u_head, bsz, seq, "win")
        nsa_p = nsa_combine(u_head, 0, o_cmp, o_sel, o_win)
        sb_p = sb_prompt(u_rest, bsz, seq)
        hg_p, s_p = hgrn(u_rest, 0, bsz, seq, lbp, hg_norm[l], jnp.zeros((bsz, HG_HEADS, HEAD_DIM, HEAD_DIM), F32))

        kvc_s = compress_finish(compress_segments_sample(cmp_pool, l, pt_flat, db, n_pages, wcat, pecat), w2s)
        o_cmp_s, sel_s = cmp_select(u_head, n_p, db, ds, kvc_s, past // CMP_STRIDE - 1,
                                    -(-(past + ds) // SEL_BLOCK), past, tq=ds)
        o_sel_s = sel_sample(u_head, n_p, sel_pool, l, pt_flat, sel_s, db, ds, n_pages)
        o_win_s = win_sample(u_head, n_p, win_buf, l, db, ds)
        nsa_s = nsa_combine(u_head, n_p, o_cmp_s, o_sel_s, o_win_s)
        sb_s = sb_sample(u_rest, n_p, sb_pool, l, pt_flat, db, ds, n_pages)
        hg_s, s_s = hgrn(u_rest, n_p, db, ds, lbp, hg_norm[l], state_hgrn[l])

        o_nsa = _slab([nsa_p, nsa_s], m, BF16)
        o_sb = _slab([sb_p, sb_s], m, BF16)
        o_hg = _slab([hg_p, hg_s], m, BF16)
        merged = matmul([o_nsa, o_sb, o_hg], [w_br_nsa, w_br_sb, w_br_hg], l, 0, d, "merge", BF16,
                        extras=[(u_rest, R_MERGE), (u_rest, R_MERGE + d), (u_rest, R_MERGE + 2 * d)])
        x = matmul([merged], [w_out], l, 0, d, "residual", F32, extras=[(x, 0)])

        i = l // 2
        if l % 2 == 0:
            h2 = rmsnorm(x, ffn_norm[l], BF16, TM)
            act = matmul([h2], [ffn_w1, ffn_w3], i, 0, ffn_w1.shape[-1], "swiglu", BF16)
            x = matmul([act], [ffn_w2], i, 0, d, "residual", F32, extras=[(x, 0)])
        else:
            x = moe_layer(x, ffn_norm[l], router_w[i], router_b[i], moe_w1, moe_w3, moe_w2, i, n_tok, TM)

        def head_cols(off, r0, r1, lead):
            return u_head[r0:r1, off:off + KV_W].reshape(lead + kvs)

        win_p = head_cols(OFF_WIN, 0, n_p, (bsz, seq))[:, seq - min(WINDOW, seq):]
        win_s = jnp.concatenate([cache_nsa_win_kv[l], head_cols(OFF_WIN, n_p, n_tok, (db, ds))], axis=1)[:, ds:]
        states.append((
            head_cols(OFF_CMP, 0, n_p, (bsz, seq)), head_cols(OFF_SEL, 0, n_p, (bsz, seq)), win_p,
            u_rest[:n_p, R_SB_K:R_SB_K + 2 * SB_W].reshape(bsz, seq, 2, SB_HEADS, HEAD_DIM), s_p,
            head_cols(OFF_CMP, n_p, n_tok, (db, ds)), head_cols(OFF_SEL, n_p, n_tok, (db, ds)), win_s,
            u_rest[n_p:n_tok, R_SB_K:R_SB_K + 2 * SB_W].reshape(db, ds, 2, SB_HEADS, HEAD_DIM), s_s))

    y = rmsnorm(x, final_norm, F32, TM)
    stacked = [jnp.stack([st[i] for st in states]) for i in range(10)]
    return (y[:n_p].reshape(bsz, seq, d), y[n_p:n_tok].reshape(db, ds, d), *stacked)
```

```python
import functools

import jax
import jax.numpy as jnp
import numpy as np
from jax import lax
from jax.experimental import pallas as pl
from jax.experimental.pallas import tpu as pltpu

F32 = jnp.float32
BF16 = jnp.bfloat16

HEAD_DIM = 128
SCALE = HEAD_DIM ** -0.5
NSA_HEADS = 8
NSA_KV_HEADS = 2
NSA_GROUP = NSA_HEADS // NSA_KV_HEADS
CMP_LEN = 32
CMP_STRIDE = 16
SEL_BLOCK = 64
SEL_TOPK = 16
WINDOW = 512
SB_HEADS = 8
HG_HEADS = 8
HG_CHUNK = 64
HG_SUB = 16
N_EXPERTS = 8
TOP_K = 2
MOE_BLOCK = 256
NORM_EPS = 1e-6
NEG_BIG = -1e30

LANES = 128
SUBLANES = 8
VMEM_LIMIT = 56 * 1024 * 1024

NSA_W = NSA_HEADS * HEAD_DIM
KV_W = 2 * NSA_KV_HEADS * HEAD_DIM
OFF_CMP = NSA_W
OFF_SEL = OFF_CMP + KV_W
OFF_WIN = OFF_SEL + KV_W
OFF_NSA_GATE = OFF_WIN + KV_W
N_GATE = 3 * NSA_HEADS
HEAD_COLS = OFF_NSA_GATE + LANES
REST_START = OFF_NSA_GATE + N_GATE
SB_W = SB_HEADS * HEAD_DIM
HG_W = HG_HEADS * HEAD_DIM
R_SB_Q = 0
R_SB_K = R_SB_Q + SB_W
R_SB_V = R_SB_K + SB_W
R_HG_Q = R_SB_V + SB_W
R_HG_F = R_HG_Q + HG_W
R_HG_I = R_HG_F + HG_W
R_HG_G = R_HG_I + HG_W
R_MERGE = R_HG_G + HG_W


def _cp(n_axes, vmem=VMEM_LIMIT):
    return pltpu.CompilerParams(dimension_semantics=("arbitrary",) * n_axes, vmem_limit_bytes=vmem)


def _tile(n, pref, quantum=LANES):
    if n <= pref:
        return n
    t = (pref // quantum) * quantum
    while t > quantum and n % t:
        t -= quantum
    assert n % t == 0, (n, pref)
    return t


def _dot(a, b):
    return jnp.dot(a, b, preferred_element_type=F32)


def _dot_nt(a, b):
    return lax.dot_general(a, b, (((1,), (1,)), ((), ())), preferred_element_type=F32)


def _dot_tn(a, b):
    return lax.dot_general(a, b, (((0,), (0,)), ((), ())), preferred_element_type=F32)


def _split3(x):
    hi = x.astype(BF16)
    r = x - hi.astype(F32)
    mid = r.astype(BF16)
    lo = (r - mid.astype(F32)).astype(BF16)
    return hi, mid, lo


def _split2(x):
    hi = x.astype(BF16)
    return hi, (x - hi.astype(F32)).astype(BF16)


def _sigmoid(x):
    return 1.0 / (1.0 + jnp.exp(-x))


def _silu(x):
    return x * _sigmoid(x)


def _log_sigmoid(x):
    return jnp.minimum(x, 0.0) - jnp.log1p(jnp.exp(-jnp.abs(x)))


def _rmsnorm_kernel(x_ref, g_ref, o_ref):
    x = x_ref[...]
    y = x * lax.rsqrt(jnp.mean(x * x, axis=-1, keepdims=True) + NORM_EPS)
    o_ref[...] = (y * g_ref[...]).astype(o_ref.dtype)


def rmsnorm(x, g, out_dtype, tm):
    m, d = x.shape
    return pl.pallas_call(
        _rmsnorm_kernel,
        grid=(m // tm,),
        in_specs=[pl.BlockSpec((tm, d), lambda i: (i, 0)), pl.BlockSpec((1, d), lambda i: (0, 0))],
        out_specs=pl.BlockSpec((tm, d), lambda i: (i, 0)),
        out_shape=jax.ShapeDtypeStruct((m, d), out_dtype),
        compiler_params=_cp(1),
        name="rmsnorm",
    )(x, g.reshape(1, d))


def _mm_epilogue(mode, prods, x_refs):
    if mode == "plain":
        return prods[0]
    if mode == "merge":
        out = _sigmoid(x_refs[0][...]) * prods[0]
        for x_ref, p in zip(x_refs[1:], prods[1:]):
            out = out + _sigmoid(x_ref[...]) * p
        return out
    if mode == "residual":
        return x_refs[0][...] + prods[0]
    return _silu(prods[0]) * prods[1]


def _mm_kernel(*refs, n_a, n_w, n_extra, mode, cast):
    a_refs = refs[:n_a]
    w_refs = refs[n_a:n_a + n_w]
    x_refs = refs[n_a + n_w:n_a + n_w + n_extra]
    o_ref = refs[n_a + n_w + n_extra]
    wbf_refs = refs[n_a + n_w + n_extra + 1:]

    if cast:
        @pl.when(pl.program_id(1) == 0)
        def _():
            for w_ref, wbf_ref in zip(w_refs, wbf_refs):
                wbf_ref[...] = w_ref[...].astype(BF16)
        ws = [r[...] for r in wbf_refs]
    else:
        ws = [r[...] for r in w_refs]

    prods = [_dot(a_refs[min(i, n_a - 1)][...], w) for i, w in enumerate(ws)]
    o_ref[...] = _mm_epilogue(mode, prods, x_refs).astype(o_ref.dtype)


def matmul(a_list, w_list, w_index, col0, n_out, mode, out_dtype, extras=(), tm=512, tn=512):
    m = a_list[0].shape[0]
    tn = _tile(n_out, tn)
    assert col0 % tn == 0 and m % tm == 0
    cast = w_list[0].dtype != BF16
    in_specs, args, scratch = [], [], []
    for a in a_list:
        in_specs.append(pl.BlockSpec((tm, a.shape[1]), lambda j, i: (i, 0)))
        args.append(a)
    for w in w_list:
        k = w.shape[-2]
        if w.ndim == 3:
            in_specs.append(pl.BlockSpec((None, k, tn), lambda j, i: (w_index, 0, j + col0 // tn)))
        else:
            in_specs.append(pl.BlockSpec((k, tn), lambda j, i: (0, j + col0 // tn)))
        args.append(w)
        if cast:
            scratch.append(pltpu.VMEM((k, tn), BF16))
    for x, off in extras:
        assert off % tn == 0
        in_specs.append(pl.BlockSpec((tm, tn), lambda j, i, off=off: (i, j + off // tn)))
        args.append(x)
    return pl.pallas_call(
        functools.partial(_mm_kernel, n_a=len(a_list), n_w=len(w_list), n_extra=len(extras), mode=mode, cast=cast),
        grid=(n_out // tn, m // tm),
        in_specs=in_specs,
        out_specs=pl.BlockSpec((tm, tn), lambda j, i: (i, j)),
        out_shape=jax.ShapeDtypeStruct((m, n_out), out_dtype),
        scratch_shapes=scratch,
        compiler_params=_cp(2),
        name="mm_" + mode,
    )(*args)


def _suffix_matrix():
    j = lax.broadcasted_iota(jnp.int32, (LANES, 2 * LANES), 0)
    s = lax.broadcasted_iota(jnp.int32, (LANES, 2 * LANES), 1)
    return jnp.where((j > s) | (s >= LANES), 1.0, 0.0).astype(BF16)


def _sb_weights(z, mask, carry, umat):
    t = jnp.log(1.0 + jnp.exp(-jnp.abs(z)))
    ls_pos = jnp.minimum(z, 0.0) - t
    c = ls_pos - z
    if mask is not None:
        c = jnp.where(mask, c, 0.0)
    n_sub = z.shape[1] // LANES
    pieces = [None] * n_sub
    for sb in reversed(range(n_sub)):
        sl = slice(sb * LANES, (sb + 1) * LANES)
        hi, lo = _split2(c[:, sl])
        r = _dot(hi, umat) + _dot(lo, umat)
        w = jnp.exp(ls_pos[:, sl] + (carry + r[:, :LANES]))
        pieces[sb] = w if mask is None else jnp.where(mask[:, sl], w, 0.0)
        carry = carry + r[:, LANES:]
    a = pieces[0] if n_sub == 1 else jnp.concatenate(pieces, axis=1)
    return a, carry


SB_ZERO_LOG = -104.0


def _sb_prompt_kernel(q_ref, k_ref, v_ref, o_ref, acc_ref, carry_ref, *, t):
    qi = pl.program_id(2)
    umat = _suffix_matrix()
    q = (q_ref[...] * SCALE).astype(BF16)
    row = lax.broadcasted_iota(jnp.int32, (t, t), 0)
    col = lax.broadcasted_iota(jnp.int32, (t, t), 1)

    def tile(kb, mask):
        rows = pl.ds(pl.multiple_of(kb * t, t), t)
        z = _dot_nt(q, k_ref[rows, :].astype(BF16))
        a, carry = _sb_weights(z, mask, carry_ref[...], umat)
        acc_ref[...] += _dot(a.astype(BF16), v_ref[rows, :].astype(BF16))
        carry_ref[...] = carry
        return jnp.max(carry)

    acc_ref[...] = jnp.zeros_like(acc_ref)
    carry_ref[...] = jnp.zeros_like(carry_ref)
    top = tile(qi, col < row)

    def cond(st):
        return jnp.logical_and(st[0] >= 0, st[1] > SB_ZERO_LOG)

    def body(st):
        return st[0] - 1, tile(st[0], None)

    lax.while_loop(cond, body, (qi - 1, top))
    o_ref[...] = acc_ref[...].astype(o_ref.dtype)


def sb_prompt(u_rest, batch, seq, t=256):
    t = min(t, seq)
    nq = seq // t
    cq, ck, cv = R_SB_Q // LANES, R_SB_K // LANES, R_SB_V // LANES
    return pl.pallas_call(
        functools.partial(_sb_prompt_kernel, t=t),
        grid=(batch, SB_HEADS, nq),
        in_specs=[pl.BlockSpec((t, LANES), lambda b, h, i: (b * nq + i, cq + h)),
                  pl.BlockSpec((seq, LANES), lambda b, h, i: (b, ck + h)),
                  pl.BlockSpec((seq, LANES), lambda b, h, i: (b, cv + h))],
        out_specs=pl.BlockSpec((t, LANES), lambda b, h, i: (b * nq + i, h)),
        out_shape=jax.ShapeDtypeStruct((batch * seq, SB_W), BF16),
        scratch_shapes=[pltpu.VMEM((t, LANES), F32), pltpu.VMEM((t, LANES), F32)],
        compiler_params=_cp(3),
        name="sb_prompt",
    )(u_rest, u_rest, u_rest)


def _hgrn_chunk(qr, fr, v, gr, lbp, norm_w, st, c, sub, c_real):
    log_lb, log_1m_lb, one_m_lb = lbp[0:1], lbp[1:2], lbp[2:3]
    q = _silu(qr)
    k = one_m_lb * _sigmoid(-fr)
    bb = log_1m_lb + _log_sigmoid(fr)
    mx = jnp.maximum(log_lb, bb)
    logf = mx + jnp.log1p(jnp.exp(-jnp.abs(log_lb - bb)))
    row = lax.broadcasted_iota(jnp.int32, (c, c), 0)
    col = lax.broadcasted_iota(jnp.int32, (c, c), 1)
    tri = jnp.where(row >= col, 1.0, 0.0).astype(BF16)
    hi, mid, lo = _split3(logf)
    b = _dot(tri, hi) + _dot(tri, mid) + _dot(tri, lo)
    o = _dot_nt((q * jnp.exp(b)).astype(BF16), st.astype(BF16))
    ridx = lax.broadcasted_iota(jnp.int32, (c, HEAD_DIM), 0)
    lane = lax.broadcasted_iota(jnp.int32, (sub, c), 1)
    att_rows = []
    for i in range(c // sub):
        r0 = i * sub
        b_i = b[r0:r0 + sub]
        q_i = q[r0:r0 + sub]
        att_i = jnp.zeros((sub, c), F32)
        if i > 0:
            rho = b_i[0:1]
            earlier = ridx < r0
            k_dec = jnp.where(earlier, k * jnp.exp(jnp.where(earlier, rho - b, 0.0)), 0.0)
            att_i = _dot_nt((q_i * jnp.exp(b_i - rho)).astype(BF16), k_dec.astype(BF16))
        trow = lax.broadcasted_iota(jnp.int32, (sub, 1), 0)
        for s in range(sub):
            d = q_i * jnp.exp(jnp.where(trow >= s, b_i - b_i[s:s + 1], 0.0)) * k[r0 + s:r0 + s + 1]
            colsum = jnp.sum(d, axis=1, keepdims=True)
            att_i = att_i + jnp.where((lane == r0 + s) & (trow >= s), colsum, 0.0)
        att_rows.append(att_i)
    att = att_rows[0] if len(att_rows) == 1 else jnp.concatenate(att_rows, axis=0)
    o = o + _dot(att.astype(BF16), v.astype(BF16))
    b_end = b[c_real - 1:c_real]
    real = ridx < c_real
    k_end = jnp.where(real, k * jnp.exp(jnp.where(real, b_end - b, 0.0)), 0.0)
    st = st * jnp.exp(b_end) + _dot_tn(v.astype(BF16), k_end.astype(BF16))
    o = o * lax.rsqrt(jnp.mean(o * o, axis=-1, keepdims=True) + NORM_EPS) * norm_w
    return o * _silu(gr), st


def _hgrn_kernel(q_ref, f_ref, i_ref, g_ref, lbp_ref, nw_ref, s0_ref, o_ref, s_out_ref, st_ref, *,
                 c, sub, n_chunks, hb):
    t = pl.program_id(2)

    @pl.when(t == 0)
    def _():
        for j in range(hb):
            st_ref[j] = s0_ref[j].T

    nw = nw_ref[...]
    in_refs = (q_ref, f_ref, i_ref, g_ref)

    if c < HG_SUB:
        pad = jnp.zeros((HG_SUB - c, HEAD_DIM), F32)
        for j in range(hb):
            cols = slice(j * HEAD_DIM, (j + 1) * HEAD_DIM)
            ins = [jnp.concatenate([r[:, cols], pad], axis=0) for r in in_refs]
            o, st = _hgrn_chunk(*ins, lbp_ref[j], nw, st_ref[j], HG_SUB, HG_SUB, c)
            st_ref[j] = st
            o_ref[:, cols] = o[:c].astype(o_ref.dtype)
    else:
        def body(ci, carry):
            rows = pl.ds(pl.multiple_of(ci * c, c), c)
            for j in range(hb):
                cols = slice(j * HEAD_DIM, (j + 1) * HEAD_DIM)
                o, st = _hgrn_chunk(*[r[rows, cols] for r in in_refs], lbp_ref[j], nw, st_ref[j], c, sub, c)
                st_ref[j] = st
                o_ref[rows, cols] = o.astype(o_ref.dtype)
            return carry

        lax.fori_loop(0, n_chunks, body, 0)

    @pl.when(t == pl.num_programs(2) - 1)
    def _():
        for j in range(hb):
            s_out_ref[j] = st_ref[j].T


def hgrn_lb_params(lb):
    rows = jnp.stack([jnp.log(lb), jnp.log1p(-lb), 1.0 - lb], axis=1)
    return jnp.concatenate([rows, jnp.zeros((lb.shape[0], SUBLANES - 3, lb.shape[1]), F32)], axis=1)


def hgrn(u_rest, row0, batch, seq, lbp, norm_w, s0, tl=512, hb=4):
    c = min(HG_CHUNK, seq)
    sub = min(HG_SUB, c)
    tl = min(tl, seq)
    nt = seq // tl
    rb0 = row0 // tl
    bw = hb * HEAD_DIM
    assert row0 % tl == 0 and seq % tl == 0 and tl % c == 0 and (c % HG_SUB == 0 or nt == 1) and HG_HEADS % hb == 0

    def col(off):
        return pl.BlockSpec((tl, bw), lambda b, h, t, off=off: (rb0 + b * nt + t, off // bw + h))

    o, s_out = pl.pallas_call(
        functools.partial(_hgrn_kernel, c=c, sub=sub, n_chunks=tl // c, hb=hb),
        grid=(batch, HG_HEADS // hb, nt),
        in_specs=[col(R_HG_Q), col(R_HG_F), col(R_HG_I), col(R_HG_G),
                  pl.BlockSpec((hb, SUBLANES, LANES), lambda b, h, t: (h, 0, 0)),
                  pl.BlockSpec((1, LANES), lambda b, h, t: (0, 0)),
                  pl.BlockSpec((None, hb, HEAD_DIM, HEAD_DIM), lambda b, h, t: (b, h, 0, 0))],
        out_specs=[pl.BlockSpec((tl, bw), lambda b, h, t: (b * nt + t, h)),
                   pl.BlockSpec((None, hb, HEAD_DIM, HEAD_DIM), lambda b, h, t: (b, h, 0, 0))],
        out_shape=[jax.ShapeDtypeStruct((batch * seq, HG_W), F32),
                   jax.ShapeDtypeStruct((batch, HG_HEADS, HEAD_DIM, HEAD_DIM), F32)],
        scratch_shapes=[pltpu.VMEM((hb, HEAD_DIM, HEAD_DIM), F32)],
        compiler_params=_cp(3),
        name="hgrn",
    )(u_rest, u_rest, u_rest, u_rest, lbp, norm_w.reshape(1, LANES), s0)
    return o, s_out


SEG_W = CMP_STRIDE * HEAD_DIM


def compress_params(pe, w1):
    w = jnp.concatenate([w1[:SEG_W], w1[SEG_W:]], axis=1).astype(BF16)
    rows = jnp.stack([pe[:CMP_STRIDE].reshape(SEG_W), pe[CMP_STRIDE:].reshape(SEG_W)])
    return w, jnp.concatenate([rows, jnp.zeros((SUBLANES - 2, SEG_W), F32)]).astype(BF16)


def _segment_products(r, w, pe):
    bias = _dot(pe, w)
    bias = jnp.concatenate([bias[0:1, :LANES], bias[1:2, LANES:]], axis=1)
    return _dot(r.astype(BF16), w) + bias


def _cmp1_prompt_kernel(x_ref, w_ref, pe_ref, o_ref, r_ref, *, n_seg):
    for l in range(CMP_STRIDE):
        r_ref[:, l * LANES:(l + 1) * LANES] = x_ref[pl.ds(l, n_seg, stride=CMP_STRIDE), :]
    o_ref[...] = _segment_products(r_ref[...], w_ref[...], pe_ref[...])


def compress_segments_prompt(u_head, batch, seq, wcat, pecat, tr=512):
    tr = min(tr, seq)
    n_seg = tr // CMP_STRIDE
    nt = seq // tr
    c0 = OFF_CMP // LANES
    return pl.pallas_call(
        functools.partial(_cmp1_prompt_kernel, n_seg=n_seg),
        grid=(batch, 2, NSA_KV_HEADS, nt),
        in_specs=[pl.BlockSpec((tr, LANES), lambda b, kv, g, t: (b * nt + t, c0 + kv * NSA_KV_HEADS + g)),
                  pl.BlockSpec((None, SEG_W, 2 * LANES), lambda b, kv, g, t: (kv, 0, 0)),
                  pl.BlockSpec((None, SUBLANES, SEG_W), lambda b, kv, g, t: (kv, 0, 0))],
        out_specs=pl.BlockSpec((None, None, None, n_seg, 2 * LANES), lambda b, kv, g, t: (b, kv, g, t, 0)),
        out_shape=jax.ShapeDtypeStruct((batch, 2, NSA_KV_HEADS, seq // CMP_STRIDE, 2 * LANES), F32),
        scratch_shapes=[pltpu.VMEM((n_seg, SEG_W), F32)],
        compiler_params=_cp(4),
        name="cmp_segments_prompt",
    )(u_head, wcat, pecat)


def _cmp2_kernel(pq_ref, w2_ref, o_ref, *, nc):
    pq = pq_ref[...]
    q_next = pltpu.roll(pq[:, LANES:], shift=nc - 1, axis=0)
    hid = _silu(pq[:, :LANES] + q_next)
    out = _dot(hid.astype(BF16), w2_ref[...].astype(BF16))
    row = lax.broadcasted_iota(jnp.int32, out.shape, 0)
    o_ref[...] = jnp.where(row < nc - 1, out, 0.0)


def compress_finish(pq, w2):
    batch, _, _, nc, _ = pq.shape
    return pl.pallas_call(
        functools.partial(_cmp2_kernel, nc=nc),
        grid=(batch, 2, NSA_KV_HEADS),
        in_specs=[pl.BlockSpec((None, None, None, nc, 2 * LANES), lambda b, kv, g: (b, kv, g, 0, 0)),
                  pl.BlockSpec((None, HEAD_DIM, HEAD_DIM), lambda b, kv, g: (kv, 0, 0))],
        out_specs=pl.BlockSpec((None, None, None, nc, LANES), lambda b, kv, g: (b, kv, g, 0, 0)),
        out_shape=jax.ShapeDtypeStruct((batch, 2, NSA_KV_HEADS, nc, LANES), F32),
        compiler_params=_cp(3),
        name="cmp_finish",
    )(pq, w2)


def _cmp_select_kernel(q_ref, kc_ref, vc_ref, o_ref, sel_ref, score_ref, *, tq, tqp, nc, n_cmp, n_slc, nsp, pos0):
    qi = pl.program_id(2)
    qb = q_ref[...]
    parts = []
    for h in range(NSA_GROUP):
        qh = qb[:, h * HEAD_DIM:(h + 1) * HEAD_DIM]
        if tqp > tq:
            qh = jnp.concatenate([qh, jnp.zeros((tqp - tq, HEAD_DIM), F32)], axis=0)
        parts.append(qh)
    q4 = jnp.concatenate(parts, axis=0).astype(BF16)
    rows = NSA_GROUP * tqp
    st = _dot_nt(kc_ref[...].astype(BF16), q4) * SCALE
    ci = lax.broadcasted_iota(jnp.int32, (nc, rows), 0)
    tok = lax.broadcasted_iota(jnp.int32, (nc, rows), 1) & (tqp - 1)
    tpos = pos0 + qi * tq + tok
    valid = (ci < n_cmp) & (ci * CMP_STRIDE + CMP_LEN - 1 <= tpos)
    st = jnp.where(valid, st, NEG_BIG)
    m = jnp.max(st, axis=0, keepdims=True)
    e = jnp.where(valid, jnp.exp(st - m), 0.0)
    den = jnp.sum(e, axis=0, keepdims=True)
    pt = e / jnp.where(den > 0, den, 1.0)
    o = _dot_tn(pt.astype(BF16), vc_ref[...].astype(BF16))
    for h in range(NSA_GROUP):
        o_ref[:, h * HEAD_DIM:(h + 1) * HEAD_DIM] = o[h * tqp:h * tqp + tq]
    psum = pt[:, 0:tqp]
    for h in range(1, NSA_GROUP):
        psum = psum + pt[:, h * tqp:(h + 1) * tqp]
    jj = lax.broadcasted_iota(jnp.int32, (nsp, nc), 0)
    ii = lax.broadcasted_iota(jnp.int32, (nsp, nc), 1)
    cover = ((ii * CMP_STRIDE < jj * SEL_BLOCK + SEL_BLOCK) & (ii * CMP_STRIDE + CMP_LEN - 1 >= jj * SEL_BLOCK)
             & (ii < n_cmp))
    cover = jnp.where(cover, 1.0, 0.0).astype(BF16)
    hi, mid, lo = _split3(psum)
    imp = _dot(cover, hi) + _dot(cover, mid) + _dot(cover, lo)
    j = lax.broadcasted_iota(jnp.int32, (nsp, tqp), 0)
    tpos2 = pos0 + qi * tq + lax.broadcasted_iota(jnp.int32, (nsp, tqp), 1)
    cur = tpos2 // SEL_BLOCK
    forced = (j == 0) | (j == cur) | (j == cur - 1)
    ok = (j * SEL_BLOCK <= tpos2) & (j < n_slc)
    score = jnp.where(ok, jnp.where(forced, NSA_GROUP + 1.0, imp), -1.0)
    score_ref[...] = score

    def body(jp, rank):
        other = score_ref[pl.ds(jp, 1), :]
        ahead = jnp.where(other > score, 1.0, jnp.where((other == score) & (jp < j), 1.0, 0.0))
        return rank + ahead

    rank = lax.fori_loop(0, n_slc, body, jnp.zeros((nsp, tqp), F32))
    sel_t = jnp.where((rank < SEL_TOPK) & (score >= 0), 1.0, 0.0)
    sel_ref[...] = sel_t.T[:tq]


def cmp_select(u_head, row0, batch, seq, kv_cmp, n_cmp, n_slc, pos0, tq=256):
    tq = min(tq, seq)
    tqp = max(tq, LANES)
    nt = seq // tq
    rb0 = row0 // tq
    nc = kv_cmp.shape[3]
    nsp = -(-n_slc // LANES) * LANES
    gw = NSA_GROUP * HEAD_DIM
    assert row0 % tq == 0 and seq % tq == 0 and tqp & (tqp - 1) == 0
    return pl.pallas_call(
        functools.partial(_cmp_select_kernel, tq=tq, tqp=tqp, nc=nc, n_cmp=n_cmp, n_slc=n_slc, nsp=nsp, pos0=pos0),
        grid=(batch, NSA_KV_HEADS, nt),
        in_specs=[pl.BlockSpec((tq, gw), lambda b, g, t: (rb0 + b * nt + t, g)),
                  pl.BlockSpec((None, None, None, nc, LANES), lambda b, g, t: (b, 0, g, 0, 0)),
                  pl.BlockSpec((None, None, None, nc, LANES), lambda b, g, t: (b, 1, g, 0, 0))],
        out_specs=[pl.BlockSpec((tq, gw), lambda b, g, t: (b * nt + t, g)),
                   pl.BlockSpec((None, None, tq, nsp), lambda b, g, t: (b, g, t, 0))],
        out_shape=[jax.ShapeDtypeStruct((batch * seq, NSA_W), F32),
                   jax.ShapeDtypeStruct((batch, NSA_KV_HEADS, seq, nsp), F32)],
        scratch_shapes=[pltpu.VMEM((nsp, tqp), F32)],
        compiler_params=_cp(3),
        name="cmp_select",
    )(u_head, kv_cmp, kv_cmp)


def _flash_step(s, mask, v, m_ref, l_ref, acc_ref):
    s = jnp.where(mask, s, NEG_BIG)
    m_prev = m_ref[...]
    m_new = jnp.maximum(m_prev, jnp.max(s, axis=1, keepdims=True))
    e = jnp.where(mask, jnp.exp(s - m_new), 0.0)
    alpha = jnp.exp(m_prev - m_new)
    l_ref[...] = alpha * l_ref[...] + jnp.sum(e, axis=1, keepdims=True)
    acc_ref[...] = alpha * acc_ref[...] + _dot(e.astype(BF16), v)
    m_ref[...] = m_new


def _flash_init(m_ref, l_ref, acc_ref):
    m_ref[...] = jnp.full_like(m_ref, NEG_BIG)
    l_ref[...] = jnp.zeros_like(l_ref)
    acc_ref[...] = jnp.zeros_like(acc_ref)


def _flash_result(l_ref, acc_ref):
    l = l_ref[...]
    return acc_ref[...] / jnp.where(l > 0, l, 1.0)


def _stack_heads(qb, pad_to=None):
    parts = []
    for h in range(NSA_GROUP):
        qh = qb[:, h * HEAD_DIM:(h + 1) * HEAD_DIM]
        if pad_to is not None and pad_to > qh.shape[0]:
            qh = jnp.concatenate([qh, jnp.zeros((pad_to - qh.shape[0], HEAD_DIM), qh.dtype)], axis=0)
        parts.append(qh)
    return jnp.concatenate(parts, axis=0)


def _flash_prompt_kernel(qi_tab, kb_tab, first_tab, last_tab, *refs, tq, tk, mode, nsp):
    if mode == "sel":
        q_ref, k_ref, v_ref, sel_ref, o_ref, m_ref, l_ref, acc_ref = refs
    else:
        q_ref, k_ref, v_ref, o_ref, m_ref, l_ref, acc_ref = refs
    p = pl.program_id(2)
    qi = qi_tab[p]
    kb = kb_tab[p]

    @pl.when(first_tab[p] == 1)
    def _():
        _flash_init(m_ref, l_ref, acc_ref)

    q4 = _stack_heads(q_ref[...]).astype(BF16)
    s = _dot_nt(q4, k_ref[...].astype(BF16)) * SCALE
    qpos = qi * tq + lax.broadcasted_iota(jnp.int32, (tq, tk), 0)
    kpos = kb * tk + lax.broadcasted_iota(jnp.int32, (tq, tk), 1)
    if mode == "win":
        d = qpos - kpos
        mask = (d >= 0) & (d < WINDOW)
    else:
        jj = lax.broadcasted_iota(jnp.int32, (nsp, tk), 0)
        kk = lax.broadcasted_iota(jnp.int32, (nsp, tk), 1)
        expand = jnp.where(jj == kb * (tk // SEL_BLOCK) + kk // SEL_BLOCK, 1.0, 0.0).astype(BF16)
        chosen = _dot(sel_ref[...].astype(BF16), expand)
        mask = (chosen > 0.5) & (kpos <= qpos)
    mask4 = jnp.concatenate([mask] * NSA_GROUP, axis=0)
    _flash_step(s, mask4, v_ref[...].astype(BF16), m_ref, l_ref, acc_ref)

    @pl.when(last_tab[p] == 1)
    def _():
        o = _flash_result(l_ref, acc_ref)
        for h in range(NSA_GROUP):
            o_ref[:, h * HEAD_DIM:(h + 1) * HEAD_DIM] = o[h * tq:(h + 1) * tq]


def flash_prompt(u_head, batch, seq, mode, sel=None, tq=256, tk=None):
    tq = min(tq, seq)
    tk = min(tk or (256 if mode == "win" else 512), seq)
    nq, nk = seq // tq, seq // tk
    pairs = []
    for qi in range(nq):
        if mode == "win":
            lo = max(0, (qi * tq - WINDOW + 1) // tk)
        else:
            lo = 0
        hi = (qi * tq + tq - 1) // tk
        kbs = list(range(lo, hi + 1))
        pairs += [(qi, kb, int(kb == kbs[0]), int(kb == kbs[-1])) for kb in kbs]
    tabs = [jnp.asarray([p[i] for p in pairs], jnp.int32) for i in range(4)]
    off = OFF_WIN if mode == "win" else OFF_SEL
    ck = off // LANES
    cv = ck + NSA_KV_HEADS
    gw = NSA_GROUP * HEAD_DIM
    in_specs = [
        pl.BlockSpec((tq, gw), lambda b, g, p, qt, kt, ft, lt: (b * nq + qt[p], g)),
        pl.BlockSpec((tk, LANES), lambda b, g, p, qt, kt, ft, lt: (b * nk + kt[p], ck + g)),
        pl.BlockSpec((tk, LANES), lambda b, g, p, qt, kt, ft, lt: (b * nk + kt[p], cv + g)),
    ]
    args = [u_head, u_head, u_head]
    nsp = 0
    if mode == "sel":
        nsp = sel.shape[-1]
        in_specs.append(pl.BlockSpec((None, None, tq, nsp), lambda b, g, p, qt, kt, ft, lt: (b, g, qt[p], 0)))
        args.append(sel)
    rows = NSA_GROUP * tq
    grid_spec = pltpu.PrefetchScalarGridSpec(
        num_scalar_prefetch=4,
        grid=(batch, NSA_KV_HEADS, len(pairs)),
        in_specs=in_specs,
        out_specs=pl.BlockSpec((tq, gw), lambda b, g, p, qt, kt, ft, lt: (b * nq + qt[p], g)),
        scratch_shapes=[pltpu.VMEM((rows, 1), F32), pltpu.VMEM((rows, 1), F32), pltpu.VMEM((rows, HEAD_DIM), F32)],
    )
    return pl.pallas_call(
        functools.partial(_flash_prompt_kernel, tq=tq, tk=tk, mode=mode, nsp=nsp),
        grid_spec=grid_spec,
        out_shape=jax.ShapeDtypeStruct((batch * seq, NSA_W), F32),
        compiler_params=_cp(3),
        name="flash_" + mode,
    )(*tabs, *args)


def _win_prompt_kernel(q_ref, k_ref, v_ref, o_ref, *, tq, nk, seq):
    q0 = pl.program_id(2) * tq
    start = pl.multiple_of(jnp.clip(q0 - WINDOW, 0, seq - nk), tq)
    keys = pl.ds(start, nk)
    q4 = (_stack_heads(q_ref[...]) * SCALE).astype(BF16)
    s = _dot_nt(q4, k_ref[keys, :].astype(BF16))
    d = (q0 + lax.broadcasted_iota(jnp.int32, (tq, nk), 0)) - (start + lax.broadcasted_iota(jnp.int32, (tq, nk), 1))
    bias = jnp.where((d >= 0) & (d < WINDOW), 0.0, NEG_BIG)
    s = s + jnp.concatenate([bias] * NSA_GROUP, axis=0)
    e = jnp.exp(s - jnp.max(s, axis=1, keepdims=True))
    o = _dot(e.astype(BF16), v_ref[keys, :].astype(BF16)) / jnp.sum(e, axis=1, keepdims=True)
    for h in range(NSA_GROUP):
        o_ref[:, h * HEAD_DIM:(h + 1) * HEAD_DIM] = o[h * tq:(h + 1) * tq]


def win_prompt(u_head, batch, seq, tq=256):
    tq = min(tq, seq)
    nk = min(seq, WINDOW + tq)
    nq = seq // tq
    ck = OFF_WIN // LANES
    cv = ck + NSA_KV_HEADS
    gw = NSA_GROUP * HEAD_DIM
    assert seq % tq == 0 and WINDOW % tq == 0
    return pl.pallas_call(
        functools.partial(_win_prompt_kernel, tq=tq, nk=nk, seq=seq),
        grid=(batch, NSA_KV_HEADS, nq),
        in_specs=[pl.BlockSpec((tq, gw), lambda b, g, i: (b * nq + i, g)),
                  pl.BlockSpec((seq, LANES), lambda b, g, i: (b, ck + g)),
                  pl.BlockSpec((seq, LANES), lambda b, g, i: (b, cv + g))],
        out_specs=pl.BlockSpec((tq, gw), lambda b, g, i: (b * nq + i, g)),
        out_shape=jax.ShapeDtypeStruct((batch * seq, NSA_W), F32),
        compiler_params=_cp(3),
        name="win_prompt",
    )(u_head, u_head, u_head)


def _nsa_combine_kernel(gate_ref, a_ref, b_ref, c_ref, o_ref):
    gate = _sigmoid(gate_ref[...])
    for h in range(NSA_HEADS):
        sl = slice(h * HEAD_DIM, (h + 1) * HEAD_DIM)
        out = (gate[:, h:h + 1] * a_ref[:, sl] + gate[:, NSA_HEADS + h:NSA_HEADS + h + 1] * b_ref[:, sl]
               + gate[:, 2 * NSA_HEADS + h:2 * NSA_HEADS + h + 1] * c_ref[:, sl])
        o_ref[:, sl] = out.astype(o_ref.dtype)


def nsa_combine(u_head, row0, o_cmp, o_sel, o_win, tm=256):
    n = o_cmp.shape[0]
    tm = min(tm, n)
    rb0 = row0 // tm
    assert row0 % tm == 0 and n % tm == 0
    spec = pl.BlockSpec((tm, NSA_W), lambda i: (i, 0))
    return pl.pallas_call(
        _nsa_combine_kernel,
        grid=(n // tm,),
        in_specs=[pl.BlockSpec((tm, LANES), lambda i: (rb0 + i, OFF_NSA_GATE // LANES)), spec, spec, spec],
        out_specs=spec,
        out_shape=jax.ShapeDtypeStruct((n, NSA_W), F32),
        compiler_params=_cp(1),
        name="nsa_combine",
    )(u_head, o_cmp, o_sel, o_win)


def _page_rows(page_ref, first, n_rows, stride):
    return page_ref[pl.ds(first, n_rows, stride=stride), :]


def _pad_rows(x, n):
    if x.shape[0] >= n:
        return x
    return jnp.concatenate([x, jnp.zeros((n - x.shape[0], x.shape[1]), x.dtype)], axis=0)


def _cmp1_sample_kernel(pt_ref, *refs, pps, page):
    page_refs = refs[:pps]
    w_ref, pe_ref, o_ref, r_ref = refs[pps:]
    segs = page // CMP_STRIDE
    stride = CMP_STRIDE * 2 * NSA_KV_HEADS
    for kv in range(2):
        for g in range(NSA_KV_HEADS):
            for k in range(pps):
                for l in range(CMP_STRIDE):
                    r_ref[k * segs:(k + 1) * segs, l * LANES:(l + 1) * LANES] = _page_rows(
                        page_refs[k], l * 2 * NSA_KV_HEADS + kv * NSA_KV_HEADS + g, segs, stride)
            o_ref[kv, g] = _segment_products(r_ref[...], w_ref[kv], pe_ref[kv])


def compress_segments_sample(pool, layer, pt_flat, batch, n_pages, wcat, pecat, pps=16):
    page = pool.shape[2] // (2 * NSA_KV_HEADS)
    pps = min(pps, n_pages)
    assert n_pages % pps == 0 and page % CMP_STRIDE == 0
    segs = page // CMP_STRIDE
    in_specs = [pl.BlockSpec((None, None, pool.shape[2], LANES),
                             lambda b, s, pt, k=k: (layer, pt[b * n_pages + s * pps + k], 0, 0)) for k in range(pps)]
    in_specs += [pl.BlockSpec((2, SEG_W, 2 * LANES), lambda b, s, pt: (0, 0, 0)),
                 pl.BlockSpec((2, SUBLANES, SEG_W), lambda b, s, pt: (0, 0, 0))]
    grid_spec = pltpu.PrefetchScalarGridSpec(
        num_scalar_prefetch=1,
        grid=(batch, n_pages // pps),
        in_specs=in_specs,
        out_specs=pl.BlockSpec((None, 2, NSA_KV_HEADS, pps * segs, 2 * LANES), lambda b, s, pt: (b, 0, 0, s, 0)),
        scratch_shapes=[pltpu.VMEM((pps * segs, SEG_W), F32)],
    )
    return pl.pallas_call(
        functools.partial(_cmp1_sample_kernel, pps=pps, page=page),
        grid_spec=grid_spec,
        out_shape=jax.ShapeDtypeStruct((batch, 2, NSA_KV_HEADS, n_pages * segs, 2 * LANES), F32),
        compiler_params=_cp(2),
        name="cmp_segments_sample",
    )(pt_flat, *([pool] * pps), wcat, pecat)


def _sel_sample_kernel(pt_ref, *refs, pps, page, ds, past, nsp):
    q_ref, new_ref, sel_ref = refs[:3]
    page_refs = refs[3:3 + pps]
    o_ref, m_ref, l_ref, acc_ref = refs[3 + pps:]
    s = pl.program_id(1)
    rows = NSA_GROUP * ds
    keys = pps * page
    stride = 2 * NSA_KV_HEADS

    @pl.when(s == 0)
    def _():
        _flash_init(m_ref, l_ref, acc_ref)

    for g in range(NSA_KV_HEADS):
        q4 = _stack_heads(q_ref[:, g * NSA_GROUP * HEAD_DIM:(g + 1) * NSA_GROUP * HEAD_DIM]).astype(BF16)
        sel4 = jnp.concatenate([sel_ref[g]] * NSA_GROUP, axis=0)
        k_all = jnp.concatenate([_page_rows(r, g, page, stride) for r in page_refs], axis=0).astype(BF16)
        v_all = jnp.concatenate([_page_rows(r, NSA_KV_HEADS + g, page, stride) for r in page_refs],
                                axis=0).astype(BF16)
        sc = _dot_nt(q4, k_all) * SCALE
        jj = lax.broadcasted_iota(jnp.int32, (nsp, keys), 0)
        kk = lax.broadcasted_iota(jnp.int32, (nsp, keys), 1)
        expand = jnp.where(jj == s * (keys // SEL_BLOCK) + kk // SEL_BLOCK, 1.0, 0.0).astype(BF16)
        chosen = _dot(sel4.astype(BF16), expand)
        _flash_step(sc, chosen > 0.5, v_all, m_ref.at[g], l_ref.at[g], acc_ref.at[g])

        @pl.when(s == pl.num_programs(1) - 1)
        def _():
            k_new = _pad_rows(new_ref[:, g * HEAD_DIM:(g + 1) * HEAD_DIM], LANES).astype(BF16)
            v_new = _pad_rows(new_ref[:, (NSA_KV_HEADS + g) * HEAD_DIM:(NSA_KV_HEADS + g + 1) * HEAD_DIM],
                              LANES).astype(BF16)
            sn = _dot_nt(q4, k_new) * SCALE
            blk = past // SEL_BLOCK
            j = lax.broadcasted_iota(jnp.int32, (rows, LANES), 1)
            t = lax.broadcasted_iota(jnp.int32, (rows, LANES), 0) % ds
            mask = (sel4[:, blk:blk + 1] > 0.5) & (j <= t)
            _flash_step(sn, mask, v_new, m_ref.at[g], l_ref.at[g], acc_ref.at[g])
            o = _flash_result(l_ref.at[g], acc_ref.at[g])
            for h in range(NSA_GROUP):
                c0 = (g * NSA_GROUP + h) * HEAD_DIM
                o_ref[:, c0:c0 + HEAD_DIM] = o[h * ds:(h + 1) * ds]


def sel_sample(u_head, row0, pool, layer, pt_flat, sel, batch, ds, n_pages, pps=8):
    page = pool.shape[2] // (2 * NSA_KV_HEADS)
    pps = min(pps, n_pages)
    past = n_pages * page
    nsp = sel.shape[-1]
    rows = NSA_GROUP * ds
    assert n_pages % pps == 0 and row0 % ds == 0 and past % SEL_BLOCK == 0 and ds <= SEL_BLOCK
    assert page % SEL_BLOCK == 0
    in_specs = [pl.BlockSpec((ds, NSA_W), lambda b, s, pt: (row0 // ds + b, 0)),
                pl.BlockSpec((ds, KV_W), lambda b, s, pt: (row0 // ds + b, OFF_SEL // KV_W)),
                pl.BlockSpec((None, NSA_KV_HEADS, ds, nsp), lambda b, s, pt: (b, 0, 0, 0))]
    in_specs += [pl.BlockSpec((None, None, pool.shape[2], LANES),
                              lambda b, s, pt, k=k: (layer, pt[b * n_pages + s * pps + k], 0, 0)) for k in range(pps)]
    grid_spec = pltpu.PrefetchScalarGridSpec(
        num_scalar_prefetch=1,
        grid=(batch, n_pages // pps),
        in_specs=in_specs,
        out_specs=pl.BlockSpec((ds, NSA_W), lambda b, s, pt: (b, 0)),
        scratch_shapes=[pltpu.VMEM((NSA_KV_HEADS, rows, 1), F32), pltpu.VMEM((NSA_KV_HEADS, rows, 1), F32),
                        pltpu.VMEM((NSA_KV_HEADS, rows, HEAD_DIM), F32)],
    )
    return pl.pallas_call(
        functools.partial(_sel_sample_kernel, pps=pps, page=page, ds=ds, past=past, nsp=nsp),
        grid_spec=grid_spec,
        out_shape=jax.ShapeDtypeStruct((batch * ds, NSA_W), F32),
        compiler_params=_cp(2),
        name="sel_sample",
    )(pt_flat, u_head, u_head, sel, *([pool] * pps))


def _win_sample_kernel(q_ref, new_ref, buf_ref, o_ref, m_ref, l_ref, acc_ref, *, ds, nbuf):
    rows = NSA_GROUP * ds
    stride = 2 * NSA_KV_HEADS
    keys = nbuf + LANES
    i = lax.broadcasted_iota(jnp.int32, (rows, keys), 1)
    t = lax.broadcasted_iota(jnp.int32, (rows, keys), 0) % ds
    d = jnp.where(i < nbuf, t + nbuf - i, t - (i - nbuf))
    mask = (d >= 0) & (d < WINDOW) & (i < nbuf + ds)
    for g in range(NSA_KV_HEADS):
        _flash_init(m_ref, l_ref, acc_ref)
        q4 = _stack_heads(q_ref[:, g * NSA_GROUP * HEAD_DIM:(g + 1) * NSA_GROUP * HEAD_DIM]).astype(BF16)
        k_new = _pad_rows(new_ref[:, g * HEAD_DIM:(g + 1) * HEAD_DIM], LANES)
        v_new = _pad_rows(new_ref[:, (NSA_KV_HEADS + g) * HEAD_DIM:(NSA_KV_HEADS + g + 1) * HEAD_DIM], LANES)
        k_all = jnp.concatenate([_page_rows(buf_ref, g, nbuf, stride), k_new], axis=0).astype(BF16)
        v_all = jnp.concatenate([_page_rows(buf_ref, NSA_KV_HEADS + g, nbuf, stride), v_new], axis=0).astype(BF16)
        _flash_step(_dot_nt(q4, k_all) * SCALE, mask, v_all, m_ref, l_ref, acc_ref)
        o = _flash_result(l_ref, acc_ref)
        for h in range(NSA_GROUP):
            c0 = (g * NSA_GROUP + h) * HEAD_DIM
            o_ref[:, c0:c0 + HEAD_DIM] = o[h * ds:(h + 1) * ds]


def win_sample(u_head, row0, buf, layer, batch, ds):
    nbuf = buf.shape[2] // (2 * NSA_KV_HEADS)
    rows = NSA_GROUP * ds
    return pl.pallas_call(
        functools.partial(_win_sample_kernel, ds=ds, nbuf=nbuf),
        grid=(batch,),
        in_specs=[pl.BlockSpec((ds, NSA_W), lambda b: (row0 // ds + b, 0)),
                  pl.BlockSpec((ds, KV_W), lambda b: (row0 // ds + b, OFF_WIN // KV_W)),
                  pl.BlockSpec((None, None, buf.shape[2], LANES), lambda b: (layer, b, 0, 0))],
        out_specs=pl.BlockSpec((ds, NSA_W), lambda b: (b, 0)),
        out_shape=jax.ShapeDtypeStruct((batch * ds, NSA_W), F32),
        scratch_shapes=[pltpu.VMEM((rows, 1), F32), pltpu.VMEM((rows, 1), F32), pltpu.VMEM((rows, HEAD_DIM), F32)],
        compiler_params=_cp(1),
        name="win_sample",
    )(u_head, u_head, buf)


SB_ROWS = 16


def _sb_sample_kernel(pt_ref, q_ref, kn_ref, vn_ref, pool_hbm, o_ref, acc_ref, carry_ref, kv_ref, sem, *,
                      pps, page, ds, n_pages, layer):
    b = pl.program_id(0)
    stride = 2 * SB_HEADS
    umat = _suffix_matrix()
    q_heads = [_pad_rows(q_ref[:, h * HEAD_DIM:(h + 1) * HEAD_DIM] * SCALE, SB_ROWS).astype(BF16)
               for h in range(SB_HEADS)]
    rows = SB_HEADS * SB_ROWS
    real_row = lax.broadcasted_iota(jnp.int32, (rows, LANES), 0) % SB_ROWS < ds

    def accumulate(k_heads, v_heads, mask):
        z = jnp.concatenate([_dot_nt(q_heads[h], k_heads[h]) for h in range(SB_HEADS)], axis=0)
        a, carry = _sb_weights(z, mask, carry_ref[...], umat)
        carry_ref[...] = carry
        a = a.astype(BF16)
        for h in range(SB_HEADS):
            sl = slice(h * SB_ROWS, (h + 1) * SB_ROWS)
            acc_ref[sl, :] += _dot(a[sl], v_heads[h])
        return jnp.max(jnp.where(real_row, carry, NEG_BIG))

    acc_ref[...] = jnp.zeros_like(acc_ref)
    carry_ref[...] = jnp.zeros_like(carry_ref)
    k_new = [_pad_rows(kn_ref[:, h * HEAD_DIM:(h + 1) * HEAD_DIM], LANES).astype(BF16) for h in range(SB_HEADS)]
    v_new = [_pad_rows(vn_ref[:, h * HEAD_DIM:(h + 1) * HEAD_DIM], LANES).astype(BF16) for h in range(SB_HEADS)]
    j = lax.broadcasted_iota(jnp.int32, (rows, LANES), 1)
    t = lax.broadcasted_iota(jnp.int32, (rows, LANES), 0) % SB_ROWS
    top = accumulate(k_new, v_new, j < t)

    def page_copy(g, k):
        pid = pt_ref[b * n_pages + n_pages - (g + 1) * pps + k]
        return pltpu.make_async_copy(pool_hbm.at[layer, pid], kv_ref.at[k], sem)

    def cond(st):
        return jnp.logical_and(st[0] < n_pages // pps, st[1] > SB_ZERO_LOG)

    def body(st):
        g = st[0]
        for k in range(pps):
            page_copy(g, k).start()
        for k in range(pps):
            page_copy(g, k).wait()
        k_heads = [jnp.concatenate([_page_rows(kv_ref.at[k], h, page, stride) for k in range(pps)],
                                   axis=0).astype(BF16) for h in range(SB_HEADS)]
        v_heads = [jnp.concatenate([_page_rows(kv_ref.at[k], SB_HEADS + h, page, stride) for k in range(pps)],
                                   axis=0).astype(BF16) for h in range(SB_HEADS)]
        return g + 1, accumulate(k_heads, v_heads, None)

    lax.while_loop(cond, body, (0, top))
    for h in range(SB_HEADS):
        o_ref[:, h * HEAD_DIM:(h + 1) * HEAD_DIM] = acc_ref[h * SB_ROWS:h * SB_ROWS + ds, :]


def sb_sample(u_rest, row0, pool, layer, pt_flat, batch, ds, n_pages, pps=2):
    page = pool.shape[2] // (2 * SB_HEADS)
    pps = min(pps, n_pages)
    assert n_pages % pps == 0 and row0 % ds == 0 and ds <= SB_ROWS
    rb = row0 // ds
    grid_spec = pltpu.PrefetchScalarGridSpec(
        num_scalar_prefetch=1,
        grid=(batch,),
        in_specs=[pl.BlockSpec((ds, SB_W), lambda b, pt: (rb + b, R_SB_Q // SB_W)),
                  pl.BlockSpec((ds, SB_W), lambda b, pt: (rb + b, R_SB_K // SB_W)),
                  pl.BlockSpec((ds, SB_W), lambda b, pt: (rb + b, R_SB_V // SB_W)),
                  pl.BlockSpec(memory_space=pl.ANY)],
        out_specs=pl.BlockSpec((ds, SB_W), lambda b, pt: (b, 0)),
        scratch_shapes=[pltpu.VMEM((SB_HEADS * SB_ROWS, HEAD_DIM), F32), pltpu.VMEM((SB_HEADS * SB_ROWS, LANES), F32),
                        pltpu.VMEM((pps, pool.shape[2], LANES), F32), pltpu.SemaphoreType.DMA],
    )
    return pl.pallas_call(
        functools.partial(_sb_sample_kernel, pps=pps, page=page, ds=ds, n_pages=n_pages, layer=layer),
        grid_spec=grid_spec,
        out_shape=jax.ShapeDtypeStruct((batch * ds, SB_W), F32),
        compiler_params=_cp(1),
        name="sb_sample",
    )(pt_flat, u_rest, u_rest, u_rest, pool)


def _router_kernel(x_ref, g_ref, w_ref, b_ref, h_ref, o_ref):
    x = x_ref[...]
    h = x * lax.rsqrt(jnp.mean(x * x, axis=-1, keepdims=True) + NORM_EPS) * g_ref[...]
    h_ref[...] = h
    hh, hm, hl = _split3(h)
    wh, wm, wl = _split3(w_ref[...])
    logits = (_dot(hh, wh) + _dot(hh, wm) + _dot(hm, wh) + _dot(hh, wl) + _dot(hl, wh) + _dot(hm, wm)) + b_ref[...]
    lane = lax.broadcasted_iota(jnp.int32, logits.shape, 1)
    logits = jnp.where(lane < N_EXPERTS, logits, NEG_BIG)
    m1 = jnp.max(logits, axis=1, keepdims=True)
    i1 = jnp.min(jnp.where(logits == m1, lane, LANES), axis=1, keepdims=True)
    rest = jnp.where(lane == i1, NEG_BIG, logits)
    m2 = jnp.max(rest, axis=1, keepdims=True)
    i2 = jnp.min(jnp.where(rest == m2, lane, LANES), axis=1, keepdims=True)
    e = jnp.exp(m2 - m1)
    g1 = 1.0 / (1.0 + e)
    g2 = e / (1.0 + e)
    o_ref[...] = jnp.where(lane == 0, i1.astype(F32), jnp.where(lane == 1, i2.astype(F32),
                           jnp.where(lane == 2, g1, jnp.where(lane == 3, g2, 0.0))))


def moe_router(x, g, rw, rb, tm):
    m, d = x.shape
    w = jnp.zeros((d, LANES), F32).at[:, :N_EXPERTS].set(rw)
    b = jnp.zeros((1, LANES), F32).at[0, :N_EXPERTS].set(rb.astype(F32))
    return pl.pallas_call(
        _router_kernel,
        grid=(m // tm,),
        in_specs=[pl.BlockSpec((tm, d), lambda i: (i, 0)), pl.BlockSpec((1, d), lambda i: (0, 0)),
                  pl.BlockSpec((d, LANES), lambda i: (0, 0)), pl.BlockSpec((1, LANES), lambda i: (0, 0))],
        out_specs=[pl.BlockSpec((tm, d), lambda i: (i, 0)), pl.BlockSpec((tm, LANES), lambda i: (i, 0))],
        out_shape=[jax.ShapeDtypeStruct((m, d), F32), jax.ShapeDtypeStruct((m, LANES), F32)],
        compiler_params=_cp(1),
        name="moe_router",
    )(x, g.reshape(1, d), w, b)


def _row_copy(src_hbm, row, dst_ref, r, sem):
    return pltpu.make_async_copy(src_hbm.at[pl.ds(row, 1), :], dst_ref.at[pl.ds(r, 1), :], sem)


def _gather_kernel(tok_ref, h_hbm, o_ref, buf_ref, sem):
    blk = pl.program_id(0)
    n = buf_ref.shape[0]

    def issue(r, c):
        _row_copy(h_hbm, tok_ref[blk * n + r], buf_ref, r, sem).start()
        return c

    lax.fori_loop(0, n, issue, 0)

    def wait(r, c):
        _row_copy(h_hbm, 0, buf_ref, r, sem).wait()
        return c

    lax.fori_loop(0, n, wait, 0)
    o_ref[...] = buf_ref[...].astype(o_ref.dtype)


def moe_gather(h, tok_buf):
    p = tok_buf.shape[0]
    d = h.shape[1]
    grid_spec = pltpu.PrefetchScalarGridSpec(
        num_scalar_prefetch=1,
        grid=(p // MOE_BLOCK,),
        in_specs=[pl.BlockSpec(memory_space=pl.ANY)],
        out_specs=pl.BlockSpec((MOE_BLOCK, d), lambda i, tok: (i, 0)),
        scratch_shapes=[pltpu.VMEM((MOE_BLOCK, d), F32), pltpu.SemaphoreType.DMA],
    )
    return pl.pallas_call(
        _gather_kernel,
        grid_spec=grid_spec,
        out_shape=jax.ShapeDtypeStruct((p, d), BF16),
        compiler_params=_cp(1),
        name="moe_gather",
    )(tok_buf, h)


def _moe_mm_kernel(be_ref, *refs, n_w, mode):
    a_ref = refs[0]
    w_refs = refs[1:1 + n_w]
    o_ref = refs[1 + n_w]
    wbf_refs = refs[2 + n_w:]
    blk = pl.program_id(1)
    changed = jnp.logical_or(blk == 0, be_ref[blk] != be_ref[jnp.maximum(blk - 1, 0)])

    @pl.when(changed)
    def _():
        for w_ref, wbf_ref in zip(w_refs, wbf_refs):
            wbf_ref[...] = w_ref[...].astype(BF16)

    prods = [_dot(a_ref[...], r[...]) for r in wbf_refs]
    o_ref[...] = _mm_epilogue(mode, prods, ()).astype(o_ref.dtype)


def moe_matmul(a, w_list, moe_index, blk_e, mode, out_dtype, tn=512):
    p, k = a.shape
    n_out = w_list[0].shape[-1]
    tn = _tile(n_out, tn)
    grid_spec = pltpu.PrefetchScalarGridSpec(
        num_scalar_prefetch=1,
        grid=(n_out // tn, p // MOE_BLOCK),
        in_specs=[pl.BlockSpec((MOE_BLOCK, k), lambda j, i, be: (i, 0))]
        + [pl.BlockSpec((None, None, k, tn), lambda j, i, be: (moe_index, be[i], 0, j)) for _ in w_list],
        out_specs=pl.BlockSpec((MOE_BLOCK, tn), lambda j, i, be: (i, j)),
        scratch_shapes=[pltpu.VMEM((k, tn), BF16) for _ in w_list],
    )
    return pl.pallas_call(
        functools.partial(_moe_mm_kernel, n_w=len(w_list), mode=mode),
        grid_spec=grid_spec,
        out_shape=jax.ShapeDtypeStruct((p, n_out), out_dtype),
        compiler_params=_cp(2),
        name="moe_mm_" + mode,
    )(blk_e, a, *w_list)


def _moe_combine_kernel(pos_ref, y_hbm, x_ref, gate_ref, o_ref, buf_ref, sem, *, tm):
    i = pl.program_id(0)

    def issue(r, c):
        for k in range(TOP_K):
            _row_copy(y_hbm, pos_ref[(i * tm + r) * TOP_K + k], buf_ref.at[k], r, sem).start()
        return c

    lax.fori_loop(0, tm, issue, 0)

    def wait(r, c):
        for k in range(TOP_K):
            _row_copy(y_hbm, 0, buf_ref.at[k], r, sem).wait()
        return c

    lax.fori_loop(0, tm, wait, 0)
    gate = gate_ref[...]
    out = x_ref[...]
    y = gate[:, 2:3] * buf_ref[0]
    for k in range(1, TOP_K):
        y = y + gate[:, 2 + k:3 + k] * buf_ref[k]
    o_ref[...] = out + y


def moe_combine(x, yb, pos, gates, tm=256):
    m, d = x.shape
    grid_spec = pltpu.PrefetchScalarGridSpec(
        num_scalar_prefetch=1,
        grid=(m // tm,),
        in_specs=[pl.BlockSpec(memory_space=pl.ANY),
                  pl.BlockSpec((tm, d), lambda i, pos: (i, 0)),
                  pl.BlockSpec((tm, LANES), lambda i, pos: (i, 0))],
        out_specs=pl.BlockSpec((tm, d), lambda i, pos: (i, 0)),
        scratch_shapes=[pltpu.VMEM((TOP_K, tm, d), F32), pltpu.SemaphoreType.DMA],
    )
    return pl.pallas_call(
        functools.partial(_moe_combine_kernel, tm=tm),
        grid_spec=grid_spec,
        out_shape=jax.ShapeDtypeStruct((m, d), F32),
        compiler_params=_cp(1),
        name="moe_combine",
    )(pos, yb, x, gates)


def moe_layer(x, g, rw, rb, w1, w3, w2, moe_index, n_tok, tm):
    m, d = x.shape
    h, route = moe_router(x, g, rw, rb, tm)
    top_i = route[:n_tok, :TOP_K].astype(jnp.int32)
    a = n_tok * TOP_K
    e_flat = top_i.reshape(a)
    order = jnp.argsort(e_flat)
    e_s = e_flat[order]
    tok_s = (order // TOP_K).astype(jnp.int32)
    counts = jnp.bincount(e_flat, length=N_EXPERTS)
    start = jnp.cumsum(counts) - counts
    padded = (counts + MOE_BLOCK - 1) // MOE_BLOCK * MOE_BLOCK
    pend = jnp.cumsum(padded)
    pstart = pend - padded
    dest = (pstart[e_s] + jnp.arange(a) - start[e_s]).astype(jnp.int32)
    nb = -(-a // MOE_BLOCK) + N_EXPERTS
    p = nb * MOE_BLOCK
    tok_buf = jnp.full((p,), n_tok, jnp.int32).at[dest].set(tok_s)
    blk_e = jnp.minimum(jnp.searchsorted(pend, jnp.arange(nb) * MOE_BLOCK, side="right"),
                        N_EXPERTS - 1).astype(jnp.int32)
    pos = jnp.zeros((m * TOP_K,), jnp.int32).at[order].set(dest)
    row_ok = (jnp.arange(m) < n_tok)[:, None]
    gates = jnp.where(row_ok, route, 0.0)
    xs = moe_gather(h, tok_buf)
    act = moe_matmul(xs, [w1, w3], moe_index, blk_e, "swiglu", BF16)
    yb = moe_matmul(act, [w2], moe_index, blk_e, "plain", F32)
    return moe_combine(x, yb, pos, gates)


TM = 512


def _slab(parts, m, dtype):
    rows = sum(p.shape[0] for p in parts)
    parts = [p.astype(dtype) for p in parts]
    return jnp.concatenate(parts + [jnp.zeros((m - rows, parts[0].shape[1]), dtype)], axis=0)


def kernel(x_prompt, x_sample, cache_nsa_cmp_kv, cache_nsa_sel_kv, cache_sb_kv, cache_nsa_win_kv, state_hgrn,
           page_table, attn_norm, w_in, cmp_pe_k, cmp_w1_k, cmp_w2_k, cmp_pe_v, cmp_w1_v, cmp_w2_v, hg_lb_logits,
           hg_norm, w_br_nsa, w_br_sb, w_br_hg, w_out, ffn_norm, ffn_w1, ffn_w3, ffn_w2, router_w, router_b,
           moe_w1, moe_w3, moe_w2, final_norm):
    bsz, seq, d = x_prompt.shape
    db, ds, _ = x_sample.shape
    depth = attn_norm.shape[0]
    n_p, n_s = bsz * seq, db * ds
    n_tok = n_p + n_s
    m = -(-(n_tok + 1) // TM) * TM
    n_pool, page = cache_nsa_cmp_kv.shape[1:3]
    n_pages = page_table.shape[1]
    past = n_pages * page
    assert past % CMP_STRIDE == 0 and ds < CMP_STRIDE and seq % CMP_STRIDE == 0

    x = _slab([x_prompt.reshape(n_p, d), x_sample.reshape(n_s, d)], m, F32)
    pt_flat = page_table.reshape(-1).astype(jnp.int32)
    cmp_pool = cache_nsa_cmp_kv.reshape(depth, n_pool, page * 2 * NSA_KV_HEADS, HEAD_DIM)
    sel_pool = cache_nsa_sel_kv.reshape(depth, n_pool, page * 2 * NSA_KV_HEADS, HEAD_DIM)
    sb_pool = cache_sb_kv.reshape(depth, n_pool, page * 2 * SB_HEADS, HEAD_DIM)
    nbuf = cache_nsa_win_kv.shape[2]
    win_buf = cache_nsa_win_kv.reshape(depth, db, nbuf * 2 * NSA_KV_HEADS, HEAD_DIM)
    w_rest = w_in[:, :, REST_START:].astype(BF16)
    rest_cols = w_rest.shape[-1]
    lb_all = jnp.cumsum(jax.nn.softmax(hg_lb_logits.astype(F32), axis=0), axis=0)
    kvs = (2, NSA_KV_HEADS, HEAD_DIM)

    states = []
    for l in range(depth):
        h = rmsnorm(x, attn_norm[l], BF16, TM)
        u_head = matmul([h], [w_in], l, 0, HEAD_COLS, "plain", F32, tn=HEAD_COLS // 3)
        u_rest = matmul([h], [w_rest], l, 0, rest_cols, "plain", F32, tn=1024)

        wk, pk = compress_params(cmp_pe_k[l], cmp_w1_k[l])
        wv, pv = compress_params(cmp_pe_v[l], cmp_w1_v[l])
        wcat, pecat = jnp.stack([wk, wv]), jnp.stack([pk, pv])
        w2s = jnp.stack([cmp_w2_k[l], cmp_w2_v[l]])
        lb = (lb_all[l] - lb_all[0]).reshape(HG_HEADS, HEAD_DIM)
        lbp = hgrn_lb_params(lb)

        kvc = compress_finish(compress_segments_prompt(u_head, bsz, seq, wcat, pecat), w2s)
        o_cmp, sel = cmp_select(u_head, 0, bsz, seq, kvc, seq // CMP_STRIDE - 1, -(-seq // SEL_BLOCK), 0)
        o_sel = flash_prompt(u_head, bsz, seq, "sel", sel=sel)
        o_win = win_prompt(u_head, bsz, seq)
        nsa_p = nsa_combine(u_head, 0, o_cmp, o_sel, o_win)
        sb_p = sb_prompt(u_rest, bsz, seq)
        hg_p, s_p = hgrn(u_rest, 0, bsz, seq, lbp, hg_norm[l], jnp.zeros((bsz, HG_HEADS, HEAD_DIM, HEAD_DIM), F32))

        kvc_s = compress_finish(compress_segments_sample(cmp_pool, l, pt_flat, db, n_pages, wcat, pecat), w2s)
        o_cmp_s, sel_s = cmp_select(u_head, n_p, db, ds, kvc_s, past // CMP_STRIDE - 1,
                                    -(-(past + ds) // SEL_BLOCK), past, tq=ds)
        o_sel_s = sel_sample(u_head, n_p, sel_pool, l, pt_flat, sel_s, db, ds, n_pages)
        o_win_s = win_sample(u_head, n_p, win_buf, l, db, ds)
        nsa_s = nsa_combine(u_head, n_p, o_cmp_s, o_sel_s, o_win_s)
        sb_s = sb_sample(u_rest, n_p, sb_pool, l, pt_flat, db, ds, n_pages)
        hg_s, s_s = hgrn(u_rest, n_p, db, ds, lbp, hg_norm[l], state_hgrn[l])

        o_nsa = _slab([nsa_p, nsa_s], m, BF16)
        o_sb = _slab([sb_p, sb_s], m, BF16)
        o_hg = _slab([hg_p, hg_s], m, BF16)
        merged = matmul([o_nsa, o_sb, o_hg], [w_br_nsa, w_br_sb, w_br_hg], l, 0, d, "merge", BF16,
                        extras=[(u_rest, R_MERGE), (u_rest, R_MERGE + d), (u_rest, R_MERGE + 2 * d)])
        x = matmul([merged], [w_out], l, 0, d, "residual", F32, extras=[(x, 0)])

        i = l // 2
        if l % 2 == 0:
            h2 = rmsnorm(x, ffn_norm[l], BF16, TM)
            act = matmul([h2], [ffn_w1, ffn_w3], i, 0, ffn_w1.shape[-1], "swiglu", BF16)
            x = matmul([act], [ffn_w2], i, 0, d, "residual", F32, extras=[(x, 0)])
        else:
            x = moe_layer(x, ffn_norm[l], router_w[i], router_b[i], moe_w1, moe_w3, moe_w2, i, n_tok, TM)

        def head_cols(off, r0, r1, lead):
            return u_head[r0:r1, off:off + KV_W].reshape(lead + kvs)

        win_p = head_cols(OFF_WIN, 0, n_p, (bsz, seq))[:, seq - min(WINDOW, seq):]
        win_s = jnp.concatenate([cache_nsa_win_kv[l], head_cols(OFF_WIN, n_p, n_tok, (db, ds))], axis=1)[:, ds:]
        states.append((
            head_cols(OFF_CMP, 0, n_p, (bsz, seq)), head_cols(OFF_SEL, 0, n_p, (bsz, seq)), win_p,
            u_rest[:n_p, R_SB_K:R_SB_K + 2 * SB_W].reshape(bsz, seq, 2, SB_HEADS, HEAD_DIM), s_p,
            head_cols(OFF_CMP, n_p, n_tok, (db, ds)), head_cols(OFF_SEL, n_p, n_tok, (db, ds)), win_s,
            u_rest[n_p:n_tok, R_SB_K:R_SB_K + 2 * SB_W].reshape(db, ds, 2, SB_HEADS, HEAD_DIM), s_s))

    y = rmsnorm(x, final_norm, F32, TM)
    stacked = [jnp.stack([st[i] for st in states]) for i in range(10)]
    return (y[:n_p].reshape(bsz, seq, d), y[n_p:n_tok].reshape(db, ds, d), *stacked)
```

```python
import functools

import jax
import jax.numpy as jnp
import numpy as np
from jax import lax
from jax.experimental import pallas as pl
from jax.experimental.pallas import tpu as pltpu

F32 = jnp.float32
BF16 = jnp.bfloat16

HEAD_DIM = 128
SCALE = HEAD_DIM ** -0.5
NSA_HEADS = 8
NSA_KV_HEADS = 2
NSA_GROUP = NSA_HEADS // NSA_KV_HEADS
CMP_LEN = 32
CMP_STRIDE = 16
SEL_BLOCK = 64
SEL_TOPK = 16
WINDOW = 512
SB_HEADS = 8
HG_HEADS = 8
HG_CHUNK = 64
HG_SUB = 16
N_EXPERTS = 8
TOP_K = 2
MOE_BLOCK = 512
NORM_EPS = 1e-6
NEG_BIG = -1e30

LANES = 128
SUBLANES = 8
VMEM_LIMIT = 56 * 1024 * 1024

NSA_W = NSA_HEADS * HEAD_DIM
KV_W = 2 * NSA_KV_HEADS * HEAD_DIM
OFF_CMP = NSA_W
OFF_SEL = OFF_CMP + KV_W
OFF_WIN = OFF_SEL + KV_W
OFF_NSA_GATE = OFF_WIN + KV_W
N_GATE = 3 * NSA_HEADS
HEAD_COLS = OFF_NSA_GATE + LANES
REST_START = OFF_NSA_GATE + N_GATE
SB_W = SB_HEADS * HEAD_DIM
HG_W = HG_HEADS * HEAD_DIM
R_SB_Q = 0
R_SB_K = R_SB_Q + SB_W
R_SB_V = R_SB_K + SB_W
R_HG_Q = R_SB_V + SB_W
R_HG_F = R_HG_Q + HG_W
R_HG_I = R_HG_F + HG_W
R_HG_G = R_HG_I + HG_W
R_MERGE = R_HG_G + HG_W


def _cp(n_axes, vmem=VMEM_LIMIT):
    return pltpu.CompilerParams(dimension_semantics=("arbitrary",) * n_axes, vmem_limit_bytes=vmem)


def _tile(n, pref, quantum=LANES):
    if n <= pref:
        return n
    t = (pref // quantum) * quantum
    while t > quantum and n % t:
        t -= quantum
    assert n % t == 0, (n, pref)
    return t


def _dot(a, b):
    return jnp.dot(a, b, preferred_element_type=F32)


def _dot_nt(a, b):
    return lax.dot_general(a, b, (((1,), (1,)), ((), ())), preferred_element_type=F32)


def _dot_tn(a, b):
    return lax.dot_general(a, b, (((0,), (0,)), ((), ())), preferred_element_type=F32)


def _split3(x):
    hi = x.astype(BF16)
    r = x - hi.astype(F32)
    mid = r.astype(BF16)
    lo = (r - mid.astype(F32)).astype(BF16)
    return hi, mid, lo


def _split2(x):
    hi = x.astype(BF16)
    return hi, (x - hi.astype(F32)).astype(BF16)


def _sigmoid(x):
    return 1.0 / (1.0 + jnp.exp(-x))


def _silu(x):
    return x * _sigmoid(x)


def _log_sigmoid(x):
    return jnp.minimum(x, 0.0) - jnp.log1p(jnp.exp(-jnp.abs(x)))


def _rmsnorm_kernel(x_ref, g_ref, o_ref):
    x = x_ref[...]
    y = x * lax.rsqrt(jnp.mean(x * x, axis=-1, keepdims=True) + NORM_EPS)
    o_ref[...] = (y * g_ref[...]).astype(o_ref.dtype)


def rmsnorm(x, g, out_dtype, tm):
    m, d = x.shape
    return pl.pallas_call(
        _rmsnorm_kernel,
        grid=(m // tm,),
        in_specs=[pl.BlockSpec((tm, d), lambda i: (i, 0)), pl.BlockSpec((1, d), lambda i: (0, 0))],
        out_specs=pl.BlockSpec((tm, d), lambda i: (i, 0)),
        out_shape=jax.ShapeDtypeStruct((m, d), out_dtype),
        compiler_params=_cp(1),
        name="rmsnorm",
    )(x, g.reshape(1, d))


def _mm_epilogue(mode, prods, x_refs):
    if mode == "plain":
        return prods[0]
    if mode == "merge":
        out = _sigmoid(x_refs[0][...]) * prods[0]
        for x_ref, p in zip(x_refs[1:], prods[1:]):
            out = out + _sigmoid(x_ref[...]) * p
        return out
    if mode == "residual":
        return x_refs[0][...] + prods[0]
    return _silu(prods[0]) * prods[1]


def _mm_kernel(*refs, n_a, n_w, n_extra, mode, cast):
    a_refs = refs[:n_a]
    w_refs = refs[n_a:n_a + n_w]
    x_refs = refs[n_a + n_w:n_a + n_w + n_extra]
    o_ref = refs[n_a + n_w + n_extra]
    wbf_refs = refs[n_a + n_w + n_extra + 1:]

    if cast:
        @pl.when(pl.program_id(1) == 0)
        def _():
            for w_ref, wbf_ref in zip(w_refs, wbf_refs):
                wbf_ref[...] = w_ref[...].astype(BF16)
        ws = [r[...] for r in wbf_refs]
    else:
        ws = [r[...] for r in w_refs]

    prods = [_dot(a_refs[min(i, n_a - 1)][...], w) for i, w in enumerate(ws)]
    o_ref[...] = _mm_epilogue(mode, prods, x_refs).astype(o_ref.dtype)


def matmul(a_list, w_list, w_index, col0, n_out, mode, out_dtype, extras=(), tm=512, tn=512):
    m = a_list[0].shape[0]
    tn = _tile(n_out, tn)
    assert col0 % tn == 0 and m % tm == 0
    cast = w_list[0].dtype != BF16
    in_specs, args, scratch = [], [], []
    for a in a_list:
        in_specs.append(pl.BlockSpec((tm, a.shape[1]), lambda j, i: (i, 0)))
        args.append(a)
    for w in w_list:
        k = w.shape[-2]
        if w.ndim == 3:
            in_specs.append(pl.BlockSpec((None, k, tn), lambda j, i: (w_index, 0, j + col0 // tn)))
        else:
            in_specs.append(pl.BlockSpec((k, tn), lambda j, i: (0, j + col0 // tn)))
        args.append(w)
        if cast:
            scratch.append(pltpu.VMEM((k, tn), BF16))
    for x, off in extras:
        assert off % tn == 0
        in_specs.append(pl.BlockSpec((tm, tn), lambda j, i, off=off: (i, j + off // tn)))
        args.append(x)
    return pl.pallas_call(
        functools.partial(_mm_kernel, n_a=len(a_list), n_w=len(w_list), n_extra=len(extras), mode=mode, cast=cast),
        grid=(n_out // tn, m // tm),
        in_specs=in_specs,
        out_specs=pl.BlockSpec((tm, tn), lambda j, i: (i, j)),
        out_shape=jax.ShapeDtypeStruct((m, n_out), out_dtype),
        scratch_shapes=scratch,
        compiler_params=_cp(2),
        name="mm_" + mode,
    )(*args)


def _inproj_rest_kernel(a_ref, wa_ref, wb_ref, o_ref, wbf_ref, *, shift, tn):
    @pl.when(pl.program_id(1) == 0)
    def _():
        w = jnp.concatenate([wa_ref[...], wb_ref[...]], axis=1)
        wbf_ref[...] = w[:, shift:shift + tn].astype(BF16)

    o_ref[...] = _dot(a_ref[...], wbf_ref[...])


def inproj_rest(a, w_in, layer, n_out, tm, tn=512):
    m, k = a.shape
    shift = REST_START % LANES
    base = REST_START - shift
    tn = _tile(n_out, tn)
    assert base % tn == 0 and m % tm == 0 and REST_START + n_out == w_in.shape[-1]
    return pl.pallas_call(
        functools.partial(_inproj_rest_kernel, shift=shift, tn=tn),
        grid=(n_out // tn, m // tm),
        in_specs=[pl.BlockSpec((tm, k), lambda j, i: (i, 0)),
                  pl.BlockSpec((None, k, tn), lambda j, i: (layer, 0, base // tn + j)),
                  pl.BlockSpec((None, k, LANES), lambda j, i: (layer, 0, (base + (j + 1) * tn) // LANES))],
        out_specs=pl.BlockSpec((tm, tn), lambda j, i: (i, j)),
        out_shape=jax.ShapeDtypeStruct((m, n_out), F32),
        scratch_shapes=[pltpu.VMEM((k, tn), BF16)],
        compiler_params=_cp(2),
        name="inproj_rest",
    )(a, w_in, w_in)


def _suffix_matrix():
    j = lax.broadcasted_iota(jnp.int32, (LANES, 2 * LANES), 0)
    s = lax.broadcasted_iota(jnp.int32, (LANES, 2 * LANES), 1)
    return jnp.where((j > s) | (s >= LANES), 1.0, 0.0).astype(BF16)


def _sb_weights(z, mask, carry, umat):
    t = jnp.log(1.0 + jnp.exp(-jnp.abs(z)))
    ls_pos = jnp.minimum(z, 0.0) - t
    c = ls_pos - z
    if mask is not None:
        c = jnp.where(mask, c, 0.0)
    n_sub = z.shape[1] // LANES
    pieces = [None] * n_sub
    for sb in reversed(range(n_sub)):
        sl = slice(sb * LANES, (sb + 1) * LANES)
        hi, lo = _split2(c[:, sl])
        r = _dot(hi, umat) + _dot(lo, umat)
        w = jnp.exp(ls_pos[:, sl] + (carry + r[:, :LANES]))
        pieces[sb] = w if mask is None else jnp.where(mask[:, sl], w, 0.0)
        carry = carry + r[:, LANES:]
    a = pieces[0] if n_sub == 1 else jnp.concatenate(pieces, axis=1)
    return a, carry


SB_ZERO_LOG = -104.0


def _sb_prompt_kernel(q_ref, k_ref, v_ref, o_ref, acc_ref, carry_ref, *, t):
    qi = pl.program_id(2)
    umat = _suffix_matrix()
    q = (q_ref[...] * SCALE).astype(BF16)
    row = lax.broadcasted_iota(jnp.int32, (t, t), 0)
    col = lax.broadcasted_iota(jnp.int32, (t, t), 1)

    def tile(kb, mask):
        rows = pl.ds(pl.multiple_of(kb * t, t), t)
        z = _dot_nt(q, k_ref[rows, :].astype(BF16))
        a, carry = _sb_weights(z, mask, carry_ref[...], umat)
        acc_ref[...] += _dot(a.astype(BF16), v_ref[rows, :].astype(BF16))
        carry_ref[...] = carry
        return jnp.max(carry)

    acc_ref[...] = jnp.zeros_like(acc_ref)
    carry_ref[...] = jnp.zeros_like(carry_ref)
    top = tile(qi, col < row)

    def cond(st):
        return jnp.logical_and(st[0] >= 0, st[1] > SB_ZERO_LOG)

    def body(st):
        return st[0] - 1, tile(st[0], None)

    lax.while_loop(cond, body, (qi - 1, top))
    o_ref[...] = acc_ref[...].astype(o_ref.dtype)


def sb_prompt(u_rest, batch, seq, t=256):
    t = min(t, seq)
    nq = seq // t
    cq, ck, cv = R_SB_Q // LANES, R_SB_K // LANES, R_SB_V // LANES
    return pl.pallas_call(
        functools.partial(_sb_prompt_kernel, t=t),
        grid=(batch, SB_HEADS, nq),
        in_specs=[pl.BlockSpec((t, LANES), lambda b, h, i: (b * nq + i, cq + h)),
                  pl.BlockSpec((seq, LANES), lambda b, h, i: (b, ck + h)),
                  pl.BlockSpec((seq, LANES), lambda b, h, i: (b, cv + h))],
        out_specs=pl.BlockSpec((t, LANES), lambda b, h, i: (b * nq + i, h)),
        out_shape=jax.ShapeDtypeStruct((batch * seq, SB_W), BF16),
        scratch_shapes=[pltpu.VMEM((t, LANES), F32), pltpu.VMEM((t, LANES), F32)],
        compiler_params=_cp(3),
        name="sb_prompt",
    )(u_rest, u_rest, u_rest)


def _hgrn_chunk(qr, fr, v, gr, lbp, norm_w, st, c, sub, c_real):
    log_lb, log_1m_lb, one_m_lb = lbp[0:1], lbp[1:2], lbp[2:3]
    q = _silu(qr)
    k = one_m_lb * _sigmoid(-fr)
    bb = log_1m_lb + _log_sigmoid(fr)
    mx = jnp.maximum(log_lb, bb)
    logf = mx + jnp.log1p(jnp.exp(-jnp.abs(log_lb - bb)))
    row = lax.broadcasted_iota(jnp.int32, (c, c), 0)
    col = lax.broadcasted_iota(jnp.int32, (c, c), 1)
    tri = jnp.where(row >= col, 1.0, 0.0).astype(BF16)
    hi, mid, lo = _split3(logf)
    b = _dot(tri, hi) + _dot(tri, mid) + _dot(tri, lo)
    o = _dot_nt((q * jnp.exp(b)).astype(BF16), st.astype(BF16))
    ridx = lax.broadcasted_iota(jnp.int32, (c, HEAD_DIM), 0)
    lane = lax.broadcasted_iota(jnp.int32, (sub, c), 1)
    att_rows = []
    for i in range(c // sub):
        r0 = i * sub
        b_i = b[r0:r0 + sub]
        q_i = q[r0:r0 + sub]
        att_i = jnp.zeros((sub, c), F32)
        if i > 0:
            rho = b_i[0:1]
            earlier = ridx < r0
            k_dec = jnp.where(earlier, k * jnp.exp(jnp.where(earlier, rho - b, 0.0)), 0.0)
            att_i = _dot_nt((q_i * jnp.exp(b_i - rho)).astype(BF16), k_dec.astype(BF16))
        trow = lax.broadcasted_iota(jnp.int32, (sub, 1), 0)
        for s in range(sub):
            d = q_i * jnp.exp(jnp.where(trow >= s, b_i - b_i[s:s + 1], 0.0)) * k[r0 + s:r0 + s + 1]
            colsum = jnp.sum(d, axis=1, keepdims=True)
            att_i = att_i + jnp.where((lane == r0 + s) & (trow >= s), colsum, 0.0)
        att_rows.append(att_i)
    att = att_rows[0] if len(att_rows) == 1 else jnp.concatenate(att_rows, axis=0)
    o = o + _dot(att.astype(BF16), v.astype(BF16))
    b_end = b[c_real - 1:c_real]
    real = ridx < c_real
    k_end = jnp.where(real, k * jnp.exp(jnp.where(real, b_end - b, 0.0)), 0.0)
    st = st * jnp.exp(b_end) + _dot_tn(v.astype(BF16), k_end.astype(BF16))
    o = o * lax.rsqrt(jnp.mean(o * o, axis=-1, keepdims=True) + NORM_EPS) * norm_w
    return o * _silu(gr), st


def _hgrn_kernel(q_ref, f_ref, i_ref, g_ref, lbp_ref, nw_ref, s0_ref, o_ref, s_out_ref, st_ref, *,
                 c, sub, n_chunks, hb):
    t = pl.program_id(2)

    @pl.when(t == 0)
    def _():
        for j in range(hb):
            st_ref[j] = s0_ref[j].T

    nw = nw_ref[...]
    in_refs = (q_ref, f_ref, i_ref, g_ref)

    if c < HG_SUB:
        pad = jnp.zeros((HG_SUB - c, HEAD_DIM), F32)
        for j in range(hb):
            cols = slice(j * HEAD_DIM, (j + 1) * HEAD_DIM)
            ins = [jnp.concatenate([r[:, cols], pad], axis=0) for r in in_refs]
            o, st = _hgrn_chunk(*ins, lbp_ref[j], nw, st_ref[j], HG_SUB, HG_SUB, c)
            st_ref[j] = st
            o_ref[:, cols] = o[:c].astype(o_ref.dtype)
    else:
        def body(ci, carry):
            rows = pl.ds(pl.multiple_of(ci * c, c), c)
            for j in range(hb):
                cols = slice(j * HEAD_DIM, (j + 1) * HEAD_DIM)
                o, st = _hgrn_chunk(*[r[rows, cols] for r in in_refs], lbp_ref[j], nw, st_ref[j], c, sub, c)
                st_ref[j] = st
                o_ref[rows, cols] = o.astype(o_ref.dtype)
            return carry

        lax.fori_loop(0, n_chunks, body, 0)

    @pl.when(t == pl.num_programs(2) - 1)
    def _():
        for j in range(hb):
            s_out_ref[j] = st_ref[j].T


def hgrn_lb_params(lb):
    rows = jnp.stack([jnp.log(lb), jnp.log1p(-lb), 1.0 - lb], axis=1)
    return jnp.concatenate([rows, jnp.zeros((lb.shape[0], SUBLANES - 3, lb.shape[1]), F32)], axis=1)


def hgrn(u_rest, row0, batch, seq, lbp, norm_w, s0, tl=512, hb=8):
    c = min(HG_CHUNK, seq)
    sub = min(HG_SUB, c)
    tl = min(tl, seq)
    nt = seq // tl
    rb0 = row0 // tl
    bw = hb * HEAD_DIM
    assert row0 % tl == 0 and seq % tl == 0 and tl % c == 0 and (c % HG_SUB == 0 or nt == 1) and HG_HEADS % hb == 0

    def col(off):
        return pl.BlockSpec((tl, bw), lambda b, h, t, off=off: (rb0 + b * nt + t, off // bw + h))

    o, s_out = pl.pallas_call(
        functools.partial(_hgrn_kernel, c=c, sub=sub, n_chunks=tl // c, hb=hb),
        grid=(batch, HG_HEADS // hb, nt),
        in_specs=[col(R_HG_Q), col(R_HG_F), col(R_HG_I), col(R_HG_G),
                  pl.BlockSpec((hb, SUBLANES, LANES), lambda b, h, t: (h, 0, 0)),
                  pl.BlockSpec((1, LANES), lambda b, h, t: (0, 0)),
                  pl.BlockSpec((None, hb, HEAD_DIM, HEAD_DIM), lambda b, h, t: (b, h, 0, 0))],
        out_specs=[pl.BlockSpec((tl, bw), lambda b, h, t: (b * nt + t, h)),
                   pl.BlockSpec((None, hb, HEAD_DIM, HEAD_DIM), lambda b, h, t: (b, h, 0, 0))],
        out_shape=[jax.ShapeDtypeStruct((batch * seq, HG_W), F32),
                   jax.ShapeDtypeStruct((batch, HG_HEADS, HEAD_DIM, HEAD_DIM), F32)],
        scratch_shapes=[pltpu.VMEM((hb, HEAD_DIM, HEAD_DIM), F32)],
        compiler_params=_cp(3),
        name="hgrn",
    )(u_rest, u_rest, u_rest, u_rest, lbp, norm_w.reshape(1, LANES), s0)
    return o, s_out


SEG_W = CMP_STRIDE * HEAD_DIM


def compress_params(pe, w1):
    w = jnp.concatenate([w1[:SEG_W], w1[SEG_W:]], axis=1).astype(BF16)
    rows = jnp.stack([pe[:CMP_STRIDE].reshape(SEG_W), pe[CMP_STRIDE:].reshape(SEG_W)])
    return w, jnp.concatenate([rows, jnp.zeros((SUBLANES - 2, SEG_W), F32)]).astype(BF16)


def _segment_products(r, w, pe):
    bias = _dot(pe, w)
    bias = jnp.concatenate([bias[0:1, :LANES], bias[1:2, LANES:]], axis=1)
    return _dot(r.astype(BF16), w) + bias


def _cmp1_prompt_kernel(x_ref, w_ref, pe_ref, o_ref, r_ref, *, n_seg):
    for l in range(CMP_STRIDE):
        r_ref[:, l * LANES:(l + 1) * LANES] = x_ref[pl.ds(l, n_seg, stride=CMP_STRIDE), :]
    o_ref[...] = _segment_products(r_ref[...], w_ref[...], pe_ref[...])


def compress_segments_prompt(u_head, batch, seq, wcat, pecat, tr=512):
    tr = min(tr, seq)
    n_seg = tr // CMP_STRIDE
    nt = seq // tr
    c0 = OFF_CMP // LANES
    return pl.pallas_call(
        functools.partial(_cmp1_prompt_kernel, n_seg=n_seg),
        grid=(batch, 2, NSA_KV_HEADS, nt),
        in_specs=[pl.BlockSpec((tr, LANES), lambda b, kv, g, t: (b * nt + t, c0 + kv * NSA_KV_HEADS + g)),
                  pl.BlockSpec((None, SEG_W, 2 * LANES), lambda b, kv, g, t: (kv, 0, 0)),
                  pl.BlockSpec((None, SUBLANES, SEG_W), lambda b, kv, g, t: (kv, 0, 0))],
        out_specs=pl.BlockSpec((None, None, None, n_seg, 2 * LANES), lambda b, kv, g, t: (b, kv, g, t, 0)),
        out_shape=jax.ShapeDtypeStruct((batch, 2, NSA_KV_HEADS, seq // CMP_STRIDE, 2 * LANES), F32),
        scratch_shapes=[pltpu.VMEM((n_seg, SEG_W), F32)],
        compiler_params=_cp(4),
        name="cmp_segments_prompt",
    )(u_head, wcat, pecat)


def _cmp2_kernel(pq_ref, w2_ref, o_ref, *, nc):
    pq = pq_ref[...]
    q_next = pltpu.roll(pq[:, LANES:], shift=nc - 1, axis=0)
    hid = _silu(pq[:, :LANES] + q_next)
    out = _dot(hid.astype(BF16), w2_ref[...].astype(BF16))
    row = lax.broadcasted_iota(jnp.int32, out.shape, 0)
    o_ref[...] = jnp.where(row < nc - 1, out, 0.0)


def compress_finish(pq, w2):
    batch, _, _, nc, _ = pq.shape
    return pl.pallas_call(
        functools.partial(_cmp2_kernel, nc=nc),
        grid=(batch, 2, NSA_KV_HEADS),
        in_specs=[pl.BlockSpec((None, None, None, nc, 2 * LANES), lambda b, kv, g: (b, kv, g, 0, 0)),
                  pl.BlockSpec((None, HEAD_DIM, HEAD_DIM), lambda b, kv, g: (kv, 0, 0))],
        out_specs=pl.BlockSpec((None, None, None, nc, LANES), lambda b, kv, g: (b, kv, g, 0, 0)),
        out_shape=jax.ShapeDtypeStruct((batch, 2, NSA_KV_HEADS, nc, LANES), F32),
        compiler_params=_cp(3),
        name="cmp_finish",
    )(pq, w2)


def _cmp_select_kernel(q_ref, kc_ref, vc_ref, o_ref, sel_ref, score_ref, *, tq, tqp, nc, n_cmp, n_slc, nsp, pos0):
    qi = pl.program_id(2)
    qb = q_ref[...]
    parts = []
    for h in range(NSA_GROUP):
        qh = qb[:, h * HEAD_DIM:(h + 1) * HEAD_DIM]
        if tqp > tq:
            qh = jnp.concatenate([qh, jnp.zeros((tqp - tq, HEAD_DIM), F32)], axis=0)
        parts.append(qh)
    q4 = jnp.concatenate(parts, axis=0).astype(BF16)
    rows = NSA_GROUP * tqp
    st = _dot_nt(kc_ref[...].astype(BF16), q4) * SCALE
    ci = lax.broadcasted_iota(jnp.int32, (nc, rows), 0)
    tok = lax.broadcasted_iota(jnp.int32, (nc, rows), 1) & (tqp - 1)
    tpos = pos0 + qi * tq + tok
    valid = (ci < n_cmp) & (ci * CMP_STRIDE + CMP_LEN - 1 <= tpos)
    st = jnp.where(valid, st, NEG_BIG)
    m = jnp.max(st, axis=0, keepdims=True)
    e = jnp.where(valid, jnp.exp(st - m), 0.0)
    den = jnp.sum(e, axis=0, keepdims=True)
    pt = e / jnp.where(den > 0, den, 1.0)
    o = _dot_tn(pt.astype(BF16), vc_ref[...].astype(BF16))
    for h in range(NSA_GROUP):
        o_ref[:, h * HEAD_DIM:(h + 1) * HEAD_DIM] = o[h * tqp:h * tqp + tq]
    psum = pt[:, 0:tqp]
    for h in range(1, NSA_GROUP):
        psum = psum + pt[:, h * tqp:(h + 1) * tqp]
    jj = lax.broadcasted_iota(jnp.int32, (nsp, nc), 0)
    ii = lax.broadcasted_iota(jnp.int32, (nsp, nc), 1)
    cover = ((ii * CMP_STRIDE < jj * SEL_BLOCK + SEL_BLOCK) & (ii * CMP_STRIDE + CMP_LEN - 1 >= jj * SEL_BLOCK)
             & (ii < n_cmp))
    cover = jnp.where(cover, 1.0, 0.0).astype(BF16)
    hi, mid, lo = _split3(psum)
    imp = _dot(cover, hi) + _dot(cover, mid) + _dot(cover, lo)
    j = lax.broadcasted_iota(jnp.int32, (nsp, tqp), 0)
    tpos2 = pos0 + qi * tq + lax.broadcasted_iota(jnp.int32, (nsp, tqp), 1)
    cur = tpos2 // SEL_BLOCK
    forced = (j == 0) | (j == cur) | (j == cur - 1)
    ok = (j * SEL_BLOCK <= tpos2) & (j < n_slc)
    score = jnp.where(ok, jnp.where(forced, NSA_GROUP + 1.0, imp), -1.0)
    score_ref[...] = score

    def body(jp, rank):
        other = score_ref[pl.ds(jp, 1), :]
        ahead = jnp.where(other > score, 1.0, jnp.where((other == score) & (jp < j), 1.0, 0.0))
        return rank + ahead

    rank = lax.fori_loop(0, n_slc, body, jnp.zeros((nsp, tqp), F32))
    sel_t = jnp.where((rank < SEL_TOPK) & (score >= 0), 1.0, 0.0)
    sel_ref[...] = sel_t.T[:tq]


def cmp_select(u_head, row0, batch, seq, kv_cmp, n_cmp, n_slc, pos0, tq=256):
    tq = min(tq, seq)
    tqp = max(tq, LANES)
    nt = seq // tq
    rb0 = row0 // tq
    nc = kv_cmp.shape[3]
    nsp = -(-n_slc // LANES) * LANES
    gw = NSA_GROUP * HEAD_DIM
    assert row0 % tq == 0 and seq % tq == 0 and tqp & (tqp - 1) == 0
    return pl.pallas_call(
        functools.partial(_cmp_select_kernel, tq=tq, tqp=tqp, nc=nc, n_cmp=n_cmp, n_slc=n_slc, nsp=nsp, pos0=pos0),
        grid=(batch, NSA_KV_HEADS, nt),
        in_specs=[pl.BlockSpec((tq, gw), lambda b, g, t: (rb0 + b * nt + t, g)),
                  pl.BlockSpec((None, None, None, nc, LANES), lambda b, g, t: (b, 0, g, 0, 0)),
                  pl.BlockSpec((None, None, None, nc, LANES), lambda b, g, t: (b, 1, g, 0, 0))],
        out_specs=[pl.BlockSpec((tq, gw), lambda b, g, t: (b * nt + t, g)),
                   pl.BlockSpec((None, None, tq, nsp), lambda b, g, t: (b, g, t, 0))],
        out_shape=[jax.ShapeDtypeStruct((batch * seq, NSA_W), F32),
                   jax.ShapeDtypeStruct((batch, NSA_KV_HEADS, seq, nsp), F32)],
        scratch_shapes=[pltpu.VMEM((nsp, tqp), F32)],
        compiler_params=_cp(3),
        name="cmp_select",
    )(u_head, kv_cmp, kv_cmp)


def _flash_step(s, mask, v, m_ref, l_ref, acc_ref):
    s = jnp.where(mask, s, NEG_BIG)
    m_prev = m_ref[...]
    m_new = jnp.maximum(m_prev, jnp.max(s, axis=1, keepdims=True))
    e = jnp.where(mask, jnp.exp(s - m_new), 0.0)
    alpha = jnp.exp(m_prev - m_new)
    l_ref[...] = alpha * l_ref[...] + jnp.sum(e, axis=1, keepdims=True)
    acc_ref[...] = alpha * acc_ref[...] + _dot(e.astype(BF16), v)
    m_ref[...] = m_new


def _flash_init(m_ref, l_ref, acc_ref):
    m_ref[...] = jnp.full_like(m_ref, NEG_BIG)
    l_ref[...] = jnp.zeros_like(l_ref)
    acc_ref[...] = jnp.zeros_like(acc_ref)


def _flash_result(l_ref, acc_ref):
    l = l_ref[...]
    return acc_ref[...] / jnp.where(l > 0, l, 1.0)


def _stack_heads(qb, pad_to=None):
    parts = []
    for h in range(NSA_GROUP):
        qh = qb[:, h * HEAD_DIM:(h + 1) * HEAD_DIM]
        if pad_to is not None and pad_to > qh.shape[0]:
            qh = jnp.concatenate([qh, jnp.zeros((pad_to - qh.shape[0], HEAD_DIM), qh.dtype)], axis=0)
        parts.append(qh)
    return jnp.concatenate(parts, axis=0)


def _flash_prompt_kernel(qi_tab, kb_tab, first_tab, last_tab, *refs, tq, tk, mode, nsp):
    if mode == "sel":
        q_ref, k_ref, v_ref, sel_ref, o_ref, m_ref, l_ref, acc_ref = refs
    else:
        q_ref, k_ref, v_ref, o_ref, m_ref, l_ref, acc_ref = refs
    p = pl.program_id(2)
    qi = qi_tab[p]
    kb = kb_tab[p]

    @pl.when(first_tab[p] == 1)
    def _():
        _flash_init(m_ref, l_ref, acc_ref)

    q4 = _stack_heads(q_ref[...]).astype(BF16)
    s = _dot_nt(q4, k_ref[...].astype(BF16)) * SCALE
    qpos = qi * tq + lax.broadcasted_iota(jnp.int32, (tq, tk), 0)
    kpos = kb * tk + lax.broadcasted_iota(jnp.int32, (tq, tk), 1)
    if mode == "win":
        d = qpos - kpos
        mask = (d >= 0) & (d < WINDOW)
    else:
        jj = lax.broadcasted_iota(jnp.int32, (nsp, tk), 0)
        kk = lax.broadcasted_iota(jnp.int32, (nsp, tk), 1)
        expand = jnp.where(jj == kb * (tk // SEL_BLOCK) + kk // SEL_BLOCK, 1.0, 0.0).astype(BF16)
        chosen = _dot(sel_ref[...].astype(BF16), expand)
        mask = (chosen > 0.5) & (kpos <= qpos)
    mask4 = jnp.concatenate([mask] * NSA_GROUP, axis=0)
    _flash_step(s, mask4, v_ref[...].astype(BF16), m_ref, l_ref, acc_ref)

    @pl.when(last_tab[p] == 1)
    def _():
        o = _flash_result(l_ref, acc_ref)
        for h in range(NSA_GROUP):
            o_ref[:, h * HEAD_DIM:(h + 1) * HEAD_DIM] = o[h * tq:(h + 1) * tq]


def flash_prompt(u_head, batch, seq, mode, sel=None, tq=256, tk=None):
    tq = min(tq, seq)
    tk = min(tk or (256 if mode == "win" else 512), seq)
    nq, nk = seq // tq, seq // tk
    pairs = []
    for qi in range(nq):
        if mode == "win":
            lo = max(0, (qi * tq - WINDOW + 1) // tk)
        else:
            lo = 0
        hi = (qi * tq + tq - 1) // tk
        kbs = list(range(lo, hi + 1))
        pairs += [(qi, kb, int(kb == kbs[0]), int(kb == kbs[-1])) for kb in kbs]
    tabs = [jnp.asarray([p[i] for p in pairs], jnp.int32) for i in range(4)]
    off = OFF_WIN if mode == "win" else OFF_SEL
    ck = off // LANES
    cv = ck + NSA_KV_HEADS
    gw = NSA_GROUP * HEAD_DIM
    in_specs = [
        pl.BlockSpec((tq, gw), lambda b, g, p, qt, kt, ft, lt: (b * nq + qt[p], g)),
        pl.BlockSpec((tk, LANES), lambda b, g, p, qt, kt, ft, lt: (b * nk + kt[p], ck + g)),
        pl.BlockSpec((tk, LANES), lambda b, g, p, qt, kt, ft, lt: (b * nk + kt[p], cv + g)),
    ]
    args = [u_head, u_head, u_head]
    nsp = 0
    if mode == "sel":
        nsp = sel.shape[-1]
        in_specs.append(pl.BlockSpec((None, None, tq, nsp), lambda b, g, p, qt, kt, ft, lt: (b, g, qt[p], 0)))
        args.append(sel)
    rows = NSA_GROUP * tq
    grid_spec = pltpu.PrefetchScalarGridSpec(
        num_scalar_prefetch=4,
        grid=(batch, NSA_KV_HEADS, len(pairs)),
        in_specs=in_specs,
        out_specs=pl.BlockSpec((tq, gw), lambda b, g, p, qt, kt, ft, lt: (b * nq + qt[p], g)),
        scratch_shapes=[pltpu.VMEM((rows, 1), F32), pltpu.VMEM((rows, 1), F32), pltpu.VMEM((rows, HEAD_DIM), F32)],
    )
    return pl.pallas_call(
        functools.partial(_flash_prompt_kernel, tq=tq, tk=tk, mode=mode, nsp=nsp),
        grid_spec=grid_spec,
        out_shape=jax.ShapeDtypeStruct((batch * seq, NSA_W), F32),
        compiler_params=_cp(3),
        name="flash_" + mode,
    )(*tabs, *args)


def _sel_prompt_kernel(q_ref, k_ref, v_ref, sel_ref, o_ref, m_ref, l_ref, acc_ref, *, tq, tk, nsp):
    qi = pl.program_id(2)
    q4 = (_stack_heads(q_ref[...]) * SCALE).astype(BF16)
    selb = sel_ref[...].astype(BF16)
    _flash_init(m_ref, l_ref, acc_ref)
    row = lax.broadcasted_iota(jnp.int32, (tq, tk), 0)
    col = lax.broadcasted_iota(jnp.int32, (tq, tk), 1)
    jj = lax.broadcasted_iota(jnp.int32, (nsp, tk), 0)
    kk = lax.broadcasted_iota(jnp.int32, (nsp, tk), 1) // SEL_BLOCK

    def body(kb, c):
        keys = pl.ds(pl.multiple_of(kb * tk, tk), tk)
        s = _dot_nt(q4, k_ref[keys, :].astype(BF16))
        expand = jnp.where(jj == kb * (tk // SEL_BLOCK) + kk, 1.0, 0.0).astype(BF16)
        chosen = _dot(selb, expand)
        ok = (chosen > 0.5) & (kb * tk + col <= qi * tq + row)
        bias = jnp.where(ok, 0.0, 2.0 * NEG_BIG)
        s = (s.reshape(NSA_GROUP, tq, tk) + bias[None]).reshape(NSA_GROUP * tq, tk)
        m_prev = m_ref[...]
        m_new = jnp.maximum(m_prev, jnp.max(s, axis=1, keepdims=True))
        e = jnp.exp(s - m_new)
        alpha = jnp.exp(m_prev - m_new)
        l_ref[...] = alpha * l_ref[...] + jnp.sum(e, axis=1, keepdims=True)
        acc_ref[...] = alpha * acc_ref[...] + _dot(e.astype(BF16), v_ref[keys, :].astype(BF16))
        m_ref[...] = m_new
        return c

    lax.fori_loop(0, (qi * tq + tq - 1) // tk + 1, body, 0)
    o = _flash_result(l_ref, acc_ref)
    for h in range(NSA_GROUP):
        o_ref[:, h * HEAD_DIM:(h + 1) * HEAD_DIM] = o[h * tq:(h + 1) * tq]


def sel_prompt(u_head, batch, seq, sel, tq=256, tk=512):
    tq = min(tq, seq)
    tk = min(tk, seq)
    nq = seq // tq
    nsp = sel.shape[-1]
    ck = OFF_SEL // LANES
    cv = ck + NSA_KV_HEADS
    gw = NSA_GROUP * HEAD_DIM
    rows = NSA_GROUP * tq
    assert seq % tq == 0 and seq % tk == 0 and tk % SEL_BLOCK == 0
    return pl.pallas_call(
        functools.partial(_sel_prompt_kernel, tq=tq, tk=tk, nsp=nsp),
        grid=(batch, NSA_KV_HEADS, nq),
        in_specs=[pl.BlockSpec((tq, gw), lambda b, g, i: (b * nq + i, g)),
                  pl.BlockSpec((seq, LANES), lambda b, g, i: (b, ck + g)),
                  pl.BlockSpec((seq, LANES), lambda b, g, i: (b, cv + g)),
                  pl.BlockSpec((None, None, tq, nsp), lambda b, g, i: (b, g, i, 0))],
        out_specs=pl.BlockSpec((tq, gw), lambda b, g, i: (b * nq + i, g)),
        out_shape=jax.ShapeDtypeStruct((batch * seq, NSA_W), F32),
        scratch_shapes=[pltpu.VMEM((rows, 1), F32), pltpu.VMEM((rows, 1), F32), pltpu.VMEM((rows, HEAD_DIM), F32)],
        compiler_params=_cp(3),
        name="sel_prompt",
    )(u_head, u_head, u_head, sel)


def _win_prompt_kernel(q_ref, k_ref, v_ref, o_ref, *, tq, nk, seq):
    q0 = pl.program_id(2) * tq
    start = pl.multiple_of(jnp.clip(q0 - WINDOW, 0, seq - nk), tq)
    keys = pl.ds(start, nk)
    q4 = (_stack_heads(q_ref[...]) * SCALE).astype(BF16)
    s = _dot_nt(q4, k_ref[keys, :].astype(BF16))
    d = (q0 + lax.broadcasted_iota(jnp.int32, (tq, nk), 0)) - (start + lax.broadcasted_iota(jnp.int32, (tq, nk), 1))
    bias = jnp.where((d >= 0) & (d < WINDOW), 0.0, NEG_BIG)
    s = s + jnp.concatenate([bias] * NSA_GROUP, axis=0)
    e = jnp.exp(s - jnp.max(s, axis=1, keepdims=True))
    o = _dot(e.astype(BF16), v_ref[keys, :].astype(BF16)) / jnp.sum(e, axis=1, keepdims=True)
    for h in range(NSA_GROUP):
        o_ref[:, h * HEAD_DIM:(h + 1) * HEAD_DIM] = o[h * tq:(h + 1) * tq]


def win_prompt(u_head, batch, seq, tq=256):
    tq = min(tq, seq)
    nk = min(seq, WINDOW + tq)
    nq = seq // tq
    ck = OFF_WIN // LANES
    cv = ck + NSA_KV_HEADS
    gw = NSA_GROUP * HEAD_DIM
    assert seq % tq == 0 and WINDOW % tq == 0
    return pl.pallas_call(
        functools.partial(_win_prompt_kernel, tq=tq, nk=nk, seq=seq),
        grid=(batch, NSA_KV_HEADS, nq),
        in_specs=[pl.BlockSpec((tq, gw), lambda b, g, i: (b * nq + i, g)),
                  pl.BlockSpec((seq, LANES), lambda b, g, i: (b, ck + g)),
                  pl.BlockSpec((seq, LANES), lambda b, g, i: (b, cv + g))],
        out_specs=pl.BlockSpec((tq, gw), lambda b, g, i: (b * nq + i, g)),
        out_shape=jax.ShapeDtypeStruct((batch * seq, NSA_W), F32),
        compiler_params=_cp(3),
        name="win_prompt",
    )(u_head, u_head, u_head)


def _nsa_combine_kernel(gate_ref, a_ref, b_ref, c_ref, o_ref):
    gate = _sigmoid(gate_ref[...])
    for h in range(NSA_HEADS):
        sl = slice(h * HEAD_DIM, (h + 1) * HEAD_DIM)
        out = (gate[:, h:h + 1] * a_ref[:, sl] + gate[:, NSA_HEADS + h:NSA_HEADS + h + 1] * b_ref[:, sl]
               + gate[:, 2 * NSA_HEADS + h:2 * NSA_HEADS + h + 1] * c_ref[:, sl])
        o_ref[:, sl] = out.astype(o_ref.dtype)


def nsa_combine(u_head, row0, o_cmp, o_sel, o_win, tm=256):
    n = o_cmp.shape[0]
    tm = min(tm, n)
    rb0 = row0 // tm
    assert row0 % tm == 0 and n % tm == 0
    spec = pl.BlockSpec((tm, NSA_W), lambda i: (i, 0))
    return pl.pallas_call(
        _nsa_combine_kernel,
        grid=(n // tm,),
        in_specs=[pl.BlockSpec((tm, LANES), lambda i: (rb0 + i, OFF_NSA_GATE // LANES)), spec, spec, spec],
        out_specs=spec,
        out_shape=jax.ShapeDtypeStruct((n, NSA_W), F32),
        compiler_params=_cp(1),
        name="nsa_combine",
    )(u_head, o_cmp, o_sel, o_win)


def _page_rows(page_ref, first, n_rows, stride):
    return page_ref[pl.ds(first, n_rows, stride=stride), :]


def _pad_rows(x, n):
    if x.shape[0] >= n:
        return x
    return jnp.concatenate([x, jnp.zeros((n - x.shape[0], x.shape[1]), x.dtype)], axis=0)


def _cmp1_sample_kernel(pt_ref, *refs, pps, page):
    page_refs = refs[:pps]
    w_ref, pe_ref, o_ref, r_ref = refs[pps:]
    segs = page // CMP_STRIDE
    stride = CMP_STRIDE * 2 * NSA_KV_HEADS
    for kv in range(2):
        for g in range(NSA_KV_HEADS):
            for k in range(pps):
                for l in range(CMP_STRIDE):
                    r_ref[k * segs:(k + 1) * segs, l * LANES:(l + 1) * LANES] = _page_rows(
                        page_refs[k], l * 2 * NSA_KV_HEADS + kv * NSA_KV_HEADS + g, segs, stride)
            o_ref[kv, g] = _segment_products(r_ref[...], w_ref[kv], pe_ref[kv])


def compress_segments_sample(pool, layer, pt_flat, batch, n_pages, wcat, pecat, pps=16):
    page = pool.shape[2] // (2 * NSA_KV_HEADS)
    pps = min(pps, n_pages)
    assert n_pages % pps == 0 and page % CMP_STRIDE == 0
    segs = page // CMP_STRIDE
    in_specs = [pl.BlockSpec((None, None, pool.shape[2], LANES),
                             lambda b, s, pt, k=k: (layer, pt[b * n_pages + s * pps + k], 0, 0)) for k in range(pps)]
    in_specs += [pl.BlockSpec((2, SEG_W, 2 * LANES), lambda b, s, pt: (0, 0, 0)),
                 pl.BlockSpec((2, SUBLANES, SEG_W), lambda b, s, pt: (0, 0, 0))]
    grid_spec = pltpu.PrefetchScalarGridSpec(
        num_scalar_prefetch=1,
        grid=(batch, n_pages // pps),
        in_specs=in_specs,
        out_specs=pl.BlockSpec((None, 2, NSA_KV_HEADS, pps * segs, 2 * LANES), lambda b, s, pt: (b, 0, 0, s, 0)),
        scratch_shapes=[pltpu.VMEM((pps * segs, SEG_W), F32)],
    )
    return pl.pallas_call(
        functools.partial(_cmp1_sample_kernel, pps=pps, page=page),
        grid_spec=grid_spec,
        out_shape=jax.ShapeDtypeStruct((batch, 2, NSA_KV_HEADS, n_pages * segs, 2 * LANES), F32),
        compiler_params=_cp(2),
        name="cmp_segments_sample",
    )(pt_flat, *([pool] * pps), wcat, pecat)


def _sel_sample_kernel(pt_ref, *refs, pps, page, ds, past, nsp):
    q_ref, new_ref, sel_ref = refs[:3]
    page_refs = refs[3:3 + pps]
    o_ref, m_ref, l_ref, acc_ref = refs[3 + pps:]
    s = pl.program_id(1)
    rows = NSA_GROUP * ds
    keys = pps * page
    stride = 2 * NSA_KV_HEADS

    @pl.when(s == 0)
    def _():
        _flash_init(m_ref, l_ref, acc_ref)

    for g in range(NSA_KV_HEADS):
        q4 = _stack_heads(q_ref[:, g * NSA_GROUP * HEAD_DIM:(g + 1) * NSA_GROUP * HEAD_DIM]).astype(BF16)
        sel4 = jnp.concatenate([sel_ref[g]] * NSA_GROUP, axis=0)
        k_all = jnp.concatenate([_page_rows(r, g, page, stride) for r in page_refs], axis=0).astype(BF16)
        v_all = jnp.concatenate([_page_rows(r, NSA_KV_HEADS + g, page, stride) for r in page_refs],
                                axis=0).astype(BF16)
        sc = _dot_nt(q4, k_all) * SCALE
        jj = lax.broadcasted_iota(jnp.int32, (nsp, keys), 0)
        kk = lax.broadcasted_iota(jnp.int32, (nsp, keys), 1)
        expand = jnp.where(jj == s * (keys // SEL_BLOCK) + kk // SEL_BLOCK, 1.0, 0.0).astype(BF16)
        chosen = _dot(sel4.astype(BF16), expand)
        _flash_step(sc, chosen > 0.5, v_all, m_ref.at[g], l_ref.at[g], acc_ref.at[g])

        @pl.when(s == pl.num_programs(1) - 1)
        def _():
            k_new = _pad_rows(new_ref[:, g * HEAD_DIM:(g + 1) * HEAD_DIM], LANES).astype(BF16)
            v_new = _pad_rows(new_ref[:, (NSA_KV_HEADS + g) * HEAD_DIM:(NSA_KV_HEADS + g + 1) * HEAD_DIM],
                              LANES).astype(BF16)
            sn = _dot_nt(q4, k_new) * SCALE
            blk = past // SEL_BLOCK
            j = lax.broadcasted_iota(jnp.int32, (rows, LANES), 1)
            t = lax.broadcasted_iota(jnp.int32, (rows, LANES), 0) % ds
            mask = (sel4[:, blk:blk + 1] > 0.5) & (j <= t)
            _flash_step(sn, mask, v_new, m_ref.at[g], l_ref.at[g], acc_ref.at[g])
            o = _flash_result(l_ref.at[g], acc_ref.at[g])
            for h in range(NSA_GROUP):
                c0 = (g * NSA_GROUP + h) * HEAD_DIM
                o_ref[:, c0:c0 + HEAD_DIM] = o[h * ds:(h + 1) * ds]


def sel_sample(u_head, row0, pool, layer, pt_flat, sel, batch, ds, n_pages, pps=8):
    page = pool.shape[2] // (2 * NSA_KV_HEADS)
    pps = min(pps, n_pages)
    past = n_pages * page
    nsp = sel.shape[-1]
    rows = NSA_GROUP * ds
    assert n_pages % pps == 0 and row0 % ds == 0 and past % SEL_BLOCK == 0 and ds <= SEL_BLOCK
    assert page % SEL_BLOCK == 0
    in_specs = [pl.BlockSpec((ds, NSA_W), lambda b, s, pt: (row0 // ds + b, 0)),
                pl.BlockSpec((ds, KV_W), lambda b, s, pt: (row0 // ds + b, OFF_SEL // KV_W)),
                pl.BlockSpec((None, NSA_KV_HEADS, ds, nsp), lambda b, s, pt: (b, 0, 0, 0))]
    in_specs += [pl.BlockSpec((None, None, pool.shape[2], LANES),
                              lambda b, s, pt, k=k: (layer, pt[b * n_pages + s * pps + k], 0, 0)) for k in range(pps)]
    grid_spec = pltpu.PrefetchScalarGridSpec(
        num_scalar_prefetch=1,
        grid=(batch, n_pages // pps),
        in_specs=in_specs,
        out_specs=pl.BlockSpec((ds, NSA_W), lambda b, s, pt: (b, 0)),
        scratch_shapes=[pltpu.VMEM((NSA_KV_HEADS, rows, 1), F32), pltpu.VMEM((NSA_KV_HEADS, rows, 1), F32),
                        pltpu.VMEM((NSA_KV_HEADS, rows, HEAD_DIM), F32)],
    )
    return pl.pallas_call(
        functools.partial(_sel_sample_kernel, pps=pps, page=page, ds=ds, past=past, nsp=nsp),
        grid_spec=grid_spec,
        out_shape=jax.ShapeDtypeStruct((batch * ds, NSA_W), F32),
        compiler_params=_cp(2),
        name="sel_sample",
    )(pt_flat, u_head, u_head, sel, *([pool] * pps))


def _win_sample_kernel(q_ref, new_ref, buf_ref, o_ref, m_ref, l_ref, acc_ref, *, ds, nbuf):
    rows = NSA_GROUP * ds
    stride = 2 * NSA_KV_HEADS
    keys = nbuf + LANES
    i = lax.broadcasted_iota(jnp.int32, (rows, keys), 1)
    t = lax.broadcasted_iota(jnp.int32, (rows, keys), 0) % ds
    d = jnp.where(i < nbuf, t + nbuf - i, t - (i - nbuf))
    mask = (d >= 0) & (d < WINDOW) & (i < nbuf + ds)
    for g in range(NSA_KV_HEADS):
        _flash_init(m_ref, l_ref, acc_ref)
        q4 = _stack_heads(q_ref[:, g * NSA_GROUP * HEAD_DIM:(g + 1) * NSA_GROUP * HEAD_DIM]).astype(BF16)
        k_new = _pad_rows(new_ref[:, g * HEAD_DIM:(g + 1) * HEAD_DIM], LANES)
        v_new = _pad_rows(new_ref[:, (NSA_KV_HEADS + g) * HEAD_DIM:(NSA_KV_HEADS + g + 1) * HEAD_DIM], LANES)
        k_all = jnp.concatenate([_page_rows(buf_ref, g, nbuf, stride), k_new], axis=0).astype(BF16)
        v_all = jnp.concatenate([_page_rows(buf_ref, NSA_KV_HEADS + g, nbuf, stride), v_new], axis=0).astype(BF16)
        _flash_step(_dot_nt(q4, k_all) * SCALE, mask, v_all, m_ref, l_ref, acc_ref)
        o = _flash_result(l_ref, acc_ref)
        for h in range(NSA_GROUP):
            c0 = (g * NSA_GROUP + h) * HEAD_DIM
            o_ref[:, c0:c0 + HEAD_DIM] = o[h * ds:(h + 1) * ds]


def win_sample(u_head, row0, buf, layer, batch, ds):
    nbuf = buf.shape[2] // (2 * NSA_KV_HEADS)
    rows = NSA_GROUP * ds
    return pl.pallas_call(
        functools.partial(_win_sample_kernel, ds=ds, nbuf=nbuf),
        grid=(batch,),
        in_specs=[pl.BlockSpec((ds, NSA_W), lambda b: (row0 // ds + b, 0)),
                  pl.BlockSpec((ds, KV_W), lambda b: (row0 // ds + b, OFF_WIN // KV_W)),
                  pl.BlockSpec((None, None, buf.shape[2], LANES), lambda b: (layer, b, 0, 0))],
        out_specs=pl.BlockSpec((ds, NSA_W), lambda b: (b, 0)),
        out_shape=jax.ShapeDtypeStruct((batch * ds, NSA_W), F32),
        scratch_shapes=[pltpu.VMEM((rows, 1), F32), pltpu.VMEM((rows, 1), F32), pltpu.VMEM((rows, HEAD_DIM), F32)],
        compiler_params=_cp(1),
        name="win_sample",
    )(u_head, u_head, buf)


SB_ROWS = 16


def _sb_sample_kernel(pt_ref, q_ref, kn_ref, vn_ref, pool_hbm, o_ref, acc_ref, carry_ref, kv_ref, sem, *,
                      pps, page, ds, n_pages, layer):
    b = pl.program_id(0)
    stride = 2 * SB_HEADS
    umat = _suffix_matrix()
    q_heads = [_pad_rows(q_ref[:, h * HEAD_DIM:(h + 1) * HEAD_DIM] * SCALE, SB_ROWS).astype(BF16)
               for h in range(SB_HEADS)]
    rows = SB_HEADS * SB_ROWS
    real_row = lax.broadcasted_iota(jnp.int32, (rows, LANES), 0) % SB_ROWS < ds

    def accumulate(k_heads, v_heads, mask):
        z = jnp.concatenate([_dot_nt(q_heads[h], k_heads[h]) for h in range(SB_HEADS)], axis=0)
        a, carry = _sb_weights(z, mask, carry_ref[...], umat)
        carry_ref[...] = carry
        a = a.astype(BF16)
        for h in range(SB_HEADS):
            sl = slice(h * SB_ROWS, (h + 1) * SB_ROWS)
            acc_ref[sl, :] += _dot(a[sl], v_heads[h])
        return jnp.max(jnp.where(real_row, carry, NEG_BIG))

    acc_ref[...] = jnp.zeros_like(acc_ref)
    carry_ref[...] = jnp.zeros_like(carry_ref)
    k_new = [_pad_rows(kn_ref[:, h * HEAD_DIM:(h + 1) * HEAD_DIM], LANES).astype(BF16) for h in range(SB_HEADS)]
    v_new = [_pad_rows(vn_ref[:, h * HEAD_DIM:(h + 1) * HEAD_DIM], LANES).astype(BF16) for h in range(SB_HEADS)]
    j = lax.broadcasted_iota(jnp.int32, (rows, LANES), 1)
    t = lax.broadcasted_iota(jnp.int32, (rows, LANES), 0) % SB_ROWS
    top = accumulate(k_new, v_new, j < t)

    def page_copy(g, k):
        pid = pt_ref[b * n_pages + n_pages - (g + 1) * pps + k]
        return pltpu.make_async_copy(pool_hbm.at[layer, pid], kv_ref.at[k], sem)

    def cond(st):
        return jnp.logical_and(st[0] < n_pages // pps, st[1] > SB_ZERO_LOG)

    def body(st):
        g = st[0]
        for k in range(pps):
            page_copy(g, k).start()
        for k in range(pps):
            page_copy(g, k).wait()
        k_heads = [jnp.concatenate([_page_rows(kv_ref.at[k], h, page, stride) for k in range(pps)],
                                   axis=0).astype(BF16) for h in range(SB_HEADS)]
        v_heads = [jnp.concatenate([_page_rows(kv_ref.at[k], SB_HEADS + h, page, stride) for k in range(pps)],
                                   axis=0).astype(BF16) for h in range(SB_HEADS)]
        return g + 1, accumulate(k_heads, v_heads, None)

    lax.while_loop(cond, body, (0, top))
    for h in range(SB_HEADS):
        o_ref[:, h * HEAD_DIM:(h + 1) * HEAD_DIM] = acc_ref[h * SB_ROWS:h * SB_ROWS + ds, :]


def sb_sample(u_rest, row0, pool, layer, pt_flat, batch, ds, n_pages, pps=2):
    page = pool.shape[2] // (2 * SB_HEADS)
    pps = min(pps, n_pages)
    assert n_pages % pps == 0 and row0 % ds == 0 and ds <= SB_ROWS
    rb = row0 // ds
    grid_spec = pltpu.PrefetchScalarGridSpec(
        num_scalar_prefetch=1,
        grid=(batch,),
        in_specs=[pl.BlockSpec((ds, SB_W), lambda b, pt: (rb + b, R_SB_Q // SB_W)),
                  pl.BlockSpec((ds, SB_W), lambda b, pt: (rb + b, R_SB_K // SB_W)),
                  pl.BlockSpec((ds, SB_W), lambda b, pt: (rb + b, R_SB_V // SB_W)),
                  pl.BlockSpec(memory_space=pl.ANY)],
        out_specs=pl.BlockSpec((ds, SB_W), lambda b, pt: (b, 0)),
        scratch_shapes=[pltpu.VMEM((SB_HEADS * SB_ROWS, HEAD_DIM), F32), pltpu.VMEM((SB_HEADS * SB_ROWS, LANES), F32),
                        pltpu.VMEM((pps, pool.shape[2], LANES), F32), pltpu.SemaphoreType.DMA],
    )
    return pl.pallas_call(
        functools.partial(_sb_sample_kernel, pps=pps, page=page, ds=ds, n_pages=n_pages, layer=layer),
        grid_spec=grid_spec,
        out_shape=jax.ShapeDtypeStruct((batch * ds, SB_W), F32),
        compiler_params=_cp(1),
        name="sb_sample",
    )(pt_flat, u_rest, u_rest, u_rest, pool)


def _router_kernel(x_ref, g_ref, w_ref, b_ref, h_ref, o_ref):
    x = x_ref[...]
    h = x * lax.rsqrt(jnp.mean(x * x, axis=-1, keepdims=True) + NORM_EPS) * g_ref[...]
    h_ref[...] = h
    hh, hm, hl = _split3(h)
    wh, wm, wl = _split3(w_ref[...])
    logits = (_dot(hh, wh) + _dot(hh, wm) + _dot(hm, wh) + _dot(hh, wl) + _dot(hl, wh) + _dot(hm, wm)) + b_ref[...]
    lane = lax.broadcasted_iota(jnp.int32, logits.shape, 1)
    logits = jnp.where(lane < N_EXPERTS, logits, NEG_BIG)
    m1 = jnp.max(logits, axis=1, keepdims=True)
    i1 = jnp.min(jnp.where(logits == m1, lane, LANES), axis=1, keepdims=True)
    rest = jnp.where(lane == i1, NEG_BIG, logits)
    m2 = jnp.max(rest, axis=1, keepdims=True)
    i2 = jnp.min(jnp.where(rest == m2, lane, LANES), axis=1, keepdims=True)
    e = jnp.exp(m2 - m1)
    g1 = 1.0 / (1.0 + e)
    g2 = e / (1.0 + e)
    o_ref[...] = jnp.where(lane == 0, i1.astype(F32), jnp.where(lane == 1, i2.astype(F32),
                           jnp.where(lane == 2, g1, jnp.where(lane == 3, g2, 0.0))))


def moe_router(x, g, rw, rb, tm):
    m, d = x.shape
    w = jnp.zeros((d, LANES), F32).at[:, :N_EXPERTS].set(rw)
    b = jnp.zeros((1, LANES), F32).at[0, :N_EXPERTS].set(rb.astype(F32))
    return pl.pallas_call(
        _router_kernel,
        grid=(m // tm,),
        in_specs=[pl.BlockSpec((tm, d), lambda i: (i, 0)), pl.BlockSpec((1, d), lambda i: (0, 0)),
                  pl.BlockSpec((d, LANES), lambda i: (0, 0)), pl.BlockSpec((1, LANES), lambda i: (0, 0))],
        out_specs=[pl.BlockSpec((tm, d), lambda i: (i, 0)), pl.BlockSpec((tm, LANES), lambda i: (i, 0))],
        out_shape=[jax.ShapeDtypeStruct((m, d), F32), jax.ShapeDtypeStruct((m, LANES), F32)],
        compiler_params=_cp(1),
        name="moe_router",
    )(x, g.reshape(1, d), w, b)


def _row_copy(src_hbm, row, dst_ref, r, sem):
    return pltpu.make_async_copy(src_hbm.at[pl.ds(row, 1), :], dst_ref.at[pl.ds(r, 1), :], sem)


def _gather_kernel(tok_ref, nu_ref, h_hbm, o_ref, buf_ref, sem):
    blk = pl.program_id(0)
    n = buf_ref.shape[0]

    @pl.when(blk < nu_ref[0])
    def _():
        def issue(r, c):
            _row_copy(h_hbm, tok_ref[blk * n + r], buf_ref, r, sem).start()
            return c

        lax.fori_loop(0, n, issue, 0)

        def wait(r, c):
            _row_copy(h_hbm, 0, buf_ref, r, sem).wait()
            return c

        lax.fori_loop(0, n, wait, 0)
        o_ref[...] = buf_ref[...].astype(o_ref.dtype)

    @pl.when(blk >= nu_ref[0])
    def _():
        o_ref[...] = jnp.zeros_like(o_ref)


def _used_block(i, nu):
    return jnp.minimum(i, nu[0] - 1)


def moe_gather(h, tok_buf, n_used):
    p = tok_buf.shape[0]
    d = h.shape[1]
    grid_spec = pltpu.PrefetchScalarGridSpec(
        num_scalar_prefetch=2,
        grid=(p // MOE_BLOCK,),
        in_specs=[pl.BlockSpec(memory_space=pl.ANY)],
        out_specs=pl.BlockSpec((MOE_BLOCK, d), lambda i, tok, nu: (i, 0)),
        scratch_shapes=[pltpu.VMEM((MOE_BLOCK, d), F32), pltpu.SemaphoreType.DMA],
    )
    return pl.pallas_call(
        _gather_kernel,
        grid_spec=grid_spec,
        out_shape=jax.ShapeDtypeStruct((p, d), BF16),
        compiler_params=_cp(1),
        name="moe_gather",
    )(tok_buf, n_used, h)


def _moe_mm_kernel(be_ref, nu_ref, *refs, n_w, mode):
    a_ref = refs[0]
    w_refs = refs[1:1 + n_w]
    o_ref = refs[1 + n_w]
    wbf_refs = refs[2 + n_w:]
    blk = pl.program_id(1)
    changed = jnp.logical_or(blk == 0, be_ref[blk] != be_ref[jnp.maximum(blk - 1, 0)])

    @pl.when(changed)
    def _():
        for w_ref, wbf_ref in zip(w_refs, wbf_refs):
            wbf_ref[...] = w_ref[...].astype(BF16)

    @pl.when(blk < nu_ref[0])
    def _():
        prods = [_dot(a_ref[...], r[...]) for r in wbf_refs]
        o_ref[...] = _mm_epilogue(mode, prods, ()).astype(o_ref.dtype)

    @pl.when(blk >= nu_ref[0])
    def _():
        o_ref[...] = jnp.zeros_like(o_ref)


def moe_matmul(a, w_list, moe_index, blk_e, n_used, mode, out_dtype, tn=512):
    p, k = a.shape
    n_out = w_list[0].shape[-1]
    tn = _tile(n_out, tn)
    grid_spec = pltpu.PrefetchScalarGridSpec(
        num_scalar_prefetch=2,
        grid=(n_out // tn, p // MOE_BLOCK),
        in_specs=[pl.BlockSpec((MOE_BLOCK, k), lambda j, i, be, nu: (_used_block(i, nu), 0))]
        + [pl.BlockSpec((None, None, k, tn), lambda j, i, be, nu: (moe_index, be[i], 0, j)) for _ in w_list],
        out_specs=pl.BlockSpec((MOE_BLOCK, tn), lambda j, i, be, nu: (i, j)),
        scratch_shapes=[pltpu.VMEM((k, tn), BF16) for _ in w_list],
    )
    return pl.pallas_call(
        functools.partial(_moe_mm_kernel, n_w=len(w_list), mode=mode),
        grid_spec=grid_spec,
        out_shape=jax.ShapeDtypeStruct((p, n_out), out_dtype),
        compiler_params=_cp(2),
        name="moe_mm_" + mode,
    )(blk_e, n_used, a, *w_list)


def _moe_combine_kernel(pos_ref, y_hbm, x_ref, gate_ref, o_ref, buf_ref, sem, *, tm):
    i = pl.program_id(0)

    def issue(r, c):
        for k in range(TOP_K):
            _row_copy(y_hbm, pos_ref[(i * tm + r) * TOP_K + k], buf_ref.at[k], r, sem).start()
        return c

    lax.fori_loop(0, tm, issue, 0)

    def wait(r, c):
        for k in range(TOP_K):
            _row_copy(y_hbm, 0, buf_ref.at[k], r, sem).wait()
        return c

    lax.fori_loop(0, tm, wait, 0)
    gate = gate_ref[...]
    out = x_ref[...]
    y = gate[:, 2:3] * buf_ref[0]
    for k in range(1, TOP_K):
        y = y + gate[:, 2 + k:3 + k] * buf_ref[k]
    o_ref[...] = out + y


def moe_combine(x, yb, pos, gates, tm=256):
    m, d = x.shape
    grid_spec = pltpu.PrefetchScalarGridSpec(
        num_scalar_prefetch=1,
        grid=(m // tm,),
        in_specs=[pl.BlockSpec(memory_space=pl.ANY),
                  pl.BlockSpec((tm, d), lambda i, pos: (i, 0)),
                  pl.BlockSpec((tm, LANES), lambda i, pos: (i, 0))],
        out_specs=pl.BlockSpec((tm, d), lambda i, pos: (i, 0)),
        scratch_shapes=[pltpu.VMEM((TOP_K, tm, d), F32), pltpu.SemaphoreType.DMA],
    )
    return pl.pallas_call(
        functools.partial(_moe_combine_kernel, tm=tm),
        grid_spec=grid_spec,
        out_shape=jax.ShapeDtypeStruct((m, d), F32),
        compiler_params=_cp(1),
        name="moe_combine",
    )(pos, yb, x, gates)


def moe_layer(x, g, rw, rb, w1, w3, w2, moe_index, n_tok, tm):
    m, d = x.shape
    h, route = moe_router(x, g, rw, rb, tm)
    top_i = route[:n_tok, :TOP_K].astype(jnp.int32)
    a = n_tok * TOP_K
    e_flat = top_i.reshape(a)
    order = jnp.argsort(e_flat)
    e_s = e_flat[order]
    tok_s = (order // TOP_K).astype(jnp.int32)
    counts = jnp.bincount(e_flat, length=N_EXPERTS)
    start = jnp.cumsum(counts) - counts
    padded = (counts + MOE_BLOCK - 1) // MOE_BLOCK * MOE_BLOCK
    pend = jnp.cumsum(padded)
    pstart = pend - padded
    dest = (pstart[e_s] + jnp.arange(a) - start[e_s]).astype(jnp.int32)
    nb = -(-a // MOE_BLOCK) + N_EXPERTS
    p = nb * MOE_BLOCK
    n_used = (pend[-1] // MOE_BLOCK).astype(jnp.int32)
    blk_e = jnp.minimum(jnp.searchsorted(pend, jnp.arange(nb) * MOE_BLOCK, side="right"), N_EXPERTS - 1)
    slot = jnp.arange(p)
    slot_e = blk_e[slot // MOE_BLOCK]
    local = slot - pstart[slot_e]
    tok_buf = jnp.where(local < counts[slot_e], tok_s[jnp.clip(start[slot_e] + local, 0, a - 1)], n_tok)
    tok_buf = tok_buf.astype(jnp.int32)
    blk_e = jnp.where(jnp.arange(nb) < n_used, blk_e, blk_e[n_used - 1]).astype(jnp.int32)
    pos = jnp.concatenate([dest[jnp.argsort(order)], jnp.zeros((m * TOP_K - a,), jnp.int32)])
    row_ok = (jnp.arange(m) < n_tok)[:, None]
    gates = jnp.where(row_ok, route, 0.0)
    n_used = n_used.reshape(1)
    xs = moe_gather(h, tok_buf, n_used)
    act = moe_matmul(xs, [w1, w3], moe_index, blk_e, n_used, "swiglu", BF16)
    yb = moe_matmul(act, [w2], moe_index, blk_e, n_used, "plain", F32)
    return moe_combine(x, yb, pos, gates)


TM = 512


def _row_tile(m, pref=1152):
    t = (min(pref, m) // 16) * 16
    while m % t:
        t -= 16
    return t


def _slab(parts, m, dtype):
    rows = sum(p.shape[0] for p in parts)
    parts = [p.astype(dtype) for p in parts]
    return jnp.concatenate(parts + [jnp.zeros((m - rows, parts[0].shape[1]), dtype)], axis=0)


def kernel(x_prompt, x_sample, cache_nsa_cmp_kv, cache_nsa_sel_kv, cache_sb_kv, cache_nsa_win_kv, state_hgrn,
           page_table, attn_norm, w_in, cmp_pe_k, cmp_w1_k, cmp_w2_k, cmp_pe_v, cmp_w1_v, cmp_w2_v, hg_lb_logits,
           hg_norm, w_br_nsa, w_br_sb, w_br_hg, w_out, ffn_norm, ffn_w1, ffn_w3, ffn_w2, router_w, router_b,
           moe_w1, moe_w3, moe_w2, final_norm):
    bsz, seq, d = x_prompt.shape
    db, ds, _ = x_sample.shape
    depth = attn_norm.shape[0]
    n_p, n_s = bsz * seq, db * ds
    n_tok = n_p + n_s
    m = -(-(n_tok + 1) // TM) * TM
    n_pool, page = cache_nsa_cmp_kv.shape[1:3]
    n_pages = page_table.shape[1]
    past = n_pages * page
    assert past % CMP_STRIDE == 0 and ds < CMP_STRIDE and seq % CMP_STRIDE == 0

    x = _slab([x_prompt.reshape(n_p, d), x_sample.reshape(n_s, d)], m, F32)
    pt_flat = page_table.reshape(-1).astype(jnp.int32)
    cmp_pool = cache_nsa_cmp_kv.reshape(depth, n_pool, page * 2 * NSA_KV_HEADS, HEAD_DIM)
    sel_pool = cache_nsa_sel_kv.reshape(depth, n_pool, page * 2 * NSA_KV_HEADS, HEAD_DIM)
    sb_pool = cache_sb_kv.reshape(depth, n_pool, page * 2 * SB_HEADS, HEAD_DIM)
    nbuf = cache_nsa_win_kv.shape[2]
    win_buf = cache_nsa_win_kv.reshape(depth, db, nbuf * 2 * NSA_KV_HEADS, HEAD_DIM)
    rest_cols = w_in.shape[-1] - REST_START
    tmm = _row_tile(m)
    lb_all = jnp.cumsum(jax.nn.softmax(hg_lb_logits.astype(F32), axis=0), axis=0)
    kvs = (2, NSA_KV_HEADS, HEAD_DIM)

    states = []
    for l in range(depth):
        h = rmsnorm(x, attn_norm[l], BF16, TM)
        u_head = matmul([h], [w_in], l, 0, HEAD_COLS, "plain", F32, tm=tmm, tn=HEAD_COLS // 3)
        u_rest = inproj_rest(h, w_in, l, rest_cols, tmm)

        wk, pk = compress_params(cmp_pe_k[l], cmp_w1_k[l])
        wv, pv = compress_params(cmp_pe_v[l], cmp_w1_v[l])
        wcat, pecat = jnp.stack([wk, wv]), jnp.stack([pk, pv])
        w2s = jnp.stack([cmp_w2_k[l], cmp_w2_v[l]])
        lb = (lb_all[l] - lb_all[0]).reshape(HG_HEADS, HEAD_DIM)
        lbp = hgrn_lb_params(lb)

        kvc = compress_finish(compress_segments_prompt(u_head, bsz, seq, wcat, pecat), w2s)
        o_cmp, sel = cmp_select(u_head, 0, bsz, seq, kvc, seq // CMP_STRIDE - 1, -(-seq // SEL_BLOCK), 0)
        o_sel = sel_prompt(u_head, bsz, seq, sel)
        o_win = win_prompt(u_head, bsz, seq)
        nsa_p = nsa_combine(u_head, 0, o_cmp, o_sel, o_win)
        sb_p = sb_prompt(u_rest, bsz, seq)
        hg_p, s_p = hgrn(u_rest, 0, bsz, seq, lbp, hg_norm[l], jnp.zeros((bsz, HG_HEADS, HEAD_DIM, HEAD_DIM), F32))

        kvc_s = compress_finish(compress_segments_sample(cmp_pool, l, pt_flat, db, n_pages, wcat, pecat), w2s)
        o_cmp_s, sel_s = cmp_select(u_head, n_p, db, ds, kvc_s, past // CMP_STRIDE - 1,
                                    -(-(past + ds) // SEL_BLOCK), past, tq=ds)
        o_sel_s = sel_sample(u_head, n_p, sel_pool, l, pt_flat, sel_s, db, ds, n_pages)
        o_win_s = win_sample(u_head, n_p, win_buf, l, db, ds)
        nsa_s = nsa_combine(u_head, n_p, o_cmp_s, o_sel_s, o_win_s)
        sb_s = sb_sample(u_rest, n_p, sb_pool, l, pt_flat, db, ds, n_pages)
        hg_s, s_s = hgrn(u_rest, n_p, db, ds, lbp, hg_norm[l], state_hgrn[l])

        o_nsa = _slab([nsa_p, nsa_s], m, BF16)
        o_sb = _slab([sb_p, sb_s], m, BF16)
        o_hg = _slab([hg_p, hg_s], m, BF16)
        merged = matmul([o_nsa, o_sb, o_hg], [w_br_nsa, w_br_sb, w_br_hg], l, 0, d, "merge", BF16,
                        extras=[(u_rest, R_MERGE), (u_rest, R_MERGE + d), (u_rest, R_MERGE + 2 * d)], tm=tmm)
        x = matmul([merged], [w_out], l, 0, d, "residual", F32, extras=[(x, 0)], tm=tmm)

        i = l // 2
        if l % 2 == 0:
            h2 = rmsnorm(x, ffn_norm[l], BF16, TM)
            act = matmul([h2], [ffn_w1, ffn_w3], i, 0, ffn_w1.shape[-1], "swiglu", BF16, tm=tmm)
            x = matmul([act], [ffn_w2], i, 0, d, "residual", F32, extras=[(x, 0)])
        else:
            x = moe_layer(x, ffn_norm[l], router_w[i], router_b[i], moe_w1, moe_w3, moe_w2, i, n_tok, TM)

        def head_cols(off, r0, r1, lead):
            return u_head[r0:r1, off:off + KV_W].reshape(lead + kvs)

        win_p = head_cols(OFF_WIN, 0, n_p, (bsz, seq))[:, seq - min(WINDOW, seq):]
        win_s = jnp.concatenate([cache_nsa_win_kv[l], head_cols(OFF_WIN, n_p, n_tok, (db, ds))], axis=1)[:, ds:]
        states.append((
            head_cols(OFF_CMP, 0, n_p, (bsz, seq)), head_cols(OFF_SEL, 0, n_p, (bsz, seq)), win_p,
            u_rest[:n_p, R_SB_K:R_SB_K + 2 * SB_W].reshape(bsz, seq, 2, SB_HEADS, HEAD_DIM), s_p,
            head_cols(OFF_CMP, n_p, n_tok, (db, ds)), head_cols(OFF_SEL, n_p, n_tok, (db, ds)), win_s,
            u_rest[n_p:n_tok, R_SB_K:R_SB_K + 2 * SB_W].reshape(db, ds, 2, SB_HEADS, HEAD_DIM), s_s))

    y = rmsnorm(x, final_norm, F32, TM)
    stacked = [jnp.stack([st[i] for st in states]) for i in range(10)]
    return (y[:n_p].reshape(bsz, seq, d), y[n_p:n_tok].reshape(db, ds, d), *stacked)
```

```python
import functools

import jax
import jax.numpy as jnp
import numpy as np
from jax import lax
from jax.experimental import pallas as pl
from jax.experimental.pallas import tpu as pltpu

F32 = jnp.float32
BF16 = jnp.bfloat16

HEAD_DIM = 128
SCALE = HEAD_DIM ** -0.5
NSA_HEADS = 8
NSA_KV_HEADS = 2
NSA_GROUP = NSA_HEADS // NSA_KV_HEADS
CMP_LEN = 32
CMP_STRIDE = 16
SEL_BLOCK = 64
SEL_TOPK = 16
WINDOW = 512
SB_HEADS = 8
HG_HEADS = 8
HG_CHUNK = 64
HG_SUB = 16
N_EXPERTS = 8
TOP_K = 2
MOE_BLOCK = 512
NORM_EPS = 1e-6
NEG_BIG = -1e30

LANES = 128
SUBLANES = 8
VMEM_LIMIT = 56 * 1024 * 1024

NSA_W = NSA_HEADS * HEAD_DIM
KV_W = 2 * NSA_KV_HEADS * HEAD_DIM
OFF_CMP = NSA_W
OFF_SEL = OFF_CMP + KV_W
OFF_WIN = OFF_SEL + KV_W
OFF_NSA_GATE = OFF_WIN + KV_W
N_GATE = 3 * NSA_HEADS
HEAD_COLS = OFF_NSA_GATE + LANES
REST_START = OFF_NSA_GATE + N_GATE
SB_W = SB_HEADS * HEAD_DIM
HG_W = HG_HEADS * HEAD_DIM
R_SB_Q = 0
R_SB_K = R_SB_Q + SB_W
R_SB_V = R_SB_K + SB_W
R_HG_Q = R_SB_V + SB_W
R_HG_F = R_HG_Q + HG_W
R_HG_I = R_HG_F + HG_W
R_HG_G = R_HG_I + HG_W
R_MERGE = R_HG_G + HG_W


def _cp(n_axes, vmem=VMEM_LIMIT):
    return pltpu.CompilerParams(dimension_semantics=("arbitrary",) * n_axes, vmem_limit_bytes=vmem)


def _tile(n, pref, quantum=LANES):
    if n <= pref:
        return n
    t = (pref // quantum) * quantum
    while t > quantum and n % t:
        t -= quantum
    assert n % t == 0, (n, pref)
    return t


def _dot(a, b):
    return jnp.dot(a, b, preferred_element_type=F32)


def _dot_nt(a, b):
    return lax.dot_general(a, b, (((1,), (1,)), ((), ())), preferred_element_type=F32)


def _dot_tn(a, b):
    return lax.dot_general(a, b, (((0,), (0,)), ((), ())), preferred_element_type=F32)


def _split3(x):
    hi = x.astype(BF16)
    r = x - hi.astype(F32)
    mid = r.astype(BF16)
    lo = (r - mid.astype(F32)).astype(BF16)
    return hi, mid, lo


def _split2(x):
    hi = x.astype(BF16)
    return hi, (x - hi.astype(F32)).astype(BF16)


def _sigmoid(x):
    return 1.0 / (1.0 + jnp.exp(-x))


def _silu(x):
    return x * _sigmoid(x)


def _log_sigmoid(x):
    return jnp.minimum(x, 0.0) - jnp.log1p(jnp.exp(-jnp.abs(x)))


def _rmsnorm_kernel(x_ref, g_ref, o_ref):
    x = x_ref[...]
    y = x * lax.rsqrt(jnp.mean(x * x, axis=-1, keepdims=True) + NORM_EPS)
    o_ref[...] = (y * g_ref[...]).astype(o_ref.dtype)


def rmsnorm(x, g, out_dtype, tm):
    m, d = x.shape
    return pl.pallas_call(
        _rmsnorm_kernel,
        grid=(m // tm,),
        in_specs=[pl.BlockSpec((tm, d), lambda i: (i, 0)), pl.BlockSpec((1, d), lambda i: (0, 0))],
        out_specs=pl.BlockSpec((tm, d), lambda i: (i, 0)),
        out_shape=jax.ShapeDtypeStruct((m, d), out_dtype),
        compiler_params=_cp(1),
        name="rmsnorm",
    )(x, g.reshape(1, d))


def _mm_epilogue(mode, prods, x_refs):
    if mode == "plain":
        return prods[0]
    if mode == "merge":
        out = _sigmoid(x_refs[0][...]) * prods[0]
        for x_ref, p in zip(x_refs[1:], prods[1:]):
            out = out + _sigmoid(x_ref[...]) * p
        return out
    if mode == "residual":
        return x_refs[0][...] + prods[0]
    return _silu(prods[0]) * prods[1]


def _mm_kernel(*refs, n_a, n_w, n_extra, mode, cast):
    a_refs = refs[:n_a]
    w_refs = refs[n_a:n_a + n_w]
    x_refs = refs[n_a + n_w:n_a + n_w + n_extra]
    o_ref = refs[n_a + n_w + n_extra]
    wbf_refs = refs[n_a + n_w + n_extra + 1:]

    if cast:
        @pl.when(pl.program_id(1) == 0)
        def _():
            for w_ref, wbf_ref in zip(w_refs, wbf_refs):
                wbf_ref[...] = w_ref[...].astype(BF16)
        ws = [r[...] for r in wbf_refs]
    else:
        ws = [r[...] for r in w_refs]

    prods = [_dot(a_refs[min(i, n_a - 1)][...], w) for i, w in enumerate(ws)]
    o_ref[...] = _mm_epilogue(mode, prods, x_refs).astype(o_ref.dtype)


def matmul(a_list, w_list, w_index, col0, n_out, mode, out_dtype, extras=(), tm=512, tn=512):
    m = a_list[0].shape[0]
    tn = _tile(n_out, tn)
    assert col0 % tn == 0 and m % tm == 0
    cast = w_list[0].dtype != BF16
    in_specs, args, scratch = [], [], []
    for a in a_list:
        in_specs.append(pl.BlockSpec((tm, a.shape[1]), lambda j, i: (i, 0)))
        args.append(a)
    for w in w_list:
        k = w.shape[-2]
        if w.ndim == 3:
            in_specs.append(pl.BlockSpec((None, k, tn), lambda j, i: (w_index, 0, j + col0 // tn)))
        else:
            in_specs.append(pl.BlockSpec((k, tn), lambda j, i: (0, j + col0 // tn)))
        args.append(w)
        if cast:
            scratch.append(pltpu.VMEM((k, tn), BF16))
    for x, off in extras:
        assert off % tn == 0
        in_specs.append(pl.BlockSpec((tm, tn), lambda j, i, off=off: (i, j + off // tn)))
        args.append(x)
    return pl.pallas_call(
        functools.partial(_mm_kernel, n_a=len(a_list), n_w=len(w_list), n_extra=len(extras), mode=mode, cast=cast),
        grid=(n_out // tn, m // tm),
        in_specs=in_specs,
        out_specs=pl.BlockSpec((tm, tn), lambda j, i: (i, j)),
        out_shape=jax.ShapeDtypeStruct((m, n_out), out_dtype),
        scratch_shapes=scratch,
        compiler_params=_cp(2),
        name="mm_" + mode,
    )(*args)


def _inproj_t_kernel(a_ref, wt_ref, o_ref, wbf_ref):
    @pl.when(pl.program_id(1) == 0)
    def _():
        wbf_ref[...] = wt_ref[0].T.astype(BF16)

    o_ref[...] = _dot(a_ref[...], wbf_ref[...])


def inproj_t(a, w_t, layer, row0, n_out, tm, tn):
    m, k = a.shape
    tn = _tile(n_out, tn)
    assert m % tm == 0 and row0 % SUBLANES == 0
    return pl.pallas_call(
        _inproj_t_kernel,
        grid=(n_out // tn, m // tm),
        in_specs=[pl.BlockSpec((tm, k), lambda j, i: (i, 0)),
                  pl.BlockSpec((pl.Element(1), pl.Element(tn), pl.Element(k)),
                               lambda j, i: (layer, pl.multiple_of(row0 + j * tn, SUBLANES), 0))],
        out_specs=pl.BlockSpec((tm, tn), lambda j, i: (i, j)),
        out_shape=jax.ShapeDtypeStruct((m, n_out), F32),
        scratch_shapes=[pltpu.VMEM((k, tn), BF16)],
        compiler_params=_cp(2),
        name="inproj_t",
    )(a, w_t)


def _inproj_rest_kernel(a_ref, wa_ref, wb_ref, o_ref, wbf_ref, *, shift, tn):
    @pl.when(pl.program_id(1) == 0)
    def _():
        w = jnp.concatenate([wa_ref[...], wb_ref[...]], axis=1)
        wbf_ref[...] = w[:, shift:shift + tn].astype(BF16)

    o_ref[...] = _dot(a_ref[...], wbf_ref[...])


def inproj_rest(a, w_in, layer, n_out, tm, tn=512):
    m, k = a.shape
    shift = REST_START % LANES
    base = REST_START - shift
    tn = _tile(n_out, tn)
    assert base % tn == 0 and m % tm == 0 and REST_START + n_out == w_in.shape[-1]
    return pl.pallas_call(
        functools.partial(_inproj_rest_kernel, shift=shift, tn=tn),
        grid=(n_out // tn, m // tm),
        in_specs=[pl.BlockSpec((tm, k), lambda j, i: (i, 0)),
                  pl.BlockSpec((None, k, tn), lambda j, i: (layer, 0, base // tn + j)),
                  pl.BlockSpec((None, k, LANES), lambda j, i: (layer, 0, (base + (j + 1) * tn) // LANES))],
        out_specs=pl.BlockSpec((tm, tn), lambda j, i: (i, j)),
        out_shape=jax.ShapeDtypeStruct((m, n_out), F32),
        scratch_shapes=[pltpu.VMEM((k, tn), BF16)],
        compiler_params=_cp(2),
        name="inproj_rest",
    )(a, w_in, w_in)


def _suffix_matrix():
    j = lax.broadcasted_iota(jnp.int32, (LANES, 2 * LANES), 0)
    s = lax.broadcasted_iota(jnp.int32, (LANES, 2 * LANES), 1)
    return jnp.where((j > s) | (s >= LANES), 1.0, 0.0).astype(BF16)


def _sb_weights(z, mask, carry, umat):
    t = jnp.log(1.0 + jnp.exp(-jnp.abs(z)))
    ls_pos = jnp.minimum(z, 0.0) - t
    c = ls_pos - z
    if mask is not None:
        c = jnp.where(mask, c, 0.0)
    n_sub = z.shape[1] // LANES
    pieces = [None] * n_sub
    for sb in reversed(range(n_sub)):
        sl = slice(sb * LANES, (sb + 1) * LANES)
        hi, lo = _split2(c[:, sl])
        r = _dot(hi, umat) + _dot(lo, umat)
        w = jnp.exp(ls_pos[:, sl] + (carry + r[:, :LANES]))
        pieces[sb] = w if mask is None else jnp.where(mask[:, sl], w, 0.0)
        carry = carry + r[:, LANES:]
    a = pieces[0] if n_sub == 1 else jnp.concatenate(pieces, axis=1)
    return a, carry


SB_ZERO_LOG = -104.0


def _sb_prompt_kernel(q_ref, k_ref, v_ref, o_ref, acc_ref, carry_ref, *, t):
    qi = pl.program_id(2)
    umat = _suffix_matrix()
    q = (q_ref[...] * SCALE).astype(BF16)
    row = lax.broadcasted_iota(jnp.int32, (t, t), 0)
    col = lax.broadcasted_iota(jnp.int32, (t, t), 1)

    def tile(kb, mask):
        rows = pl.ds(pl.multiple_of(kb * t, t), t)
        z = _dot_nt(q, k_ref[rows, :].astype(BF16))
        a, carry = _sb_weights(z, mask, carry_ref[...], umat)
        acc_ref[...] += _dot(a.astype(BF16), v_ref[rows, :].astype(BF16))
        carry_ref[...] = carry
        return jnp.max(carry)

    acc_ref[...] = jnp.zeros_like(acc_ref)
    carry_ref[...] = jnp.zeros_like(carry_ref)
    top = tile(qi, col < row)

    def cond(st):
        return jnp.logical_and(st[0] >= 0, st[1] > SB_ZERO_LOG)

    def body(st):
        return st[0] - 1, tile(st[0], None)

    lax.while_loop(cond, body, (qi - 1, top))
    o_ref[...] = acc_ref[...].astype(o_ref.dtype)


def sb_prompt(u_rest, batch, seq, t=256):
    t = min(t, seq)
    nq = seq // t
    cq, ck, cv = R_SB_Q // LANES, R_SB_K // LANES, R_SB_V // LANES
    return pl.pallas_call(
        functools.partial(_sb_prompt_kernel, t=t),
        grid=(batch, SB_HEADS, nq),
        in_specs=[pl.BlockSpec((t, LANES), lambda b, h, i: (b * nq + i, cq + h)),
                  pl.BlockSpec((seq, LANES), lambda b, h, i: (b, ck + h)),
                  pl.BlockSpec((seq, LANES), lambda b, h, i: (b, cv + h))],
        out_specs=pl.BlockSpec((t, LANES), lambda b, h, i: (b * nq + i, h)),
        out_shape=jax.ShapeDtypeStruct((batch * seq, SB_W), BF16),
        scratch_shapes=[pltpu.VMEM((t, LANES), F32), pltpu.VMEM((t, LANES), F32)],
        compiler_params=_cp(3),
        name="sb_prompt",
    )(u_rest, u_rest, u_rest)


def _hgrn_chunk(qr, fr, v, gr, lbp, norm_w, st, c, sub, c_real):
    log_lb, log_1m_lb, one_m_lb = lbp[0:1], lbp[1:2], lbp[2:3]
    q = _silu(qr)
    k = one_m_lb * _sigmoid(-fr)
    bb = log_1m_lb + _log_sigmoid(fr)
    mx = jnp.maximum(log_lb, bb)
    logf = mx + jnp.log1p(jnp.exp(-jnp.abs(log_lb - bb)))
    row = lax.broadcasted_iota(jnp.int32, (c, c), 0)
    col = lax.broadcasted_iota(jnp.int32, (c, c), 1)
    tri = jnp.where(row >= col, 1.0, 0.0).astype(BF16)
    hi, mid, lo = _split3(logf)
    b = _dot(tri, hi) + _dot(tri, mid) + _dot(tri, lo)
    o = _dot_nt((q * jnp.exp(b)).astype(BF16), st.astype(BF16))
    ridx = lax.broadcasted_iota(jnp.int32, (c, HEAD_DIM), 0)
    lane = lax.broadcasted_iota(jnp.int32, (sub, c), 1)
    att_rows = []
    for i in range(c // sub):
        r0 = i * sub
        b_i = b[r0:r0 + sub]
        q_i = q[r0:r0 + sub]
        att_i = jnp.zeros((sub, c), F32)
        if i > 0:
            rho = b_i[0:1]
            earlier = ridx < r0
            k_dec = jnp.where(earlier, k * jnp.exp(jnp.where(earlier, rho - b, 0.0)), 0.0)
            att_i = _dot_nt((q_i * jnp.exp(b_i - rho)).astype(BF16), k_dec.astype(BF16))
        trow = lax.broadcasted_iota(jnp.int32, (sub, 1), 0)
        for s in range(sub):
            d = q_i * jnp.exp(jnp.where(trow >= s, b_i - b_i[s:s + 1], 0.0)) * k[r0 + s:r0 + s + 1]
            colsum = jnp.sum(d, axis=1, keepdims=True)
            att_i = att_i + jnp.where((lane == r0 + s) & (trow >= s), colsum, 0.0)
        att_rows.append(att_i)
    att = att_rows[0] if len(att_rows) == 1 else jnp.concatenate(att_rows, axis=0)
    o = o + _dot(att.astype(BF16), v.astype(BF16))
    b_end = b[c_real - 1:c_real]
    real = ridx < c_real
    k_end = jnp.where(real, k * jnp.exp(jnp.where(real, b_end - b, 0.0)), 0.0)
    st = st * jnp.exp(b_end) + _dot_tn(v.astype(BF16), k_end.astype(BF16))
    o = o * lax.rsqrt(jnp.mean(o * o, axis=-1, keepdims=True) + NORM_EPS) * norm_w
    return o * _silu(gr), st


def _hgrn_kernel(q_ref, f_ref, i_ref, g_ref, lbp_ref, nw_ref, s0_ref, o_ref, s_out_ref, st_ref, *,
                 c, sub, n_chunks, hb):
    t = pl.program_id(2)

    @pl.when(t == 0)
    def _():
        for j in range(hb):
            st_ref[j] = s0_ref[j].T

    nw = nw_ref[...]
    in_refs = (q_ref, f_ref, i_ref, g_ref)

    if c < HG_SUB:
        pad = jnp.zeros((HG_SUB - c, HEAD_DIM), F32)
        for j in range(hb):
            cols = slice(j * HEAD_DIM, (j + 1) * HEAD_DIM)
            ins = [jnp.concatenate([r[:, cols], pad], axis=0) for r in in_refs]
            o, st = _hgrn_chunk(*ins, lbp_ref[j], nw, st_ref[j], HG_SUB, HG_SUB, c)
            st_ref[j] = st
            o_ref[:, cols] = o[:c].astype(o_ref.dtype)
    else:
        def body(ci, carry):
            rows = pl.ds(pl.multiple_of(ci * c, c), c)
            for j in range(hb):
                cols = slice(j * HEAD_DIM, (j + 1) * HEAD_DIM)
                o, st = _hgrn_chunk(*[r[rows, cols] for r in in_refs], lbp_ref[j], nw, st_ref[j], c, sub, c)
                st_ref[j] = st
                o_ref[rows, cols] = o.astype(o_ref.dtype)
            return carry

        lax.fori_loop(0, n_chunks, body, 0)

    @pl.when(t == pl.num_programs(2) - 1)
    def _():
        for j in range(hb):
            s_out_ref[j] = st_ref[j].T


def hgrn_lb_params(lb):
    rows = jnp.stack([jnp.log(lb), jnp.log1p(-lb), 1.0 - lb], axis=1)
    return jnp.concatenate([rows, jnp.zeros((lb.shape[0], SUBLANES - 3, lb.shape[1]), F32)], axis=1)


def hgrn(u_rest, row0, batch, seq, lbp, norm_w, s0, tl=512, hb=8):
    c = min(HG_CHUNK, seq)
    sub = min(HG_SUB, c)
    tl = min(tl, seq)
    nt = seq // tl
    rb0 = row0 // tl
    bw = hb * HEAD_DIM
    assert row0 % tl == 0 and seq % tl == 0 and tl % c == 0 and (c % HG_SUB == 0 or nt == 1) and HG_HEADS % hb == 0

    def col(off):
        return pl.BlockSpec((tl, bw), lambda b, h, t, off=off: (rb0 + b * nt + t, off // bw + h))

    o, s_out = pl.pallas_call(
        functools.partial(_hgrn_kernel, c=c, sub=sub, n_chunks=tl // c, hb=hb),
        grid=(batch, HG_HEADS // hb, nt),
        in_specs=[col(R_HG_Q), col(R_HG_F), col(R_HG_I), col(R_HG_G),
                  pl.BlockSpec((hb, SUBLANES, LANES), lambda b, h, t: (h, 0, 0)),
                  pl.BlockSpec((1, LANES), lambda b, h, t: (0, 0)),
                  pl.BlockSpec((None, hb, HEAD_DIM, HEAD_DIM), lambda b, h, t: (b, h, 0, 0))],
        out_specs=[pl.BlockSpec((tl, bw), lambda b, h, t: (b * nt + t, h)),
                   pl.BlockSpec((None, hb, HEAD_DIM, HEAD_DIM), lambda b, h, t: (b, h, 0, 0))],
        out_shape=[jax.ShapeDtypeStruct((batch * seq, HG_W), F32),
                   jax.ShapeDtypeStruct((batch, HG_HEADS, HEAD_DIM, HEAD_DIM), F32)],
        scratch_shapes=[pltpu.VMEM((hb, HEAD_DIM, HEAD_DIM), F32)],
        compiler_params=_cp(3),
        name="hgrn",
    )(u_rest, u_rest, u_rest, u_rest, lbp, norm_w.reshape(1, LANES), s0)
    return o, s_out


SEG_W = CMP_STRIDE * HEAD_DIM


def compress_params(pe, w1):
    w = jnp.concatenate([w1[:SEG_W], w1[SEG_W:]], axis=1).astype(BF16)
    rows = jnp.stack([pe[:CMP_STRIDE].reshape(SEG_W), pe[CMP_STRIDE:].reshape(SEG_W)])
    return w, jnp.concatenate([rows, jnp.zeros((SUBLANES - 2, SEG_W), F32)]).astype(BF16)


def _segment_products(r, w, pe):
    bias = _dot(pe, w)
    bias = jnp.concatenate([bias[0:1, :LANES], bias[1:2, LANES:]], axis=1)
    return _dot(r.astype(BF16), w) + bias


def _cmp1_prompt_kernel(x_ref, w_ref, pe_ref, o_ref, r_ref, *, n_seg):
    for l in range(CMP_STRIDE):
        r_ref[:, l * LANES:(l + 1) * LANES] = x_ref[pl.ds(l, n_seg, stride=CMP_STRIDE), :]
    o_ref[...] = _segment_products(r_ref[...], w_ref[...], pe_ref[...])


def compress_segments_prompt(u_head, batch, seq, wcat, pecat, tr=512):
    tr = min(tr, seq)
    n_seg = tr // CMP_STRIDE
    nt = seq // tr
    c0 = OFF_CMP // LANES
    return pl.pallas_call(
        functools.partial(_cmp1_prompt_kernel, n_seg=n_seg),
        grid=(batch, 2, NSA_KV_HEADS, nt),
        in_specs=[pl.BlockSpec((tr, LANES), lambda b, kv, g, t: (b * nt + t, c0 + kv * NSA_KV_HEADS + g)),
                  pl.BlockSpec((None, SEG_W, 2 * LANES), lambda b, kv, g, t: (kv, 0, 0)),
                  pl.BlockSpec((None, SUBLANES, SEG_W), lambda b, kv, g, t: (kv, 0, 0))],
        out_specs=pl.BlockSpec((None, None, None, n_seg, 2 * LANES), lambda b, kv, g, t: (b, kv, g, t, 0)),
        out_shape=jax.ShapeDtypeStruct((batch, 2, NSA_KV_HEADS, seq // CMP_STRIDE, 2 * LANES), F32),
        scratch_shapes=[pltpu.VMEM((n_seg, SEG_W), F32)],
        compiler_params=_cp(4),
        name="cmp_segments_prompt",
    )(u_head, wcat, pecat)


def _cmp2_kernel(pq_ref, w2_ref, o_ref, *, nc):
    pq = pq_ref[...]
    q_next = pltpu.roll(pq[:, LANES:], shift=nc - 1, axis=0)
    hid = _silu(pq[:, :LANES] + q_next)
    out = _dot(hid.astype(BF16), w2_ref[...].astype(BF16))
    row = lax.broadcasted_iota(jnp.int32, out.shape, 0)
    o_ref[...] = jnp.where(row < nc - 1, out, 0.0)


def compress_finish(pq, w2):
    batch, _, _, nc, _ = pq.shape
    return pl.pallas_call(
        functools.partial(_cmp2_kernel, nc=nc),
        grid=(batch, 2, NSA_KV_HEADS),
        in_specs=[pl.BlockSpec((None, None, None, nc, 2 * LANES), lambda b, kv, g: (b, kv, g, 0, 0)),
                  pl.BlockSpec((None, HEAD_DIM, HEAD_DIM), lambda b, kv, g: (kv, 0, 0))],
        out_specs=pl.BlockSpec((None, None, None, nc, LANES), lambda b, kv, g: (b, kv, g, 0, 0)),
        out_shape=jax.ShapeDtypeStruct((batch, 2, NSA_KV_HEADS, nc, LANES), F32),
        compiler_params=_cp(3),
        name="cmp_finish",
    )(pq, w2)


def _cmp_select_kernel(q_ref, kc_ref, vc_ref, o_ref, sel_ref, score_ref, *, tq, tqp, nc, n_cmp, n_slc, nsp, pos0):
    qi = pl.program_id(2)
    qb = q_ref[...]
    parts = []
    for h in range(NSA_GROUP):
        qh = qb[:, h * HEAD_DIM:(h + 1) * HEAD_DIM]
        if tqp > tq:
            qh = jnp.concatenate([qh, jnp.zeros((tqp - tq, HEAD_DIM), F32)], axis=0)
        parts.append(qh)
    q4 = jnp.concatenate(parts, axis=0).astype(BF16)
    rows = NSA_GROUP * tqp
    st = _dot_nt(kc_ref[...].astype(BF16), q4) * SCALE
    ci = lax.broadcasted_iota(jnp.int32, (nc, rows), 0)
    tok = lax.broadcasted_iota(jnp.int32, (nc, rows), 1) & (tqp - 1)
    tpos = pos0 + qi * tq + tok
    valid = (ci < n_cmp) & (ci * CMP_STRIDE + CMP_LEN - 1 <= tpos)
    st = jnp.where(valid, st, NEG_BIG)
    m = jnp.max(st, axis=0, keepdims=True)
    e = jnp.where(valid, jnp.exp(st - m), 0.0)
    den = jnp.sum(e, axis=0, keepdims=True)
    pt = e / jnp.where(den > 0, den, 1.0)
    o = _dot_tn(pt.astype(BF16), vc_ref[...].astype(BF16))
    for h in range(NSA_GROUP):
        o_ref[:, h * HEAD_DIM:(h + 1) * HEAD_DIM] = o[h * tqp:h * tqp + tq]
    psum = pt[:, 0:tqp]
    for h in range(1, NSA_GROUP):
        psum = psum + pt[:, h * tqp:(h + 1) * tqp]
    nsr = score_ref.shape[0]
    jj = lax.broadcasted_iota(jnp.int32, (nsr, nc), 0)
    ii = lax.broadcasted_iota(jnp.int32, (nsr, nc), 1)
    cover = ((ii * CMP_STRIDE < jj * SEL_BLOCK + SEL_BLOCK) & (ii * CMP_STRIDE + CMP_LEN - 1 >= jj * SEL_BLOCK)
             & (ii < n_cmp))
    cover = jnp.where(cover, 1.0, 0.0).astype(BF16)
    hi, mid, lo = _split3(psum)
    imp = _dot(cover, hi) + _dot(cover, mid) + _dot(cover, lo)
    j = lax.broadcasted_iota(jnp.int32, (nsr, tqp), 0)
    tpos2 = pos0 + qi * tq + lax.broadcasted_iota(jnp.int32, (nsr, tqp), 1)
    cur = tpos2 // SEL_BLOCK
    forced = (j == 0) | (j == cur) | (j == cur - 1)
    ok = (j * SEL_BLOCK <= tpos2) & (j < n_slc)
    score = jnp.where(ok, jnp.where(forced, NSA_GROUP + 1.0, imp), -1.0)
    score_ref[...] = score

    def body(jp, rank):
        other = score_ref[pl.ds(jp, 1), :]
        ahead = jnp.where(other > score, 1.0, jnp.where((other == score) & (jp < j), 1.0, 0.0))
        return rank + ahead

    n_valid = jnp.minimum(n_slc, (pos0 + qi * tq + tq - 1) // SEL_BLOCK + 1)
    rank = lax.fori_loop(0, n_valid, body, jnp.zeros((nsr, tqp), F32))
    sel_t = jnp.where((rank < SEL_TOPK) & (score >= 0), 1.0, 0.0)
    if nsp > nsr:
        sel_t = jnp.concatenate([sel_t, jnp.zeros((nsp - nsr, tqp), F32)], axis=0)
    sel_ref[...] = sel_t.T[:tq]


def cmp_select(u_head, row0, batch, seq, kv_cmp, n_cmp, n_slc, pos0, tq=256):
    tq = min(tq, seq)
    tqp = max(tq, LANES)
    nt = seq // tq
    rb0 = row0 // tq
    nc = kv_cmp.shape[3]
    nsp = -(-n_slc // LANES) * LANES
    gw = NSA_GROUP * HEAD_DIM
    assert row0 % tq == 0 and seq % tq == 0 and tqp & (tqp - 1) == 0
    return pl.pallas_call(
        functools.partial(_cmp_select_kernel, tq=tq, tqp=tqp, nc=nc, n_cmp=n_cmp, n_slc=n_slc, nsp=nsp, pos0=pos0),
        grid=(batch, NSA_KV_HEADS, nt),
        in_specs=[pl.BlockSpec((tq, gw), lambda b, g, t: (rb0 + b * nt + t, g)),
                  pl.BlockSpec((None, None, None, nc, LANES), lambda b, g, t: (b, 0, g, 0, 0)),
                  pl.BlockSpec((None, None, None, nc, LANES), lambda b, g, t: (b, 1, g, 0, 0))],
        out_specs=[pl.BlockSpec((tq, gw), lambda b, g, t: (b * nt + t, g)),
                   pl.BlockSpec((None, None, tq, nsp), lambda b, g, t: (b, g, t, 0))],
        out_shape=[jax.ShapeDtypeStruct((batch * seq, NSA_W), F32),
                   jax.ShapeDtypeStruct((batch, NSA_KV_HEADS, seq, nsp), F32)],
        scratch_shapes=[pltpu.VMEM((-(-n_slc // 16) * 16, tqp), F32)],
        compiler_params=_cp(3),
        name="cmp_select",
    )(u_head, kv_cmp, kv_cmp)


def _flash_step(s, mask, v, m_ref, l_ref, acc_ref):
    s = jnp.where(mask, s, NEG_BIG)
    m_prev = m_ref[...]
    m_new = jnp.maximum(m_prev, jnp.max(s, axis=1, keepdims=True))
    e = jnp.where(mask, jnp.exp(s - m_new), 0.0)
    alpha = jnp.exp(m_prev - m_new)
    l_ref[...] = alpha * l_ref[...] + jnp.sum(e, axis=1, keepdims=True)
    acc_ref[...] = alpha * acc_ref[...] + _dot(e.astype(BF16), v)
    m_ref[...] = m_new


def _flash_init(m_ref, l_ref, acc_ref):
    m_ref[...] = jnp.full_like(m_ref, NEG_BIG)
    l_ref[...] = jnp.zeros_like(l_ref)
    acc_ref[...] = jnp.zeros_like(acc_ref)


def _flash_result(l_ref, acc_ref):
    l = l_ref[...]
    return acc_ref[...] / jnp.where(l > 0, l, 1.0)


def _stack_heads(qb, pad_to=None):
    parts = []
    for h in range(NSA_GROUP):
        qh = qb[:, h * HEAD_DIM:(h + 1) * HEAD_DIM]
        if pad_to is not None and pad_to > qh.shape[0]:
            qh = jnp.concatenate([qh, jnp.zeros((pad_to - qh.shape[0], HEAD_DIM), qh.dtype)], axis=0)
        parts.append(qh)
    return jnp.concatenate(parts, axis=0)


def _flash_prompt_kernel(qi_tab, kb_tab, first_tab, last_tab, *refs, tq, tk, mode, nsp):
    if mode == "sel":
        q_ref, k_ref, v_ref, sel_ref, o_ref, m_ref, l_ref, acc_ref = refs
    else:
        q_ref, k_ref, v_ref, o_ref, m_ref, l_ref, acc_ref = refs
    p = pl.program_id(2)
    qi = qi_tab[p]
    kb = kb_tab[p]

    @pl.when(first_tab[p] == 1)
    def _():
        _flash_init(m_ref, l_ref, acc_ref)

    q4 = _stack_heads(q_ref[...]).astype(BF16)
    s = _dot_nt(q4, k_ref[...].astype(BF16)) * SCALE
    qpos = qi * tq + lax.broadcasted_iota(jnp.int32, (tq, tk), 0)
    kpos = kb * tk + lax.broadcasted_iota(jnp.int32, (tq, tk), 1)
    if mode == "win":
        d = qpos - kpos
        mask = (d >= 0) & (d < WINDOW)
    else:
        jj = lax.broadcasted_iota(jnp.int32, (nsp, tk), 0)
        kk = lax.broadcasted_iota(jnp.int32, (nsp, tk), 1)
        expand = jnp.where(jj == kb * (tk // SEL_BLOCK) + kk // SEL_BLOCK, 1.0, 0.0).astype(BF16)
        chosen = _dot(sel_ref[...].astype(BF16), expand)
        mask = (chosen > 0.5) & (kpos <= qpos)
    mask4 = jnp.concatenate([mask] * NSA_GROUP, axis=0)
    _flash_step(s, mask4, v_ref[...].astype(BF16), m_ref, l_ref, acc_ref)

    @pl.when(last_tab[p] == 1)
    def _():
        o = _flash_result(l_ref, acc_ref)
        for h in range(NSA_GROUP):
            o_ref[:, h * HEAD_DIM:(h + 1) * HEAD_DIM] = o[h * tq:(h + 1) * tq]


def flash_prompt(u_head, batch, seq, mode, sel=None, tq=256, tk=None):
    tq = min(tq, seq)
    tk = min(tk or (256 if mode == "win" else 512), seq)
    nq, nk = seq // tq, seq // tk
    pairs = []
    for qi in range(nq):
        if mode == "win":
            lo = max(0, (qi * tq - WINDOW + 1) // tk)
        else:
            lo = 0
        hi = (qi * tq + tq - 1) // tk
        kbs = list(range(lo, hi + 1))
        pairs += [(qi, kb, int(kb == kbs[0]), int(kb == kbs[-1])) for kb in kbs]
    tabs = [jnp.asarray([p[i] for p in pairs], jnp.int32) for i in range(4)]
    off = OFF_WIN if mode == "win" else OFF_SEL
    ck = off // LANES
    cv = ck + NSA_KV_HEADS
    gw = NSA_GROUP * HEAD_DIM
    in_specs = [
        pl.BlockSpec((tq, gw), lambda b, g, p, qt, kt, ft, lt: (b * nq + qt[p], g)),
        pl.BlockSpec((tk, LANES), lambda b, g, p, qt, kt, ft, lt: (b * nk + kt[p], ck + g)),
        pl.BlockSpec((tk, LANES), lambda b, g, p, qt, kt, ft, lt: (b * nk + kt[p], cv + g)),
    ]
    args = [u_head, u_head, u_head]
    nsp = 0
    if mode == "sel":
        nsp = sel.shape[-1]
        in_specs.append(pl.BlockSpec((None, None, tq, nsp), lambda b, g, p, qt, kt, ft, lt: (b, g, qt[p], 0)))
        args.append(sel)
    rows = NSA_GROUP * tq
    grid_spec = pltpu.PrefetchScalarGridSpec(
        num_scalar_prefetch=4,
        grid=(batch, NSA_KV_HEADS, len(pairs)),
        in_specs=in_specs,
        out_specs=pl.BlockSpec((tq, gw), lambda b, g, p, qt, kt, ft, lt: (b * nq + qt[p], g)),
        scratch_shapes=[pltpu.VMEM((rows, 1), F32), pltpu.VMEM((rows, 1), F32), pltpu.VMEM((rows, HEAD_DIM), F32)],
    )
    return pl.pallas_call(
        functools.partial(_flash_prompt_kernel, tq=tq, tk=tk, mode=mode, nsp=nsp),
        grid_spec=grid_spec,
        out_shape=jax.ShapeDtypeStruct((batch * seq, NSA_W), F32),
        compiler_params=_cp(3),
        name="flash_" + mode,
    )(*tabs, *args)


def _sel_prompt_kernel(q_ref, k_ref, v_ref, sel_ref, o_ref, m_ref, l_ref, acc_ref, *, tq, tk, nsp):
    qi = pl.program_id(2)
    q4 = (_stack_heads(q_ref[...]) * SCALE).astype(BF16)
    selb = sel_ref[...].astype(BF16)
    _flash_init(m_ref, l_ref, acc_ref)
    row = lax.broadcasted_iota(jnp.int32, (tq, tk), 0)
    col = lax.broadcasted_iota(jnp.int32, (tq, tk), 1)
    jj = lax.broadcasted_iota(jnp.int32, (nsp, tk), 0)
    kk = lax.broadcasted_iota(jnp.int32, (nsp, tk), 1) // SEL_BLOCK

    def body(kb, c):
        keys = pl.ds(pl.multiple_of(kb * tk, tk), tk)
        s = _dot_nt(q4, k_ref[keys, :].astype(BF16))
        expand = jnp.where(jj == kb * (tk // SEL_BLOCK) + kk, 1.0, 0.0).astype(BF16)
        chosen = _dot(selb, expand)
        ok = (chosen > 0.5) & (kb * tk + col <= qi * tq + row)
        bias = jnp.where(ok, 0.0, 2.0 * NEG_BIG)
        s = (s.reshape(NSA_GROUP, tq, tk) + bias[None]).reshape(NSA_GROUP * tq, tk)
        m_prev = m_ref[...]
        m_new = jnp.maximum(m_prev, jnp.max(s, axis=1, keepdims=True))
        e = jnp.exp(s - m_new)
        alpha = jnp.exp(m_prev - m_new)
        l_ref[...] = alpha * l_ref[...] + jnp.sum(e, axis=1, keepdims=True)
        acc_ref[...] = alpha * acc_ref[...] + _dot(e.astype(BF16), v_ref[keys, :].astype(BF16))
        m_ref[...] = m_new
        return c

    lax.fori_loop(0, (qi * tq + tq - 1) // tk + 1, body, 0)
    o = _flash_result(l_ref, acc_ref)
    for h in range(NSA_GROUP):
        o_ref[:, h * HEAD_DIM:(h + 1) * HEAD_DIM] = o[h * tq:(h + 1) * tq]


def sel_prompt(u_head, batch, seq, sel, tq=256, tk=512):
    tq = min(tq, seq)
    tk = min(tk, seq)
    nq = seq // tq
    nsp = sel.shape[-1]
    ck = OFF_SEL // LANES
    cv = ck + NSA_KV_HEADS
    gw = NSA_GROUP * HEAD_DIM
    rows = NSA_GROUP * tq
    assert seq % tq == 0 and seq % tk == 0 and tk % SEL_BLOCK == 0
    return pl.pallas_call(
        functools.partial(_sel_prompt_kernel, tq=tq, tk=tk, nsp=nsp),
        grid=(batch, NSA_KV_HEADS, nq),
        in_specs=[pl.BlockSpec((tq, gw), lambda b, g, i: (b * nq + i, g)),
                  pl.BlockSpec((seq, LANES), lambda b, g, i: (b, ck + g)),
                  pl.BlockSpec((seq, LANES), lambda b, g, i: (b, cv + g)),
                  pl.BlockSpec((None, None, tq, nsp), lambda b, g, i: (b, g, i, 0))],
        out_specs=pl.BlockSpec((tq, gw), lambda b, g, i: (b * nq + i, g)),
        out_shape=jax.ShapeDtypeStruct((batch * seq, NSA_W), F32),
        scratch_shapes=[pltpu.VMEM((rows, 1), F32), pltpu.VMEM((rows, 1), F32), pltpu.VMEM((rows, HEAD_DIM), F32)],
        compiler_params=_cp(3),
        name="sel_prompt",
    )(u_head, u_head, u_head, sel)


def _win_prompt_kernel(q_ref, k_ref, v_ref, o_ref, *, tq, nk, seq):
    q0 = pl.program_id(2) * tq
    start = pl.multiple_of(jnp.clip(q0 - WINDOW, 0, seq - nk), tq)
    keys = pl.ds(start, nk)
    q4 = (_stack_heads(q_ref[...]) * SCALE).astype(BF16)
    s = _dot_nt(q4, k_ref[keys, :].astype(BF16))
    d = (q0 + lax.broadcasted_iota(jnp.int32, (tq, nk), 0)) - (start + lax.broadcasted_iota(jnp.int32, (tq, nk), 1))
    bias = jnp.where((d >= 0) & (d < WINDOW), 0.0, NEG_BIG)
    s = s + jnp.concatenate([bias] * NSA_GROUP, axis=0)
    e = jnp.exp(s - jnp.max(s, axis=1, keepdims=True))
    o = _dot(e.astype(BF16), v_ref[keys, :].astype(BF16)) / jnp.sum(e, axis=1, keepdims=True)
    for h in range(NSA_GROUP):
        o_ref[:, h * HEAD_DIM:(h + 1) * HEAD_DIM] = o[h * tq:(h + 1) * tq]


def win_prompt(u_head, batch, seq, tq=256):
    tq = min(tq, seq)
    nk = min(seq, WINDOW + tq)
    nq = seq // tq
    ck = OFF_WIN // LANES
    cv = ck + NSA_KV_HEADS
    gw = NSA_GROUP * HEAD_DIM
    assert seq % tq == 0 and WINDOW % tq == 0
    return pl.pallas_call(
        functools.partial(_win_prompt_kernel, tq=tq, nk=nk, seq=seq),
        grid=(batch, NSA_KV_HEADS, nq),
        in_specs=[pl.BlockSpec((tq, gw), lambda b, g, i: (b * nq + i, g)),
                  pl.BlockSpec((seq, LANES), lambda b, g, i: (b, ck + g)),
                  pl.BlockSpec((seq, LANES), lambda b, g, i: (b, cv + g))],
        out_specs=pl.BlockSpec((tq, gw), lambda b, g, i: (b * nq + i, g)),
        out_shape=jax.ShapeDtypeStruct((batch * seq, NSA_W), F32),
        compiler_params=_cp(3),
        name="win_prompt",
    )(u_head, u_head, u_head)


def _nsa_combine_kernel(gate_ref, a_ref, b_ref, c_ref, o_ref):
    gate = _sigmoid(gate_ref[...])
    for h in range(NSA_HEADS):
        sl = slice(h * HEAD_DIM, (h + 1) * HEAD_DIM)
        out = (gate[:, h:h + 1] * a_ref[:, sl] + gate[:, NSA_HEADS + h:NSA_HEADS + h + 1] * b_ref[:, sl]
               + gate[:, 2 * NSA_HEADS + h:2 * NSA_HEADS + h + 1] * c_ref[:, sl])
        o_ref[:, sl] = out.astype(o_ref.dtype)


def nsa_combine(u_head, row0, o_cmp, o_sel, o_win, tm=256):
    n = o_cmp.shape[0]
    tm = min(tm, n)
    rb0 = row0 // tm
    assert row0 % tm == 0 and n % tm == 0
    spec = pl.BlockSpec((tm, NSA_W), lambda i: (i, 0))
    return pl.pallas_call(
        _nsa_combine_kernel,
        grid=(n // tm,),
        in_specs=[pl.BlockSpec((tm, LANES), lambda i: (rb0 + i, OFF_NSA_GATE // LANES)), spec, spec, spec],
        out_specs=spec,
        out_shape=jax.ShapeDtypeStruct((n, NSA_W), F32),
        compiler_params=_cp(1),
        name="nsa_combine",
    )(u_head, o_cmp, o_sel, o_win)


def _page_rows(page_ref, first, n_rows, stride):
    return page_ref[pl.ds(first, n_rows, stride=stride), :]


def _pad_rows(x, n):
    if x.shape[0] >= n:
        return x
    return jnp.concatenate([x, jnp.zeros((n - x.shape[0], x.shape[1]), x.dtype)], axis=0)


def _cmp1_sample_kernel(pt_ref, *refs, pps, page):
    page_refs = refs[:pps]
    w_ref, pe_ref, o_ref, r_ref = refs[pps:]
    segs = page // CMP_STRIDE
    stride = CMP_STRIDE * 2 * NSA_KV_HEADS
    for kv in range(2):
        for g in range(NSA_KV_HEADS):
            for k in range(pps):
                for l in range(CMP_STRIDE):
                    r_ref[k * segs:(k + 1) * segs, l * LANES:(l + 1) * LANES] = _page_rows(
                        page_refs[k], l * 2 * NSA_KV_HEADS + kv * NSA_KV_HEADS + g, segs, stride)
            o_ref[kv, g] = _segment_products(r_ref[...], w_ref[kv], pe_ref[kv])


def compress_segments_sample(pool, layer, pt_flat, batch, n_pages, wcat, pecat, pps=16):
    page = pool.shape[2] // (2 * NSA_KV_HEADS)
    pps = min(pps, n_pages)
    assert n_pages % pps == 0 and page % CMP_STRIDE == 0
    segs = page // CMP_STRIDE
    in_specs = [pl.BlockSpec((None, None, pool.shape[2], LANES),
                             lambda b, s, pt, k=k: (layer, pt[b * n_pages + s * pps + k], 0, 0)) for k in range(pps)]
    in_specs += [pl.BlockSpec((2, SEG_W, 2 * LANES), lambda b, s, pt: (0, 0, 0)),
                 pl.BlockSpec((2, SUBLANES, SEG_W), lambda b, s, pt: (0, 0, 0))]
    grid_spec = pltpu.PrefetchScalarGridSpec(
        num_scalar_prefetch=1,
        grid=(batch, n_pages // pps),
        in_specs=in_specs,
        out_specs=pl.BlockSpec((None, 2, NSA_KV_HEADS, pps * segs, 2 * LANES), lambda b, s, pt: (b, 0, 0, s, 0)),
        scratch_shapes=[pltpu.VMEM((pps * segs, SEG_W), F32)],
    )
    return pl.pallas_call(
        functools.partial(_cmp1_sample_kernel, pps=pps, page=page),
        grid_spec=grid_spec,
        out_shape=jax.ShapeDtypeStruct((batch, 2, NSA_KV_HEADS, n_pages * segs, 2 * LANES), F32),
        compiler_params=_cp(2),
        name="cmp_segments_sample",
    )(pt_flat, *([pool] * pps), wcat, pecat)


def _sel_sample_kernel(pt_ref, *refs, pps, page, ds, past, nsp):
    q_ref, new_ref, sel_ref, selstep_ref = refs[:4]
    page_refs = refs[4:4 + pps]
    o_ref, m_ref, l_ref, acc_ref = refs[4 + pps:]
    s = pl.program_id(1)
    rows = NSA_GROUP * ds
    keys = pps * page
    bps = keys // SEL_BLOCK
    stride = 2 * NSA_KV_HEADS

    @pl.when(s == 0)
    def _():
        _flash_init(m_ref, l_ref, acc_ref)

    jj = lax.broadcasted_iota(jnp.int32, (bps, keys), 0)
    kk = lax.broadcasted_iota(jnp.int32, (bps, keys), 1)
    expand = jnp.where(jj == kk // SEL_BLOCK, 1.0, 0.0).astype(BF16)
    for g in range(NSA_KV_HEADS):
        q4 = _stack_heads(q_ref[:, g * NSA_GROUP * HEAD_DIM:(g + 1) * NSA_GROUP * HEAD_DIM]).astype(BF16)
        sel4 = jnp.concatenate([sel_ref[g]] * NSA_GROUP, axis=0)
        sel_here = jnp.concatenate([selstep_ref[g]] * NSA_GROUP, axis=0)
        k_all = jnp.concatenate([_page_rows(r, g, page, stride) for r in page_refs], axis=0).astype(BF16)
        v_all = jnp.concatenate([_page_rows(r, NSA_KV_HEADS + g, page, stride) for r in page_refs],
                                axis=0).astype(BF16)
        sc = _dot_nt(q4, k_all) * SCALE
        chosen = _dot(sel_here.astype(BF16), expand)
        _flash_step(sc, chosen > 0.5, v_all, m_ref.at[g], l_ref.at[g], acc_ref.at[g])

        @pl.when(s == pl.num_programs(1) - 1)
        def _():
            k_new = _pad_rows(new_ref[:, g * HEAD_DIM:(g + 1) * HEAD_DIM], LANES).astype(BF16)
            v_new = _pad_rows(new_ref[:, (NSA_KV_HEADS + g) * HEAD_DIM:(NSA_KV_HEADS + g + 1) * HEAD_DIM],
                              LANES).astype(BF16)
            sn = _dot_nt(q4, k_new) * SCALE
            blk = past // SEL_BLOCK
            j = lax.broadcasted_iota(jnp.int32, (rows, LANES), 1)
            t = lax.broadcasted_iota(jnp.int32, (rows, LANES), 0) % ds
            mask = (sel4[:, blk:blk + 1] > 0.5) & (j <= t)
            _flash_step(sn, mask, v_new, m_ref.at[g], l_ref.at[g], acc_ref.at[g])
            o = _flash_result(l_ref.at[g], acc_ref.at[g])
            for h in range(NSA_GROUP):
                c0 = (g * NSA_GROUP + h) * HEAD_DIM
                o_ref[:, c0:c0 + HEAD_DIM] = o[h * ds:(h + 1) * ds]


def sel_sample(u_head, row0, pool, layer, pt_flat, sel, batch, ds, n_pages, pps=32):
    page = pool.shape[2] // (2 * NSA_KV_HEADS)
    pps = min(pps, n_pages)
    past = n_pages * page
    nsp = sel.shape[-1]
    rows = NSA_GROUP * ds
    ns = n_pages // pps
    bps = pps * page // SEL_BLOCK
    assert n_pages % pps == 0 and row0 % ds == 0 and past % SEL_BLOCK == 0 and ds <= SEL_BLOCK
    assert page % SEL_BLOCK == 0
    sel_steps = sel[..., :ns * bps].reshape(batch, NSA_KV_HEADS, ds, ns, bps).transpose(0, 1, 3, 2, 4)
    in_specs = [pl.BlockSpec((ds, NSA_W), lambda b, s, pt: (row0 // ds + b, 0)),
                pl.BlockSpec((ds, KV_W), lambda b, s, pt: (row0 // ds + b, OFF_SEL // KV_W)),
                pl.BlockSpec((None, NSA_KV_HEADS, ds, nsp), lambda b, s, pt: (b, 0, 0, 0)),
                pl.BlockSpec((None, NSA_KV_HEADS, None, ds, bps), lambda b, s, pt: (b, 0, s, 0, 0))]
    in_specs += [pl.BlockSpec((None, None, pool.shape[2], LANES),
                              lambda b, s, pt, k=k: (layer, pt[b * n_pages + s * pps + k], 0, 0)) for k in range(pps)]
    grid_spec = pltpu.PrefetchScalarGridSpec(
        num_scalar_prefetch=1,
        grid=(batch, n_pages // pps),
        in_specs=in_specs,
        out_specs=pl.BlockSpec((ds, NSA_W), lambda b, s, pt: (b, 0)),
        scratch_shapes=[pltpu.VMEM((NSA_KV_HEADS, rows, 1), F32), pltpu.VMEM((NSA_KV_HEADS, rows, 1), F32),
                        pltpu.VMEM((NSA_KV_HEADS, rows, HEAD_DIM), F32)],
    )
    return pl.pallas_call(
        functools.partial(_sel_sample_kernel, pps=pps, page=page, ds=ds, past=past, nsp=nsp),
        grid_spec=grid_spec,
        out_shape=jax.ShapeDtypeStruct((batch * ds, NSA_W), F32),
        compiler_params=_cp(2),
        name="sel_sample",
    )(pt_flat, u_head, u_head, sel, sel_steps, *([pool] * pps))


def _win_sample_kernel(q_ref, new_ref, buf_ref, o_ref, m_ref, l_ref, acc_ref, *, ds, nbuf):
    rows = NSA_GROUP * ds
    stride = 2 * NSA_KV_HEADS
    keys = nbuf + LANES
    i = lax.broadcasted_iota(jnp.int32, (rows, keys), 1)
    t = lax.broadcasted_iota(jnp.int32, (rows, keys), 0) % ds
    d = jnp.where(i < nbuf, t + nbuf - i, t - (i - nbuf))
    mask = (d >= 0) & (d < WINDOW) & (i < nbuf + ds)
    for g in range(NSA_KV_HEADS):
        _flash_init(m_ref, l_ref, acc_ref)
        q4 = _stack_heads(q_ref[:, g * NSA_GROUP * HEAD_DIM:(g + 1) * NSA_GROUP * HEAD_DIM]).astype(BF16)
        k_new = _pad_rows(new_ref[:, g * HEAD_DIM:(g + 1) * HEAD_DIM], LANES)
        v_new = _pad_rows(new_ref[:, (NSA_KV_HEADS + g) * HEAD_DIM:(NSA_KV_HEADS + g + 1) * HEAD_DIM], LANES)
        k_all = jnp.concatenate([_page_rows(buf_ref, g, nbuf, stride), k_new], axis=0).astype(BF16)
        v_all = jnp.concatenate([_page_rows(buf_ref, NSA_KV_HEADS + g, nbuf, stride), v_new], axis=0).astype(BF16)
        _flash_step(_dot_nt(q4, k_all) * SCALE, mask, v_all, m_ref, l_ref, acc_ref)
        o = _flash_result(l_ref, acc_ref)
        for h in range(NSA_GROUP):
            c0 = (g * NSA_GROUP + h) * HEAD_DIM
            o_ref[:, c0:c0 + HEAD_DIM] = o[h * ds:(h + 1) * ds]


def win_sample(u_head, row0, buf, layer, batch, ds):
    nbuf = buf.shape[2] // (2 * NSA_KV_HEADS)
    rows = NSA_GROUP * ds
    return pl.pallas_call(
        functools.partial(_win_sample_kernel, ds=ds, nbuf=nbuf),
        grid=(batch,),
        in_specs=[pl.BlockSpec((ds, NSA_W), lambda b: (row0 // ds + b, 0)),
                  pl.BlockSpec((ds, KV_W), lambda b: (row0 // ds + b, OFF_WIN // KV_W)),
                  pl.BlockSpec((None, None, buf.shape[2], LANES), lambda b: (layer, b, 0, 0))],
        out_specs=pl.BlockSpec((ds, NSA_W), lambda b: (b, 0)),
        out_shape=jax.ShapeDtypeStruct((batch * ds, NSA_W), F32),
        scratch_shapes=[pltpu.VMEM((rows, 1), F32), pltpu.VMEM((rows, 1), F32), pltpu.VMEM((rows, HEAD_DIM), F32)],
        compiler_params=_cp(1),
        name="win_sample",
    )(u_head, u_head, buf)


SB_ROWS = 16


def _sb_sample_kernel(pt_ref, q_ref, kn_ref, vn_ref, pool_hbm, o_ref, acc_ref, carry_ref, kv_ref, sem, *,
                      pps, page, ds, n_pages, layer):
    b = pl.program_id(0)
    stride = 2 * SB_HEADS
    umat = _suffix_matrix()
    q_heads = [_pad_rows(q_ref[:, h * HEAD_DIM:(h + 1) * HEAD_DIM] * SCALE, SB_ROWS).astype(BF16)
               for h in range(SB_HEADS)]
    rows = SB_HEADS * SB_ROWS
    real_row = lax.broadcasted_iota(jnp.int32, (rows, LANES), 0) % SB_ROWS < ds

    def accumulate(k_heads, v_heads, mask):
        z = jnp.concatenate([_dot_nt(q_heads[h], k_heads[h]) for h in range(SB_HEADS)], axis=0)
        a, carry = _sb_weights(z, mask, carry_ref[...], umat)
        carry_ref[...] = carry
        a = a.astype(BF16)
        for h in range(SB_HEADS):
            sl = slice(h * SB_ROWS, (h + 1) * SB_ROWS)
            acc_ref[sl, :] += _dot(a[sl], v_heads[h])
        return jnp.max(jnp.where(real_row, carry, NEG_BIG))

    acc_ref[...] = jnp.zeros_like(acc_ref)
    carry_ref[...] = jnp.zeros_like(carry_ref)
    k_new = [_pad_rows(kn_ref[:, h * HEAD_DIM:(h + 1) * HEAD_DIM], LANES).astype(BF16) for h in range(SB_HEADS)]
    v_new = [_pad_rows(vn_ref[:, h * HEAD_DIM:(h + 1) * HEAD_DIM], LANES).astype(BF16) for h in range(SB_HEADS)]
    j = lax.broadcasted_iota(jnp.int32, (rows, LANES), 1)
    t = lax.broadcasted_iota(jnp.int32, (rows, LANES), 0) % SB_ROWS
    top = accumulate(k_new, v_new, j < t)

    def page_copy(g, k):
        pid = pt_ref[b * n_pages + n_pages - (g + 1) * pps + k]
        return pltpu.make_async_copy(pool_hbm.at[layer, pid], kv_ref.at[k], sem)

    def cond(st):
        return jnp.logical_and(st[0] < n_pages // pps, st[1] > SB_ZERO_LOG)

    def body(st):
        g = st[0]
        for k in range(pps):
            page_copy(g, k).start()
        for k in range(pps):
            page_copy(g, k).wait()
        k_heads = [jnp.concatenate([_page_rows(kv_ref.at[k], h, page, stride) for k in range(pps)],
                                   axis=0).astype(BF16) for h in range(SB_HEADS)]
        v_heads = [jnp.concatenate([_page_rows(kv_ref.at[k], SB_HEADS + h, page, stride) for k in range(pps)],
                                   axis=0).astype(BF16) for h in range(SB_HEADS)]
        return g + 1, accumulate(k_heads, v_heads, None)

    lax.while_loop(cond, body, (0, top))
    for h in range(SB_HEADS):
        o_ref[:, h * HEAD_DIM:(h + 1) * HEAD_DIM] = acc_ref[h * SB_ROWS:h * SB_ROWS + ds, :]


def sb_sample(u_rest, row0, pool, layer, pt_flat, batch, ds, n_pages, pps=2):
    page = pool.shape[2] // (2 * SB_HEADS)
    pps = min(pps, n_pages)
    assert n_pages % pps == 0 and row0 % ds == 0 and ds <= SB_ROWS
    rb = row0 // ds
    grid_spec = pltpu.PrefetchScalarGridSpec(
        num_scalar_prefetch=1,
        grid=(batch,),
        in_specs=[pl.BlockSpec((ds, SB_W), lambda b, pt: (rb + b, R_SB_Q // SB_W)),
                  pl.BlockSpec((ds, SB_W), lambda b, pt: (rb + b, R_SB_K // SB_W)),
                  pl.BlockSpec((ds, SB_W), lambda b, pt: (rb + b, R_SB_V // SB_W)),
                  pl.BlockSpec(memory_space=pl.ANY)],
        out_specs=pl.BlockSpec((ds, SB_W), lambda b, pt: (b, 0)),
        scratch_shapes=[pltpu.VMEM((SB_HEADS * SB_ROWS, HEAD_DIM), F32), pltpu.VMEM((SB_HEADS * SB_ROWS, LANES), F32),
                        pltpu.VMEM((pps, pool.shape[2], LANES), F32), pltpu.SemaphoreType.DMA],
    )
    return pl.pallas_call(
        functools.partial(_sb_sample_kernel, pps=pps, page=page, ds=ds, n_pages=n_pages, layer=layer),
        grid_spec=grid_spec,
        out_shape=jax.ShapeDtypeStruct((batch * ds, SB_W), F32),
        compiler_params=_cp(1),
        name="sb_sample",
    )(pt_flat, u_rest, u_rest, u_rest, pool)


def _router_kernel(x_ref, g_ref, w_ref, b_ref, h_ref, o_ref):
    x = x_ref[...]
    h = x * lax.rsqrt(jnp.mean(x * x, axis=-1, keepdims=True) + NORM_EPS) * g_ref[...]
    h_ref[...] = h
    hh, hm, hl = _split3(h)
    wh, wm, wl = _split3(w_ref[...])
    logits = (_dot(hh, wh) + _dot(hh, wm) + _dot(hm, wh) + _dot(hh, wl) + _dot(hl, wh) + _dot(hm, wm)) + b_ref[...]
    lane = lax.broadcasted_iota(jnp.int32, logits.shape, 1)
    logits = jnp.where(lane < N_EXPERTS, logits, NEG_BIG)
    m1 = jnp.max(logits, axis=1, keepdims=True)
    i1 = jnp.min(jnp.where(logits == m1, lane, LANES), axis=1, keepdims=True)
    rest = jnp.where(lane == i1, NEG_BIG, logits)
    m2 = jnp.max(rest, axis=1, keepdims=True)
    i2 = jnp.min(jnp.where(rest == m2, lane, LANES), axis=1, keepdims=True)
    e = jnp.exp(m2 - m1)
    g1 = 1.0 / (1.0 + e)
    g2 = e / (1.0 + e)
    o_ref[...] = jnp.where(lane == 0, i1.astype(F32), jnp.where(lane == 1, i2.astype(F32),
                           jnp.where(lane == 2, g1, jnp.where(lane == 3, g2, 0.0))))


def moe_router(x, g, rw, rb, tm):
    m, d = x.shape
    w = jnp.zeros((d, LANES), F32).at[:, :N_EXPERTS].set(rw)
    b = jnp.zeros((1, LANES), F32).at[0, :N_EXPERTS].set(rb.astype(F32))
    return pl.pallas_call(
        _router_kernel,
        grid=(m // tm,),
        in_specs=[pl.BlockSpec((tm, d), lambda i: (i, 0)), pl.BlockSpec((1, d), lambda i: (0, 0)),
                  pl.BlockSpec((d, LANES), lambda i: (0, 0)), pl.BlockSpec((1, LANES), lambda i: (0, 0))],
        out_specs=[pl.BlockSpec((tm, d), lambda i: (i, 0)), pl.BlockSpec((tm, LANES), lambda i: (i, 0))],
        out_shape=[jax.ShapeDtypeStruct((m, d), F32), jax.ShapeDtypeStruct((m, LANES), F32)],
        compiler_params=_cp(1),
        name="moe_router",
    )(x, g.reshape(1, d), w, b)


def _row_copy(src_hbm, row, dst_ref, r, sem):
    return pltpu.make_async_copy(src_hbm.at[pl.ds(row, 1), :], dst_ref.at[pl.ds(r, 1), :], sem)


def _gather_kernel(tok_ref, nu_ref, h_hbm, o_ref, buf_ref, sem):
    blk = pl.program_id(0)
    n = buf_ref.shape[0]

    @pl.when(blk < nu_ref[0])
    def _():
        def issue(r, c):
            _row_copy(h_hbm, tok_ref[blk * n + r], buf_ref, r, sem).start()
            return c

        lax.fori_loop(0, n, issue, 0)

        def wait(r, c):
            _row_copy(h_hbm, 0, buf_ref, r, sem).wait()
            return c

        lax.fori_loop(0, n, wait, 0)
        o_ref[...] = buf_ref[...].astype(o_ref.dtype)

    @pl.when(blk >= nu_ref[0])
    def _():
        o_ref[...] = jnp.zeros_like(o_ref)


def _used_block(i, nu):
    return jnp.minimum(i, nu[0] - 1)


def moe_gather(h, tok_buf, n_used):
    p = tok_buf.shape[0]
    d = h.shape[1]
    grid_spec = pltpu.PrefetchScalarGridSpec(
        num_scalar_prefetch=2,
        grid=(p // MOE_BLOCK,),
        in_specs=[pl.BlockSpec(memory_space=pl.ANY)],
        out_specs=pl.BlockSpec((MOE_BLOCK, d), lambda i, tok, nu: (i, 0)),
        scratch_shapes=[pltpu.VMEM((MOE_BLOCK, d), F32), pltpu.SemaphoreType.DMA],
    )
    return pl.pallas_call(
        _gather_kernel,
        grid_spec=grid_spec,
        out_shape=jax.ShapeDtypeStruct((p, d), BF16),
        compiler_params=_cp(1),
        name="moe_gather",
    )(tok_buf, n_used, h)


def _moe_mm_kernel(be_ref, nu_ref, *refs, n_w, mode):
    a_ref = refs[0]
    w_refs = refs[1:1 + n_w]
    o_ref = refs[1 + n_w]
    wbf_refs = refs[2 + n_w:]
    blk = pl.program_id(1)
    changed = jnp.logical_or(blk == 0, be_ref[blk] != be_ref[jnp.maximum(blk - 1, 0)])

    @pl.when(changed)
    def _():
        for w_ref, wbf_ref in zip(w_refs, wbf_refs):
            wbf_ref[...] = w_ref[...].astype(BF16)

    @pl.when(blk < nu_ref[0])
    def _():
        prods = [_dot(a_ref[...], r[...]) for r in wbf_refs]
        o_ref[...] = _mm_epilogue(mode, prods, ()).astype(o_ref.dtype)

    @pl.when(blk >= nu_ref[0])
    def _():
        o_ref[...] = jnp.zeros_like(o_ref)


def moe_matmul(a, w_list, moe_index, blk_e, n_used, mode, out_dtype, tn=512):
    p, k = a.shape
    n_out = w_list[0].shape[-1]
    tn = _tile(n_out, tn)
    grid_spec = pltpu.PrefetchScalarGridSpec(
        num_scalar_prefetch=2,
        grid=(n_out // tn, p // MOE_BLOCK),
        in_specs=[pl.BlockSpec((MOE_BLOCK, k), lambda j, i, be, nu: (_used_block(i, nu), 0))]
        + [pl.BlockSpec((None, None, k, tn), lambda j, i, be, nu: (moe_index, be[i], 0, j)) for _ in w_list],
        out_specs=pl.BlockSpec((MOE_BLOCK, tn), lambda j, i, be, nu: (i, j)),
        scratch_shapes=[pltpu.VMEM((k, tn), BF16) for _ in w_list],
    )
    return pl.pallas_call(
        functools.partial(_moe_mm_kernel, n_w=len(w_list), mode=mode),
        grid_spec=grid_spec,
        out_shape=jax.ShapeDtypeStruct((p, n_out), out_dtype),
        compiler_params=_cp(2),
        name="moe_mm_" + mode,
    )(blk_e, n_used, a, *w_list)


def _moe_combine_kernel(pos_ref, y_hbm, x_ref, gate_ref, o_ref, buf_ref, sem, *, tm):
    i = pl.program_id(0)

    def issue(r, c):
        for k in range(TOP_K):
            _row_copy(y_hbm, pos_ref[(i * tm + r) * TOP_K + k], buf_ref.at[k], r, sem).start()
        return c

    lax.fori_loop(0, tm, issue, 0)

    def wait(r, c):
        for k in range(TOP_K):
            _row_copy(y_hbm, 0, buf_ref.at[k], r, sem).wait()
        return c

    lax.fori_loop(0, tm, wait, 0)
    gate = gate_ref[...]
    out = x_ref[...]
    y = gate[:, 2:3] * buf_ref[0]
    for k in range(1, TOP_K):
        y = y + gate[:, 2 + k:3 + k] * buf_ref[k]
    o_ref[...] = out + y


def moe_combine(x, yb, pos, gates, tm=256):
    m, d = x.shape
    grid_spec = pltpu.PrefetchScalarGridSpec(
        num_scalar_prefetch=1,
        grid=(m // tm,),
        in_specs=[pl.BlockSpec(memory_space=pl.ANY),
                  pl.BlockSpec((tm, d), lambda i, pos: (i, 0)),
                  pl.BlockSpec((tm, LANES), lambda i, pos: (i, 0))],
        out_specs=pl.BlockSpec((tm, d), lambda i, pos: (i, 0)),
        scratch_shapes=[pltpu.VMEM((TOP_K, tm, d), F32), pltpu.SemaphoreType.DMA],
    )
    return pl.pallas_call(
        functools.partial(_moe_combine_kernel, tm=tm),
        grid_spec=grid_spec,
        out_shape=jax.ShapeDtypeStruct((m, d), F32),
        compiler_params=_cp(1),
        name="moe_combine",
    )(pos, yb, x, gates)


def moe_layer(x, g, rw, rb, w1, w3, w2, moe_index, n_tok, tm):
    m, d = x.shape
    h, route = moe_router(x, g, rw, rb, tm)
    top_i = route[:n_tok, :TOP_K].astype(jnp.int32)
    a = n_tok * TOP_K
    e_flat = top_i.reshape(a)
    order = jnp.argsort(e_flat)
    e_s = e_flat[order]
    tok_s = (order // TOP_K).astype(jnp.int32)
    counts = jnp.bincount(e_flat, length=N_EXPERTS)
    start = jnp.cumsum(counts) - counts
    padded = (counts + MOE_BLOCK - 1) // MOE_BLOCK * MOE_BLOCK
    pend = jnp.cumsum(padded)
    pstart = pend - padded
    dest = (pstart[e_s] + jnp.arange(a) - start[e_s]).astype(jnp.int32)
    nb = -(-a // MOE_BLOCK) + N_EXPERTS
    p = nb * MOE_BLOCK
    n_used = (pend[-1] // MOE_BLOCK).astype(jnp.int32)
    blk_e = jnp.minimum(jnp.searchsorted(pend, jnp.arange(nb) * MOE_BLOCK, side="right"), N_EXPERTS - 1)
    slot = jnp.arange(p)
    slot_e = blk_e[slot // MOE_BLOCK]
    local = slot - pstart[slot_e]
    tok_buf = jnp.where(local < counts[slot_e], tok_s[jnp.clip(start[slot_e] + local, 0, a - 1)], n_tok)
    tok_buf = tok_buf.astype(jnp.int32)
    blk_e = jnp.where(jnp.arange(nb) < n_used, blk_e, blk_e[n_used - 1]).astype(jnp.int32)
    pos = jnp.concatenate([dest[jnp.argsort(order)], jnp.zeros((m * TOP_K - a,), jnp.int32)])
    row_ok = (jnp.arange(m) < n_tok)[:, None]
    gates = jnp.where(row_ok, route, 0.0)
    n_used = n_used.reshape(1)
    xs = moe_gather(h, tok_buf, n_used)
    act = moe_matmul(xs, [w1, w3], moe_index, blk_e, n_used, "swiglu", BF16)
    yb = moe_matmul(act, [w2], moe_index, blk_e, n_used, "plain", F32)
    return moe_combine(x, yb, pos, gates)


TM = 512


def _row_tile(m, pref=1152):
    t = (min(pref, m) // 16) * 16
    while m % t:
        t -= 16
    return t


def _slab(parts, m, dtype):
    rows = sum(p.shape[0] for p in parts)
    parts = [p.astype(dtype) for p in parts]
    return jnp.concatenate(parts + [jnp.zeros((m - rows, parts[0].shape[1]), dtype)], axis=0)


def kernel(x_prompt, x_sample, cache_nsa_cmp_kv, cache_nsa_sel_kv, cache_sb_kv, cache_nsa_win_kv, state_hgrn,
           page_table, attn_norm, w_in, cmp_pe_k, cmp_w1_k, cmp_w2_k, cmp_pe_v, cmp_w1_v, cmp_w2_v, hg_lb_logits,
           hg_norm, w_br_nsa, w_br_sb, w_br_hg, w_out, ffn_norm, ffn_w1, ffn_w3, ffn_w2, router_w, router_b,
           moe_w1, moe_w3, moe_w2, final_norm):
    bsz, seq, d = x_prompt.shape
    db, ds, _ = x_sample.shape
    depth = attn_norm.shape[0]
    n_p, n_s = bsz * seq, db * ds
    n_tok = n_p + n_s
    m = -(-(n_tok + 1) // TM) * TM
    n_pool, page = cache_nsa_cmp_kv.shape[1:3]
    n_pages = page_table.shape[1]
    past = n_pages * page
    assert past % CMP_STRIDE == 0 and ds < CMP_STRIDE and seq % CMP_STRIDE == 0

    x = _slab([x_prompt.reshape(n_p, d), x_sample.reshape(n_s, d)], m, F32)
    pt_flat = page_table.reshape(-1).astype(jnp.int32)
    cmp_pool = cache_nsa_cmp_kv.reshape(depth, n_pool, page * 2 * NSA_KV_HEADS, HEAD_DIM)
    sel_pool = cache_nsa_sel_kv.reshape(depth, n_pool, page * 2 * NSA_KV_HEADS, HEAD_DIM)
    sb_pool = cache_sb_kv.reshape(depth, n_pool, page * 2 * SB_HEADS, HEAD_DIM)
    nbuf = cache_nsa_win_kv.shape[2]
    win_buf = cache_nsa_win_kv.reshape(depth, db, nbuf * 2 * NSA_KV_HEADS, HEAD_DIM)
    rest_cols = w_in.shape[-1] - REST_START
    tmm = _row_tile(m)
    w_in_t = jnp.swapaxes(w_in, 1, 2)
    lb_all = jnp.cumsum(jax.nn.softmax(hg_lb_logits.astype(F32), axis=0), axis=0)
    kvs = (2, NSA_KV_HEADS, HEAD_DIM)

    states = []
    for l in range(depth):
        h = rmsnorm(x, attn_norm[l], BF16, TM)
        u_head = inproj_t(h, w_in_t, l, 0, HEAD_COLS, tmm, HEAD_COLS // 3)
        u_rest = inproj_t(h, w_in_t, l, REST_START, rest_cols, tmm, 1024)

        wk, pk = compress_params(cmp_pe_k[l], cmp_w1_k[l])
        wv, pv = compress_params(cmp_pe_v[l], cmp_w1_v[l])
        wcat, pecat = jnp.stack([wk, wv]), jnp.stack([pk, pv])
        w2s = jnp.stack([cmp_w2_k[l], cmp_w2_v[l]])
        lb = (lb_all[l] - lb_all[0]).reshape(HG_HEADS, HEAD_DIM)
        lbp = hgrn_lb_params(lb)

        kvc = compress_finish(compress_segments_prompt(u_head, bsz, seq, wcat, pecat), w2s)
        o_cmp, sel = cmp_select(u_head, 0, bsz, seq, kvc, seq // CMP_STRIDE - 1, -(-seq // SEL_BLOCK), 0)
        o_sel = sel_prompt(u_head, bsz, seq, sel)
        o_win = win_prompt(u_head, bsz, seq)
        nsa_p = nsa_combine(u_head, 0, o_cmp, o_sel, o_win)
        sb_p = sb_prompt(u_rest, bsz, seq)
        hg_p, s_p = hgrn(u_rest, 0, bsz, seq, lbp, hg_norm[l], jnp.zeros((bsz, HG_HEADS, HEAD_DIM, HEAD_DIM), F32))

        kvc_s = compress_finish(compress_segments_sample(cmp_pool, l, pt_flat, db, n_pages, wcat, pecat), w2s)
        o_cmp_s, sel_s = cmp_select(u_head, n_p, db, ds, kvc_s, past // CMP_STRIDE - 1,
                                    -(-(past + ds) // SEL_BLOCK), past, tq=ds)
        o_sel_s = sel_sample(u_head, n_p, sel_pool, l, pt_flat, sel_s, db, ds, n_pages)
        o_win_s = win_sample(u_head, n_p, win_buf, l, db, ds)
        nsa_s = nsa_combine(u_head, n_p, o_cmp_s, o_sel_s, o_win_s)
        sb_s = sb_sample(u_rest, n_p, sb_pool, l, pt_flat, db, ds, n_pages)
        hg_s, s_s = hgrn(u_rest, n_p, db, ds, lbp, hg_norm[l], state_hgrn[l])

        o_nsa = _slab([nsa_p, nsa_s], m, BF16)
        o_sb = _slab([sb_p, sb_s], m, BF16)
        o_hg = _slab([hg_p, hg_s], m, BF16)
        merged = matmul([o_nsa, o_sb, o_hg], [w_br_nsa, w_br_sb, w_br_hg], l, 0, d, "merge", BF16,
                        extras=[(u_rest, R_MERGE), (u_rest, R_MERGE + d), (u_rest, R_MERGE + 2 * d)], tm=tmm)
        x = matmul([merged], [w_out], l, 0, d, "residual", F32, extras=[(x, 0)], tm=tmm)

        i = l // 2
        if l % 2 == 0:
            h2 = rmsnorm(x, ffn_norm[l], BF16, TM)
            act = matmul([h2], [ffn_w1, ffn_w3], i, 0, ffn_w1.shape[-1], "swiglu", BF16, tm=tmm)
            x = matmul([act], [ffn_w2], i, 0, d, "residual", F32, extras=[(x, 0)])
        else:
            x = moe_layer(x, ffn_norm[l], router_w[i], router_b[i], moe_w1, moe_w3, moe_w2, i, n_tok, TM)

        def head_cols(off, r0, r1, lead):
            return u_head[r0:r1, off:off + KV_W].reshape(lead + kvs)

        win_p = head_cols(OFF_WIN, 0, n_p, (bsz, seq))[:, seq - min(WINDOW, seq):]
        win_s = jnp.concatenate([cache_nsa_win_kv[l], head_cols(OFF_WIN, n_p, n_tok, (db, ds))], axis=1)[:, ds:]
        states.append((
            head_cols(OFF_CMP, 0, n_p, (bsz, seq)), head_cols(OFF_SEL, 0, n_p, (bsz, seq)), win_p,
            u_rest[:n_p, R_SB_K:R_SB_K + 2 * SB_W].reshape(bsz, seq, 2, SB_HEADS, HEAD_DIM), s_p,
            head_cols(OFF_CMP, n_p, n_tok, (db, ds)), head_cols(OFF_SEL, n_p, n_tok, (db, ds)), win_s,
            u_rest[n_p:n_tok, R_SB_K:R_SB_K + 2 * SB_W].reshape(db, ds, 2, SB_HEADS, HEAD_DIM), s_s))

    y = rmsnorm(x, final_norm, F32, TM)
    stacked = [jnp.stack([st[i] for st in states]) for i in range(10)]
    return (y[:n_p].reshape(bsz, seq, d), y[n_p:n_tok].reshape(db, ds, d), *stacked)
```

```python
import functools

import jax
import jax.numpy as jnp
import numpy as np
from jax import lax
from jax.experimental import pallas as pl
from jax.experimental.pallas import tpu as pltpu

F32 = jnp.float32
BF16 = jnp.bfloat16

HEAD_DIM = 128
SCALE = HEAD_DIM ** -0.5
NSA_HEADS = 8
NSA_KV_HEADS = 2
NSA_GROUP = NSA_HEADS // NSA_KV_HEADS
CMP_LEN = 32
CMP_STRIDE = 16
SEL_BLOCK = 64
SEL_TOPK = 16
WINDOW = 512
SB_HEADS = 8
HG_HEADS = 8
HG_CHUNK = 64
HG_SUB = 16
HG_MIN_CHUNK = 16
N_EXPERTS = 8
TOP_K = 2
MOE_BLOCK = 512
NORM_EPS = 1e-6
NEG_BIG = -1e30

LANES = 128
SUBLANES = 8
VMEM_LIMIT = 56 * 1024 * 1024

NSA_W = NSA_HEADS * HEAD_DIM
KV_W = 2 * NSA_KV_HEADS * HEAD_DIM
OFF_CMP = NSA_W
OFF_SEL = OFF_CMP + KV_W
OFF_WIN = OFF_SEL + KV_W
OFF_NSA_GATE = OFF_WIN + KV_W
N_GATE = 3 * NSA_HEADS
HEAD_COLS = OFF_NSA_GATE + LANES
REST_START = OFF_NSA_GATE + N_GATE
SB_W = SB_HEADS * HEAD_DIM
HG_W = HG_HEADS * HEAD_DIM
R_SB_Q = 0
R_SB_K = R_SB_Q + SB_W
R_SB_V = R_SB_K + SB_W
R_HG_Q = R_SB_V + SB_W
R_HG_F = R_HG_Q + HG_W
R_HG_I = R_HG_F + HG_W
R_HG_G = R_HG_I + HG_W
R_MERGE = R_HG_G + HG_W


def _cp(n_axes, vmem=VMEM_LIMIT):
    return pltpu.CompilerParams(dimension_semantics=("arbitrary",) * n_axes, vmem_limit_bytes=vmem)


def _tile(n, pref, quantum=LANES):
    if n <= pref:
        return n
    t = (pref // quantum) * quantum
    while t > quantum and n % t:
        t -= quantum
    assert n % t == 0, (n, pref)
    return t


def _dot(a, b):
    return jnp.dot(a, b, preferred_element_type=F32)


def _dot_nt(a, b):
    return lax.dot_general(a, b, (((1,), (1,)), ((), ())), preferred_element_type=F32)


def _dot_tn(a, b):
    return lax.dot_general(a, b, (((0,), (0,)), ((), ())), preferred_element_type=F32)


def _split3(x):
    hi = x.astype(BF16)
    r = x - hi.astype(F32)
    mid = r.astype(BF16)
    lo = (r - mid.astype(F32)).astype(BF16)
    return hi, mid, lo


def _split2(x):
    hi = x.astype(BF16)
    return hi, (x - hi.astype(F32)).astype(BF16)


def _sigmoid(x):
    return 1.0 / (1.0 + jnp.exp(-x))


def _silu(x):
    return x * _sigmoid(x)


def _log_sigmoid(x):
    return jnp.minimum(x, 0.0) - jnp.log1p(jnp.exp(-jnp.abs(x)))


def _rmsnorm_kernel(x_ref, g_ref, o_ref):
    x = x_ref[...]
    y = x * lax.rsqrt(jnp.mean(x * x, axis=-1, keepdims=True) + NORM_EPS)
    o_ref[...] = (y * g_ref[...]).astype(o_ref.dtype)


def rmsnorm(x, g, out_dtype, tm):
    m, d = x.shape
    return pl.pallas_call(
        _rmsnorm_kernel,
        grid=(m // tm,),
        in_specs=[pl.BlockSpec((tm, d), lambda i: (i, 0)), pl.BlockSpec((1, d), lambda i: (0, 0))],
        out_specs=pl.BlockSpec((tm, d), lambda i: (i, 0)),
        out_shape=jax.ShapeDtypeStruct((m, d), out_dtype),
        compiler_params=_cp(1),
        name="rmsnorm",
    )(x, g.reshape(1, d))


def _mm_epilogue(mode, prods, x_refs):
    if mode == "plain":
        return prods[0]
    if mode == "merge":
        out = _sigmoid(x_refs[0][...]) * prods[0]
        for x_ref, p in zip(x_refs[1:], prods[1:]):
            out = out + _sigmoid(x_ref[...]) * p
        return out
    if mode == "residual":
        return x_refs[0][...] + prods[0]
    return _silu(prods[0]) * prods[1]


def _mm_kernel(*refs, n_a, n_w, n_extra, mode, cast):
    a_refs = refs[:n_a]
    w_refs = refs[n_a:n_a + n_w]
    x_refs = refs[n_a + n_w:n_a + n_w + n_extra]
    o_ref = refs[n_a + n_w + n_extra]
    wbf_refs = refs[n_a + n_w + n_extra + 1:]

    if cast:
        @pl.when(pl.program_id(1) == 0)
        def _():
            for w_ref, wbf_ref in zip(w_refs, wbf_refs):
                wbf_ref[...] = w_ref[...].astype(BF16)
        ws = [r[...] for r in wbf_refs]
    else:
        ws = [r[...] for r in w_refs]

    prods = [_dot(a_refs[min(i, n_a - 1)][...], w) for i, w in enumerate(ws)]
    o_ref[...] = _mm_epilogue(mode, prods, x_refs).astype(o_ref.dtype)


def matmul(a_list, w_list, w_index, col0, n_out, mode, out_dtype, extras=(), tm=512, tn=512):
    m = a_list[0].shape[0]
    tn = _tile(n_out, tn)
    assert col0 % tn == 0 and m % tm == 0
    cast = w_list[0].dtype != BF16
    in_specs, args, scratch = [], [], []
    for a in a_list:
        in_specs.append(pl.BlockSpec((tm, a.shape[1]), lambda j, i: (i, 0)))
        args.append(a)
    for w in w_list:
        k = w.shape[-2]
        if w.ndim == 3:
            in_specs.append(pl.BlockSpec((None, k, tn), lambda j, i: (w_index, 0, j + col0 // tn)))
        else:
            in_specs.append(pl.BlockSpec((k, tn), lambda j, i: (0, j + col0 // tn)))
        args.append(w)
        if cast:
            scratch.append(pltpu.VMEM((k, tn), BF16))
    for x, off in extras:
        assert off % tn == 0
        in_specs.append(pl.BlockSpec((tm, tn), lambda j, i, off=off: (i, j + off // tn)))
        args.append(x)
    return pl.pallas_call(
        functools.partial(_mm_kernel, n_a=len(a_list), n_w=len(w_list), n_extra=len(extras), mode=mode, cast=cast),
        grid=(n_out // tn, m // tm),
        in_specs=in_specs,
        out_specs=pl.BlockSpec((tm, tn), lambda j, i: (i, j)),
        out_shape=jax.ShapeDtypeStruct((m, n_out), out_dtype),
        scratch_shapes=scratch,
        compiler_params=_cp(2),
        name="mm_" + mode,
    )(*args)


def _inproj_t_kernel(a_ref, wt_ref, o_ref, wbf_ref):
    @pl.when(pl.program_id(1) == 0)
    def _():
        wbf_ref[...] = wt_ref[0].T.astype(BF16)

    o_ref[...] = _dot(a_ref[...], wbf_ref[...])


def inproj_t(a, w_t, layer, row0, n_out, tm, tn):
    m, k = a.shape
    tn = _tile(n_out, tn)
    assert m % tm == 0 and row0 % SUBLANES == 0
    return pl.pallas_call(
        _inproj_t_kernel,
        grid=(n_out // tn, m // tm),
        in_specs=[pl.BlockSpec((tm, k), lambda j, i: (i, 0)),
                  pl.BlockSpec((pl.Element(1), pl.Element(tn), pl.Element(k)),
                               lambda j, i: (layer, pl.multiple_of(row0 + j * tn, SUBLANES), 0))],
        out_specs=pl.BlockSpec((tm, tn), lambda j, i: (i, j)),
        out_shape=jax.ShapeDtypeStruct((m, n_out), F32),
        scratch_shapes=[pltpu.VMEM((k, tn), BF16)],
        compiler_params=_cp(2),
        name="inproj_t",
    )(a, w_t)


def _inproj_rest_kernel(a_ref, wa_ref, wb_ref, o_ref, wbf_ref, *, shift, tn):
    @pl.when(pl.program_id(1) == 0)
    def _():
        w = jnp.concatenate([wa_ref[...], wb_ref[...]], axis=1)
        wbf_ref[...] = w[:, shift:shift + tn].astype(BF16)

    o_ref[...] = _dot(a_ref[...], wbf_ref[...])


def inproj_rest(a, w_in, layer, n_out, tm, tn=512):
    m, k = a.shape
    shift = REST_START % LANES
    base = REST_START - shift
    tn = _tile(n_out, tn)
    assert base % tn == 0 and m % tm == 0 and REST_START + n_out == w_in.shape[-1]
    return pl.pallas_call(
        functools.partial(_inproj_rest_kernel, shift=shift, tn=tn),
        grid=(n_out // tn, m // tm),
        in_specs=[pl.BlockSpec((tm, k), lambda j, i: (i, 0)),
                  pl.BlockSpec((None, k, tn), lambda j, i: (layer, 0, base // tn + j)),
                  pl.BlockSpec((None, k, LANES), lambda j, i: (layer, 0, (base + (j + 1) * tn) // LANES))],
        out_specs=pl.BlockSpec((tm, tn), lambda j, i: (i, j)),
        out_shape=jax.ShapeDtypeStruct((m, n_out), F32),
        scratch_shapes=[pltpu.VMEM((k, tn), BF16)],
        compiler_params=_cp(2),
        name="inproj_rest",
    )(a, w_in, w_in)


def _suffix_matrix():
    j = lax.broadcasted_iota(jnp.int32, (LANES, 2 * LANES), 0)
    s = lax.broadcasted_iota(jnp.int32, (LANES, 2 * LANES), 1)
    return jnp.where((j > s) | (s >= LANES), 1.0, 0.0).astype(BF16)


def _sb_weights(z, mask, carry, umat):
    t = jnp.log(1.0 + jnp.exp(-jnp.abs(z)))
    ls_pos = jnp.minimum(z, 0.0) - t
    c = ls_pos - z
    if mask is not None:
        c = jnp.where(mask, c, 0.0)
    n_sub = z.shape[1] // LANES
    pieces = [None] * n_sub
    for sb in reversed(range(n_sub)):
        sl = slice(sb * LANES, (sb + 1) * LANES)
        hi, lo = _split2(c[:, sl])
        r = _dot(hi, umat) + _dot(lo, umat)
        w = jnp.exp(ls_pos[:, sl] + (carry + r[:, :LANES]))
        pieces[sb] = w if mask is None else jnp.where(mask[:, sl], w, 0.0)
        carry = carry + r[:, LANES:]
    a = pieces[0] if n_sub == 1 else jnp.concatenate(pieces, axis=1)
    return a, carry


SB_ZERO_LOG = -104.0


def _sb_prompt_kernel(q_ref, k_ref, v_ref, o_ref, acc_ref, carry_ref, *, t, hb):
    qi = pl.program_id(2)
    umat = _suffix_matrix()
    qs = [(q_ref[:, j * HEAD_DIM:(j + 1) * HEAD_DIM] * SCALE).astype(BF16) for j in range(hb)]
    row = lax.broadcasted_iota(jnp.int32, (t, t), 0)
    col = lax.broadcasted_iota(jnp.int32, (t, t), 1)

    def tile(kb, mask):
        rows = pl.ds(pl.multiple_of(kb * t, t), t)
        top = None
        for j in range(hb):
            cols = slice(j * HEAD_DIM, (j + 1) * HEAD_DIM)
            z = _dot_nt(qs[j], k_ref[rows, cols].astype(BF16))
            a, carry = _sb_weights(z, mask, carry_ref[j], umat)
            acc_ref[j] += _dot(a.astype(BF16), v_ref[rows, cols].astype(BF16))
            carry_ref[j] = carry
            top = jnp.max(carry) if top is None else jnp.maximum(top, jnp.max(carry))
        return top

    acc_ref[...] = jnp.zeros_like(acc_ref)
    carry_ref[...] = jnp.zeros_like(carry_ref)
    top = tile(qi, col < row)

    def cond(st):
        return jnp.logical_and(st[0] >= 0, st[1] > SB_ZERO_LOG)

    def body(st):
        return st[0] - 1, tile(st[0], None)

    lax.while_loop(cond, body, (qi - 1, top))
    for j in range(hb):
        o_ref[:, j * HEAD_DIM:(j + 1) * HEAD_DIM] = acc_ref[j].astype(o_ref.dtype)


def sb_prompt(u_rest, batch, seq, t=256, hb=4):
    t = min(t, seq)
    nq = seq // t
    bw = hb * HEAD_DIM
    cq, ck, cv = R_SB_Q // bw, R_SB_K // bw, R_SB_V // bw
    assert SB_HEADS % hb == 0
    return pl.pallas_call(
        functools.partial(_sb_prompt_kernel, t=t, hb=hb),
        grid=(batch, SB_HEADS // hb, nq),
        in_specs=[pl.BlockSpec((t, bw), lambda b, h, i: (b * nq + i, cq + h)),
                  pl.BlockSpec((seq, bw), lambda b, h, i: (b, ck + h)),
                  pl.BlockSpec((seq, bw), lambda b, h, i: (b, cv + h))],
        out_specs=pl.BlockSpec((t, bw), lambda b, h, i: (b * nq + i, h)),
        out_shape=jax.ShapeDtypeStruct((batch * seq, SB_W), BF16),
        scratch_shapes=[pltpu.VMEM((hb, t, LANES), F32), pltpu.VMEM((hb, t, LANES), F32)],
        compiler_params=_cp(3),
        name="sb_prompt",
    )(u_rest, u_rest, u_rest)


def _hgrn_chunk(qr, fr, v, gr, lbp, norm_w, st, c, sub, c_real):
    log_lb, log_1m_lb, one_m_lb = lbp[0:1], lbp[1:2], lbp[2:3]
    q = _silu(qr)
    k = one_m_lb * _sigmoid(-fr)
    bb = log_1m_lb + _log_sigmoid(fr)
    mx = jnp.maximum(log_lb, bb)
    logf = mx + jnp.log1p(jnp.exp(-jnp.abs(log_lb - bb)))
    row = lax.broadcasted_iota(jnp.int32, (c, c), 0)
    col = lax.broadcasted_iota(jnp.int32, (c, c), 1)
    tri = jnp.where(row >= col, 1.0, 0.0).astype(BF16)
    hi, mid, lo = _split3(logf)
    b = _dot(tri, hi) + _dot(tri, mid) + _dot(tri, lo)
    o = _dot_nt((q * jnp.exp(b)).astype(BF16), st.astype(BF16))
    ridx = lax.broadcasted_iota(jnp.int32, (c, HEAD_DIM), 0)
    lane = lax.broadcasted_iota(jnp.int32, (sub, c), 1)
    att_rows = []
    for i in range(c // sub):
        r0 = i * sub
        b_i = b[r0:r0 + sub]
        q_i = q[r0:r0 + sub]
        att_i = jnp.zeros((sub, c), F32)
        if i > 0:
            rho = b_i[0:1]
            earlier = ridx < r0
            k_dec = jnp.where(earlier, k * jnp.exp(jnp.where(earlier, rho - b, 0.0)), 0.0)
            att_i = _dot_nt((q_i * jnp.exp(b_i - rho)).astype(BF16), k_dec.astype(BF16))
        trow = lax.broadcasted_iota(jnp.int32, (sub, 1), 0)
        for s in range(sub):
            d = q_i * jnp.exp(jnp.where(trow >= s, b_i - b_i[s:s + 1], 0.0)) * k[r0 + s:r0 + s + 1]
            colsum = jnp.sum(d, axis=1, keepdims=True)
            att_i = att_i + jnp.where((lane == r0 + s) & (trow >= s), colsum, 0.0)
        att_rows.append(att_i)
    att = att_rows[0] if len(att_rows) == 1 else jnp.concatenate(att_rows, axis=0)
    o = o + _dot(att.astype(BF16), v.astype(BF16))
    b_end = b[c_real - 1:c_real]
    real = ridx < c_real
    k_end = jnp.where(real, k * jnp.exp(jnp.where(real, b_end - b, 0.0)), 0.0)
    st = st * jnp.exp(b_end) + _dot_tn(v.astype(BF16), k_end.astype(BF16))
    o = o * lax.rsqrt(jnp.mean(o * o, axis=-1, keepdims=True) + NORM_EPS) * norm_w
    return o * _silu(gr), st


def _hgrn_kernel(q_ref, f_ref, i_ref, g_ref, lbp_ref, nw_ref, s0_ref, o_ref, s_out_ref, st_ref, *,
                 c, sub, n_chunks, hb):
    t = pl.program_id(2)

    @pl.when(t == 0)
    def _():
        for j in range(hb):
            st_ref[j] = s0_ref[j].T

    nw = nw_ref[...]
    in_refs = (q_ref, f_ref, i_ref, g_ref)

    if c < HG_MIN_CHUNK:
        pad = jnp.zeros((HG_MIN_CHUNK - c, HEAD_DIM), F32)
        for j in range(hb):
            cols = slice(j * HEAD_DIM, (j + 1) * HEAD_DIM)
            ins = [jnp.concatenate([r[:, cols], pad], axis=0) for r in in_refs]
            o, st = _hgrn_chunk(*ins, lbp_ref[j], nw, st_ref[j], HG_MIN_CHUNK, HG_SUB, c)
            st_ref[j] = st
            o_ref[:, cols] = o[:c].astype(o_ref.dtype)
    else:
        def body(ci, carry):
            rows = pl.ds(pl.multiple_of(ci * c, c), c)
            for j in range(hb):
                cols = slice(j * HEAD_DIM, (j + 1) * HEAD_DIM)
                o, st = _hgrn_chunk(*[r[rows, cols] for r in in_refs], lbp_ref[j], nw, st_ref[j], c, sub, c)
                st_ref[j] = st
                o_ref[rows, cols] = o.astype(o_ref.dtype)
            return carry

        lax.fori_loop(0, n_chunks, body, 0)

    @pl.when(t == pl.num_programs(2) - 1)
    def _():
        for j in range(hb):
            s_out_ref[j] = st_ref[j].T


def hgrn_lb_params(lb):
    rows = jnp.stack([jnp.log(lb), jnp.log1p(-lb), 1.0 - lb], axis=1)
    return jnp.concatenate([rows, jnp.zeros((lb.shape[0], SUBLANES - 3, lb.shape[1]), F32)], axis=1)


def hgrn(u_rest, row0, batch, seq, lbp, norm_w, s0, tl=512, hb=8):
    c = min(HG_CHUNK, seq)
    sub = min(HG_SUB, c)
    tl = min(tl, seq)
    nt = seq // tl
    rb0 = row0 // tl
    bw = hb * HEAD_DIM
    assert row0 % tl == 0 and seq % tl == 0 and tl % c == 0 and (c % HG_SUB == 0 or nt == 1) and HG_HEADS % hb == 0

    def col(off):
        return pl.BlockSpec((tl, bw), lambda b, h, t, off=off: (rb0 + b * nt + t, off // bw + h))

    o, s_out = pl.pallas_call(
        functools.partial(_hgrn_kernel, c=c, sub=sub, n_chunks=tl // c, hb=hb),
        grid=(batch, HG_HEADS // hb, nt),
        in_specs=[col(R_HG_Q), col(R_HG_F), col(R_HG_I), col(R_HG_G),
                  pl.BlockSpec((hb, SUBLANES, LANES), lambda b, h, t: (h, 0, 0)),
                  pl.BlockSpec((1, LANES), lambda b, h, t: (0, 0)),
                  pl.BlockSpec((None, hb, HEAD_DIM, HEAD_DIM), lambda b, h, t: (b, h, 0, 0))],
        out_specs=[pl.BlockSpec((tl, bw), lambda b, h, t: (b * nt + t, h)),
                   pl.BlockSpec((None, hb, HEAD_DIM, HEAD_DIM), lambda b, h, t: (b, h, 0, 0))],
        out_shape=[jax.ShapeDtypeStruct((batch * seq, HG_W), BF16 if tl % 16 == 0 else F32),
                   jax.ShapeDtypeStruct((batch, HG_HEADS, HEAD_DIM, HEAD_DIM), F32)],
        scratch_shapes=[pltpu.VMEM((hb, HEAD_DIM, HEAD_DIM), F32)],
        compiler_params=_cp(3),
        name="hgrn",
    )(u_rest, u_rest, u_rest, u_rest, lbp, norm_w.reshape(1, LANES), s0)
    return o, s_out


SEG_W = CMP_STRIDE * HEAD_DIM


def compress_params(pe, w1):
    w = jnp.concatenate([w1[:SEG_W], w1[SEG_W:]], axis=1).astype(BF16)
    rows = jnp.stack([pe[:CMP_STRIDE].reshape(SEG_W), pe[CMP_STRIDE:].reshape(SEG_W)])
    return w, jnp.concatenate([rows, jnp.zeros((SUBLANES - 2, SEG_W), F32)]).astype(BF16)


def _segment_products(r, w, pe):
    bias = _dot(pe, w)
    bias = jnp.concatenate([bias[0:1, :LANES], bias[1:2, LANES:]], axis=1)
    return _dot(r.astype(BF16), w) + bias


def _cmp1_prompt_kernel(x_ref, w_ref, pe_ref, o_ref, r_ref, *, n_seg):
    for l in range(CMP_STRIDE):
        r_ref[:, l * LANES:(l + 1) * LANES] = x_ref[pl.ds(l, n_seg, stride=CMP_STRIDE), :]
    o_ref[...] = _segment_products(r_ref[...], w_ref[...], pe_ref[...])


def compress_segments_prompt(u_head, batch, seq, wcat, pecat, tr=512):
    tr = min(tr, seq)
    n_seg = tr // CMP_STRIDE
    nt = seq // tr
    c0 = OFF_CMP // LANES
    return pl.pallas_call(
        functools.partial(_cmp1_prompt_kernel, n_seg=n_seg),
        grid=(batch, 2, NSA_KV_HEADS, nt),
        in_specs=[pl.BlockSpec((tr, LANES), lambda b, kv, g, t: (b * nt + t, c0 + kv * NSA_KV_HEADS + g)),
                  pl.BlockSpec((None, SEG_W, 2 * LANES), lambda b, kv, g, t: (kv, 0, 0)),
                  pl.BlockSpec((None, SUBLANES, SEG_W), lambda b, kv, g, t: (kv, 0, 0))],
        out_specs=pl.BlockSpec((None, None, None, n_seg, 2 * LANES), lambda b, kv, g, t: (b, kv, g, t, 0)),
        out_shape=jax.ShapeDtypeStruct((batch, 2, NSA_KV_HEADS, seq // CMP_STRIDE, 2 * LANES), F32),
        scratch_shapes=[pltpu.VMEM((n_seg, SEG_W), F32)],
        compiler_params=_cp(4),
        name="cmp_segments_prompt",
    )(u_head, wcat, pecat)


def _cmp2_kernel(pq_ref, w2_ref, o_ref, *, nc):
    pq = pq_ref[...]
    q_next = pltpu.roll(pq[:, LANES:], shift=nc - 1, axis=0)
    hid = _silu(pq[:, :LANES] + q_next)
    out = _dot(hid.astype(BF16), w2_ref[...].astype(BF16))
    row = lax.broadcasted_iota(jnp.int32, out.shape, 0)
    o_ref[...] = jnp.where(row < nc - 1, out, 0.0)


def compress_finish(pq, w2):
    batch, _, _, nc, _ = pq.shape
    return pl.pallas_call(
        functools.partial(_cmp2_kernel, nc=nc),
        grid=(batch, 2, NSA_KV_HEADS),
        in_specs=[pl.BlockSpec((None, None, None, nc, 2 * LANES), lambda b, kv, g: (b, kv, g, 0, 0)),
                  pl.BlockSpec((None, HEAD_DIM, HEAD_DIM), lambda b, kv, g: (kv, 0, 0))],
        out_specs=pl.BlockSpec((None, None, None, nc, LANES), lambda b, kv, g: (b, kv, g, 0, 0)),
        out_shape=jax.ShapeDtypeStruct((batch, 2, NSA_KV_HEADS, nc, LANES), F32),
        compiler_params=_cp(3),
        name="cmp_finish",
    )(pq, w2)


def _cmp_select_kernel(q_ref, kc_ref, vc_ref, o_ref, sel_ref, score_ref, *, tq, tqp, nc, n_cmp, n_slc, nsp, pos0):
    qi = pl.program_id(2)
    qb = q_ref[...]
    parts = []
    for h in range(NSA_GROUP):
        qh = qb[:, h * HEAD_DIM:(h + 1) * HEAD_DIM]
        if tqp > tq:
            qh = jnp.concatenate([qh, jnp.zeros((tqp - tq, HEAD_DIM), F32)], axis=0)
        parts.append(qh)
    q4 = jnp.concatenate(parts, axis=0).astype(BF16)
    rows = NSA_GROUP * tqp
    st = _dot_nt(kc_ref[...].astype(BF16), q4) * SCALE
    ci = lax.broadcasted_iota(jnp.int32, (nc, rows), 0)
    tok = lax.broadcasted_iota(jnp.int32, (nc, rows), 1) & (tqp - 1)
    tpos = pos0 + qi * tq + tok
    valid = (ci < n_cmp) & (ci * CMP_STRIDE + CMP_LEN - 1 <= tpos)
    st = jnp.where(valid, st, NEG_BIG)
    m = jnp.max(st, axis=0, keepdims=True)
    e = jnp.where(valid, jnp.exp(st - m), 0.0)
    den = jnp.sum(e, axis=0, keepdims=True)
    pt = e / jnp.where(den > 0, den, 1.0)
    o = _dot_tn(pt.astype(BF16), vc_ref[...].astype(BF16))
    for h in range(NSA_GROUP):
        o_ref[:, h * HEAD_DIM:(h + 1) * HEAD_DIM] = o[h * tqp:h * tqp + tq]
    psum = pt[:, 0:tqp]
    for h in range(1, NSA_GROUP):
        psum = psum + pt[:, h * tqp:(h + 1) * tqp]
    nsr = score_ref.shape[0]
    jj = lax.broadcasted_iota(jnp.int32, (nsr, nc), 0)
    ii = lax.broadcasted_iota(jnp.int32, (nsr, nc), 1)
    cover = ((ii * CMP_STRIDE < jj * SEL_BLOCK + SEL_BLOCK) & (ii * CMP_STRIDE + CMP_LEN - 1 >= jj * SEL_BLOCK)
             & (ii < n_cmp))
    cover = jnp.where(cover, 1.0, 0.0).astype(BF16)
    hi, mid, lo = _split3(psum)
    imp = _dot(cover, hi) + _dot(cover, mid) + _dot(cover, lo)
    j = lax.broadcasted_iota(jnp.int32, (nsr, tqp), 0)
    tpos2 = pos0 + qi * tq + lax.broadcasted_iota(jnp.int32, (nsr, tqp), 1)
    cur = tpos2 // SEL_BLOCK
    forced = (j == 0) | (j == cur) | (j == cur - 1)
    ok = (j * SEL_BLOCK <= tpos2) & (j < n_slc)
    score = jnp.where(ok, jnp.where(forced, NSA_GROUP + 1.0, imp), -1.0)
    score_ref[...] = score

    def body(jp, rank):
        other = score_ref[pl.ds(jp, 1), :]
        ahead = jnp.where(other > score, 1.0, jnp.where((other == score) & (jp < j), 1.0, 0.0))
        return rank + ahead

    n_valid = jnp.minimum(n_slc, (pos0 + qi * tq + tq - 1) // SEL_BLOCK + 1)
    rank = lax.fori_loop(0, n_valid, body, jnp.zeros((nsr, tqp), F32))
    sel_t = jnp.where((rank < SEL_TOPK) & (score >= 0), 1.0, 0.0)
    if nsp > nsr:
        sel_t = jnp.concatenate([sel_t, jnp.zeros((nsp - nsr, tqp), F32)], axis=0)
    sel_ref[...] = sel_t.T[:tq]


def cmp_select(u_head, row0, batch, seq, kv_cmp, n_cmp, n_slc, pos0, tq=256):
    tq = min(tq, seq)
    tqp = max(tq, LANES)
    nt = seq // tq
    rb0 = row0 // tq
    nc = kv_cmp.shape[3]
    nsp = -(-n_slc // LANES) * LANES
    gw = NSA_GROUP * HEAD_DIM
    assert row0 % tq == 0 and seq % tq == 0 and tqp & (tqp - 1) == 0
    return pl.pallas_call(
        functools.partial(_cmp_select_kernel, tq=tq, tqp=tqp, nc=nc, n_cmp=n_cmp, n_slc=n_slc, nsp=nsp, pos0=pos0),
        grid=(batch, NSA_KV_HEADS, nt),
        in_specs=[pl.BlockSpec((tq, gw), lambda b, g, t: (rb0 + b * nt + t, g)),
                  pl.BlockSpec((None, None, None, nc, LANES), lambda b, g, t: (b, 0, g, 0, 0)),
                  pl.BlockSpec((None, None, None, nc, LANES), lambda b, g, t: (b, 1, g, 0, 0))],
        out_specs=[pl.BlockSpec((tq, gw), lambda b, g, t: (b * nt + t, g)),
                   pl.BlockSpec((None, None, tq, nsp), lambda b, g, t: (b, g, t, 0))],
        out_shape=[jax.ShapeDtypeStruct((batch * seq, NSA_W), F32),
                   jax.ShapeDtypeStruct((batch, NSA_KV_HEADS, seq, nsp), F32)],
        scratch_shapes=[pltpu.VMEM((-(-n_slc // 16) * 16, tqp), F32)],
        compiler_params=_cp(3),
        name="cmp_select",
    )(u_head, kv_cmp, kv_cmp)


def _flash_step(s, mask, v, m_ref, l_ref, acc_ref):
    s = jnp.where(mask, s, NEG_BIG)
    m_prev = m_ref[...]
    m_new = jnp.maximum(m_prev, jnp.max(s, axis=1, keepdims=True))
    e = jnp.where(mask, jnp.exp(s - m_new), 0.0)
    alpha = jnp.exp(m_prev - m_new)
    l_ref[...] = alpha * l_ref[...] + jnp.sum(e, axis=1, keepdims=True)
    acc_ref[...] = alpha * acc_ref[...] + _dot(e.astype(BF16), v)
    m_ref[...] = m_new


def _flash_init(m_ref, l_ref, acc_ref):
    m_ref[...] = jnp.full_like(m_ref, NEG_BIG)
    l_ref[...] = jnp.zeros_like(l_ref)
    acc_ref[...] = jnp.zeros_like(acc_ref)


def _flash_result(l_ref, acc_ref):
    l = l_ref[...]
    return acc_ref[...] / jnp.where(l > 0, l, 1.0)


def _stack_heads(qb, pad_to=None):
    parts = []
    for h in range(NSA_GROUP):
        qh = qb[:, h * HEAD_DIM:(h + 1) * HEAD_DIM]
        if pad_to is not None and pad_to > qh.shape[0]:
            qh = jnp.concatenate([qh, jnp.zeros((pad_to - qh.shape[0], HEAD_DIM), qh.dtype)], axis=0)
        parts.append(qh)
    return jnp.concatenate(parts, axis=0)


def _flash_prompt_kernel(qi_tab, kb_tab, first_tab, last_tab, *refs, tq, tk, mode, nsp):
    if mode == "sel":
        q_ref, k_ref, v_ref, sel_ref, o_ref, m_ref, l_ref, acc_ref = refs
    else:
        q_ref, k_ref, v_ref, o_ref, m_ref, l_ref, acc_ref = refs
    p = pl.program_id(2)
    qi = qi_tab[p]
    kb = kb_tab[p]

    @pl.when(first_tab[p] == 1)
    def _():
        _flash_init(m_ref, l_ref, acc_ref)

    q4 = _stack_heads(q_ref[...]).astype(BF16)
    s = _dot_nt(q4, k_ref[...].astype(BF16)) * SCALE
    qpos = qi * tq + lax.broadcasted_iota(jnp.int32, (tq, tk), 0)
    kpos = kb * tk + lax.broadcasted_iota(jnp.int32, (tq, tk), 1)
    if mode == "win":
        d = qpos - kpos
        mask = (d >= 0) & (d < WINDOW)
    else:
        jj = lax.broadcasted_iota(jnp.int32, (nsp, tk), 0)
        kk = lax.broadcasted_iota(jnp.int32, (nsp, tk), 1)
        expand = jnp.where(jj == kb * (tk // SEL_BLOCK) + kk // SEL_BLOCK, 1.0, 0.0).astype(BF16)
        chosen = _dot(sel_ref[...].astype(BF16), expand)
        mask = (chosen > 0.5) & (kpos <= qpos)
    mask4 = jnp.concatenate([mask] * NSA_GROUP, axis=0)
    _flash_step(s, mask4, v_ref[...].astype(BF16), m_ref, l_ref, acc_ref)

    @pl.when(last_tab[p] == 1)
    def _():
        o = _flash_result(l_ref, acc_ref)
        for h in range(NSA_GROUP):
            o_ref[:, h * HEAD_DIM:(h + 1) * HEAD_DIM] = o[h * tq:(h + 1) * tq]


def flash_prompt(u_head, batch, seq, mode, sel=None, tq=256, tk=None):
    tq = min(tq, seq)
    tk = min(tk or (256 if mode == "win" else 512), seq)
    nq, nk = seq // tq, seq // tk
    pairs = []
    for qi in range(nq):
        if mode == "win":
            lo = max(0, (qi * tq - WINDOW + 1) // tk)
        else:
            lo = 0
        hi = (qi * tq + tq - 1) // tk
        kbs = list(range(lo, hi + 1))
        pairs += [(qi, kb, int(kb == kbs[0]), int(kb == kbs[-1])) for kb in kbs]
    tabs = [jnp.asarray([p[i] for p in pairs], jnp.int32) for i in range(4)]
    off = OFF_WIN if mode == "win" else OFF_SEL
    ck = off // LANES
    cv = ck + NSA_KV_HEADS
    gw = NSA_GROUP * HEAD_DIM
    in_specs = [
        pl.BlockSpec((tq, gw), lambda b, g, p, qt, kt, ft, lt: (b * nq + qt[p], g)),
        pl.BlockSpec((tk, LANES), lambda b, g, p, qt, kt, ft, lt: (b * nk + kt[p], ck + g)),
        pl.BlockSpec((tk, LANES), lambda b, g, p, qt, kt, ft, lt: (b * nk + kt[p], cv + g)),
    ]
    args = [u_head, u_head, u_head]
    nsp = 0
    if mode == "sel":
        nsp = sel.shape[-1]
        in_specs.append(pl.BlockSpec((None, None, tq, nsp), lambda b, g, p, qt, kt, ft, lt: (b, g, qt[p], 0)))
        args.append(sel)
    rows = NSA_GROUP * tq
    grid_spec = pltpu.PrefetchScalarGridSpec(
        num_scalar_prefetch=4,
        grid=(batch, NSA_KV_HEADS, len(pairs)),
        in_specs=in_specs,
        out_specs=pl.BlockSpec((tq, gw), lambda b, g, p, qt, kt, ft, lt: (b * nq + qt[p], g)),
        scratch_shapes=[pltpu.VMEM((rows, 1), F32), pltpu.VMEM((rows, 1), F32), pltpu.VMEM((rows, HEAD_DIM), F32)],
    )
    return pl.pallas_call(
        functools.partial(_flash_prompt_kernel, tq=tq, tk=tk, mode=mode, nsp=nsp),
        grid_spec=grid_spec,
        out_shape=jax.ShapeDtypeStruct((batch * seq, NSA_W), F32),
        compiler_params=_cp(3),
        name="flash_" + mode,
    )(*tabs, *args)


def _sel_prompt_kernel(q_ref, k_ref, v_ref, sel_ref, o_ref, m_ref, l_ref, acc_ref, *, tq, tk, nsp):
    qi = pl.program_id(2)
    q4 = (_stack_heads(q_ref[...]) * SCALE).astype(BF16)
    selb = sel_ref[...].astype(BF16)
    _flash_init(m_ref, l_ref, acc_ref)
    row = lax.broadcasted_iota(jnp.int32, (tq, tk), 0)
    col = lax.broadcasted_iota(jnp.int32, (tq, tk), 1)
    jj = lax.broadcasted_iota(jnp.int32, (nsp, tk), 0)
    kk = lax.broadcasted_iota(jnp.int32, (nsp, tk), 1) // SEL_BLOCK

    def body(kb, c):
        keys = pl.ds(pl.multiple_of(kb * tk, tk), tk)
        s = _dot_nt(q4, k_ref[keys, :].astype(BF16))
        expand = jnp.where(jj == kb * (tk // SEL_BLOCK) + kk, 1.0, 0.0).astype(BF16)
        chosen = _dot(selb, expand)
        ok = (chosen > 0.5) & (kb * tk + col <= qi * tq + row)
        bias = jnp.where(ok, 0.0, 2.0 * NEG_BIG)
        s = (s.reshape(NSA_GROUP, tq, tk) + bias[None]).reshape(NSA_GROUP * tq, tk)
        m_prev = m_ref[...]
        m_new = jnp.maximum(m_prev, jnp.max(s, axis=1, keepdims=True))
        e = jnp.exp(s - m_new)
        alpha = jnp.exp(m_prev - m_new)
        l_ref[...] = alpha * l_ref[...] + jnp.sum(e, axis=1, keepdims=True)
        acc_ref[...] = alpha * acc_ref[...] + _dot(e.astype(BF16), v_ref[keys, :].astype(BF16))
        m_ref[...] = m_new
        return c

    lax.fori_loop(0, (qi * tq + tq - 1) // tk + 1, body, 0)
    o = _flash_result(l_ref, acc_ref)
    for h in range(NSA_GROUP):
        o_ref[:, h * HEAD_DIM:(h + 1) * HEAD_DIM] = o[h * tq:(h + 1) * tq]


def sel_prompt(u_head, batch, seq, sel, tq=256, tk=512):
    tq = min(tq, seq)
    tk = min(tk, seq)
    nq = seq // tq
    nsp = sel.shape[-1]
    ck = OFF_SEL // LANES
    cv = ck + NSA_KV_HEADS
    gw = NSA_GROUP * HEAD_DIM
    rows = NSA_GROUP * tq
    assert seq % tq == 0 and seq % tk == 0 and tk % SEL_BLOCK == 0
    return pl.pallas_call(
        functools.partial(_sel_prompt_kernel, tq=tq, tk=tk, nsp=nsp),
        grid=(batch, NSA_KV_HEADS, nq),
        in_specs=[pl.BlockSpec((tq, gw), lambda b, g, i: (b * nq + i, g)),
                  pl.BlockSpec((seq, LANES), lambda b, g, i: (b, ck + g)),
                  pl.BlockSpec((seq, LANES), lambda b, g, i: (b, cv + g)),
                  pl.BlockSpec((None, None, tq, nsp), lambda b, g, i: (b, g, i, 0))],
        out_specs=pl.BlockSpec((tq, gw), lambda b, g, i: (b * nq + i, g)),
        out_shape=jax.ShapeDtypeStruct((batch * seq, NSA_W), F32),
        scratch_shapes=[pltpu.VMEM((rows, 1), F32), pltpu.VMEM((rows, 1), F32), pltpu.VMEM((rows, HEAD_DIM), F32)],
        compiler_params=_cp(3),
        name="sel_prompt",
    )(u_head, u_head, u_head, sel)


def _win_prompt_kernel(q_ref, k_ref, v_ref, o_ref, *, tq, nk, seq):
    q0 = pl.program_id(2) * tq
    start = pl.multiple_of(jnp.clip(q0 - WINDOW, 0, seq - nk), tq)
    keys = pl.ds(start, nk)
    q4 = (_stack_heads(q_ref[...]) * SCALE).astype(BF16)
    s = _dot_nt(q4, k_ref[keys, :].astype(BF16))
    d = (q0 + lax.broadcasted_iota(jnp.int32, (tq, nk), 0)) - (start + lax.broadcasted_iota(jnp.int32, (tq, nk), 1))
    bias = jnp.where((d >= 0) & (d < WINDOW), 0.0, NEG_BIG)
    s = s + jnp.concatenate([bias] * NSA_GROUP, axis=0)
    e = jnp.exp(s - jnp.max(s, axis=1, keepdims=True))
    o = _dot(e.astype(BF16), v_ref[keys, :].astype(BF16)) / jnp.sum(e, axis=1, keepdims=True)
    for h in range(NSA_GROUP):
        o_ref[:, h * HEAD_DIM:(h + 1) * HEAD_DIM] = o[h * tq:(h + 1) * tq]


def win_prompt(u_head, batch, seq, tq=256):
    tq = min(tq, seq)
    nk = min(seq, WINDOW + tq)
    nq = seq // tq
    ck = OFF_WIN // LANES
    cv = ck + NSA_KV_HEADS
    gw = NSA_GROUP * HEAD_DIM
    assert seq % tq == 0 and WINDOW % tq == 0
    return pl.pallas_call(
        functools.partial(_win_prompt_kernel, tq=tq, nk=nk, seq=seq),
        grid=(batch, NSA_KV_HEADS, nq),
        in_specs=[pl.BlockSpec((tq, gw), lambda b, g, i: (b * nq + i, g)),
                  pl.BlockSpec((seq, LANES), lambda b, g, i: (b, ck + g)),
                  pl.BlockSpec((seq, LANES), lambda b, g, i: (b, cv + g))],
        out_specs=pl.BlockSpec((tq, gw), lambda b, g, i: (b * nq + i, g)),
        out_shape=jax.ShapeDtypeStruct((batch * seq, NSA_W), F32),
        compiler_params=_cp(3),
        name="win_prompt",
    )(u_head, u_head, u_head)


def _nsa_combine_kernel(gate_ref, a_ref, b_ref, c_ref, o_ref):
    gate = _sigmoid(gate_ref[...])
    for h in range(NSA_HEADS):
        sl = slice(h * HEAD_DIM, (h + 1) * HEAD_DIM)
        out = (gate[:, h:h + 1] * a_ref[:, sl] + gate[:, NSA_HEADS + h:NSA_HEADS + h + 1] * b_ref[:, sl]
               + gate[:, 2 * NSA_HEADS + h:2 * NSA_HEADS + h + 1] * c_ref[:, sl])
        o_ref[:, sl] = out.astype(o_ref.dtype)


def nsa_combine(u_head, row0, o_cmp, o_sel, o_win, tm=256):
    n = o_cmp.shape[0]
    tm = min(tm, n)
    rb0 = row0 // tm
    assert row0 % tm == 0 and n % tm == 0
    spec = pl.BlockSpec((tm, NSA_W), lambda i: (i, 0))
    return pl.pallas_call(
        _nsa_combine_kernel,
        grid=(n // tm,),
        in_specs=[pl.BlockSpec((tm, LANES), lambda i: (rb0 + i, OFF_NSA_GATE // LANES)), spec, spec, spec],
        out_specs=spec,
        out_shape=jax.ShapeDtypeStruct((n, NSA_W), BF16 if tm % 16 == 0 else F32),
        compiler_params=_cp(1),
        name="nsa_combine",
    )(u_head, o_cmp, o_sel, o_win)


def _page_rows(page_ref, first, n_rows, stride):
    return page_ref[pl.ds(first, n_rows, stride=stride), :]


def _pad_rows(x, n):
    if x.shape[0] >= n:
        return x
    return jnp.concatenate([x, jnp.zeros((n - x.shape[0], x.shape[1]), x.dtype)], axis=0)


def _cmp1_sample_kernel(pt_ref, *refs, pps, page):
    page_refs = refs[:pps]
    w_ref, pe_ref, o_ref, r_ref = refs[pps:]
    segs = page // CMP_STRIDE
    stride = CMP_STRIDE * 2 * NSA_KV_HEADS
    for kv in range(2):
        for g in range(NSA_KV_HEADS):
            for k in range(pps):
                for l in range(CMP_STRIDE):
                    r_ref[k * segs:(k + 1) * segs, l * LANES:(l + 1) * LANES] = _page_rows(
                        page_refs[k], l * 2 * NSA_KV_HEADS + kv * NSA_KV_HEADS + g, segs, stride)
            o_ref[kv, g] = _segment_products(r_ref[...], w_ref[kv], pe_ref[kv])


def compress_segments_sample(pool, layer, pt_flat, batch, n_pages, wcat, pecat, pps=16):
    page = pool.shape[2] // (2 * NSA_KV_HEADS)
    pps = min(pps, n_pages)
    assert n_pages % pps == 0 and page % CMP_STRIDE == 0
    segs = page // CMP_STRIDE
    in_specs = [pl.BlockSpec((None, None, pool.shape[2], LANES),
                             lambda b, s, pt, k=k: (layer, pt[b * n_pages + s * pps + k], 0, 0)) for k in range(pps)]
    in_specs += [pl.BlockSpec((2, SEG_W, 2 * LANES), lambda b, s, pt: (0, 0, 0)),
                 pl.BlockSpec((2, SUBLANES, SEG_W), lambda b, s, pt: (0, 0, 0))]
    grid_spec = pltpu.PrefetchScalarGridSpec(
        num_scalar_prefetch=1,
        grid=(batch, n_pages // pps),
        in_specs=in_specs,
        out_specs=pl.BlockSpec((None, 2, NSA_KV_HEADS, pps * segs, 2 * LANES), lambda b, s, pt: (b, 0, 0, s, 0)),
        scratch_shapes=[pltpu.VMEM((pps * segs, SEG_W), F32)],
    )
    return pl.pallas_call(
        functools.partial(_cmp1_sample_kernel, pps=pps, page=page),
        grid_spec=grid_spec,
        out_shape=jax.ShapeDtypeStruct((batch, 2, NSA_KV_HEADS, n_pages * segs, 2 * LANES), F32),
        compiler_params=_cp(2),
        name="cmp_segments_sample",
    )(pt_flat, *([pool] * pps), wcat, pecat)


def _sel_sample_kernel(pt_ref, *refs, pps, page, ds, past, nsp):
    q_ref, new_ref, sel_ref, selstep_ref = refs[:4]
    page_refs = refs[4:4 + pps]
    o_ref, m_ref, l_ref, acc_ref = refs[4 + pps:]
    s = pl.program_id(1)
    rows = NSA_GROUP * ds
    keys = pps * page
    bps = keys // SEL_BLOCK
    stride = 2 * NSA_KV_HEADS

    @pl.when(s == 0)
    def _():
        _flash_init(m_ref, l_ref, acc_ref)

    jj = lax.broadcasted_iota(jnp.int32, (bps, keys), 0)
    kk = lax.broadcasted_iota(jnp.int32, (bps, keys), 1)
    expand = jnp.where(jj == kk // SEL_BLOCK, 1.0, 0.0).astype(BF16)
    for g in range(NSA_KV_HEADS):
        q4 = _stack_heads(q_ref[:, g * NSA_GROUP * HEAD_DIM:(g + 1) * NSA_GROUP * HEAD_DIM]).astype(BF16)
        sel4 = jnp.concatenate([sel_ref[g]] * NSA_GROUP, axis=0)
        sel_here = jnp.concatenate([selstep_ref[g]] * NSA_GROUP, axis=0)
        k_all = jnp.concatenate([_page_rows(r, g, page, stride) for r in page_refs], axis=0).astype(BF16)
        v_all = jnp.concatenate([_page_rows(r, NSA_KV_HEADS + g, page, stride) for r in page_refs],
                                axis=0).astype(BF16)
        sc = _dot_nt(q4, k_all) * SCALE
        chosen = _dot(sel_here.astype(BF16), expand)
        _flash_step(sc, chosen > 0.5, v_all, m_ref.at[g], l_ref.at[g], acc_ref.at[g])

        @pl.when(s == pl.num_programs(1) - 1)
        def _():
            k_new = _pad_rows(new_ref[:, g * HEAD_DIM:(g + 1) * HEAD_DIM], LANES).astype(BF16)
            v_new = _pad_rows(new_ref[:, (NSA_KV_HEADS + g) * HEAD_DIM:(NSA_KV_HEADS + g + 1) * HEAD_DIM],
                              LANES).astype(BF16)
            sn = _dot_nt(q4, k_new) * SCALE
            blk = past // SEL_BLOCK
            j = lax.broadcasted_iota(jnp.int32, (rows, LANES), 1)
            t = lax.broadcasted_iota(jnp.int32, (rows, LANES), 0) % ds
            mask = (sel4[:, blk:blk + 1] > 0.5) & (j <= t)
            _flash_step(sn, mask, v_new, m_ref.at[g], l_ref.at[g], acc_ref.at[g])
            o = _flash_result(l_ref.at[g], acc_ref.at[g])
            for h in range(NSA_GROUP):
                c0 = (g * NSA_GROUP + h) * HEAD_DIM
                o_ref[:, c0:c0 + HEAD_DIM] = o[h * ds:(h + 1) * ds]


def sel_sample(u_head, row0, pool, layer, pt_flat, sel, batch, ds, n_pages, pps=32):
    page = pool.shape[2] // (2 * NSA_KV_HEADS)
    pps = min(pps, n_pages)
    past = n_pages * page
    nsp = sel.shape[-1]
    rows = NSA_GROUP * ds
    ns = n_pages // pps
    bps = pps * page // SEL_BLOCK
    assert n_pages % pps == 0 and row0 % ds == 0 and past % SEL_BLOCK == 0 and ds <= SEL_BLOCK
    assert page % SEL_BLOCK == 0
    sel_steps = sel[..., :ns * bps].reshape(batch, NSA_KV_HEADS, ds, ns, bps).transpose(0, 1, 3, 2, 4)
    in_specs = [pl.BlockSpec((ds, NSA_W), lambda b, s, pt: (row0 // ds + b, 0)),
                pl.BlockSpec((ds, KV_W), lambda b, s, pt: (row0 // ds + b, OFF_SEL // KV_W)),
                pl.BlockSpec((None, NSA_KV_HEADS, ds, nsp), lambda b, s, pt: (b, 0, 0, 0)),
                pl.BlockSpec((None, NSA_KV_HEADS, None, ds, bps), lambda b, s, pt: (b, 0, s, 0, 0))]
    in_specs += [pl.BlockSpec((None, None, pool.shape[2], LANES),
                              lambda b, s, pt, k=k: (layer, pt[b * n_pages + s * pps + k], 0, 0)) for k in range(pps)]
    grid_spec = pltpu.PrefetchScalarGridSpec(
        num_scalar_prefetch=1,
        grid=(batch, n_pages // pps),
        in_specs=in_specs,
        out_specs=pl.BlockSpec((ds, NSA_W), lambda b, s, pt: (b, 0)),
        scratch_shapes=[pltpu.VMEM((NSA_KV_HEADS, rows, 1), F32), pltpu.VMEM((NSA_KV_HEADS, rows, 1), F32),
                        pltpu.VMEM((NSA_KV_HEADS, rows, HEAD_DIM), F32)],
    )
    return pl.pallas_call(
        functools.partial(_sel_sample_kernel, pps=pps, page=page, ds=ds, past=past, nsp=nsp),
        grid_spec=grid_spec,
        out_shape=jax.ShapeDtypeStruct((batch * ds, NSA_W), F32),
        compiler_params=_cp(2),
        name="sel_sample",
    )(pt_flat, u_head, u_head, sel, sel_steps, *([pool] * pps))


def _win_sample_kernel(q_ref, new_ref, buf_ref, o_ref, m_ref, l_ref, acc_ref, *, ds, nbuf):
    rows = NSA_GROUP * ds
    stride = 2 * NSA_KV_HEADS
    keys = nbuf + LANES
    i = lax.broadcasted_iota(jnp.int32, (rows, keys), 1)
    t = lax.broadcasted_iota(jnp.int32, (rows, keys), 0) % ds
    d = jnp.where(i < nbuf, t + nbuf - i, t - (i - nbuf))
    mask = (d >= 0) & (d < WINDOW) & (i < nbuf + ds)
    for g in range(NSA_KV_HEADS):
        _flash_init(m_ref, l_ref, acc_ref)
        q4 = _stack_heads(q_ref[:, g * NSA_GROUP * HEAD_DIM:(g + 1) * NSA_GROUP * HEAD_DIM]).astype(BF16)
        k_new = _pad_rows(new_ref[:, g * HEAD_DIM:(g + 1) * HEAD_DIM], LANES)
        v_new = _pad_rows(new_ref[:, (NSA_KV_HEADS + g) * HEAD_DIM:(NSA_KV_HEADS + g + 1) * HEAD_DIM], LANES)
        k_all = jnp.concatenate([_page_rows(buf_ref, g, nbuf, stride), k_new], axis=0).astype(BF16)
        v_all = jnp.concatenate([_page_rows(buf_ref, NSA_KV_HEADS + g, nbuf, stride), v_new], axis=0).astype(BF16)
        _flash_step(_dot_nt(q4, k_all) * SCALE, mask, v_all, m_ref, l_ref, acc_ref)
        o = _flash_result(l_ref, acc_ref)
        for h in range(NSA_GROUP):
            c0 = (g * NSA_GROUP + h) * HEAD_DIM
            o_ref[:, c0:c0 + HEAD_DIM] = o[h * ds:(h + 1) * ds]


def win_sample(u_head, row0, buf, layer, batch, ds):
    nbuf = buf.shape[2] // (2 * NSA_KV_HEADS)
    rows = NSA_GROUP * ds
    return pl.pallas_call(
        functools.partial(_win_sample_kernel, ds=ds, nbuf=nbuf),
        grid=(batch,),
        in_specs=[pl.BlockSpec((ds, NSA_W), lambda b: (row0 // ds + b, 0)),
                  pl.BlockSpec((ds, KV_W), lambda b: (row0 // ds + b, OFF_WIN // KV_W)),
                  pl.BlockSpec((None, None, buf.shape[2], LANES), lambda b: (layer, b, 0, 0))],
        out_specs=pl.BlockSpec((ds, NSA_W), lambda b: (b, 0)),
        out_shape=jax.ShapeDtypeStruct((batch * ds, NSA_W), F32),
        scratch_shapes=[pltpu.VMEM((rows, 1), F32), pltpu.VMEM((rows, 1), F32), pltpu.VMEM((rows, HEAD_DIM), F32)],
        compiler_params=_cp(1),
        name="win_sample",
    )(u_head, u_head, buf)


SB_ROWS = 16


def _sb_sample_kernel(pt_ref, q_ref, kn_ref, vn_ref, pool_hbm, o_ref, acc_ref, carry_ref, kv_ref, sem, *,
                      pps, page, ds, n_pages, layer):
    b = pl.program_id(0)
    stride = 2 * SB_HEADS
    umat = _suffix_matrix()
    q_heads = [_pad_rows(q_ref[:, h * HEAD_DIM:(h + 1) * HEAD_DIM] * SCALE, SB_ROWS).astype(BF16)
               for h in range(SB_HEADS)]
    rows = SB_HEADS * SB_ROWS
    real_row = lax.broadcasted_iota(jnp.int32, (rows, LANES), 0) % SB_ROWS < ds

    def accumulate(k_heads, v_heads, mask):
        z = jnp.concatenate([_dot_nt(q_heads[h], k_heads[h]) for h in range(SB_HEADS)], axis=0)
        a, carry = _sb_weights(z, mask, carry_ref[...], umat)
        carry_ref[...] = carry
        a = a.astype(BF16)
        for h in range(SB_HEADS):
            sl = slice(h * SB_ROWS, (h + 1) * SB_ROWS)
            acc_ref[sl, :] += _dot(a[sl], v_heads[h])
        return jnp.max(jnp.where(real_row, carry, NEG_BIG))

    acc_ref[...] = jnp.zeros_like(acc_ref)
    carry_ref[...] = jnp.zeros_like(carry_ref)
    k_new = [_pad_rows(kn_ref[:, h * HEAD_DIM:(h + 1) * HEAD_DIM], LANES).astype(BF16) for h in range(SB_HEADS)]
    v_new = [_pad_rows(vn_ref[:, h * HEAD_DIM:(h + 1) * HEAD_DIM], LANES).astype(BF16) for h in range(SB_HEADS)]
    j = lax.broadcasted_iota(jnp.int32, (rows, LANES), 1)
    t = lax.broadcasted_iota(jnp.int32, (rows, LANES), 0) % SB_ROWS
    top = accumulate(k_new, v_new, j < t)

    def page_copy(g, k):
        pid = pt_ref[b * n_pages + n_pages - (g + 1) * pps + k]
        return pltpu.make_async_copy(pool_hbm.at[layer, pid], kv_ref.at[k], sem)

    def cond(st):
        return jnp.logical_and(st[0] < n_pages // pps, st[1] > SB_ZERO_LOG)

    def body(st):
        g = st[0]
        for k in range(pps):
            page_copy(g, k).start()
        for k in range(pps):
            page_copy(g, k).wait()
        k_heads = [jnp.concatenate([_page_rows(kv_ref.at[k], h, page, stride) for k in range(pps)],
                                   axis=0).astype(BF16) for h in range(SB_HEADS)]
        v_heads = [jnp.concatenate([_page_rows(kv_ref.at[k], SB_HEADS + h, page, stride) for k in range(pps)],
                                   axis=0).astype(BF16) for h in range(SB_HEADS)]
        return g + 1, accumulate(k_heads, v_heads, None)

    lax.while_loop(cond, body, (0, top))
    for h in range(SB_HEADS):
        o_ref[:, h * HEAD_DIM:(h + 1) * HEAD_DIM] = acc_ref[h * SB_ROWS:h * SB_ROWS + ds, :]


def sb_sample(u_rest, row0, pool, layer, pt_flat, batch, ds, n_pages, pps=2):
    page = pool.shape[2] // (2 * SB_HEADS)
    pps = min(pps, n_pages)
    assert n_pages % pps == 0 and row0 % ds == 0 and ds <= SB_ROWS
    rb = row0 // ds
    grid_spec = pltpu.PrefetchScalarGridSpec(
        num_scalar_prefetch=1,
        grid=(batch,),
        in_specs=[pl.BlockSpec((ds, SB_W), lambda b, pt: (rb + b, R_SB_Q // SB_W)),
                  pl.BlockSpec((ds, SB_W), lambda b, pt: (rb + b, R_SB_K // SB_W)),
                  pl.BlockSpec((ds, SB_W), lambda b, pt: (rb + b, R_SB_V // SB_W)),
                  pl.BlockSpec(memory_space=pl.ANY)],
        out_specs=pl.BlockSpec((ds, SB_W), lambda b, pt: (b, 0)),
        scratch_shapes=[pltpu.VMEM((SB_HEADS * SB_ROWS, HEAD_DIM), F32), pltpu.VMEM((SB_HEADS * SB_ROWS, LANES), F32),
                        pltpu.VMEM((pps, pool.shape[2], LANES), F32), pltpu.SemaphoreType.DMA],
    )
    return pl.pallas_call(
        functools.partial(_sb_sample_kernel, pps=pps, page=page, ds=ds, n_pages=n_pages, layer=layer),
        grid_spec=grid_spec,
        out_shape=jax.ShapeDtypeStruct((batch * ds, SB_W), F32),
        compiler_params=_cp(1),
        name="sb_sample",
    )(pt_flat, u_rest, u_rest, u_rest, pool)


def _router_kernel(x_ref, g_ref, w_ref, b_ref, h_ref, o_ref):
    x = x_ref[...]
    h = x * lax.rsqrt(jnp.mean(x * x, axis=-1, keepdims=True) + NORM_EPS) * g_ref[...]
    h_ref[...] = h
    hh, hm, hl = _split3(h)
    wh, wm, wl = _split3(w_ref[...])
    logits = (_dot(hh, wh) + _dot(hh, wm) + _dot(hm, wh) + _dot(hh, wl) + _dot(hl, wh) + _dot(hm, wm)) + b_ref[...]
    lane = lax.broadcasted_iota(jnp.int32, logits.shape, 1)
    logits = jnp.where(lane < N_EXPERTS, logits, NEG_BIG)
    m1 = jnp.max(logits, axis=1, keepdims=True)
    i1 = jnp.min(jnp.where(logits == m1, lane, LANES), axis=1, keepdims=True)
    rest = jnp.where(lane == i1, NEG_BIG, logits)
    m2 = jnp.max(rest, axis=1, keepdims=True)
    i2 = jnp.min(jnp.where(rest == m2, lane, LANES), axis=1, keepdims=True)
    e = jnp.exp(m2 - m1)
    g1 = 1.0 / (1.0 + e)
    g2 = e / (1.0 + e)
    o_ref[...] = jnp.where(lane == 0, i1.astype(F32), jnp.where(lane == 1, i2.astype(F32),
                           jnp.where(lane == 2, g1, jnp.where(lane == 3, g2, 0.0))))


def moe_router(x, g, rw, rb, tm):
    m, d = x.shape
    w = jnp.zeros((d, LANES), F32).at[:, :N_EXPERTS].set(rw)
    b = jnp.zeros((1, LANES), F32).at[0, :N_EXPERTS].set(rb.astype(F32))
    return pl.pallas_call(
        _router_kernel,
        grid=(m // tm,),
        in_specs=[pl.BlockSpec((tm, d), lambda i: (i, 0)), pl.BlockSpec((1, d), lambda i: (0, 0)),
                  pl.BlockSpec((d, LANES), lambda i: (0, 0)), pl.BlockSpec((1, LANES), lambda i: (0, 0))],
        out_specs=[pl.BlockSpec((tm, d), lambda i: (i, 0)), pl.BlockSpec((tm, LANES), lambda i: (i, 0))],
        out_shape=[jax.ShapeDtypeStruct((m, d), F32), jax.ShapeDtypeStruct((m, LANES), F32)],
        compiler_params=_cp(1),
        name="moe_router",
    )(x, g.reshape(1, d), w, b)


def _row_copy(src_hbm, row, dst_ref, r, sem):
    return pltpu.make_async_copy(src_hbm.at[pl.ds(row, 1), :], dst_ref.at[pl.ds(r, 1), :], sem)


def _gather_kernel(tok_ref, nu_ref, h_hbm, o_ref, buf_ref, sem):
    blk = pl.program_id(0)
    n = buf_ref.shape[0]

    @pl.when(blk < nu_ref[0])
    def _():
        def issue(r, c):
            _row_copy(h_hbm, tok_ref[blk * n + r], buf_ref, r, sem).start()
            return c

        lax.fori_loop(0, n, issue, 0, unroll=8)
        pltpu.make_async_copy(h_hbm.at[pl.ds(0, n), :], buf_ref, sem).wait()
        o_ref[...] = buf_ref[...].astype(o_ref.dtype)

    @pl.when(blk >= nu_ref[0])
    def _():
        o_ref[...] = jnp.zeros_like(o_ref)


def _used_block(i, nu):
    return jnp.minimum(i, nu[0] - 1)


def moe_gather(h, tok_buf, n_used):
    p = tok_buf.shape[0]
    d = h.shape[1]
    grid_spec = pltpu.PrefetchScalarGridSpec(
        num_scalar_prefetch=2,
        grid=(p // MOE_BLOCK,),
        in_specs=[pl.BlockSpec(memory_space=pl.ANY)],
        out_specs=pl.BlockSpec((MOE_BLOCK, d), lambda i, tok, nu: (i, 0)),
        scratch_shapes=[pltpu.VMEM((MOE_BLOCK, d), F32), pltpu.SemaphoreType.DMA],
    )
    return pl.pallas_call(
        _gather_kernel,
        grid_spec=grid_spec,
        out_shape=jax.ShapeDtypeStruct((p, d), BF16),
        compiler_params=_cp(1),
        name="moe_gather",
    )(tok_buf, n_used, h)


def _moe_mm_kernel(be_ref, nu_ref, *refs, n_w, mode):
    a_ref = refs[0]
    w_refs = refs[1:1 + n_w]
    o_ref = refs[1 + n_w]
    wbf_refs = refs[2 + n_w:]
    blk = pl.program_id(1)
    changed = jnp.logical_or(blk == 0, be_ref[blk] != be_ref[jnp.maximum(blk - 1, 0)])

    @pl.when(changed)
    def _():
        for w_ref, wbf_ref in zip(w_refs, wbf_refs):
            wbf_ref[...] = w_ref[...].astype(BF16)

    @pl.when(blk < nu_ref[0])
    def _():
        prods = [_dot(a_ref[...], r[...]) for r in wbf_refs]
        o_ref[...] = _mm_epilogue(mode, prods, ()).astype(o_ref.dtype)

    @pl.when(blk >= nu_ref[0])
    def _():
        o_ref[...] = jnp.zeros_like(o_ref)


def moe_matmul(a, w_list, moe_index, blk_e, n_used, mode, out_dtype, tn=512):
    p, k = a.shape
    n_out = w_list[0].shape[-1]
    tn = _tile(n_out, tn)
    grid_spec = pltpu.PrefetchScalarGridSpec(
        num_scalar_prefetch=2,
        grid=(n_out // tn, p // MOE_BLOCK),
        in_specs=[pl.BlockSpec((MOE_BLOCK, k), lambda j, i, be, nu: (_used_block(i, nu), 0))]
        + [pl.BlockSpec((None, None, k, tn), lambda j, i, be, nu: (moe_index, be[i], 0, j)) for _ in w_list],
        out_specs=pl.BlockSpec((MOE_BLOCK, tn), lambda j, i, be, nu: (i, j)),
        scratch_shapes=[pltpu.VMEM((k, tn), BF16) for _ in w_list],
    )
    return pl.pallas_call(
        functools.partial(_moe_mm_kernel, n_w=len(w_list), mode=mode),
        grid_spec=grid_spec,
        out_shape=jax.ShapeDtypeStruct((p, n_out), out_dtype),
        compiler_params=_cp(2),
        name="moe_mm_" + mode,
    )(blk_e, n_used, a, *w_list)


def _moe_combine_kernel(pos_ref, y_hbm, x_ref, gate_ref, o_ref, buf_ref, sem, *, tm):
    i = pl.program_id(0)

    def issue(r, c):
        for k in range(TOP_K):
            _row_copy(y_hbm, pos_ref[(i * tm + r) * TOP_K + k], buf_ref.at[k], r, sem).start()
        return c

    lax.fori_loop(0, tm, issue, 0, unroll=8)
    for k in range(TOP_K):
        pltpu.make_async_copy(y_hbm.at[pl.ds(0, tm), :], buf_ref.at[k], sem).wait()
    gate = gate_ref[...]
    out = x_ref[...]
    y = gate[:, 2:3] * buf_ref[0]
    for k in range(1, TOP_K):
        y = y + gate[:, 2 + k:3 + k] * buf_ref[k]
    o_ref[...] = out + y


def moe_combine(x, yb, pos, gates, tm=256):
    m, d = x.shape
    grid_spec = pltpu.PrefetchScalarGridSpec(
        num_scalar_prefetch=1,
        grid=(m // tm,),
        in_specs=[pl.BlockSpec(memory_space=pl.ANY),
                  pl.BlockSpec((tm, d), lambda i, pos: (i, 0)),
                  pl.BlockSpec((tm, LANES), lambda i, pos: (i, 0))],
        out_specs=pl.BlockSpec((tm, d), lambda i, pos: (i, 0)),
        scratch_shapes=[pltpu.VMEM((TOP_K, tm, d), F32), pltpu.SemaphoreType.DMA],
    )
    return pl.pallas_call(
        functools.partial(_moe_combine_kernel, tm=tm),
        grid_spec=grid_spec,
        out_shape=jax.ShapeDtypeStruct((m, d), F32),
        compiler_params=_cp(1),
        name="moe_combine",
    )(pos, yb, x, gates)


def moe_layer(x, g, rw, rb, w1, w3, w2, moe_index, n_tok, tm):
    m, d = x.shape
    h, route = moe_router(x, g, rw, rb, tm)
    top_i = route[:n_tok, :TOP_K].astype(jnp.int32)
    a = n_tok * TOP_K
    e_flat = top_i.reshape(a)
    order = jnp.argsort(e_flat)
    e_s = e_flat[order]
    tok_s = (order // TOP_K).astype(jnp.int32)
    counts = jnp.bincount(e_flat, length=N_EXPERTS)
    start = jnp.cumsum(counts) - counts
    padded = (counts + MOE_BLOCK - 1) // MOE_BLOCK * MOE_BLOCK
    pend = jnp.cumsum(padded)
    pstart = pend - padded
    dest = (pstart[e_s] + jnp.arange(a) - start[e_s]).astype(jnp.int32)
    nb = -(-a // MOE_BLOCK) + N_EXPERTS
    p = nb * MOE_BLOCK
    n_used = (pend[-1] // MOE_BLOCK).astype(jnp.int32)
    blk_e = jnp.minimum(jnp.searchsorted(pend, jnp.arange(nb) * MOE_BLOCK, side="right"), N_EXPERTS - 1)
    slot = jnp.arange(p)
    slot_e = blk_e[slot // MOE_BLOCK]
    local = slot - pstart[slot_e]
    tok_buf = jnp.where(local < counts[slot_e], tok_s[jnp.clip(start[slot_e] + local, 0, a - 1)], n_tok)
    tok_buf = tok_buf.astype(jnp.int32)
    blk_e = jnp.where(jnp.arange(nb) < n_used, blk_e, blk_e[n_used - 1]).astype(jnp.int32)
    pos = jnp.concatenate([dest[jnp.argsort(order)], jnp.zeros((m * TOP_K - a,), jnp.int32)])
    row_ok = (jnp.arange(m) < n_tok)[:, None]
    gates = jnp.where(row_ok, route, 0.0)
    n_used = n_used.reshape(1)
    xs = moe_gather(h, tok_buf, n_used)
    act = moe_matmul(xs, [w1, w3], moe_index, blk_e, n_used, "swiglu", BF16)
    yb = moe_matmul(act, [w2], moe_index, blk_e, n_used, "plain", F32)
    return moe_combine(x, yb, pos, gates)


TM = 512


def _row_tile(m, pref=1152):
    t = (min(pref, m) // 16) * 16
    while m % t:
        t -= 16
    return t


def _slab(parts, m, dtype):
    rows = sum(p.shape[0] for p in parts)
    parts = [p.astype(dtype) for p in parts]
    return jnp.concatenate(parts + [jnp.zeros((m - rows, parts[0].shape[1]), dtype)], axis=0)


def kernel(x_prompt, x_sample, cache_nsa_cmp_kv, cache_nsa_sel_kv, cache_sb_kv, cache_nsa_win_kv, state_hgrn,
           page_table, attn_norm, w_in, cmp_pe_k, cmp_w1_k, cmp_w2_k, cmp_pe_v, cmp_w1_v, cmp_w2_v, hg_lb_logits,
           hg_norm, w_br_nsa, w_br_sb, w_br_hg, w_out, ffn_norm, ffn_w1, ffn_w3, ffn_w2, router_w, router_b,
           moe_w1, moe_w3, moe_w2, final_norm):
    bsz, seq, d = x_prompt.shape
    db, ds, _ = x_sample.shape
    depth = attn_norm.shape[0]
    n_p, n_s = bsz * seq, db * ds
    n_tok = n_p + n_s
    m = -(-(n_tok + 1) // TM) * TM
    n_pool, page = cache_nsa_cmp_kv.shape[1:3]
    n_pages = page_table.shape[1]
    past = n_pages * page
    assert past % CMP_STRIDE == 0 and ds < CMP_STRIDE and seq % CMP_STRIDE == 0

    x = _slab([x_prompt.reshape(n_p, d), x_sample.reshape(n_s, d)], m, F32)
    pt_flat = page_table.reshape(-1).astype(jnp.int32)
    cmp_pool = cache_nsa_cmp_kv.reshape(depth, n_pool, page * 2 * NSA_KV_HEADS, HEAD_DIM)
    sel_pool = cache_nsa_sel_kv.reshape(depth, n_pool, page * 2 * NSA_KV_HEADS, HEAD_DIM)
    sb_pool = cache_sb_kv.reshape(depth, n_pool, page * 2 * SB_HEADS, HEAD_DIM)
    nbuf = cache_nsa_win_kv.shape[2]
    win_buf = cache_nsa_win_kv.reshape(depth, db, nbuf * 2 * NSA_KV_HEADS, HEAD_DIM)
    rest_cols = w_in.shape[-1] - REST_START
    tmm = _row_tile(m)
    w_in_t = jnp.swapaxes(w_in, 1, 2)
    lb_all = jnp.cumsum(jax.nn.softmax(hg_lb_logits.astype(F32), axis=0), axis=0)
    kvs = (2, NSA_KV_HEADS, HEAD_DIM)

    states = []
    for l in range(depth):
        h = rmsnorm(x, attn_norm[l], BF16, TM)
        u_head = inproj_t(h, w_in_t, l, 0, HEAD_COLS, tmm, HEAD_COLS // 3)
        u_rest = inproj_t(h, w_in_t, l, REST_START, rest_cols, tmm, 1024)

        wk, pk = compress_params(cmp_pe_k[l], cmp_w1_k[l])
        wv, pv = compress_params(cmp_pe_v[l], cmp_w1_v[l])
        wcat, pecat = jnp.stack([wk, wv]), jnp.stack([pk, pv])
        w2s = jnp.stack([cmp_w2_k[l], cmp_w2_v[l]])
        lb = (lb_all[l] - lb_all[0]).reshape(HG_HEADS, HEAD_DIM)
        lbp = hgrn_lb_params(lb)

        kvc = compress_finish(compress_segments_prompt(u_head, bsz, seq, wcat, pecat), w2s)
        o_cmp, sel = cmp_select(u_head, 0, bsz, seq, kvc, seq // CMP_STRIDE - 1, -(-seq // SEL_BLOCK), 0)
        o_sel = sel_prompt(u_head, bsz, seq, sel)
        o_win = win_prompt(u_head, bsz, seq)
        nsa_p = nsa_combine(u_head, 0, o_cmp, o_sel, o_win)
        sb_p = sb_prompt(u_rest, bsz, seq)
        hg_p, s_p = hgrn(u_rest, 0, bsz, seq, lbp, hg_norm[l], jnp.zeros((bsz, HG_HEADS, HEAD_DIM, HEAD_DIM), F32))

        kvc_s = compress_finish(compress_segments_sample(cmp_pool, l, pt_flat, db, n_pages, wcat, pecat), w2s)
        o_cmp_s, sel_s = cmp_select(u_head, n_p, db, ds, kvc_s, past // CMP_STRIDE - 1,
                                    -(-(past + ds) // SEL_BLOCK), past, tq=ds)
        o_sel_s = sel_sample(u_head, n_p, sel_pool, l, pt_flat, sel_s, db, ds, n_pages)
        o_win_s = win_sample(u_head, n_p, win_buf, l, db, ds)
        nsa_s = nsa_combine(u_head, n_p, o_cmp_s, o_sel_s, o_win_s)
        sb_s = sb_sample(u_rest, n_p, sb_pool, l, pt_flat, db, ds, n_pages)
        hg_s, s_s = hgrn(u_rest, n_p, db, ds, lbp, hg_norm[l], state_hgrn[l])

        o_nsa = _slab([nsa_p, nsa_s], m, BF16)
        o_sb = _slab([sb_p, sb_s], m, BF16)
        o_hg = _slab([hg_p, hg_s], m, BF16)
        merged = matmul([o_nsa, o_sb, o_hg], [w_br_nsa, w_br_sb, w_br_hg], l, 0, d, "merge", BF16,
                        extras=[(u_rest, R_MERGE), (u_rest, R_MERGE + d), (u_rest, R_MERGE + 2 * d)], tm=tmm)
        x = matmul([merged], [w_out], l, 0, d, "residual", F32, extras=[(x, 0)], tm=tmm)

        i = l // 2
        if l % 2 == 0:
            h2 = rmsnorm(x, ffn_norm[l], BF16, TM)
            act = matmul([h2], [ffn_w1, ffn_w3], i, 0, ffn_w1.shape[-1], "swiglu", BF16, tm=tmm)
            x = matmul([act], [ffn_w2], i, 0, d, "residual", F32, extras=[(x, 0)])
        else:
            x = moe_layer(x, ffn_norm[l], router_w[i], router_b[i], moe_w1, moe_w3, moe_w2, i, n_tok, TM)

        def head_cols(off, r0, r1, lead):
            return u_head[r0:r1, off:off + KV_W].reshape(lead + kvs)

        win_p = head_cols(OFF_WIN, 0, n_p, (bsz, seq))[:, seq - min(WINDOW, seq):]
        win_s = jnp.concatenate([cache_nsa_win_kv[l], head_cols(OFF_WIN, n_p, n_tok, (db, ds))], axis=1)[:, ds:]
        states.append((
            head_cols(OFF_CMP, 0, n_p, (bsz, seq)), head_cols(OFF_SEL, 0, n_p, (bsz, seq)), win_p,
            u_rest[:n_p, R_SB_K:R_SB_K + 2 * SB_W].reshape(bsz, seq, 2, SB_HEADS, HEAD_DIM), s_p,
            head_cols(OFF_CMP, n_p, n_tok, (db, ds)), head_cols(OFF_SEL, n_p, n_tok, (db, ds)), win_s,
            u_rest[n_p:n_tok, R_SB_K:R_SB_K + 2 * SB_W].reshape(db, ds, 2, SB_HEADS, HEAD_DIM), s_s))

    y = rmsnorm(x, final_norm, F32, TM)
    stacked = [jnp.stack([st[i] for st in states]) for i in range(10)]
    return (y[:n_p].reshape(bsz, seq, d), y[n_p:n_tok].reshape(db, ds, d), *stacked)
```

```python
import functools

import jax
import jax.numpy as jnp
import numpy as np
from jax import lax
from jax.experimental import pallas as pl
from jax.experimental.pallas import tpu as pltpu

F32 = jnp.float32
BF16 = jnp.bfloat16

HEAD_DIM = 128
SCALE = HEAD_DIM ** -0.5
NSA_HEADS = 8
NSA_KV_HEADS = 2
NSA_GROUP = NSA_HEADS // NSA_KV_HEADS
CMP_LEN = 32
CMP_STRIDE = 16
SEL_BLOCK = 64
SEL_TOPK = 16
WINDOW = 512
SB_HEADS = 8
HG_HEADS = 8
HG_CHUNK = 64
HG_SUB = 16
HG_MIN_CHUNK = 16
N_EXPERTS = 8
TOP_K = 2
MOE_BLOCK = 512
NORM_EPS = 1e-6
NEG_BIG = -1e30

LANES = 128
SUBLANES = 8
VMEM_LIMIT = 56 * 1024 * 1024

NSA_W = NSA_HEADS * HEAD_DIM
KV_W = 2 * NSA_KV_HEADS * HEAD_DIM
OFF_CMP = NSA_W
OFF_SEL = OFF_CMP + KV_W
OFF_WIN = OFF_SEL + KV_W
OFF_NSA_GATE = OFF_WIN + KV_W
N_GATE = 3 * NSA_HEADS
HEAD_COLS = OFF_NSA_GATE + LANES
REST_START = OFF_NSA_GATE + N_GATE
SB_W = SB_HEADS * HEAD_DIM
HG_W = HG_HEADS * HEAD_DIM
R_SB_Q = 0
R_SB_K = R_SB_Q + SB_W
R_SB_V = R_SB_K + SB_W
R_HG_Q = R_SB_V + SB_W
R_HG_F = R_HG_Q + HG_W
R_HG_I = R_HG_F + HG_W
R_HG_G = R_HG_I + HG_W
R_MERGE = R_HG_G + HG_W


def _cp(n_axes, vmem=VMEM_LIMIT):
    return pltpu.CompilerParams(dimension_semantics=("arbitrary",) * n_axes, vmem_limit_bytes=vmem)


def _tile(n, pref, quantum=LANES):
    if n <= pref:
        return n
    t = (pref // quantum) * quantum
    while t > quantum and n % t:
        t -= quantum
    assert n % t == 0, (n, pref)
    return t


def _dot(a, b):
    return jnp.dot(a, b, preferred_element_type=F32)


def _dot_nt(a, b):
    return lax.dot_general(a, b, (((1,), (1,)), ((), ())), preferred_element_type=F32)


def _dot_tn(a, b):
    return lax.dot_general(a, b, (((0,), (0,)), ((), ())), preferred_element_type=F32)


def _split3(x):
    hi = x.astype(BF16)
    r = x - hi.astype(F32)
    mid = r.astype(BF16)
    lo = (r - mid.astype(F32)).astype(BF16)
    return hi, mid, lo


def _split2(x):
    hi = x.astype(BF16)
    return hi, (x - hi.astype(F32)).astype(BF16)


def _sigmoid(x):
    return 1.0 / (1.0 + jnp.exp(-x))


def _silu(x):
    return x * _sigmoid(x)


def _log_sigmoid(x):
    return jnp.minimum(x, 0.0) - jnp.log1p(jnp.exp(-jnp.abs(x)))


def _rmsnorm_kernel(x_ref, g_ref, o_ref):
    x = x_ref[...]
    y = x * lax.rsqrt(jnp.mean(x * x, axis=-1, keepdims=True) + NORM_EPS)
    o_ref[...] = (y * g_ref[...]).astype(o_ref.dtype)


def rmsnorm(x, g, out_dtype, tm):
    m, d = x.shape
    return pl.pallas_call(
        _rmsnorm_kernel,
        grid=(m // tm,),
        in_specs=[pl.BlockSpec((tm, d), lambda i: (i, 0)), pl.BlockSpec((1, d), lambda i: (0, 0))],
        out_specs=pl.BlockSpec((tm, d), lambda i: (i, 0)),
        out_shape=jax.ShapeDtypeStruct((m, d), out_dtype),
        compiler_params=_cp(1),
        name="rmsnorm",
    )(x, g.reshape(1, d))


def _mm_epilogue(mode, prods, x_refs):
    if mode == "plain":
        return prods[0]
    if mode == "merge":
        out = _sigmoid(x_refs[0][...]) * prods[0]
        for x_ref, p in zip(x_refs[1:], prods[1:]):
            out = out + _sigmoid(x_ref[...]) * p
        return out
    if mode == "residual":
        return x_refs[0][...] + prods[0]
    return _silu(prods[0]) * prods[1]


def _mm_kernel(*refs, n_a, n_w, n_extra, mode, cast, main_tiles):
    n_in = n_a if main_tiles is None else 2 * n_a
    a_in, rest = refs[:n_in], refs[n_in:]
    if main_tiles is None:
        a_vals = [r[...] for r in a_in]
    else:
        in_main = pl.program_id(1) < main_tiles
        a_vals = [jnp.where(in_main, a_in[2 * t][...], a_in[2 * t + 1][...]) for t in range(n_a)]
    w_refs = rest[:n_w]
    x_refs = rest[n_w:n_w + n_extra]
    o_ref = rest[n_w + n_extra]
    wbf_refs = rest[n_w + n_extra + 1:]

    if cast:
        @pl.when(pl.program_id(1) == 0)
        def _():
            for w_ref, wbf_ref in zip(w_refs, wbf_refs):
                wbf_ref[...] = w_ref[...].astype(BF16)
        ws = [r[...] for r in wbf_refs]
    else:
        ws = [r[...] for r in w_refs]

    prods = [_dot(a_vals[min(i, n_a - 1)], w) for i, w in enumerate(ws)]
    o_ref[...] = _mm_epilogue(mode, prods, x_refs).astype(o_ref.dtype)


def matmul(a_list, w_list, w_index, col0, n_out, mode, out_dtype, extras=(), tm=512, tn=512, m=None):
    split = isinstance(a_list[0], tuple)
    m = m or a_list[0].shape[0]
    tn = _tile(n_out, tn)
    assert col0 % tn == 0 and m % tm == 0
    cast = w_list[0].dtype != BF16
    in_specs, args, scratch = [], [], []
    main_tiles = None
    for a in a_list:
        if split:
            main, tail = a
            main_tiles = main.shape[0] // tm
            assert main.shape[0] % tm == 0 and m == (main_tiles + 1) * tm and tail.shape[0] <= tm
            tail = jnp.concatenate([tail, jnp.zeros((tm - tail.shape[0], tail.shape[1]), tail.dtype)], axis=0)
            in_specs.append(pl.BlockSpec((tm, main.shape[1]), lambda j, i, n=main_tiles: (jnp.minimum(i, n - 1), 0)))
            in_specs.append(pl.BlockSpec((tm, main.shape[1]), lambda j, i: (0, 0)))
            args += [main, tail]
            continue
        in_specs.append(pl.BlockSpec((tm, a.shape[1]), lambda j, i: (i, 0)))
        args.append(a)
    for w in w_list:
        k = w.shape[-2]
        if w.ndim == 3:
            in_specs.append(pl.BlockSpec((None, k, tn), lambda j, i: (w_index, 0, j + col0 // tn)))
        else:
            in_specs.append(pl.BlockSpec((k, tn), lambda j, i: (0, j + col0 // tn)))
        args.append(w)
        if cast:
            scratch.append(pltpu.VMEM((k, tn), BF16))
    for x, off in extras:
        assert off % tn == 0
        in_specs.append(pl.BlockSpec((tm, tn), lambda j, i, off=off: (i, j + off // tn)))
        args.append(x)
    return pl.pallas_call(
        functools.partial(_mm_kernel, n_a=len(a_list), n_w=len(w_list), n_extra=len(extras), mode=mode, cast=cast,
                          main_tiles=main_tiles),
        grid=(n_out // tn, m // tm),
        in_specs=in_specs,
        out_specs=pl.BlockSpec((tm, tn), lambda j, i: (i, j)),
        out_shape=jax.ShapeDtypeStruct((m, n_out), out_dtype),
        scratch_shapes=scratch,
        compiler_params=_cp(2),
        name="mm_" + mode,
    )(*args)


def _inproj_t_kernel(a_ref, wt_ref, o_ref, wbf_ref):
    @pl.when(pl.program_id(1) == 0)
    def _():
        wbf_ref[...] = wt_ref[0].T.astype(BF16)

    o_ref[...] = _dot(a_ref[...], wbf_ref[...])


def inproj_t(a, w_t, layer, row0, n_out, tm, tn):
    m, k = a.shape
    tn = _tile(n_out, tn)
    assert m % tm == 0 and row0 % SUBLANES == 0
    return pl.pallas_call(
        _inproj_t_kernel,
        grid=(n_out // tn, m // tm),
        in_specs=[pl.BlockSpec((tm, k), lambda j, i: (i, 0)),
                  pl.BlockSpec((pl.Element(1), pl.Element(tn), pl.Element(k)),
                               lambda j, i: (layer, pl.multiple_of(row0 + j * tn, SUBLANES), 0))],
        out_specs=pl.BlockSpec((tm, tn), lambda j, i: (i, j)),
        out_shape=jax.ShapeDtypeStruct((m, n_out), F32),
        scratch_shapes=[pltpu.VMEM((k, tn), BF16)],
        compiler_params=_cp(2),
        name="inproj_t",
    )(a, w_t)


def _inproj_rest_kernel(a_ref, wa_ref, wb_ref, o_ref, wbf_ref, *, shift, tn):
    @pl.when(pl.program_id(1) == 0)
    def _():
        w = jnp.concatenate([wa_ref[...], wb_ref[...]], axis=1)
        wbf_ref[...] = w[:, shift:shift + tn].astype(BF16)

    o_ref[...] = _dot(a_ref[...], wbf_ref[...])


def inproj_rest(a, w_in, layer, n_out, tm, tn=512):
    m, k = a.shape
    shift = REST_START % LANES
    base = REST_START - shift
    tn = _tile(n_out, tn)
    assert base % tn == 0 and m % tm == 0 and REST_START + n_out == w_in.shape[-1]
    return pl.pallas_call(
        functools.partial(_inproj_rest_kernel, shift=shift, tn=tn),
        grid=(n_out // tn, m // tm),
        in_specs=[pl.BlockSpec((tm, k), lambda j, i: (i, 0)),
                  pl.BlockSpec((None, k, tn), lambda j, i: (layer, 0, base // tn + j)),
                  pl.BlockSpec((None, k, LANES), lambda j, i: (layer, 0, (base + (j + 1) * tn) // LANES))],
        out_specs=pl.BlockSpec((tm, tn), lambda j, i: (i, j)),
        out_shape=jax.ShapeDtypeStruct((m, n_out), F32),
        scratch_shapes=[pltpu.VMEM((k, tn), BF16)],
        compiler_params=_cp(2),
        name="inproj_rest",
    )(a, w_in, w_in)


def _suffix_matrix():
    j = lax.broadcasted_iota(jnp.int32, (LANES, 2 * LANES), 0)
    s = lax.broadcasted_iota(jnp.int32, (LANES, 2 * LANES), 1)
    return jnp.where((j > s) | (s >= LANES), 1.0, 0.0).astype(BF16)


def _sb_weights(z, mask, carry, umat):
    t = jnp.log(1.0 + jnp.exp(-jnp.abs(z)))
    ls_pos = jnp.minimum(z, 0.0) - t
    c = ls_pos - z
    if mask is not None:
        c = jnp.where(mask, c, 0.0)
    n_sub = z.shape[1] // LANES
    pieces = [None] * n_sub
    for sb in reversed(range(n_sub)):
        sl = slice(sb * LANES, (sb + 1) * LANES)
        hi, lo = _split2(c[:, sl])
        r = _dot(hi, umat) + _dot(lo, umat)
        w = jnp.exp(ls_pos[:, sl] + (carry + r[:, :LANES]))
        pieces[sb] = w if mask is None else jnp.where(mask[:, sl], w, 0.0)
        carry = carry + r[:, LANES:]
    a = pieces[0] if n_sub == 1 else jnp.concatenate(pieces, axis=1)
    return a, carry


SB_ZERO_LOG = -104.0


def _sb_prompt_kernel(q_ref, k_ref, v_ref, o_ref, acc_ref, carry_ref, *, t, hb):
    qi = pl.program_id(2)
    umat = _suffix_matrix()
    qs = [(q_ref[:, j * HEAD_DIM:(j + 1) * HEAD_DIM] * SCALE).astype(BF16) for j in range(hb)]
    row = lax.broadcasted_iota(jnp.int32, (t, t), 0)
    col = lax.broadcasted_iota(jnp.int32, (t, t), 1)

    def tile(kb, mask):
        rows = pl.ds(pl.multiple_of(kb * t, t), t)
        top = None
        for j in range(hb):
            cols = slice(j * HEAD_DIM, (j + 1) * HEAD_DIM)
            z = _dot_nt(qs[j], k_ref[rows, cols].astype(BF16))
            a, carry = _sb_weights(z, mask, carry_ref[j], umat)
            acc_ref[j] += _dot(a.astype(BF16), v_ref[rows, cols].astype(BF16))
            carry_ref[j] = carry
            top = jnp.max(carry) if top is None else jnp.maximum(top, jnp.max(carry))
        return top

    acc_ref[...] = jnp.zeros_like(acc_ref)
    carry_ref[...] = jnp.zeros_like(carry_ref)
    top = tile(qi, col < row)

    def cond(st):
        return jnp.logical_and(st[0] >= 0, st[1] > SB_ZERO_LOG)

    def body(st):
        return st[0] - 1, tile(st[0], None)

    lax.while_loop(cond, body, (qi - 1, top))
    for j in range(hb):
        o_ref[:, j * HEAD_DIM:(j + 1) * HEAD_DIM] = acc_ref[j].astype(o_ref.dtype)


def sb_prompt(u_rest, batch, seq, t=256, hb=4):
    t = min(t, seq)
    nq = seq // t
    bw = hb * HEAD_DIM
    cq, ck, cv = R_SB_Q // bw, R_SB_K // bw, R_SB_V // bw
    assert SB_HEADS % hb == 0
    return pl.pallas_call(
        functools.partial(_sb_prompt_kernel, t=t, hb=hb),
        grid=(batch, SB_HEADS // hb, nq),
        in_specs=[pl.BlockSpec((t, bw), lambda b, h, i: (b * nq + i, cq + h)),
                  pl.BlockSpec((seq, bw), lambda b, h, i: (b, ck + h)),
                  pl.BlockSpec((seq, bw), lambda b, h, i: (b, cv + h))],
        out_specs=pl.BlockSpec((t, bw), lambda b, h, i: (b * nq + i, h)),
        out_shape=jax.ShapeDtypeStruct((batch * seq, SB_W), BF16),
        scratch_shapes=[pltpu.VMEM((hb, t, LANES), F32), pltpu.VMEM((hb, t, LANES), F32)],
        compiler_params=_cp(3),
        name="sb_prompt",
    )(u_rest, u_rest, u_rest)


def _hgrn_chunk(qr, fr, v, gr, lbp, norm_w, st, c, sub, c_real):
    log_lb, log_1m_lb, one_m_lb = lbp[0:1], lbp[1:2], lbp[2:3]
    q = _silu(qr)
    k = one_m_lb * _sigmoid(-fr)
    bb = log_1m_lb + _log_sigmoid(fr)
    mx = jnp.maximum(log_lb, bb)
    logf = mx + jnp.log1p(jnp.exp(-jnp.abs(log_lb - bb)))
    row = lax.broadcasted_iota(jnp.int32, (c, c), 0)
    col = lax.broadcasted_iota(jnp.int32, (c, c), 1)
    tri = jnp.where(row >= col, 1.0, 0.0).astype(BF16)
    hi, mid, lo = _split3(logf)
    b = _dot(tri, hi) + _dot(tri, mid) + _dot(tri, lo)
    o = _dot_nt((q * jnp.exp(b)).astype(BF16), st.astype(BF16))
    ridx = lax.broadcasted_iota(jnp.int32, (c, HEAD_DIM), 0)
    lane = lax.broadcasted_iota(jnp.int32, (sub, c), 1)
    att_rows = []
    for i in range(c // sub):
        r0 = i * sub
        b_i = b[r0:r0 + sub]
        q_i = q[r0:r0 + sub]
        att_i = jnp.zeros((sub, c), F32)
        if i > 0:
            rho = b_i[0:1]
            earlier = ridx < r0
            k_dec = jnp.where(earlier, k * jnp.exp(jnp.where(earlier, rho - b, 0.0)), 0.0)
            att_i = _dot_nt((q_i * jnp.exp(b_i - rho)).astype(BF16), k_dec.astype(BF16))
        trow = lax.broadcasted_iota(jnp.int32, (sub, 1), 0)
        for s in range(sub):
            d = q_i * jnp.exp(jnp.where(trow >= s, b_i - b_i[s:s + 1], 0.0)) * k[r0 + s:r0 + s + 1]
            colsum = jnp.sum(d, axis=1, keepdims=True)
            att_i = att_i + jnp.where((lane == r0 + s) & (trow >= s), colsum, 0.0)
        att_rows.append(att_i)
    att = att_rows[0] if len(att_rows) == 1 else jnp.concatenate(att_rows, axis=0)
    o = o + _dot(att.astype(BF16), v.astype(BF16))
    b_end = b[c_real - 1:c_real]
    real = ridx < c_real
    k_end = jnp.where(real, k * jnp.exp(jnp.where(real, b_end - b, 0.0)), 0.0)
    st = st * jnp.exp(b_end) + _dot_tn(v.astype(BF16), k_end.astype(BF16))
    o = o * lax.rsqrt(jnp.mean(o * o, axis=-1, keepdims=True) + NORM_EPS) * norm_w
    return o * _silu(gr), st


def _hgrn_kernel(q_ref, f_ref, i_ref, g_ref, lbp_ref, nw_ref, s0_ref, o_ref, s_out_ref, st_ref, *,
                 c, sub, n_chunks, hb):
    t = pl.program_id(2)

    @pl.when(t == 0)
    def _():
        for j in range(hb):
            st_ref[j] = s0_ref[j].T

    nw = nw_ref[...]
    in_refs = (q_ref, f_ref, i_ref, g_ref)

    if c < HG_MIN_CHUNK:
        pad = jnp.zeros((HG_MIN_CHUNK - c, HEAD_DIM), F32)
        for j in range(hb):
            cols = slice(j * HEAD_DIM, (j + 1) * HEAD_DIM)
            ins = [jnp.concatenate([r[:, cols], pad], axis=0) for r in in_refs]
            o, st = _hgrn_chunk(*ins, lbp_ref[j], nw, st_ref[j], HG_MIN_CHUNK, HG_SUB, c)
            st_ref[j] = st
            o_ref[:, cols] = o[:c].astype(o_ref.dtype)
    else:
        def body(ci, carry):
            rows = pl.ds(pl.multiple_of(ci * c, c), c)
            for j in range(hb):
                cols = slice(j * HEAD_DIM, (j + 1) * HEAD_DIM)
                o, st = _hgrn_chunk(*[r[rows, cols] for r in in_refs], lbp_ref[j], nw, st_ref[j], c, sub, c)
                st_ref[j] = st
                o_ref[rows, cols] = o.astype(o_ref.dtype)
            return carry

        lax.fori_loop(0, n_chunks, body, 0)

    @pl.when(t == pl.num_programs(2) - 1)
    def _():
        for j in range(hb):
            s_out_ref[j] = st_ref[j].T


def hgrn_lb_params(lb):
    rows = jnp.stack([jnp.log(lb), jnp.log1p(-lb), 1.0 - lb], axis=1)
    return jnp.concatenate([rows, jnp.zeros((lb.shape[0], SUBLANES - 3, lb.shape[1]), F32)], axis=1)


def hgrn(u_rest, row0, batch, seq, lbp, norm_w, s0, tl=512, hb=8):
    c = min(HG_CHUNK, seq)
    sub = min(HG_SUB, c)
    tl = min(tl, seq)
    nt = seq // tl
    rb0 = row0 // tl
    bw = hb * HEAD_DIM
    assert row0 % tl == 0 and seq % tl == 0 and tl % c == 0 and (c % HG_SUB == 0 or nt == 1) and HG_HEADS % hb == 0

    def col(off):
        return pl.BlockSpec((tl, bw), lambda b, h, t, off=off: (rb0 + b * nt + t, off // bw + h))

    o, s_out = pl.pallas_call(
        functools.partial(_hgrn_kernel, c=c, sub=sub, n_chunks=tl // c, hb=hb),
        grid=(batch, HG_HEADS // hb, nt),
        in_specs=[col(R_HG_Q), col(R_HG_F), col(R_HG_I), col(R_HG_G),
                  pl.BlockSpec((hb, SUBLANES, LANES), lambda b, h, t: (h, 0, 0)),
                  pl.BlockSpec((1, LANES), lambda b, h, t: (0, 0)),
                  pl.BlockSpec((None, hb, HEAD_DIM, HEAD_DIM), lambda b, h, t: (b, h, 0, 0))],
        out_specs=[pl.BlockSpec((tl, bw), lambda b, h, t: (b * nt + t, h)),
                   pl.BlockSpec((None, hb, HEAD_DIM, HEAD_DIM), lambda b, h, t: (b, h, 0, 0))],
        out_shape=[jax.ShapeDtypeStruct((batch * seq, HG_W), BF16 if tl % 16 == 0 else F32),
                   jax.ShapeDtypeStruct((batch, HG_HEADS, HEAD_DIM, HEAD_DIM), F32)],
        scratch_shapes=[pltpu.VMEM((hb, HEAD_DIM, HEAD_DIM), F32)],
        compiler_params=_cp(3),
        name="hgrn",
    )(u_rest, u_rest, u_rest, u_rest, lbp, norm_w.reshape(1, LANES), s0)
    return o, s_out


SEG_W = CMP_STRIDE * HEAD_DIM


def compress_params(pe, w1):
    w = jnp.concatenate([w1[:SEG_W], w1[SEG_W:]], axis=1).astype(BF16)
    rows = jnp.stack([pe[:CMP_STRIDE].reshape(SEG_W), pe[CMP_STRIDE:].reshape(SEG_W)])
    return w, jnp.concatenate([rows, jnp.zeros((SUBLANES - 2, SEG_W), F32)]).astype(BF16)


def _segment_products(r, w, pe):
    bias = _dot(pe, w)
    bias = jnp.concatenate([bias[0:1, :LANES], bias[1:2, LANES:]], axis=1)
    return _dot(r.astype(BF16), w) + bias


def _cmp1_prompt_kernel(x_ref, w_ref, pe_ref, o_ref, r_ref, *, n_seg):
    for l in range(CMP_STRIDE):
        r_ref[:, l * LANES:(l + 1) * LANES] = x_ref[pl.ds(l, n_seg, stride=CMP_STRIDE), :]
    o_ref[...] = _segment_products(r_ref[...], w_ref[...], pe_ref[...])


def compress_segments_prompt(u_head, batch, seq, wcat, pecat, tr=512):
    tr = min(tr, seq)
    n_seg = tr // CMP_STRIDE
    nt = seq // tr
    c0 = OFF_CMP // LANES
    return pl.pallas_call(
        functools.partial(_cmp1_prompt_kernel, n_seg=n_seg),
        grid=(batch, 2, NSA_KV_HEADS, nt),
        in_specs=[pl.BlockSpec((tr, LANES), lambda b, kv, g, t: (b * nt + t, c0 + kv * NSA_KV_HEADS + g)),
                  pl.BlockSpec((None, SEG_W, 2 * LANES), lambda b, kv, g, t: (kv, 0, 0)),
                  pl.BlockSpec((None, SUBLANES, SEG_W), lambda b, kv, g, t: (kv, 0, 0))],
        out_specs=pl.BlockSpec((None, None, None, n_seg, 2 * LANES), lambda b, kv, g, t: (b, kv, g, t, 0)),
        out_shape=jax.ShapeDtypeStruct((batch, 2, NSA_KV_HEADS, seq // CMP_STRIDE, 2 * LANES), F32),
        scratch_shapes=[pltpu.VMEM((n_seg, SEG_W), F32)],
        compiler_params=_cp(4),
        name="cmp_segments_prompt",
    )(u_head, wcat, pecat)


def _cmp2_kernel(pq_ref, w2_ref, o_ref, *, nc):
    pq = pq_ref[...]
    q_next = pltpu.roll(pq[:, LANES:], shift=nc - 1, axis=0)
    hid = _silu(pq[:, :LANES] + q_next)
    out = _dot(hid.astype(BF16), w2_ref[...].astype(BF16))
    row = lax.broadcasted_iota(jnp.int32, out.shape, 0)
    o_ref[...] = jnp.where(row < nc - 1, out, 0.0)


def compress_finish(pq, w2):
    batch, _, _, nc, _ = pq.shape
    return pl.pallas_call(
        functools.partial(_cmp2_kernel, nc=nc),
        grid=(batch, 2, NSA_KV_HEADS),
        in_specs=[pl.BlockSpec((None, None, None, nc, 2 * LANES), lambda b, kv, g: (b, kv, g, 0, 0)),
                  pl.BlockSpec((None, HEAD_DIM, HEAD_DIM), lambda b, kv, g: (kv, 0, 0))],
        out_specs=pl.BlockSpec((None, None, None, nc, LANES), lambda b, kv, g: (b, kv, g, 0, 0)),
        out_shape=jax.ShapeDtypeStruct((batch, 2, NSA_KV_HEADS, nc, LANES), F32),
        compiler_params=_cp(3),
        name="cmp_finish",
    )(pq, w2)


def _cmp_select_kernel(q_ref, kc_ref, vc_ref, o_ref, sel_ref, score_ref, *, tq, tqp, nc, n_cmp, n_slc, nsp, pos0):
    qi = pl.program_id(2)
    qb = q_ref[...]
    parts = []
    for h in range(NSA_GROUP):
        qh = qb[:, h * HEAD_DIM:(h + 1) * HEAD_DIM]
        if tqp > tq:
            qh = jnp.concatenate([qh, jnp.zeros((tqp - tq, HEAD_DIM), F32)], axis=0)
        parts.append(qh)
    q4 = jnp.concatenate(parts, axis=0).astype(BF16)
    rows = NSA_GROUP * tqp
    st = _dot_nt(kc_ref[...].astype(BF16), q4) * SCALE
    ci = lax.broadcasted_iota(jnp.int32, (nc, rows), 0)
    tok = lax.broadcasted_iota(jnp.int32, (nc, rows), 1) & (tqp - 1)
    tpos = pos0 + qi * tq + tok
    valid = (ci < n_cmp) & (ci * CMP_STRIDE + CMP_LEN - 1 <= tpos)
    st = jnp.where(valid, st, NEG_BIG)
    m = jnp.max(st, axis=0, keepdims=True)
    e = jnp.where(valid, jnp.exp(st - m), 0.0)
    den = jnp.sum(e, axis=0, keepdims=True)
    pt = e / jnp.where(den > 0, den, 1.0)
    o = _dot_tn(pt.astype(BF16), vc_ref[...].astype(BF16))
    for h in range(NSA_GROUP):
        o_ref[:, h * HEAD_DIM:(h + 1) * HEAD_DIM] = o[h * tqp:h * tqp + tq]
    psum = pt[:, 0:tqp]
    for h in range(1, NSA_GROUP):
        psum = psum + pt[:, h * tqp:(h + 1) * tqp]
    nsr = score_ref.shape[0]
    jj = lax.broadcasted_iota(jnp.int32, (nsr, nc), 0)
    ii = lax.broadcasted_iota(jnp.int32, (nsr, nc), 1)
    cover = ((ii * CMP_STRIDE < jj * SEL_BLOCK + SEL_BLOCK) & (ii * CMP_STRIDE + CMP_LEN - 1 >= jj * SEL_BLOCK)
             & (ii < n_cmp))
    cover = jnp.where(cover, 1.0, 0.0).astype(BF16)
    hi, mid, lo = _split3(psum)
    imp = _dot(cover, hi) + _dot(cover, mid) + _dot(cover, lo)
    j = lax.broadcasted_iota(jnp.int32, (nsr, tqp), 0)
    tpos2 = pos0 + qi * tq + lax.broadcasted_iota(jnp.int32, (nsr, tqp), 1)
    cur = tpos2 // SEL_BLOCK
    forced = (j == 0) | (j == cur) | (j == cur - 1)
    ok = (j * SEL_BLOCK <= tpos2) & (j < n_slc)
    score = jnp.where(ok, jnp.where(forced, NSA_GROUP + 1.0, imp), -1.0)
    score_ref[...] = score

    def body(jp, rank):
        other = score_ref[pl.ds(jp, 1), :]
        ahead = jnp.where(other > score, 1.0, jnp.where((other == score) & (jp < j), 1.0, 0.0))
        return rank + ahead

    n_valid = jnp.minimum(n_slc, (pos0 + qi * tq + tq - 1) // SEL_BLOCK + 1)
    rank = lax.fori_loop(0, n_valid, body, jnp.zeros((nsr, tqp), F32))
    sel_t = jnp.where((rank < SEL_TOPK) & (score >= 0), 1.0, 0.0)
    if nsp > nsr:
        sel_t = jnp.concatenate([sel_t, jnp.zeros((nsp - nsr, tqp), F32)], axis=0)
    sel_ref[...] = sel_t.T[:tq]


def cmp_select(u_head, row0, batch, seq, kv_cmp, n_cmp, n_slc, pos0, tq=256):
    tq = min(tq, seq)
    tqp = max(tq, LANES)
    nt = seq // tq
    rb0 = row0 // tq
    nc = kv_cmp.shape[3]
    nsp = -(-n_slc // LANES) * LANES
    gw = NSA_GROUP * HEAD_DIM
    assert row0 % tq == 0 and seq % tq == 0 and tqp & (tqp - 1) == 0
    return pl.pallas_call(
        functools.partial(_cmp_select_kernel, tq=tq, tqp=tqp, nc=nc, n_cmp=n_cmp, n_slc=n_slc, nsp=nsp, pos0=pos0),
        grid=(batch, NSA_KV_HEADS, nt),
        in_specs=[pl.BlockSpec((tq, gw), lambda b, g, t: (rb0 + b * nt + t, g)),
                  pl.BlockSpec((None, None, None, nc, LANES), lambda b, g, t: (b, 0, g, 0, 0)),
                  pl.BlockSpec((None, None, None, nc, LANES), lambda b, g, t: (b, 1, g, 0, 0))],
        out_specs=[pl.BlockSpec((tq, gw), lambda b, g, t: (b * nt + t, g)),
                   pl.BlockSpec((None, None, tq, nsp), lambda b, g, t: (b, g, t, 0))],
        out_shape=[jax.ShapeDtypeStruct((batch * seq, NSA_W), F32),
                   jax.ShapeDtypeStruct((batch, NSA_KV_HEADS, seq, nsp), F32)],
        scratch_shapes=[pltpu.VMEM((-(-n_slc // 16) * 16, tqp), F32)],
        compiler_params=_cp(3),
        name="cmp_select",
    )(u_head, kv_cmp, kv_cmp)


def _flash_step(s, mask, v, m_ref, l_ref, acc_ref):
    s = jnp.where(mask, s, NEG_BIG)
    m_prev = m_ref[...]
    m_new = jnp.maximum(m_prev, jnp.max(s, axis=1, keepdims=True))
    e = jnp.where(mask, jnp.exp(s - m_new), 0.0)
    alpha = jnp.exp(m_prev - m_new)
    l_ref[...] = alpha * l_ref[...] + jnp.sum(e, axis=1, keepdims=True)
    acc_ref[...] = alpha * acc_ref[...] + _dot(e.astype(BF16), v)
    m_ref[...] = m_new


def _flash_init(m_ref, l_ref, acc_ref):
    m_ref[...] = jnp.full_like(m_ref, NEG_BIG)
    l_ref[...] = jnp.zeros_like(l_ref)
    acc_ref[...] = jnp.zeros_like(acc_ref)


def _flash_result(l_ref, acc_ref):
    l = l_ref[...]
    return acc_ref[...] / jnp.where(l > 0, l, 1.0)


def _stack_heads(qb, pad_to=None):
    parts = []
    for h in range(NSA_GROUP):
        qh = qb[:, h * HEAD_DIM:(h + 1) * HEAD_DIM]
        if pad_to is not None and pad_to > qh.shape[0]:
            qh = jnp.concatenate([qh, jnp.zeros((pad_to - qh.shape[0], HEAD_DIM), qh.dtype)], axis=0)
        parts.append(qh)
    return jnp.concatenate(parts, axis=0)


def _flash_prompt_kernel(qi_tab, kb_tab, first_tab, last_tab, *refs, tq, tk, mode, nsp):
    if mode == "sel":
        q_ref, k_ref, v_ref, sel_ref, o_ref, m_ref, l_ref, acc_ref = refs
    else:
        q_ref, k_ref, v_ref, o_ref, m_ref, l_ref, acc_ref = refs
    p = pl.program_id(2)
    qi = qi_tab[p]
    kb = kb_tab[p]

    @pl.when(first_tab[p] == 1)
    def _():
        _flash_init(m_ref, l_ref, acc_ref)

    q4 = _stack_heads(q_ref[...]).astype(BF16)
    s = _dot_nt(q4, k_ref[...].astype(BF16)) * SCALE
    qpos = qi * tq + lax.broadcasted_iota(jnp.int32, (tq, tk), 0)
    kpos = kb * tk + lax.broadcasted_iota(jnp.int32, (tq, tk), 1)
    if mode == "win":
        d = qpos - kpos
        mask = (d >= 0) & (d < WINDOW)
    else:
        jj = lax.broadcasted_iota(jnp.int32, (nsp, tk), 0)
        kk = lax.broadcasted_iota(jnp.int32, (nsp, tk), 1)
        expand = jnp.where(jj == kb * (tk // SEL_BLOCK) + kk // SEL_BLOCK, 1.0, 0.0).astype(BF16)
        chosen = _dot(sel_ref[...].astype(BF16), expand)
        mask = (chosen > 0.5) & (kpos <= qpos)
    mask4 = jnp.concatenate([mask] * NSA_GROUP, axis=0)
    _flash_step(s, mask4, v_ref[...].astype(BF16), m_ref, l_ref, acc_ref)

    @pl.when(last_tab[p] == 1)
    def _():
        o = _flash_result(l_ref, acc_ref)
        for h in range(NSA_GROUP):
            o_ref[:, h * HEAD_DIM:(h + 1) * HEAD_DIM] = o[h * tq:(h + 1) * tq]


def flash_prompt(u_head, batch, seq, mode, sel=None, tq=256, tk=None):
    tq = min(tq, seq)
    tk = min(tk or (256 if mode == "win" else 512), seq)
    nq, nk = seq // tq, seq // tk
    pairs = []
    for qi in range(nq):
        if mode == "win":
            lo = max(0, (qi * tq - WINDOW + 1) // tk)
        else:
            lo = 0
        hi = (qi * tq + tq - 1) // tk
        kbs = list(range(lo, hi + 1))
        pairs += [(qi, kb, int(kb == kbs[0]), int(kb == kbs[-1])) for kb in kbs]
    tabs = [jnp.asarray([p[i] for p in pairs], jnp.int32) for i in range(4)]
    off = OFF_WIN if mode == "win" else OFF_SEL
    ck = off // LANES
    cv = ck + NSA_KV_HEADS
    gw = NSA_GROUP * HEAD_DIM
    in_specs = [
        pl.BlockSpec((tq, gw), lambda b, g, p, qt, kt, ft, lt: (b * nq + qt[p], g)),
        pl.BlockSpec((tk, LANES), lambda b, g, p, qt, kt, ft, lt: (b * nk + kt[p], ck + g)),
        pl.BlockSpec((tk, LANES), lambda b, g, p, qt, kt, ft, lt: (b * nk + kt[p], cv + g)),
    ]
    args = [u_head, u_head, u_head]
    nsp = 0
    if mode == "sel":
        nsp = sel.shape[-1]
        in_specs.append(pl.BlockSpec((None, None, tq, nsp), lambda b, g, p, qt, kt, ft, lt: (b, g, qt[p], 0)))
        args.append(sel)
    rows = NSA_GROUP * tq
    grid_spec = pltpu.PrefetchScalarGridSpec(
        num_scalar_prefetch=4,
        grid=(batch, NSA_KV_HEADS, len(pairs)),
        in_specs=in_specs,
        out_specs=pl.BlockSpec((tq, gw), lambda b, g, p, qt, kt, ft, lt: (b * nq + qt[p], g)),
        scratch_shapes=[pltpu.VMEM((rows, 1), F32), pltpu.VMEM((rows, 1), F32), pltpu.VMEM((rows, HEAD_DIM), F32)],
    )
    return pl.pallas_call(
        functools.partial(_flash_prompt_kernel, tq=tq, tk=tk, mode=mode, nsp=nsp),
        grid_spec=grid_spec,
        out_shape=jax.ShapeDtypeStruct((batch * seq, NSA_W), F32),
        compiler_params=_cp(3),
        name="flash_" + mode,
    )(*tabs, *args)


def _sel_prompt_kernel(q_ref, k_ref, v_ref, sel_ref, o_ref, m_ref, l_ref, acc_ref, *, tq, tk, nsp):
    qi = pl.program_id(2)
    q4 = (_stack_heads(q_ref[...]) * SCALE).astype(BF16)
    selb = sel_ref[...].astype(BF16)
    _flash_init(m_ref, l_ref, acc_ref)
    row = lax.broadcasted_iota(jnp.int32, (tq, tk), 0)
    col = lax.broadcasted_iota(jnp.int32, (tq, tk), 1)
    jj = lax.broadcasted_iota(jnp.int32, (nsp, tk), 0)
    kk = lax.broadcasted_iota(jnp.int32, (nsp, tk), 1) // SEL_BLOCK

    def body(kb, c):
        keys = pl.ds(pl.multiple_of(kb * tk, tk), tk)
        s = _dot_nt(q4, k_ref[keys, :].astype(BF16))
        expand = jnp.where(jj == kb * (tk // SEL_BLOCK) + kk, 1.0, 0.0).astype(BF16)
        chosen = _dot(selb, expand)
        ok = (chosen > 0.5) & (kb * tk + col <= qi * tq + row)
        bias = jnp.where(ok, 0.0, 2.0 * NEG_BIG)
        s = (s.reshape(NSA_GROUP, tq, tk) + bias[None]).reshape(NSA_GROUP * tq, tk)
        m_prev = m_ref[...]
        m_new = jnp.maximum(m_prev, jnp.max(s, axis=1, keepdims=True))
        e = jnp.exp(s - m_new)
        alpha = jnp.exp(m_prev - m_new)
        l_ref[...] = alpha * l_ref[...] + jnp.sum(e, axis=1, keepdims=True)
        acc_ref[...] = alpha * acc_ref[...] + _dot(e.astype(BF16), v_ref[keys, :].astype(BF16))
        m_ref[...] = m_new
        return c

    lax.fori_loop(0, (qi * tq + tq - 1) // tk + 1, body, 0)
    o = _flash_result(l_ref, acc_ref)
    for h in range(NSA_GROUP):
        o_ref[:, h * HEAD_DIM:(h + 1) * HEAD_DIM] = o[h * tq:(h + 1) * tq]


def sel_prompt(u_head, batch, seq, sel, tq=256, tk=512):
    tq = min(tq, seq)
    tk = min(tk, seq)
    nq = seq // tq
    nsp = sel.shape[-1]
    ck = OFF_SEL // LANES
    cv = ck + NSA_KV_HEADS
    gw = NSA_GROUP * HEAD_DIM
    rows = NSA_GROUP * tq
    assert seq % tq == 0 and seq % tk == 0 and tk % SEL_BLOCK == 0
    return pl.pallas_call(
        functools.partial(_sel_prompt_kernel, tq=tq, tk=tk, nsp=nsp),
        grid=(batch, NSA_KV_HEADS, nq),
        in_specs=[pl.BlockSpec((tq, gw), lambda b, g, i: (b * nq + i, g)),
                  pl.BlockSpec((seq, LANES), lambda b, g, i: (b, ck + g)),
                  pl.BlockSpec((seq, LANES), lambda b, g, i: (b, cv + g)),
                  pl.BlockSpec((None, None, tq, nsp), lambda b, g, i: (b, g, i, 0))],
        out_specs=pl.BlockSpec((tq, gw), lambda b, g, i: (b * nq + i, g)),
        out_shape=jax.ShapeDtypeStruct((batch * seq, NSA_W), F32),
        scratch_shapes=[pltpu.VMEM((rows, 1), F32), pltpu.VMEM((rows, 1), F32), pltpu.VMEM((rows, HEAD_DIM), F32)],
        compiler_params=_cp(3),
        name="sel_prompt",
    )(u_head, u_head, u_head, sel)


def _win_prompt_kernel(q_ref, k_ref, v_ref, o_ref, *, tq, nk, seq):
    q0 = pl.program_id(2) * tq
    start = pl.multiple_of(jnp.clip(q0 - WINDOW, 0, seq - nk), tq)
    keys = pl.ds(start, nk)
    q4 = (_stack_heads(q_ref[...]) * SCALE).astype(BF16)
    s = _dot_nt(q4, k_ref[keys, :].astype(BF16))
    d = (q0 + lax.broadcasted_iota(jnp.int32, (tq, nk), 0)) - (start + lax.broadcasted_iota(jnp.int32, (tq, nk), 1))
    bias = jnp.where((d >= 0) & (d < WINDOW), 0.0, NEG_BIG)
    s = s + jnp.concatenate([bias] * NSA_GROUP, axis=0)
    e = jnp.exp(s - jnp.max(s, axis=1, keepdims=True))
    o = _dot(e.astype(BF16), v_ref[keys, :].astype(BF16)) / jnp.sum(e, axis=1, keepdims=True)
    for h in range(NSA_GROUP):
        o_ref[:, h * HEAD_DIM:(h + 1) * HEAD_DIM] = o[h * tq:(h + 1) * tq]


def win_prompt(u_head, batch, seq, tq=256):
    tq = min(tq, seq)
    nk = min(seq, WINDOW + tq)
    nq = seq // tq
    ck = OFF_WIN // LANES
    cv = ck + NSA_KV_HEADS
    gw = NSA_GROUP * HEAD_DIM
    assert seq % tq == 0 and WINDOW % tq == 0
    return pl.pallas_call(
        functools.partial(_win_prompt_kernel, tq=tq, nk=nk, seq=seq),
        grid=(batch, NSA_KV_HEADS, nq),
        in_specs=[pl.BlockSpec((tq, gw), lambda b, g, i: (b * nq + i, g)),
                  pl.BlockSpec((seq, LANES), lambda b, g, i: (b, ck + g)),
                  pl.BlockSpec((seq, LANES), lambda b, g, i: (b, cv + g))],
        out_specs=pl.BlockSpec((tq, gw), lambda b, g, i: (b * nq + i, g)),
        out_shape=jax.ShapeDtypeStruct((batch * seq, NSA_W), F32),
        compiler_params=_cp(3),
        name="win_prompt",
    )(u_head, u_head, u_head)


def _nsa_combine_kernel(gate_ref, a_ref, b_ref, c_ref, o_ref):
    gate = _sigmoid(gate_ref[...])
    for h in range(NSA_HEADS):
        sl = slice(h * HEAD_DIM, (h + 1) * HEAD_DIM)
        out = (gate[:, h:h + 1] * a_ref[:, sl] + gate[:, NSA_HEADS + h:NSA_HEADS + h + 1] * b_ref[:, sl]
               + gate[:, 2 * NSA_HEADS + h:2 * NSA_HEADS + h + 1] * c_ref[:, sl])
        o_ref[:, sl] = out.astype(o_ref.dtype)


def nsa_combine(u_head, row0, o_cmp, o_sel, o_win, tm=256):
    n = o_cmp.shape[0]
    tm = min(tm, n)
    rb0 = row0 // tm
    assert row0 % tm == 0 and n % tm == 0
    spec = pl.BlockSpec((tm, NSA_W), lambda i: (i, 0))
    return pl.pallas_call(
        _nsa_combine_kernel,
        grid=(n // tm,),
        in_specs=[pl.BlockSpec((tm, LANES), lambda i: (rb0 + i, OFF_NSA_GATE // LANES)), spec, spec, spec],
        out_specs=spec,
        out_shape=jax.ShapeDtypeStruct((n, NSA_W), BF16 if tm % 16 == 0 else F32),
        compiler_params=_cp(1),
        name="nsa_combine",
    )(u_head, o_cmp, o_sel, o_win)


def _page_rows(page_ref, first, n_rows, stride):
    return page_ref[pl.ds(first, n_rows, stride=stride), :]


def _pad_rows(x, n):
    if x.shape[0] >= n:
        return x
    return jnp.concatenate([x, jnp.zeros((n - x.shape[0], x.shape[1]), x.dtype)], axis=0)


def _cmp1_sample_kernel(pt_ref, *refs, pps, page):
    page_refs = refs[:pps]
    w_ref, pe_ref, o_ref, r_ref = refs[pps:]
    segs = page // CMP_STRIDE
    stride = CMP_STRIDE * 2 * NSA_KV_HEADS
    for kv in range(2):
        for g in range(NSA_KV_HEADS):
            for k in range(pps):
                for l in range(CMP_STRIDE):
                    r_ref[k * segs:(k + 1) * segs, l * LANES:(l + 1) * LANES] = _page_rows(
                        page_refs[k], l * 2 * NSA_KV_HEADS + kv * NSA_KV_HEADS + g, segs, stride)
            o_ref[kv, g] = _segment_products(r_ref[...], w_ref[kv], pe_ref[kv])


def compress_segments_sample(pool, layer, pt_flat, batch, n_pages, wcat, pecat, pps=16):
    page = pool.shape[2] // (2 * NSA_KV_HEADS)
    pps = min(pps, n_pages)
    assert n_pages % pps == 0 and page % CMP_STRIDE == 0
    segs = page // CMP_STRIDE
    in_specs = [pl.BlockSpec((None, None, pool.shape[2], LANES),
                             lambda b, s, pt, k=k: (layer, pt[b * n_pages + s * pps + k], 0, 0)) for k in range(pps)]
    in_specs += [pl.BlockSpec((2, SEG_W, 2 * LANES), lambda b, s, pt: (0, 0, 0)),
                 pl.BlockSpec((2, SUBLANES, SEG_W), lambda b, s, pt: (0, 0, 0))]
    grid_spec = pltpu.PrefetchScalarGridSpec(
        num_scalar_prefetch=1,
        grid=(batch, n_pages // pps),
        in_specs=in_specs,
        out_specs=pl.BlockSpec((None, 2, NSA_KV_HEADS, pps * segs, 2 * LANES), lambda b, s, pt: (b, 0, 0, s, 0)),
        scratch_shapes=[pltpu.VMEM((pps * segs, SEG_W), F32)],
    )
    return pl.pallas_call(
        functools.partial(_cmp1_sample_kernel, pps=pps, page=page),
        grid_spec=grid_spec,
        out_shape=jax.ShapeDtypeStruct((batch, 2, NSA_KV_HEADS, n_pages * segs, 2 * LANES), F32),
        compiler_params=_cp(2),
        name="cmp_segments_sample",
    )(pt_flat, *([pool] * pps), wcat, pecat)


def _sel_sample_kernel(pt_ref, *refs, pps, page, ds, past, nsp):
    q_ref, new_ref, sel_ref, selstep_ref = refs[:4]
    page_refs = refs[4:4 + pps]
    o_ref, m_ref, l_ref, acc_ref = refs[4 + pps:]
    s = pl.program_id(1)
    rows = NSA_GROUP * ds
    keys = pps * page
    bps = keys // SEL_BLOCK
    stride = 2 * NSA_KV_HEADS

    @pl.when(s == 0)
    def _():
        _flash_init(m_ref, l_ref, acc_ref)

    jj = lax.broadcasted_iota(jnp.int32, (bps, keys), 0)
    kk = lax.broadcasted_iota(jnp.int32, (bps, keys), 1)
    expand = jnp.where(jj == kk // SEL_BLOCK, 1.0, 0.0).astype(BF16)
    for g in range(NSA_KV_HEADS):
        q4 = _stack_heads(q_ref[:, g * NSA_GROUP * HEAD_DIM:(g + 1) * NSA_GROUP * HEAD_DIM]).astype(BF16)
        sel4 = jnp.concatenate([sel_ref[g]] * NSA_GROUP, axis=0)
        sel_here = jnp.concatenate([selstep_ref[g]] * NSA_GROUP, axis=0)
        k_all = jnp.concatenate([_page_rows(r, g, page, stride) for r in page_refs], axis=0).astype(BF16)
        v_all = jnp.concatenate([_page_rows(r, NSA_KV_HEADS + g, page, stride) for r in page_refs],
                                axis=0).astype(BF16)
        sc = _dot_nt(q4, k_all) * SCALE
        chosen = _dot(sel_here.astype(BF16), expand)
        _flash_step(sc, chosen > 0.5, v_all, m_ref.at[g], l_ref.at[g], acc_ref.at[g])

        @pl.when(s == pl.num_programs(1) - 1)
        def _():
            k_new = _pad_rows(new_ref[:, g * HEAD_DIM:(g + 1) * HEAD_DIM], LANES).astype(BF16)
            v_new = _pad_rows(new_ref[:, (NSA_KV_HEADS + g) * HEAD_DIM:(NSA_KV_HEADS + g + 1) * HEAD_DIM],
                              LANES).astype(BF16)
            sn = _dot_nt(q4, k_new) * SCALE
            blk = past // SEL_BLOCK
            j = lax.broadcasted_iota(jnp.int32, (rows, LANES), 1)
            t = lax.broadcasted_iota(jnp.int32, (rows, LANES), 0) % ds
            mask = (sel4[:, blk:blk + 1] > 0.5) & (j <= t)
            _flash_step(sn, mask, v_new, m_ref.at[g], l_ref.at[g], acc_ref.at[g])
            o = _flash_result(l_ref.at[g], acc_ref.at[g])
            for h in range(NSA_GROUP):
                c0 = (g * NSA_GROUP + h) * HEAD_DIM
                o_ref[:, c0:c0 + HEAD_DIM] = o[h * ds:(h + 1) * ds]


def sel_sample(u_head, row0, pool, layer, pt_flat, sel, batch, ds, n_pages, pps=32):
    page = pool.shape[2] // (2 * NSA_KV_HEADS)
    pps = min(pps, n_pages)
    past = n_pages * page
    nsp = sel.shape[-1]
    rows = NSA_GROUP * ds
    ns = n_pages // pps
    bps = pps * page // SEL_BLOCK
    assert n_pages % pps == 0 and row0 % ds == 0 and past % SEL_BLOCK == 0 and ds <= SEL_BLOCK
    assert page % SEL_BLOCK == 0
    sel_steps = sel[..., :ns * bps].reshape(batch, NSA_KV_HEADS, ds, ns, bps).transpose(0, 1, 3, 2, 4)
    in_specs = [pl.BlockSpec((ds, NSA_W), lambda b, s, pt: (row0 // ds + b, 0)),
                pl.BlockSpec((ds, KV_W), lambda b, s, pt: (row0 // ds + b, OFF_SEL // KV_W)),
                pl.BlockSpec((None, NSA_KV_HEADS, ds, nsp), lambda b, s, pt: (b, 0, 0, 0)),
                pl.BlockSpec((None, NSA_KV_HEADS, None, ds, bps), lambda b, s, pt: (b, 0, s, 0, 0))]
    in_specs += [pl.BlockSpec((None, None, pool.shape[2], LANES),
                              lambda b, s, pt, k=k: (layer, pt[b * n_pages + s * pps + k], 0, 0)) for k in range(pps)]
    grid_spec = pltpu.PrefetchScalarGridSpec(
        num_scalar_prefetch=1,
        grid=(batch, n_pages // pps),
        in_specs=in_specs,
        out_specs=pl.BlockSpec((ds, NSA_W), lambda b, s, pt: (b, 0)),
        scratch_shapes=[pltpu.VMEM((NSA_KV_HEADS, rows, 1), F32), pltpu.VMEM((NSA_KV_HEADS, rows, 1), F32),
                        pltpu.VMEM((NSA_KV_HEADS, rows, HEAD_DIM), F32)],
    )
    return pl.pallas_call(
        functools.partial(_sel_sample_kernel, pps=pps, page=page, ds=ds, past=past, nsp=nsp),
        grid_spec=grid_spec,
        out_shape=jax.ShapeDtypeStruct((batch * ds, NSA_W), F32),
        compiler_params=_cp(2),
        name="sel_sample",
    )(pt_flat, u_head, u_head, sel, sel_steps, *([pool] * pps))


def _win_sample_kernel(q_ref, new_ref, buf_ref, o_ref, m_ref, l_ref, acc_ref, *, ds, nbuf):
    rows = NSA_GROUP * ds
    stride = 2 * NSA_KV_HEADS
    keys = nbuf + LANES
    i = lax.broadcasted_iota(jnp.int32, (rows, keys), 1)
    t = lax.broadcasted_iota(jnp.int32, (rows, keys), 0) % ds
    d = jnp.where(i < nbuf, t + nbuf - i, t - (i - nbuf))
    mask = (d >= 0) & (d < WINDOW) & (i < nbuf + ds)
    for g in range(NSA_KV_HEADS):
        _flash_init(m_ref, l_ref, acc_ref)
        q4 = _stack_heads(q_ref[:, g * NSA_GROUP * HEAD_DIM:(g + 1) * NSA_GROUP * HEAD_DIM]).astype(BF16)
        k_new = _pad_rows(new_ref[:, g * HEAD_DIM:(g + 1) * HEAD_DIM], LANES)
        v_new = _pad_rows(new_ref[:, (NSA_KV_HEADS + g) * HEAD_DIM:(NSA_KV_HEADS + g + 1) * HEAD_DIM], LANES)
        k_all = jnp.concatenate([_page_rows(buf_ref, g, nbuf, stride), k_new], axis=0).astype(BF16)
        v_all = jnp.concatenate([_page_rows(buf_ref, NSA_KV_HEADS + g, nbuf, stride), v_new], axis=0).astype(BF16)
        _flash_step(_dot_nt(q4, k_all) * SCALE, mask, v_all, m_ref, l_ref, acc_ref)
        o = _flash_result(l_ref, acc_ref)
        for h in range(NSA_GROUP):
            c0 = (g * NSA_GROUP + h) * HEAD_DIM
            o_ref[:, c0:c0 + HEAD_DIM] = o[h * ds:(h + 1) * ds]


def win_sample(u_head, row0, buf, layer, batch, ds):
    nbuf = buf.shape[2] // (2 * NSA_KV_HEADS)
    rows = NSA_GROUP * ds
    return pl.pallas_call(
        functools.partial(_win_sample_kernel, ds=ds, nbuf=nbuf),
        grid=(batch,),
        in_specs=[pl.BlockSpec((ds, NSA_W), lambda b: (row0 // ds + b, 0)),
                  pl.BlockSpec((ds, KV_W), lambda b: (row0 // ds + b, OFF_WIN // KV_W)),
                  pl.BlockSpec((None, None, buf.shape[2], LANES), lambda b: (layer, b, 0, 0))],
        out_specs=pl.BlockSpec((ds, NSA_W), lambda b: (b, 0)),
        out_shape=jax.ShapeDtypeStruct((batch * ds, NSA_W), F32),
        scratch_shapes=[pltpu.VMEM((rows, 1), F32), pltpu.VMEM((rows, 1), F32), pltpu.VMEM((rows, HEAD_DIM), F32)],
        compiler_params=_cp(1),
        name="win_sample",
    )(u_head, u_head, buf)


SB_ROWS = 16


def _sb_sample_kernel(pt_ref, q_ref, kn_ref, vn_ref, pool_hbm, o_ref, acc_ref, carry_ref, kv_ref, sem, *,
                      pps, page, ds, n_pages, layer):
    b = pl.program_id(0)
    stride = 2 * SB_HEADS
    umat = _suffix_matrix()
    q_heads = [_pad_rows(q_ref[:, h * HEAD_DIM:(h + 1) * HEAD_DIM] * SCALE, SB_ROWS).astype(BF16)
               for h in range(SB_HEADS)]
    rows = SB_HEADS * SB_ROWS
    real_row = lax.broadcasted_iota(jnp.int32, (rows, LANES), 0) % SB_ROWS < ds

    def accumulate(k_heads, v_heads, mask):
        z = jnp.concatenate([_dot_nt(q_heads[h], k_heads[h]) for h in range(SB_HEADS)], axis=0)
        a, carry = _sb_weights(z, mask, carry_ref[...], umat)
        carry_ref[...] = carry
        a = a.astype(BF16)
        for h in range(SB_HEADS):
            sl = slice(h * SB_ROWS, (h + 1) * SB_ROWS)
            acc_ref[sl, :] += _dot(a[sl], v_heads[h])
        return jnp.max(jnp.where(real_row, carry, NEG_BIG))

    acc_ref[...] = jnp.zeros_like(acc_ref)
    carry_ref[...] = jnp.zeros_like(carry_ref)
    k_new = [_pad_rows(kn_ref[:, h * HEAD_DIM:(h + 1) * HEAD_DIM], LANES).astype(BF16) for h in range(SB_HEADS)]
    v_new = [_pad_rows(vn_ref[:, h * HEAD_DIM:(h + 1) * HEAD_DIM], LANES).astype(BF16) for h in range(SB_HEADS)]
    j = lax.broadcasted_iota(jnp.int32, (rows, LANES), 1)
    t = lax.broadcasted_iota(jnp.int32, (rows, LANES), 0) % SB_ROWS
    top = accumulate(k_new, v_new, j < t)

    def page_copy(g, k):
        pid = pt_ref[b * n_pages + n_pages - (g + 1) * pps + k]
        return pltpu.make_async_copy(pool_hbm.at[layer, pid], kv_ref.at[k], sem)

    def cond(st):
        return jnp.logical_and(st[0] < n_pages // pps, st[1] > SB_ZERO_LOG)

    def body(st):
        g = st[0]
        for k in range(pps):
            page_copy(g, k).start()
        for k in range(pps):
            page_copy(g, k).wait()
        k_heads = [jnp.concatenate([_page_rows(kv_ref.at[k], h, page, stride) for k in range(pps)],
                                   axis=0).astype(BF16) for h in range(SB_HEADS)]
        v_heads = [jnp.concatenate([_page_rows(kv_ref.at[k], SB_HEADS + h, page, stride) for k in range(pps)],
                                   axis=0).astype(BF16) for h in range(SB_HEADS)]
        return g + 1, accumulate(k_heads, v_heads, None)

    lax.while_loop(cond, body, (0, top))
    for h in range(SB_HEADS):
        o_ref[:, h * HEAD_DIM:(h + 1) * HEAD_DIM] = acc_ref[h * SB_ROWS:h * SB_ROWS + ds, :]


def sb_sample(u_rest, row0, pool, layer, pt_flat, batch, ds, n_pages, pps=2):
    page = pool.shape[2] // (2 * SB_HEADS)
    pps = min(pps, n_pages)
    assert n_pages % pps == 0 and row0 % ds == 0 and ds <= SB_ROWS
    rb = row0 // ds
    grid_spec = pltpu.PrefetchScalarGridSpec(
        num_scalar_prefetch=1,
        grid=(batch,),
        in_specs=[pl.BlockSpec((ds, SB_W), lambda b, pt: (rb + b, R_SB_Q // SB_W)),
                  pl.BlockSpec((ds, SB_W), lambda b, pt: (rb + b, R_SB_K // SB_W)),
                  pl.BlockSpec((ds, SB_W), lambda b, pt: (rb + b, R_SB_V // SB_W)),
                  pl.BlockSpec(memory_space=pl.ANY)],
        out_specs=pl.BlockSpec((ds, SB_W), lambda b, pt: (b, 0)),
        scratch_shapes=[pltpu.VMEM((SB_HEADS * SB_ROWS, HEAD_DIM), F32), pltpu.VMEM((SB_HEADS * SB_ROWS, LANES), F32),
                        pltpu.VMEM((pps, pool.shape[2], LANES), F32), pltpu.SemaphoreType.DMA],
    )
    return pl.pallas_call(
        functools.partial(_sb_sample_kernel, pps=pps, page=page, ds=ds, n_pages=n_pages, layer=layer),
        grid_spec=grid_spec,
        out_shape=jax.ShapeDtypeStruct((batch * ds, SB_W), F32),
        compiler_params=_cp(1),
        name="sb_sample",
    )(pt_flat, u_rest, u_rest, u_rest, pool)


def _router_kernel(x_ref, g_ref, w_ref, b_ref, h_ref, o_ref):
    x = x_ref[...]
    h = x * lax.rsqrt(jnp.mean(x * x, axis=-1, keepdims=True) + NORM_EPS) * g_ref[...]
    h_ref[...] = h
    hh, hm, hl = _split3(h)
    wh, wm, wl = _split3(w_ref[...])
    logits = (_dot(hh, wh) + _dot(hh, wm) + _dot(hm, wh) + _dot(hh, wl) + _dot(hl, wh) + _dot(hm, wm)) + b_ref[...]
    lane = lax.broadcasted_iota(jnp.int32, logits.shape, 1)
    logits = jnp.where(lane < N_EXPERTS, logits, NEG_BIG)
    m1 = jnp.max(logits, axis=1, keepdims=True)
    i1 = jnp.min(jnp.where(logits == m1, lane, LANES), axis=1, keepdims=True)
    rest = jnp.where(lane == i1, NEG_BIG, logits)
    m2 = jnp.max(rest, axis=1, keepdims=True)
    i2 = jnp.min(jnp.where(rest == m2, lane, LANES), axis=1, keepdims=True)
    e = jnp.exp(m2 - m1)
    g1 = 1.0 / (1.0 + e)
    g2 = e / (1.0 + e)
    o_ref[...] = jnp.where(lane == 0, i1.astype(F32), jnp.where(lane == 1, i2.astype(F32),
                           jnp.where(lane == 2, g1, jnp.where(lane == 3, g2, 0.0))))


def moe_router(x, g, rw, rb, tm):
    m, d = x.shape
    w = jnp.zeros((d, LANES), F32).at[:, :N_EXPERTS].set(rw)
    b = jnp.zeros((1, LANES), F32).at[0, :N_EXPERTS].set(rb.astype(F32))
    return pl.pallas_call(
        _router_kernel,
        grid=(m // tm,),
        in_specs=[pl.BlockSpec((tm, d), lambda i: (i, 0)), pl.BlockSpec((1, d), lambda i: (0, 0)),
                  pl.BlockSpec((d, LANES), lambda i: (0, 0)), pl.BlockSpec((1, LANES), lambda i: (0, 0))],
        out_specs=[pl.BlockSpec((tm, d), lambda i: (i, 0)), pl.BlockSpec((tm, LANES), lambda i: (i, 0))],
        out_shape=[jax.ShapeDtypeStruct((m, d), F32), jax.ShapeDtypeStruct((m, LANES), F32)],
        compiler_params=_cp(1),
        name="moe_router",
    )(x, g.reshape(1, d), w, b)


def _row_copy(src_hbm, row, dst_ref, r, sem):
    return pltpu.make_async_copy(src_hbm.at[pl.ds(row, 1), :], dst_ref.at[pl.ds(r, 1), :], sem)


def _gather_kernel(tok_ref, nu_ref, h_hbm, o_ref, buf_ref, sem):
    blk = pl.program_id(0)
    n = buf_ref.shape[0]

    @pl.when(blk < nu_ref[0])
    def _():
        def issue(r, c):
            _row_copy(h_hbm, tok_ref[blk * n + r], buf_ref, r, sem).start()
            return c

        lax.fori_loop(0, n, issue, 0, unroll=8)
        pltpu.make_async_copy(h_hbm.at[pl.ds(0, n), :], buf_ref, sem).wait()
        o_ref[...] = buf_ref[...].astype(o_ref.dtype)

    @pl.when(blk >= nu_ref[0])
    def _():
        o_ref[...] = jnp.zeros_like(o_ref)


def _used_block(i, nu):
    return jnp.minimum(i, nu[0] - 1)


def moe_gather(h, tok_buf, n_used):
    p = tok_buf.shape[0]
    d = h.shape[1]
    grid_spec = pltpu.PrefetchScalarGridSpec(
        num_scalar_prefetch=2,
        grid=(p // MOE_BLOCK,),
        in_specs=[pl.BlockSpec(memory_space=pl.ANY)],
        out_specs=pl.BlockSpec((MOE_BLOCK, d), lambda i, tok, nu: (i, 0)),
        scratch_shapes=[pltpu.VMEM((MOE_BLOCK, d), F32), pltpu.SemaphoreType.DMA],
    )
    return pl.pallas_call(
        _gather_kernel,
        grid_spec=grid_spec,
        out_shape=jax.ShapeDtypeStruct((p, d), BF16),
        compiler_params=_cp(1),
        name="moe_gather",
    )(tok_buf, n_used, h)


def _moe_mm_kernel(be_ref, nu_ref, *refs, n_w, mode):
    a_ref = refs[0]
    w_refs = refs[1:1 + n_w]
    o_ref = refs[1 + n_w]
    wbf_refs = refs[2 + n_w:]
    blk = pl.program_id(1)
    changed = jnp.logical_or(blk == 0, be_ref[blk] != be_ref[jnp.maximum(blk - 1, 0)])

    @pl.when(changed)
    def _():
        for w_ref, wbf_ref in zip(w_refs, wbf_refs):
            wbf_ref[...] = w_ref[...].astype(BF16)

    @pl.when(blk < nu_ref[0])
    def _():
        prods = [_dot(a_ref[...], r[...]) for r in wbf_refs]
        o_ref[...] = _mm_epilogue(mode, prods, ()).astype(o_ref.dtype)

    @pl.when(blk >= nu_ref[0])
    def _():
        o_ref[...] = jnp.zeros_like(o_ref)


def moe_matmul(a, w_list, moe_index, blk_e, n_used, mode, out_dtype, tn=512):
    p, k = a.shape
    n_out = w_list[0].shape[-1]
    tn = _tile(n_out, tn)
    grid_spec = pltpu.PrefetchScalarGridSpec(
        num_scalar_prefetch=2,
        grid=(n_out // tn, p // MOE_BLOCK),
        in_specs=[pl.BlockSpec((MOE_BLOCK, k), lambda j, i, be, nu: (_used_block(i, nu), 0))]
        + [pl.BlockSpec((None, None, k, tn), lambda j, i, be, nu: (moe_index, be[i], 0, j)) for _ in w_list],
        out_specs=pl.BlockSpec((MOE_BLOCK, tn), lambda j, i, be, nu: (i, j)),
        scratch_shapes=[pltpu.VMEM((k, tn), BF16) for _ in w_list],
    )
    return pl.pallas_call(
        functools.partial(_moe_mm_kernel, n_w=len(w_list), mode=mode),
        grid_spec=grid_spec,
        out_shape=jax.ShapeDtypeStruct((p, n_out), out_dtype),
        compiler_params=_cp(2),
        name="moe_mm_" + mode,
    )(blk_e, n_used, a, *w_list)


def _moe_combine_kernel(pos_ref, y_hbm, x_ref, gate_ref, gain_ref, o_ref, buf_ref, sem, *, tm, norm):
    i = pl.program_id(0)

    def issue(r, c):
        for k in range(TOP_K):
            _row_copy(y_hbm, pos_ref[(i * tm + r) * TOP_K + k], buf_ref.at[k], r, sem).start()
        return c

    lax.fori_loop(0, tm, issue, 0, unroll=8)
    for k in range(TOP_K):
        pltpu.make_async_copy(y_hbm.at[pl.ds(0, tm), :], buf_ref.at[k], sem).wait()
    gate = gate_ref[...]
    out = x_ref[...]
    y = gate[:, 2:3] * buf_ref[0]
    for k in range(1, TOP_K):
        y = y + gate[:, 2 + k:3 + k] * buf_ref[k]
    out = out + y
    if norm:
        out = out * lax.rsqrt(jnp.mean(out * out, axis=-1, keepdims=True) + NORM_EPS) * gain_ref[...]
    o_ref[...] = out


def moe_combine(x, yb, pos, gates, gain=None, tm=256):
    m, d = x.shape
    norm = gain is not None
    gain = jnp.ones((d,), F32) if gain is None else gain
    grid_spec = pltpu.PrefetchScalarGridSpec(
        num_scalar_prefetch=1,
        grid=(m // tm,),
        in_specs=[pl.BlockSpec(memory_space=pl.ANY),
                  pl.BlockSpec((tm, d), lambda i, pos: (i, 0)),
                  pl.BlockSpec((tm, LANES), lambda i, pos: (i, 0)),
                  pl.BlockSpec((1, d), lambda i, pos: (0, 0))],
        out_specs=pl.BlockSpec((tm, d), lambda i, pos: (i, 0)),
        scratch_shapes=[pltpu.VMEM((TOP_K, tm, d), F32), pltpu.SemaphoreType.DMA],
    )
    return pl.pallas_call(
        functools.partial(_moe_combine_kernel, tm=tm, norm=norm),
        grid_spec=grid_spec,
        out_shape=jax.ShapeDtypeStruct((m, d), F32),
        compiler_params=_cp(1),
        name="moe_combine",
    )(pos, yb, x, gates, gain.reshape(1, d).astype(F32))


def moe_layer(x, g, rw, rb, w1, w3, w2, moe_index, n_tok, tm, out_gain=None):
    m, d = x.shape
    h, route = moe_router(x, g, rw, rb, tm)
    top_i = route[:n_tok, :TOP_K].astype(jnp.int32)
    a = n_tok * TOP_K
    e_flat = top_i.reshape(a)
    order = jnp.argsort(e_flat)
    e_s = e_flat[order]
    tok_s = (order // TOP_K).astype(jnp.int32)
    counts = jnp.bincount(e_flat, length=N_EXPERTS)
    start = jnp.cumsum(counts) - counts
    padded = (counts + MOE_BLOCK - 1) // MOE_BLOCK * MOE_BLOCK
    pend = jnp.cumsum(padded)
    pstart = pend - padded
    dest = (pstart[e_s] + jnp.arange(a) - start[e_s]).astype(jnp.int32)
    nb = -(-a // MOE_BLOCK) + N_EXPERTS
    p = nb * MOE_BLOCK
    n_used = (pend[-1] // MOE_BLOCK).astype(jnp.int32)
    blk_e = jnp.minimum(jnp.searchsorted(pend, jnp.arange(nb) * MOE_BLOCK, side="right"), N_EXPERTS - 1)
    slot = jnp.arange(p)
    slot_e = blk_e[slot // MOE_BLOCK]
    local = slot - pstart[slot_e]
    tok_buf = jnp.where(local < counts[slot_e], tok_s[jnp.clip(start[slot_e] + local, 0, a - 1)], n_tok)
    tok_buf = tok_buf.astype(jnp.int32)
    blk_e = jnp.where(jnp.arange(nb) < n_used, blk_e, blk_e[n_used - 1]).astype(jnp.int32)
    pos = jnp.concatenate([dest[jnp.argsort(order)], jnp.zeros((m * TOP_K - a,), jnp.int32)])
    row_ok = (jnp.arange(m) < n_tok)[:, None]
    gates = jnp.where(row_ok, route, 0.0)
    n_used = n_used.reshape(1)
    xs = moe_gather(h, tok_buf, n_used)
    act = moe_matmul(xs, [w1, w3], moe_index, blk_e, n_used, "swiglu", BF16)
    yb = moe_matmul(act, [w2], moe_index, blk_e, n_used, "plain", F32)
    return moe_combine(x, yb, pos, gates, out_gain)


TM = 512


def _row_tile(m, pref=1152):
    t = (min(pref, m) // 16) * 16
    while m % t:
        t -= 16
    return t


def _slab(parts, m, dtype):
    rows = sum(p.shape[0] for p in parts)
    parts = [p.astype(dtype) for p in parts]
    return jnp.concatenate(parts + [jnp.zeros((m - rows, parts[0].shape[1]), dtype)], axis=0)


def kernel(x_prompt, x_sample, cache_nsa_cmp_kv, cache_nsa_sel_kv, cache_sb_kv, cache_nsa_win_kv, state_hgrn,
           page_table, attn_norm, w_in, cmp_pe_k, cmp_w1_k, cmp_w2_k, cmp_pe_v, cmp_w1_v, cmp_w2_v, hg_lb_logits,
           hg_norm, w_br_nsa, w_br_sb, w_br_hg, w_out, ffn_norm, ffn_w1, ffn_w3, ffn_w2, router_w, router_b,
           moe_w1, moe_w3, moe_w2, final_norm):
    bsz, seq, d = x_prompt.shape
    db, ds, _ = x_sample.shape
    depth = attn_norm.shape[0]
    n_p, n_s = bsz * seq, db * ds
    n_tok = n_p + n_s
    m = -(-(n_tok + 1) // TM) * TM
    n_pool, page = cache_nsa_cmp_kv.shape[1:3]
    n_pages = page_table.shape[1]
    past = n_pages * page
    assert past % CMP_STRIDE == 0 and ds < CMP_STRIDE and seq % CMP_STRIDE == 0

    x = _slab([x_prompt.reshape(n_p, d), x_sample.reshape(n_s, d)], m, F32)
    pt_flat = page_table.reshape(-1).astype(jnp.int32)
    cmp_pool = cache_nsa_cmp_kv.reshape(depth, n_pool, page * 2 * NSA_KV_HEADS, HEAD_DIM)
    sel_pool = cache_nsa_sel_kv.reshape(depth, n_pool, page * 2 * NSA_KV_HEADS, HEAD_DIM)
    sb_pool = cache_sb_kv.reshape(depth, n_pool, page * 2 * SB_HEADS, HEAD_DIM)
    nbuf = cache_nsa_win_kv.shape[2]
    win_buf = cache_nsa_win_kv.reshape(depth, db, nbuf * 2 * NSA_KV_HEADS, HEAD_DIM)
    rest_cols = w_in.shape[-1] - REST_START
    tmm = _row_tile(m)
    w_in_t = jnp.swapaxes(w_in, 1, 2)
    lb_all = jnp.cumsum(jax.nn.softmax(hg_lb_logits.astype(F32), axis=0), axis=0)
    kvs = (2, NSA_KV_HEADS, HEAD_DIM)

    states = []
    for l in range(depth):
        h = rmsnorm(x, attn_norm[l], BF16, TM)
        u_head = inproj_t(h, w_in_t, l, 0, HEAD_COLS, tmm, HEAD_COLS // 3)
        u_rest = inproj_t(h, w_in_t, l, REST_START, rest_cols, tmm, 1024)

        wk, pk = compress_params(cmp_pe_k[l], cmp_w1_k[l])
        wv, pv = compress_params(cmp_pe_v[l], cmp_w1_v[l])
        wcat, pecat = jnp.stack([wk, wv]), jnp.stack([pk, pv])
        w2s = jnp.stack([cmp_w2_k[l], cmp_w2_v[l]])
        lb = (lb_all[l] - lb_all[0]).reshape(HG_HEADS, HEAD_DIM)
        lbp = hgrn_lb_params(lb)

        kvc = compress_finish(compress_segments_prompt(u_head, bsz, seq, wcat, pecat), w2s)
        o_cmp, sel = cmp_select(u_head, 0, bsz, seq, kvc, seq // CMP_STRIDE - 1, -(-seq // SEL_BLOCK), 0)
        o_sel = sel_prompt(u_head, bsz, seq, sel)
        o_win = win_prompt(u_head, bsz, seq)
        nsa_p = nsa_combine(u_head, 0, o_cmp, o_sel, o_win)
        sb_p = sb_prompt(u_rest, bsz, seq)
        hg_p, s_p = hgrn(u_rest, 0, bsz, seq, lbp, hg_norm[l], jnp.zeros((bsz, HG_HEADS, HEAD_DIM, HEAD_DIM), F32))

        kvc_s = compress_finish(compress_segments_sample(cmp_pool, l, pt_flat, db, n_pages, wcat, pecat), w2s)
        o_cmp_s, sel_s = cmp_select(u_head, n_p, db, ds, kvc_s, past // CMP_STRIDE - 1,
                                    -(-(past + ds) // SEL_BLOCK), past, tq=ds)
        o_sel_s = sel_sample(u_head, n_p, sel_pool, l, pt_flat, sel_s, db, ds, n_pages)
        o_win_s = win_sample(u_head, n_p, win_buf, l, db, ds)
        nsa_s = nsa_combine(u_head, n_p, o_cmp_s, o_sel_s, o_win_s)
        sb_s = sb_sample(u_rest, n_p, sb_pool, l, pt_flat, db, ds, n_pages)
        hg_s, s_s = hgrn(u_rest, n_p, db, ds, lbp, hg_norm[l], state_hgrn[l])

        branches = [(nsa_p.astype(BF16), nsa_s.astype(BF16)), (sb_p.astype(BF16), sb_s.astype(BF16)),
                    (hg_p.astype(BF16), hg_s.astype(BF16))]
        merged = matmul(branches, [w_br_nsa, w_br_sb, w_br_hg], l, 0, d, "merge", BF16,
                        extras=[(u_rest, R_MERGE), (u_rest, R_MERGE + d), (u_rest, R_MERGE + 2 * d)], tm=TM, m=m)
        x = matmul([merged], [w_out], l, 0, d, "residual", F32, extras=[(x, 0)], tm=tmm)

        i = l // 2
        if l % 2 == 0:
            h2 = rmsnorm(x, ffn_norm[l], BF16, TM)
            act = matmul([h2], [ffn_w1, ffn_w3], i, 0, ffn_w1.shape[-1], "swiglu", BF16, tm=tmm)
            x = matmul([act], [ffn_w2], i, 0, d, "residual", F32, extras=[(x, 0)])
        else:
            x = moe_layer(x, ffn_norm[l], router_w[i], router_b[i], moe_w1, moe_w3, moe_w2, i, n_tok, TM,
                          out_gain=final_norm if l == depth - 1 else None)

        def head_cols(off, r0, r1, lead):
            return u_head[r0:r1, off:off + KV_W].reshape(lead + kvs)

        win_p = head_cols(OFF_WIN, 0, n_p, (bsz, seq))[:, seq - min(WINDOW, seq):]
        win_s = jnp.concatenate([cache_nsa_win_kv[l], head_cols(OFF_WIN, n_p, n_tok, (db, ds))], axis=1)[:, ds:]
        states.append((
            head_cols(OFF_CMP, 0, n_p, (bsz, seq)), head_cols(OFF_SEL, 0, n_p, (bsz, seq)), win_p,
            u_rest[:n_p, R_SB_K:R_SB_K + 2 * SB_W].reshape(bsz, seq, 2, SB_HEADS, HEAD_DIM), s_p,
            head_cols(OFF_CMP, n_p, n_tok, (db, ds)), head_cols(OFF_SEL, n_p, n_tok, (db, ds)), win_s,
            u_rest[n_p:n_tok, R_SB_K:R_SB_K + 2 * SB_W].reshape(db, ds, 2, SB_HEADS, HEAD_DIM), s_s))

    y = x if depth % 2 == 0 else rmsnorm(x, final_norm, F32, TM)
    stacked = [jnp.stack([st[i] for st in states]) for i in range(10)]
    return (y[:n_p].reshape(bsz, seq, d), y[n_p:n_tok].reshape(db, ds, d), *stacked)
```

```python
import functools

import jax
import jax.numpy as jnp
import numpy as np
from jax import lax
from jax.experimental import pallas as pl
from jax.experimental.pallas import tpu as pltpu

F32 = jnp.float32
BF16 = jnp.bfloat16

HEAD_DIM = 128
SCALE = HEAD_DIM ** -0.5
NSA_HEADS = 8
NSA_KV_HEADS = 2
NSA_GROUP = NSA_HEADS // NSA_KV_HEADS
CMP_LEN = 32
CMP_STRIDE = 16
SEL_BLOCK = 64
SEL_TOPK = 16
WINDOW = 512
SB_HEADS = 8
HG_HEADS = 8
HG_CHUNK = 64
HG_SUB = 16
HG_MIN_CHUNK = 16
N_EXPERTS = 8
TOP_K = 2
MOE_BLOCK = 512
NORM_EPS = 1e-6
NEG_BIG = -1e30

LANES = 128
SUBLANES = 8
VMEM_LIMIT = 56 * 1024 * 1024

NSA_W = NSA_HEADS * HEAD_DIM
KV_W = 2 * NSA_KV_HEADS * HEAD_DIM
OFF_CMP = NSA_W
OFF_SEL = OFF_CMP + KV_W
OFF_WIN = OFF_SEL + KV_W
OFF_NSA_GATE = OFF_WIN + KV_W
N_GATE = 3 * NSA_HEADS
HEAD_COLS = OFF_NSA_GATE + LANES
REST_START = OFF_NSA_GATE + N_GATE
SB_W = SB_HEADS * HEAD_DIM
HG_W = HG_HEADS * HEAD_DIM
R_SB_Q = 0
R_SB_K = R_SB_Q + SB_W
R_SB_V = R_SB_K + SB_W
R_HG_Q = R_SB_V + SB_W
R_HG_F = R_HG_Q + HG_W
R_HG_I = R_HG_F + HG_W
R_HG_G = R_HG_I + HG_W
R_MERGE = R_HG_G + HG_W


def _cp(n_axes, vmem=VMEM_LIMIT):
    return pltpu.CompilerParams(dimension_semantics=("arbitrary",) * n_axes, vmem_limit_bytes=vmem)


def _tile(n, pref, quantum=LANES):
    if n <= pref:
        return n
    t = (pref // quantum) * quantum
    while t > quantum and n % t:
        t -= quantum
    assert n % t == 0, (n, pref)
    return t


def _dot(a, b):
    return jnp.dot(a, b, preferred_element_type=F32)


def _dot_nt(a, b):
    return lax.dot_general(a, b, (((1,), (1,)), ((), ())), preferred_element_type=F32)


def _dot_tn(a, b):
    return lax.dot_general(a, b, (((0,), (0,)), ((), ())), preferred_element_type=F32)


def _split3(x):
    hi = x.astype(BF16)
    r = x - hi.astype(F32)
    mid = r.astype(BF16)
    lo = (r - mid.astype(F32)).astype(BF16)
    return hi, mid, lo


def _split2(x):
    hi = x.astype(BF16)
    return hi, (x - hi.astype(F32)).astype(BF16)


def _sigmoid(x):
    return 1.0 / (1.0 + jnp.exp(-x))


def _silu(x):
    return x * _sigmoid(x)


def _log_sigmoid(x):
    return jnp.minimum(x, 0.0) - jnp.log1p(jnp.exp(-jnp.abs(x)))


def _rmsnorm_kernel(x_ref, g_ref, o_ref):
    x = x_ref[...]
    y = x * lax.rsqrt(jnp.mean(x * x, axis=-1, keepdims=True) + NORM_EPS)
    o_ref[...] = (y * g_ref[...]).astype(o_ref.dtype)


def rmsnorm(x, g, out_dtype, tm):
    m, d = x.shape
    return pl.pallas_call(
        _rmsnorm_kernel,
        grid=(m // tm,),
        in_specs=[pl.BlockSpec((tm, d), lambda i: (i, 0)), pl.BlockSpec((1, d), lambda i: (0, 0))],
        out_specs=pl.BlockSpec((tm, d), lambda i: (i, 0)),
        out_shape=jax.ShapeDtypeStruct((m, d), out_dtype),
        compiler_params=_cp(1),
        name="rmsnorm",
    )(x, g.reshape(1, d))


def _mm_epilogue(mode, prods, x_refs):
    if mode == "plain":
        return prods[0]
    if mode == "merge":
        out = _sigmoid(x_refs[0][...]) * prods[0]
        for x_ref, p in zip(x_refs[1:], prods[1:]):
            out = out + _sigmoid(x_ref[...]) * p
        return out
    if mode == "residual":
        return x_refs[0][...] + prods[0]
    return _silu(prods[0]) * prods[1]


def _mm_kernel(*refs, n_a, n_w, n_extra, mode, cast, main_tiles):
    n_in = n_a if main_tiles is None else 2 * n_a
    a_in, rest = refs[:n_in], refs[n_in:]
    if main_tiles is None:
        a_vals = [r[...] for r in a_in]
    else:
        in_main = pl.program_id(1) < main_tiles
        a_vals = [jnp.where(in_main, a_in[2 * t][...], a_in[2 * t + 1][...]) for t in range(n_a)]
    w_refs = rest[:n_w]
    x_refs = rest[n_w:n_w + n_extra]
    o_ref = rest[n_w + n_extra]
    wbf_refs = rest[n_w + n_extra + 1:]

    if cast:
        @pl.when(pl.program_id(1) == 0)
        def _():
            for w_ref, wbf_ref in zip(w_refs, wbf_refs):
                wbf_ref[...] = w_ref[...].astype(BF16)
        ws = [r[...] for r in wbf_refs]
    else:
        ws = [r[...] for r in w_refs]

    prods = [_dot(a_vals[min(i, n_a - 1)], w) for i, w in enumerate(ws)]
    o_ref[...] = _mm_epilogue(mode, prods, x_refs).astype(o_ref.dtype)


def matmul(a_list, w_list, w_index, col0, n_out, mode, out_dtype, extras=(), tm=512, tn=512, m=None):
    split = isinstance(a_list[0], tuple)
    m = m or a_list[0].shape[0]
    tn = _tile(n_out, tn)
    assert col0 % tn == 0 and m % tm == 0
    cast = w_list[0].dtype != BF16
    in_specs, args, scratch = [], [], []
    main_tiles = None
    for a in a_list:
        if split:
            main, tail = a
            main_tiles = main.shape[0] // tm
            assert main.shape[0] % tm == 0 and m == (main_tiles + 1) * tm and tail.shape[0] <= tm
            tail = jnp.concatenate([tail, jnp.zeros((tm - tail.shape[0], tail.shape[1]), tail.dtype)], axis=0)
            in_specs.append(pl.BlockSpec((tm, main.shape[1]), lambda j, i, n=main_tiles: (jnp.minimum(i, n - 1), 0)))
            in_specs.append(pl.BlockSpec((tm, main.shape[1]), lambda j, i: (0, 0)))
            args += [main, tail]
            continue
        in_specs.append(pl.BlockSpec((tm, a.shape[1]), lambda j, i: (i, 0)))
        args.append(a)
    for w in w_list:
        k = w.shape[-2]
        if w.ndim == 3:
            in_specs.append(pl.BlockSpec((None, k, tn), lambda j, i: (w_index, 0, j + col0 // tn)))
        else:
            in_specs.append(pl.BlockSpec((k, tn), lambda j, i: (0, j + col0 // tn)))
        args.append(w)
        if cast:
            scratch.append(pltpu.VMEM((k, tn), BF16))
    for x, off in extras:
        assert off % tn == 0
        in_specs.append(pl.BlockSpec((tm, tn), lambda j, i, off=off: (i, j + off // tn)))
        args.append(x)
    return pl.pallas_call(
        functools.partial(_mm_kernel, n_a=len(a_list), n_w=len(w_list), n_extra=len(extras), mode=mode, cast=cast,
                          main_tiles=main_tiles),
        grid=(n_out // tn, m // tm),
        in_specs=in_specs,
        out_specs=pl.BlockSpec((tm, tn), lambda j, i: (i, j)),
        out_shape=jax.ShapeDtypeStruct((m, n_out), out_dtype),
        scratch_shapes=scratch,
        compiler_params=_cp(2),
        name="mm_" + mode,
    )(*args)


def _inproj_t_kernel(a_ref, wt_ref, o_ref, wbf_ref):
    @pl.when(pl.program_id(1) == 0)
    def _():
        wbf_ref[...] = wt_ref[0].T.astype(BF16)

    o_ref[...] = _dot(a_ref[...], wbf_ref[...])


def inproj_t(a, w_t, layer, row0, n_out, tm, tn):
    m, k = a.shape
    tn = _tile(n_out, tn)
    assert m % tm == 0 and row0 % SUBLANES == 0
    return pl.pallas_call(
        _inproj_t_kernel,
        grid=(n_out // tn, m // tm),
        in_specs=[pl.BlockSpec((tm, k), lambda j, i: (i, 0)),
                  pl.BlockSpec((pl.Element(1), pl.Element(tn), pl.Element(k)),
                               lambda j, i: (layer, pl.multiple_of(row0 + j * tn, SUBLANES), 0))],
        out_specs=pl.BlockSpec((tm, tn), lambda j, i: (i, j)),
        out_shape=jax.ShapeDtypeStruct((m, n_out), F32),
        scratch_shapes=[pltpu.VMEM((k, tn), BF16)],
        compiler_params=_cp(2),
        name="inproj_t",
    )(a, w_t)


def _inproj_rest_kernel(a_ref, wa_ref, wb_ref, o_ref, wbf_ref, *, shift, tn):
    @pl.when(pl.program_id(1) == 0)
    def _():
        w = jnp.concatenate([wa_ref[...], wb_ref[...]], axis=1)
        wbf_ref[...] = w[:, shift:shift + tn].astype(BF16)

    o_ref[...] = _dot(a_ref[...], wbf_ref[...])


def inproj_rest(a, w_in, layer, n_out, tm, tn=512):
    m, k = a.shape
    shift = REST_START % LANES
    base = REST_START - shift
    tn = _tile(n_out, tn)
    assert base % tn == 0 and m % tm == 0 and REST_START + n_out == w_in.shape[-1]
    return pl.pallas_call(
        functools.partial(_inproj_rest_kernel, shift=shift, tn=tn),
        grid=(n_out // tn, m // tm),
        in_specs=[pl.BlockSpec((tm, k), lambda j, i: (i, 0)),
                  pl.BlockSpec((None, k, tn), lambda j, i: (layer, 0, base // tn + j)),
                  pl.BlockSpec((None, k, LANES), lambda j, i: (layer, 0, (base + (j + 1) * tn) // LANES))],
        out_specs=pl.BlockSpec((tm, tn), lambda j, i: (i, j)),
        out_shape=jax.ShapeDtypeStruct((m, n_out), F32),
        scratch_shapes=[pltpu.VMEM((k, tn), BF16)],
        compiler_params=_cp(2),
        name="inproj_rest",
    )(a, w_in, w_in)


def _suffix_matrix():
    j = lax.broadcasted_iota(jnp.int32, (LANES, 2 * LANES), 0)
    s = lax.broadcasted_iota(jnp.int32, (LANES, 2 * LANES), 1)
    return jnp.where((j > s) | (s >= LANES), 1.0, 0.0).astype(BF16)


def _sb_weights(z, mask, carry, umat):
    t = jnp.log(1.0 + jnp.exp(-jnp.abs(z)))
    ls_pos = jnp.minimum(z, 0.0) - t
    c = ls_pos - z
    if mask is not None:
        c = jnp.where(mask, c, 0.0)
    n_sub = z.shape[1] // LANES
    pieces = [None] * n_sub
    for sb in reversed(range(n_sub)):
        sl = slice(sb * LANES, (sb + 1) * LANES)
        hi, lo = _split2(c[:, sl])
        r = _dot(hi, umat) + _dot(lo, umat)
        w = jnp.exp(ls_pos[:, sl] + (carry + r[:, :LANES]))
        pieces[sb] = w if mask is None else jnp.where(mask[:, sl], w, 0.0)
        carry = carry + r[:, LANES:]
    a = pieces[0] if n_sub == 1 else jnp.concatenate(pieces, axis=1)
    return a, carry


SB_ZERO_LOG = -104.0


def _sb_prompt_kernel(q_ref, k_ref, v_ref, o_ref, acc_ref, carry_ref, *, t, hb):
    qi = pl.program_id(2)
    umat = _suffix_matrix()
    qs = [(q_ref[:, j * HEAD_DIM:(j + 1) * HEAD_DIM] * SCALE).astype(BF16) for j in range(hb)]
    row = lax.broadcasted_iota(jnp.int32, (t, t), 0)
    col = lax.broadcasted_iota(jnp.int32, (t, t), 1)

    def tile(kb, mask):
        rows = pl.ds(pl.multiple_of(kb * t, t), t)
        top = None
        for j in range(hb):
            cols = slice(j * HEAD_DIM, (j + 1) * HEAD_DIM)
            z = _dot_nt(qs[j], k_ref[rows, cols].astype(BF16))
            a, carry = _sb_weights(z, mask, carry_ref[j], umat)
            acc_ref[j] += _dot(a.astype(BF16), v_ref[rows, cols].astype(BF16))
            carry_ref[j] = carry
            top = jnp.max(carry) if top is None else jnp.maximum(top, jnp.max(carry))
        return top

    acc_ref[...] = jnp.zeros_like(acc_ref)
    carry_ref[...] = jnp.zeros_like(carry_ref)
    top = tile(qi, col < row)

    def cond(st):
        return jnp.logical_and(st[0] >= 0, st[1] > SB_ZERO_LOG)

    def body(st):
        return st[0] - 1, tile(st[0], None)

    lax.while_loop(cond, body, (qi - 1, top))
    for j in range(hb):
        o_ref[:, j * HEAD_DIM:(j + 1) * HEAD_DIM] = acc_ref[j].astype(o_ref.dtype)


def sb_prompt(u_rest, batch, seq, t=256, hb=4):
    t = min(t, seq)
    nq = seq // t
    bw = hb * HEAD_DIM
    cq, ck, cv = R_SB_Q // bw, R_SB_K // bw, R_SB_V // bw
    assert SB_HEADS % hb == 0
    return pl.pallas_call(
        functools.partial(_sb_prompt_kernel, t=t, hb=hb),
        grid=(batch, SB_HEADS // hb, nq),
        in_specs=[pl.BlockSpec((t, bw), lambda b, h, i: (b * nq + i, cq + h)),
                  pl.BlockSpec((seq, bw), lambda b, h, i: (b, ck + h)),
                  pl.BlockSpec((seq, bw), lambda b, h, i: (b, cv + h))],
        out_specs=pl.BlockSpec((t, bw), lambda b, h, i: (b * nq + i, h)),
        out_shape=jax.ShapeDtypeStruct((batch * seq, SB_W), BF16),
        scratch_shapes=[pltpu.VMEM((hb, t, LANES), F32), pltpu.VMEM((hb, t, LANES), F32)],
        compiler_params=_cp(3),
        name="sb_prompt",
    )(u_rest, u_rest, u_rest)


def _hgrn_chunk(qr, fr, v, gr, lbp, norm_w, st, c, sub, c_real):
    log_lb, log_1m_lb, one_m_lb = lbp[0:1], lbp[1:2], lbp[2:3]
    q = _silu(qr)
    k = one_m_lb * _sigmoid(-fr)
    bb = log_1m_lb + _log_sigmoid(fr)
    mx = jnp.maximum(log_lb, bb)
    logf = mx + jnp.log1p(jnp.exp(-jnp.abs(log_lb - bb)))
    row = lax.broadcasted_iota(jnp.int32, (c, c), 0)
    col = lax.broadcasted_iota(jnp.int32, (c, c), 1)
    tri = jnp.where(row >= col, 1.0, 0.0).astype(BF16)
    hi, mid, lo = _split3(logf)
    b = _dot(tri, hi) + _dot(tri, mid) + _dot(tri, lo)
    o = _dot_nt((q * jnp.exp(b)).astype(BF16), st.astype(BF16))
    ridx = lax.broadcasted_iota(jnp.int32, (c, HEAD_DIM), 0)
    lane = lax.broadcasted_iota(jnp.int32, (sub, c), 1)
    att_rows = []
    for i in range(c // sub):
        r0 = i * sub
        b_i = b[r0:r0 + sub]
        q_i = q[r0:r0 + sub]
        att_i = jnp.zeros((sub, c), F32)
        if i > 0:
            rho = b_i[0:1]
            earlier = ridx < r0
            k_dec = jnp.where(earlier, k * jnp.exp(jnp.where(earlier, rho - b, 0.0)), 0.0)
            att_i = _dot_nt((q_i * jnp.exp(b_i - rho)).astype(BF16), k_dec.astype(BF16))
        trow = lax.broadcasted_iota(jnp.int32, (sub, 1), 0)
        for s in range(sub):
            d = q_i * jnp.exp(jnp.where(trow >= s, b_i - b_i[s:s + 1], 0.0)) * k[r0 + s:r0 + s + 1]
            colsum = jnp.sum(d, axis=1, keepdims=True)
            att_i = att_i + jnp.where((lane == r0 + s) & (trow >= s), colsum, 0.0)
        att_rows.append(att_i)
    att = att_rows[0] if len(att_rows) == 1 else jnp.concatenate(att_rows, axis=0)
    o = o + _dot(att.astype(BF16), v.astype(BF16))
    b_end = b[c_real - 1:c_real]
    real = ridx < c_real
    k_end = jnp.where(real, k * jnp.exp(jnp.where(real, b_end - b, 0.0)), 0.0)
    st = st * jnp.exp(b_end) + _dot_tn(v.astype(BF16), k_end.astype(BF16))
    o = o * lax.rsqrt(jnp.mean(o * o, axis=-1, keepdims=True) + NORM_EPS) * norm_w
    return o * _silu(gr), st


def _hgrn_kernel(q_ref, f_ref, i_ref, g_ref, lbp_ref, nw_ref, s0_ref, o_ref, s_out_ref, st_ref, *,
                 c, sub, n_chunks, hb):
    t = pl.program_id(2)

    @pl.when(t == 0)
    def _():
        for j in range(hb):
            st_ref[j] = s0_ref[j].T

    nw = nw_ref[...]
    in_refs = (q_ref, f_ref, i_ref, g_ref)

    if c < HG_MIN_CHUNK:
        pad = jnp.zeros((HG_MIN_CHUNK - c, HEAD_DIM), F32)
        for j in range(hb):
            cols = slice(j * HEAD_DIM, (j + 1) * HEAD_DIM)
            ins = [jnp.concatenate([r[:, cols], pad], axis=0) for r in in_refs]
            o, st = _hgrn_chunk(*ins, lbp_ref[j], nw, st_ref[j], HG_MIN_CHUNK, HG_SUB, c)
            st_ref[j] = st
            o_ref[:, cols] = o[:c].astype(o_ref.dtype)
    else:
        def body(ci, carry):
            rows = pl.ds(pl.multiple_of(ci * c, c), c)
            for j in range(hb):
                cols = slice(j * HEAD_DIM, (j + 1) * HEAD_DIM)
                o, st = _hgrn_chunk(*[r[rows, cols] for r in in_refs], lbp_ref[j], nw, st_ref[j], c, sub, c)
                st_ref[j] = st
                o_ref[rows, cols] = o.astype(o_ref.dtype)
            return carry

        lax.fori_loop(0, n_chunks, body, 0)

    @pl.when(t == pl.num_programs(2) - 1)
    def _():
        for j in range(hb):
            s_out_ref[j] = st_ref[j].T


def hgrn_lb_params(lb):
    rows = jnp.stack([jnp.log(lb), jnp.log1p(-lb), 1.0 - lb], axis=1)
    return jnp.concatenate([rows, jnp.zeros((lb.shape[0], SUBLANES - 3, lb.shape[1]), F32)], axis=1)


def hgrn(u_rest, row0, batch, seq, lbp, norm_w, s0, tl=512, hb=8):
    c = min(HG_CHUNK, seq)
    sub = min(HG_SUB, c)
    tl = min(tl, seq)
    nt = seq // tl
    rb0 = row0 // tl
    bw = hb * HEAD_DIM
    assert row0 % tl == 0 and seq % tl == 0 and tl % c == 0 and (c % HG_SUB == 0 or nt == 1) and HG_HEADS % hb == 0

    def col(off):
        return pl.BlockSpec((tl, bw), lambda b, h, t, off=off: (rb0 + b * nt + t, off // bw + h))

    o, s_out = pl.pallas_call(
        functools.partial(_hgrn_kernel, c=c, sub=sub, n_chunks=tl // c, hb=hb),
        grid=(batch, HG_HEADS // hb, nt),
        in_specs=[col(R_HG_Q), col(R_HG_F), col(R_HG_I), col(R_HG_G),
                  pl.BlockSpec((hb, SUBLANES, LANES), lambda b, h, t: (h, 0, 0)),
                  pl.BlockSpec((1, LANES), lambda b, h, t: (0, 0)),
                  pl.BlockSpec((None, hb, HEAD_DIM, HEAD_DIM), lambda b, h, t: (b, h, 0, 0))],
        out_specs=[pl.BlockSpec((tl, bw), lambda b, h, t: (b * nt + t, h)),
                   pl.BlockSpec((None, hb, HEAD_DIM, HEAD_DIM), lambda b, h, t: (b, h, 0, 0))],
        out_shape=[jax.ShapeDtypeStruct((batch * seq, HG_W), BF16 if tl % 16 == 0 else F32),
                   jax.ShapeDtypeStruct((batch, HG_HEADS, HEAD_DIM, HEAD_DIM), F32)],
        scratch_shapes=[pltpu.VMEM((hb, HEAD_DIM, HEAD_DIM), F32)],
        compiler_params=_cp(3),
        name="hgrn",
    )(u_rest, u_rest, u_rest, u_rest, lbp, norm_w.reshape(1, LANES), s0)
    return o, s_out


SEG_W = CMP_STRIDE * HEAD_DIM


def compress_params(pe, w1):
    w = jnp.concatenate([w1[:SEG_W], w1[SEG_W:]], axis=1).astype(BF16)
    rows = jnp.stack([pe[:CMP_STRIDE].reshape(SEG_W), pe[CMP_STRIDE:].reshape(SEG_W)])
    return w, jnp.concatenate([rows, jnp.zeros((SUBLANES - 2, SEG_W), F32)]).astype(BF16)


def _segment_products(r, w, pe):
    bias = _dot(pe, w)
    bias = jnp.concatenate([bias[0:1, :LANES], bias[1:2, LANES:]], axis=1)
    return _dot(r.astype(BF16), w) + bias


def _cmp1_prompt_kernel(x_ref, w_ref, pe_ref, o_ref, r_ref, *, n_seg):
    for l in range(CMP_STRIDE):
        r_ref[:, l * LANES:(l + 1) * LANES] = x_ref[pl.ds(l, n_seg, stride=CMP_STRIDE), :]
    o_ref[...] = _segment_products(r_ref[...], w_ref[...], pe_ref[...])


def compress_segments_prompt(u_head, batch, seq, wcat, pecat, tr=512):
    tr = min(tr, seq)
    n_seg = tr // CMP_STRIDE
    nt = seq // tr
    c0 = OFF_CMP // LANES
    return pl.pallas_call(
        functools.partial(_cmp1_prompt_kernel, n_seg=n_seg),
        grid=(batch, 2, NSA_KV_HEADS, nt),
        in_specs=[pl.BlockSpec((tr, LANES), lambda b, kv, g, t: (b * nt + t, c0 + kv * NSA_KV_HEADS + g)),
                  pl.BlockSpec((None, SEG_W, 2 * LANES), lambda b, kv, g, t: (kv, 0, 0)),
                  pl.BlockSpec((None, SUBLANES, SEG_W), lambda b, kv, g, t: (kv, 0, 0))],
        out_specs=pl.BlockSpec((None, None, None, n_seg, 2 * LANES), lambda b, kv, g, t: (b, kv, g, t, 0)),
        out_shape=jax.ShapeDtypeStruct((batch, 2, NSA_KV_HEADS, seq // CMP_STRIDE, 2 * LANES), F32),
        scratch_shapes=[pltpu.VMEM((n_seg, SEG_W), F32)],
        compiler_params=_cp(4),
        name="cmp_segments_prompt",
    )(u_head, wcat, pecat)


def _cmp2_kernel(pq_ref, w2_ref, o_ref, *, nc):
    pq = pq_ref[...]
    q_next = pltpu.roll(pq[:, LANES:], shift=nc - 1, axis=0)
    hid = _silu(pq[:, :LANES] + q_next)
    out = _dot(hid.astype(BF16), w2_ref[...].astype(BF16))
    row = lax.broadcasted_iota(jnp.int32, out.shape, 0)
    o_ref[...] = jnp.where(row < nc - 1, out, 0.0)


def compress_finish(pq, w2):
    batch, _, _, nc, _ = pq.shape
    return pl.pallas_call(
        functools.partial(_cmp2_kernel, nc=nc),
        grid=(batch, 2, NSA_KV_HEADS),
        in_specs=[pl.BlockSpec((None, None, None, nc, 2 * LANES), lambda b, kv, g: (b, kv, g, 0, 0)),
                  pl.BlockSpec((None, HEAD_DIM, HEAD_DIM), lambda b, kv, g: (kv, 0, 0))],
        out_specs=pl.BlockSpec((None, None, None, nc, LANES), lambda b, kv, g: (b, kv, g, 0, 0)),
        out_shape=jax.ShapeDtypeStruct((batch, 2, NSA_KV_HEADS, nc, LANES), F32),
        compiler_params=_cp(3),
        name="cmp_finish",
    )(pq, w2)


RANK_PAIRWISE_MAX_TOKENS = 16


def _cmp_select_kernel(q_ref, kc_ref, vc_ref, o_ref, sel_ref, score_ref, *, tq, tqp, nc, n_cmp, n_slc, nsp, pos0):
    qi = pl.program_id(2)
    qb = q_ref[...]
    parts = []
    for h in range(NSA_GROUP):
        qh = qb[:, h * HEAD_DIM:(h + 1) * HEAD_DIM]
        if tqp > tq:
            qh = jnp.concatenate([qh, jnp.zeros((tqp - tq, HEAD_DIM), F32)], axis=0)
        parts.append(qh)
    q4 = jnp.concatenate(parts, axis=0).astype(BF16)
    rows = NSA_GROUP * tqp
    st = _dot_nt(kc_ref[...].astype(BF16), q4) * SCALE
    ci = lax.broadcasted_iota(jnp.int32, (nc, rows), 0)
    tok = lax.broadcasted_iota(jnp.int32, (nc, rows), 1) & (tqp - 1)
    tpos = pos0 + qi * tq + tok
    valid = (ci < n_cmp) & (ci * CMP_STRIDE + CMP_LEN - 1 <= tpos)
    st = jnp.where(valid, st, NEG_BIG)
    m = jnp.max(st, axis=0, keepdims=True)
    e = jnp.where(valid, jnp.exp(st - m), 0.0)
    den = jnp.sum(e, axis=0, keepdims=True)
    pt = e / jnp.where(den > 0, den, 1.0)
    o = _dot_tn(pt.astype(BF16), vc_ref[...].astype(BF16))
    for h in range(NSA_GROUP):
        o_ref[:, h * HEAD_DIM:(h + 1) * HEAD_DIM] = o[h * tqp:h * tqp + tq]
    psum = pt[:, 0:tqp]
    for h in range(1, NSA_GROUP):
        psum = psum + pt[:, h * tqp:(h + 1) * tqp]
    nsr = score_ref.shape[0]
    jj = lax.broadcasted_iota(jnp.int32, (nsr, nc), 0)
    ii = lax.broadcasted_iota(jnp.int32, (nsr, nc), 1)
    cover = ((ii * CMP_STRIDE < jj * SEL_BLOCK + SEL_BLOCK) & (ii * CMP_STRIDE + CMP_LEN - 1 >= jj * SEL_BLOCK)
             & (ii < n_cmp))
    cover = jnp.where(cover, 1.0, 0.0).astype(BF16)
    hi, mid, lo = _split3(psum)
    imp = _dot(cover, hi) + _dot(cover, mid) + _dot(cover, lo)
    j = lax.broadcasted_iota(jnp.int32, (nsr, tqp), 0)
    tpos2 = pos0 + qi * tq + lax.broadcasted_iota(jnp.int32, (nsr, tqp), 1)
    cur = tpos2 // SEL_BLOCK
    forced = (j == 0) | (j == cur) | (j == cur - 1)
    ok = (j * SEL_BLOCK <= tpos2) & (j < n_slc)
    score = jnp.where(ok, jnp.where(forced, NSA_GROUP + 1.0, imp), -1.0)

    if tq <= RANK_PAIRWISE_MAX_TOKENS:
        if nsp > nsr:
            score = jnp.concatenate([score, jnp.full((nsp - nsr, tqp), -1.0, F32)], axis=0)
        score_rows = score.T
        jp_i = lax.broadcasted_iota(jnp.int32, (nsp, nsp), 0)
        j_i = lax.broadcasted_iota(jnp.int32, (nsp, nsp), 1)
        rows_out = []
        for t in range(tq):
            other = score[:, t:t + 1]
            mine = score_rows[t:t + 1, :]
            ahead = jnp.where(other > mine, 1.0, jnp.where((other == mine) & (jp_i < j_i), 1.0, 0.0))
            rank_t = jnp.sum(ahead, axis=0, keepdims=True)
            rows_out.append(jnp.where((rank_t < SEL_TOPK) & (mine >= 0), 1.0, 0.0))
        sel_ref[...] = jnp.concatenate(rows_out, axis=0)
        return

    score_ref[...] = score

    def body(jp, rank):
        other = score_ref[pl.ds(jp, 1), :]
        ahead = jnp.where(other > score, 1.0, jnp.where((other == score) & (jp < j), 1.0, 0.0))
        return rank + ahead

    n_valid = jnp.minimum(n_slc, (pos0 + qi * tq + tq - 1) // SEL_BLOCK + 1)
    rank = lax.fori_loop(0, n_valid, body, jnp.zeros((nsr, tqp), F32))
    sel_t = jnp.where((rank < SEL_TOPK) & (score >= 0), 1.0, 0.0)
    if nsp > nsr:
        sel_t = jnp.concatenate([sel_t, jnp.zeros((nsp - nsr, tqp), F32)], axis=0)
    sel_ref[...] = sel_t.T[:tq]


def cmp_select(u_head, row0, batch, seq, kv_cmp, n_cmp, n_slc, pos0, tq=256):
    tq = min(tq, seq)
    tqp = max(tq, LANES)
    nt = seq // tq
    rb0 = row0 // tq
    nc = kv_cmp.shape[3]
    nsp = -(-n_slc // LANES) * LANES
    gw = NSA_GROUP * HEAD_DIM
    assert row0 % tq == 0 and seq % tq == 0 and tqp & (tqp - 1) == 0
    return pl.pallas_call(
        functools.partial(_cmp_select_kernel, tq=tq, tqp=tqp, nc=nc, n_cmp=n_cmp, n_slc=n_slc, nsp=nsp, pos0=pos0),
        grid=(batch, NSA_KV_HEADS, nt),
        in_specs=[pl.BlockSpec((tq, gw), lambda b, g, t: (rb0 + b * nt + t, g)),
                  pl.BlockSpec((None, None, None, nc, LANES), lambda b, g, t: (b, 0, g, 0, 0)),
                  pl.BlockSpec((None, None, None, nc, LANES), lambda b, g, t: (b, 1, g, 0, 0))],
        out_specs=[pl.BlockSpec((tq, gw), lambda b, g, t: (b * nt + t, g)),
                   pl.BlockSpec((None, None, tq, nsp), lambda b, g, t: (b, g, t, 0))],
        out_shape=[jax.ShapeDtypeStruct((batch * seq, NSA_W), F32),
                   jax.ShapeDtypeStruct((batch, NSA_KV_HEADS, seq, nsp), F32)],
        scratch_shapes=[pltpu.VMEM((-(-n_slc // 16) * 16, tqp), F32)],
        compiler_params=_cp(3),
        name="cmp_select",
    )(u_head, kv_cmp, kv_cmp)


def _flash_step(s, mask, v, m_ref, l_ref, acc_ref):
    s = jnp.where(mask, s, NEG_BIG)
    m_prev = m_ref[...]
    m_new = jnp.maximum(m_prev, jnp.max(s, axis=1, keepdims=True))
    e = jnp.where(mask, jnp.exp(s - m_new), 0.0)
    alpha = jnp.exp(m_prev - m_new)
    l_ref[...] = alpha * l_ref[...] + jnp.sum(e, axis=1, keepdims=True)
    acc_ref[...] = alpha * acc_ref[...] + _dot(e.astype(BF16), v)
    m_ref[...] = m_new


def _flash_init(m_ref, l_ref, acc_ref):
    m_ref[...] = jnp.full_like(m_ref, NEG_BIG)
    l_ref[...] = jnp.zeros_like(l_ref)
    acc_ref[...] = jnp.zeros_like(acc_ref)


def _flash_result(l_ref, acc_ref):
    l = l_ref[...]
    return acc_ref[...] / jnp.where(l > 0, l, 1.0)


def _stack_heads(qb, pad_to=None):
    parts = []
    for h in range(NSA_GROUP):
        qh = qb[:, h * HEAD_DIM:(h + 1) * HEAD_DIM]
        if pad_to is not None and pad_to > qh.shape[0]:
            qh = jnp.concatenate([qh, jnp.zeros((pad_to - qh.shape[0], HEAD_DIM), qh.dtype)], axis=0)
        parts.append(qh)
    return jnp.concatenate(parts, axis=0)


def _flash_prompt_kernel(qi_tab, kb_tab, first_tab, last_tab, *refs, tq, tk, mode, nsp):
    if mode == "sel":
        q_ref, k_ref, v_ref, sel_ref, o_ref, m_ref, l_ref, acc_ref = refs
    else:
        q_ref, k_ref, v_ref, o_ref, m_ref, l_ref, acc_ref = refs
    p = pl.program_id(2)
    qi = qi_tab[p]
    kb = kb_tab[p]

    @pl.when(first_tab[p] == 1)
    def _():
        _flash_init(m_ref, l_ref, acc_ref)

    q4 = _stack_heads(q_ref[...]).astype(BF16)
    s = _dot_nt(q4, k_ref[...].astype(BF16)) * SCALE
    qpos = qi * tq + lax.broadcasted_iota(jnp.int32, (tq, tk), 0)
    kpos = kb * tk + lax.broadcasted_iota(jnp.int32, (tq, tk), 1)
    if mode == "win":
        d = qpos - kpos
        mask = (d >= 0) & (d < WINDOW)
    else:
        jj = lax.broadcasted_iota(jnp.int32, (nsp, tk), 0)
        kk = lax.broadcasted_iota(jnp.int32, (nsp, tk), 1)
        expand = jnp.where(jj == kb * (tk // SEL_BLOCK) + kk // SEL_BLOCK, 1.0, 0.0).astype(BF16)
        chosen = _dot(sel_ref[...].astype(BF16), expand)
        mask = (chosen > 0.5) & (kpos <= qpos)
    mask4 = jnp.concatenate([mask] * NSA_GROUP, axis=0)
    _flash_step(s, mask4, v_ref[...].astype(BF16), m_ref, l_ref, acc_ref)

    @pl.when(last_tab[p] == 1)
    def _():
        o = _flash_result(l_ref, acc_ref)
        for h in range(NSA_GROUP):
            o_ref[:, h * HEAD_DIM:(h + 1) * HEAD_DIM] = o[h * tq:(h + 1) * tq]


def flash_prompt(u_head, batch, seq, mode, sel=None, tq=256, tk=None):
    tq = min(tq, seq)
    tk = min(tk or (256 if mode == "win" else 512), seq)
    nq, nk = seq // tq, seq // tk
    pairs = []
    for qi in range(nq):
        if mode == "win":
            lo = max(0, (qi * tq - WINDOW + 1) // tk)
        else:
            lo = 0
        hi = (qi * tq + tq - 1) // tk
        kbs = list(range(lo, hi + 1))
        pairs += [(qi, kb, int(kb == kbs[0]), int(kb == kbs[-1])) for kb in kbs]
    tabs = [jnp.asarray([p[i] for p in pairs], jnp.int32) for i in range(4)]
    off = OFF_WIN if mode == "win" else OFF_SEL
    ck = off // LANES
    cv = ck + NSA_KV_HEADS
    gw = NSA_GROUP * HEAD_DIM
    in_specs = [
        pl.BlockSpec((tq, gw), lambda b, g, p, qt, kt, ft, lt: (b * nq + qt[p], g)),
        pl.BlockSpec((tk, LANES), lambda b, g, p, qt, kt, ft, lt: (b * nk + kt[p], ck + g)),
        pl.BlockSpec((tk, LANES), lambda b, g, p, qt, kt, ft, lt: (b * nk + kt[p], cv + g)),
    ]
    args = [u_head, u_head, u_head]
    nsp = 0
    if mode == "sel":
        nsp = sel.shape[-1]
        in_specs.append(pl.BlockSpec((None, None, tq, nsp), lambda b, g, p, qt, kt, ft, lt: (b, g, qt[p], 0)))
        args.append(sel)
    rows = NSA_GROUP * tq
    grid_spec = pltpu.PrefetchScalarGridSpec(
        num_scalar_prefetch=4,
        grid=(batch, NSA_KV_HEADS, len(pairs)),
        in_specs=in_specs,
        out_specs=pl.BlockSpec((tq, gw), lambda b, g, p, qt, kt, ft, lt: (b * nq + qt[p], g)),
        scratch_shapes=[pltpu.VMEM((rows, 1), F32), pltpu.VMEM((rows, 1), F32), pltpu.VMEM((rows, HEAD_DIM), F32)],
    )
    return pl.pallas_call(
        functools.partial(_flash_prompt_kernel, tq=tq, tk=tk, mode=mode, nsp=nsp),
        grid_spec=grid_spec,
        out_shape=jax.ShapeDtypeStruct((batch * seq, NSA_W), F32),
        compiler_params=_cp(3),
        name="flash_" + mode,
    )(*tabs, *args)


def _sel_prompt_kernel(q_ref, k_ref, v_ref, sel_ref, o_ref, m_ref, l_ref, acc_ref, *, tq, tk, nsp):
    qi = pl.program_id(2)
    q4 = (_stack_heads(q_ref[...]) * SCALE).astype(BF16)
    selb = sel_ref[...].astype(BF16)
    _flash_init(m_ref, l_ref, acc_ref)
    row = lax.broadcasted_iota(jnp.int32, (tq, tk), 0)
    col = lax.broadcasted_iota(jnp.int32, (tq, tk), 1)
    jj = lax.broadcasted_iota(jnp.int32, (nsp, tk), 0)
    kk = lax.broadcasted_iota(jnp.int32, (nsp, tk), 1) // SEL_BLOCK

    def body(kb, c):
        keys = pl.ds(pl.multiple_of(kb * tk, tk), tk)
        s = _dot_nt(q4, k_ref[keys, :].astype(BF16))
        expand = jnp.where(jj == kb * (tk // SEL_BLOCK) + kk, 1.0, 0.0).astype(BF16)
        chosen = _dot(selb, expand)
        ok = (chosen > 0.5) & (kb * tk + col <= qi * tq + row)
        bias = jnp.where(ok, 0.0, 2.0 * NEG_BIG)
        s = (s.reshape(NSA_GROUP, tq, tk) + bias[None]).reshape(NSA_GROUP * tq, tk)
        m_prev = m_ref[...]
        m_new = jnp.maximum(m_prev, jnp.max(s, axis=1, keepdims=True))
        e = jnp.exp(s - m_new)
        alpha = jnp.exp(m_prev - m_new)
        l_ref[...] = alpha * l_ref[...] + jnp.sum(e, axis=1, keepdims=True)
        acc_ref[...] = alpha * acc_ref[...] + _dot(e.astype(BF16), v_ref[keys, :].astype(BF16))
        m_ref[...] = m_new
        return c

    lax.fori_loop(0, (qi * tq + tq - 1) // tk + 1, body, 0)
    o = _flash_result(l_ref, acc_ref)
    for h in range(NSA_GROUP):
        o_ref[:, h * HEAD_DIM:(h + 1) * HEAD_DIM] = o[h * tq:(h + 1) * tq]


def sel_prompt(u_head, batch, seq, sel, tq=256, tk=512):
    tq = min(tq, seq)
    tk = min(tk, seq)
    nq = seq // tq
    nsp = sel.shape[-1]
    ck = OFF_SEL // LANES
    cv = ck + NSA_KV_HEADS
    gw = NSA_GROUP * HEAD_DIM
    rows = NSA_GROUP * tq
    assert seq % tq == 0 and seq % tk == 0 and tk % SEL_BLOCK == 0
    return pl.pallas_call(
        functools.partial(_sel_prompt_kernel, tq=tq, tk=tk, nsp=nsp),
        grid=(batch, NSA_KV_HEADS, nq),
        in_specs=[pl.BlockSpec((tq, gw), lambda b, g, i: (b * nq + i, g)),
                  pl.BlockSpec((seq, LANES), lambda b, g, i: (b, ck + g)),
                  pl.BlockSpec((seq, LANES), lambda b, g, i: (b, cv + g)),
                  pl.BlockSpec((None, None, tq, nsp), lambda b, g, i: (b, g, i, 0))],
        out_specs=pl.BlockSpec((tq, gw), lambda b, g, i: (b * nq + i, g)),
        out_shape=jax.ShapeDtypeStruct((batch * seq, NSA_W), F32),
        scratch_shapes=[pltpu.VMEM((rows, 1), F32), pltpu.VMEM((rows, 1), F32), pltpu.VMEM((rows, HEAD_DIM), F32)],
        compiler_params=_cp(3),
        name="sel_prompt",
    )(u_head, u_head, u_head, sel)


def _win_prompt_kernel(q_ref, k_ref, v_ref, o_ref, *, tq, nk, seq):
    q0 = pl.program_id(2) * tq
    start = pl.multiple_of(jnp.clip(q0 - WINDOW, 0, seq - nk), tq)
    keys = pl.ds(start, nk)
    q4 = (_stack_heads(q_ref[...]) * SCALE).astype(BF16)
    s = _dot_nt(q4, k_ref[keys, :].astype(BF16))
    d = (q0 + lax.broadcasted_iota(jnp.int32, (tq, nk), 0)) - (start + lax.broadcasted_iota(jnp.int32, (tq, nk), 1))
    bias = jnp.where((d >= 0) & (d < WINDOW), 0.0, NEG_BIG)
    s = s + jnp.concatenate([bias] * NSA_GROUP, axis=0)
    e = jnp.exp(s - jnp.max(s, axis=1, keepdims=True))
    o = _dot(e.astype(BF16), v_ref[keys, :].astype(BF16)) / jnp.sum(e, axis=1, keepdims=True)
    for h in range(NSA_GROUP):
        o_ref[:, h * HEAD_DIM:(h + 1) * HEAD_DIM] = o[h * tq:(h + 1) * tq]


def win_prompt(u_head, batch, seq, tq=256):
    tq = min(tq, seq)
    nk = min(seq, WINDOW + tq)
    nq = seq // tq
    ck = OFF_WIN // LANES
    cv = ck + NSA_KV_HEADS
    gw = NSA_GROUP * HEAD_DIM
    assert seq % tq == 0 and WINDOW % tq == 0
    return pl.pallas_call(
        functools.partial(_win_prompt_kernel, tq=tq, nk=nk, seq=seq),
        grid=(batch, NSA_KV_HEADS, nq),
        in_specs=[pl.BlockSpec((tq, gw), lambda b, g, i: (b * nq + i, g)),
                  pl.BlockSpec((seq, LANES), lambda b, g, i: (b, ck + g)),
                  pl.BlockSpec((seq, LANES), lambda b, g, i: (b, cv + g))],
        out_specs=pl.BlockSpec((tq, gw), lambda b, g, i: (b * nq + i, g)),
        out_shape=jax.ShapeDtypeStruct((batch * seq, NSA_W), F32),
        compiler_params=_cp(3),
        name="win_prompt",
    )(u_head, u_head, u_head)


def _nsa_combine_kernel(gate_ref, a_ref, b_ref, c_ref, o_ref):
    gate = _sigmoid(gate_ref[...])
    for h in range(NSA_HEADS):
        sl = slice(h * HEAD_DIM, (h + 1) * HEAD_DIM)
        out = (gate[:, h:h + 1] * a_ref[:, sl] + gate[:, NSA_HEADS + h:NSA_HEADS + h + 1] * b_ref[:, sl]
               + gate[:, 2 * NSA_HEADS + h:2 * NSA_HEADS + h + 1] * c_ref[:, sl])
        o_ref[:, sl] = out.astype(o_ref.dtype)


def nsa_combine(u_head, row0, o_cmp, o_sel, o_win, tm=256):
    n = o_cmp.shape[0]
    tm = min(tm, n)
    rb0 = row0 // tm
    assert row0 % tm == 0 and n % tm == 0
    spec = pl.BlockSpec((tm, NSA_W), lambda i: (i, 0))
    return pl.pallas_call(
        _nsa_combine_kernel,
        grid=(n // tm,),
        in_specs=[pl.BlockSpec((tm, LANES), lambda i: (rb0 + i, OFF_NSA_GATE // LANES)), spec, spec, spec],
        out_specs=spec,
        out_shape=jax.ShapeDtypeStruct((n, NSA_W), BF16 if tm % 16 == 0 else F32),
        compiler_params=_cp(1),
        name="nsa_combine",
    )(u_head, o_cmp, o_sel, o_win)


def _page_rows(page_ref, first, n_rows, stride):
    return page_ref[pl.ds(first, n_rows, stride=stride), :]


def _pad_rows(x, n):
    if x.shape[0] >= n:
        return x
    return jnp.concatenate([x, jnp.zeros((n - x.shape[0], x.shape[1]), x.dtype)], axis=0)


def _cmp1_sample_kernel(pt_ref, *refs, pps, page):
    page_refs = refs[:pps]
    w_ref, pe_ref, o_ref, r_ref = refs[pps:]
    segs = page // CMP_STRIDE
    stride = CMP_STRIDE * 2 * NSA_KV_HEADS
    for kv in range(2):
        for g in range(NSA_KV_HEADS):
            for k in range(pps):
                for l in range(CMP_STRIDE):
                    r_ref[k * segs:(k + 1) * segs, l * LANES:(l + 1) * LANES] = _page_rows(
                        page_refs[k], l * 2 * NSA_KV_HEADS + kv * NSA_KV_HEADS + g, segs, stride)
            o_ref[kv, g] = _segment_products(r_ref[...], w_ref[kv], pe_ref[kv])


def compress_segments_sample(pool, layer, pt_flat, batch, n_pages, wcat, pecat, pps=16):
    page = pool.shape[2] // (2 * NSA_KV_HEADS)
    pps = min(pps, n_pages)
    assert n_pages % pps == 0 and page % CMP_STRIDE == 0
    segs = page // CMP_STRIDE
    in_specs = [pl.BlockSpec((None, None, pool.shape[2], LANES),
                             lambda b, s, pt, k=k: (layer, pt[b * n_pages + s * pps + k], 0, 0)) for k in range(pps)]
    in_specs += [pl.BlockSpec((2, SEG_W, 2 * LANES), lambda b, s, pt: (0, 0, 0)),
                 pl.BlockSpec((2, SUBLANES, SEG_W), lambda b, s, pt: (0, 0, 0))]
    grid_spec = pltpu.PrefetchScalarGridSpec(
        num_scalar_prefetch=1,
        grid=(batch, n_pages // pps),
        in_specs=in_specs,
        out_specs=pl.BlockSpec((None, 2, NSA_KV_HEADS, pps * segs, 2 * LANES), lambda b, s, pt: (b, 0, 0, s, 0)),
        scratch_shapes=[pltpu.VMEM((pps * segs, SEG_W), F32)],
    )
    return pl.pallas_call(
        functools.partial(_cmp1_sample_kernel, pps=pps, page=page),
        grid_spec=grid_spec,
        out_shape=jax.ShapeDtypeStruct((batch, 2, NSA_KV_HEADS, n_pages * segs, 2 * LANES), F32),
        compiler_params=_cp(2),
        name="cmp_segments_sample",
    )(pt_flat, *([pool] * pps), wcat, pecat)


def _sel_sample_kernel(pt_ref, *refs, pps, page, ds, past, nsp):
    q_ref, new_ref, sel_ref, selstep_ref = refs[:4]
    page_refs = refs[4:4 + pps]
    o_ref, m_ref, l_ref, acc_ref = refs[4 + pps:]
    s = pl.program_id(1)
    rows = NSA_GROUP * ds
    keys = pps * page
    bps = keys // SEL_BLOCK
    stride = 2 * NSA_KV_HEADS

    @pl.when(s == 0)
    def _():
        _flash_init(m_ref, l_ref, acc_ref)

    jj = lax.broadcasted_iota(jnp.int32, (bps, keys), 0)
    kk = lax.broadcasted_iota(jnp.int32, (bps, keys), 1)
    expand = jnp.where(jj == kk // SEL_BLOCK, 1.0, 0.0).astype(BF16)
    for g in range(NSA_KV_HEADS):
        q4 = _stack_heads(q_ref[:, g * NSA_GROUP * HEAD_DIM:(g + 1) * NSA_GROUP * HEAD_DIM]).astype(BF16)
        sel4 = jnp.concatenate([sel_ref[g]] * NSA_GROUP, axis=0)
        sel_here = jnp.concatenate([selstep_ref[g]] * NSA_GROUP, axis=0)
        k_all = jnp.concatenate([_page_rows(r, g, page, stride) for r in page_refs], axis=0).astype(BF16)
        v_all = jnp.concatenate([_page_rows(r, NSA_KV_HEADS + g, page, stride) for r in page_refs],
                                axis=0).astype(BF16)
        sc = _dot_nt(q4, k_all) * SCALE
        chosen = _dot(sel_here.astype(BF16), expand)
        _flash_step(sc, chosen > 0.5, v_all, m_ref.at[g], l_ref.at[g], acc_ref.at[g])

        @pl.when(s == pl.num_programs(1) - 1)
        def _():
            k_new = _pad_rows(new_ref[:, g * HEAD_DIM:(g + 1) * HEAD_DIM], LANES).astype(BF16)
            v_new = _pad_rows(new_ref[:, (NSA_KV_HEADS + g) * HEAD_DIM:(NSA_KV_HEADS + g + 1) * HEAD_DIM],
                              LANES).astype(BF16)
            sn = _dot_nt(q4, k_new) * SCALE
            blk = past // SEL_BLOCK
            j = lax.broadcasted_iota(jnp.int32, (rows, LANES), 1)
            t = lax.broadcasted_iota(jnp.int32, (rows, LANES), 0) % ds
            mask = (sel4[:, blk:blk + 1] > 0.5) & (j <= t)
            _flash_step(sn, mask, v_new, m_ref.at[g], l_ref.at[g], acc_ref.at[g])
            o = _flash_result(l_ref.at[g], acc_ref.at[g])
            for h in range(NSA_GROUP):
                c0 = (g * NSA_GROUP + h) * HEAD_DIM
                o_ref[:, c0:c0 + HEAD_DIM] = o[h * ds:(h + 1) * ds]


def sel_sample(u_head, row0, pool, layer, pt_flat, sel, batch, ds, n_pages, pps=32):
    page = pool.shape[2] // (2 * NSA_KV_HEADS)
    pps = min(pps, n_pages)
    past = n_pages * page
    nsp = sel.shape[-1]
    rows = NSA_GROUP * ds
    ns = n_pages // pps
    bps = pps * page // SEL_BLOCK
    assert n_pages % pps == 0 and row0 % ds == 0 and past % SEL_BLOCK == 0 and ds <= SEL_BLOCK
    assert page % SEL_BLOCK == 0
    sel_steps = sel[..., :ns * bps].reshape(batch, NSA_KV_HEADS, ds, ns, bps).transpose(0, 1, 3, 2, 4)
    in_specs = [pl.BlockSpec((ds, NSA_W), lambda b, s, pt: (row0 // ds + b, 0)),
                pl.BlockSpec((ds, KV_W), lambda b, s, pt: (row0 // ds + b, OFF_SEL // KV_W)),
                pl.BlockSpec((None, NSA_KV_HEADS, ds, nsp), lambda b, s, pt: (b, 0, 0, 0)),
                pl.BlockSpec((None, NSA_KV_HEADS, None, ds, bps), lambda b, s, pt: (b, 0, s, 0, 0))]
    in_specs += [pl.BlockSpec((None, None, pool.shape[2], LANES),
                              lambda b, s, pt, k=k: (layer, pt[b * n_pages + s * pps + k], 0, 0)) for k in range(pps)]
    grid_spec = pltpu.PrefetchScalarGridSpec(
        num_scalar_prefetch=1,
        grid=(batch, n_pages // pps),
        in_specs=in_specs,
        out_specs=pl.BlockSpec((ds, NSA_W), lambda b, s, pt: (b, 0)),
        scratch_shapes=[pltpu.VMEM((NSA_KV_HEADS, rows, 1), F32), pltpu.VMEM((NSA_KV_HEADS, rows, 1), F32),
                        pltpu.VMEM((NSA_KV_HEADS, rows, HEAD_DIM), F32)],
    )
    return pl.pallas_call(
        functools.partial(_sel_sample_kernel, pps=pps, page=page, ds=ds, past=past, nsp=nsp),
        grid_spec=grid_spec,
        out_shape=jax.ShapeDtypeStruct((batch * ds, NSA_W), F32),
        compiler_params=_cp(2),
        name="sel_sample",
    )(pt_flat, u_head, u_head, sel, sel_steps, *([pool] * pps))


def _win_sample_kernel(q_ref, new_ref, buf_ref, o_ref, m_ref, l_ref, acc_ref, *, ds, nbuf):
    rows = NSA_GROUP * ds
    stride = 2 * NSA_KV_HEADS
    keys = nbuf + LANES
    i = lax.broadcasted_iota(jnp.int32, (rows, keys), 1)
    t = lax.broadcasted_iota(jnp.int32, (rows, keys), 0) % ds
    d = jnp.where(i < nbuf, t + nbuf - i, t - (i - nbuf))
    mask = (d >= 0) & (d < WINDOW) & (i < nbuf + ds)
    for g in range(NSA_KV_HEADS):
        _flash_init(m_ref, l_ref, acc_ref)
        q4 = _stack_heads(q_ref[:, g * NSA_GROUP * HEAD_DIM:(g + 1) * NSA_GROUP * HEAD_DIM]).astype(BF16)
        k_new = _pad_rows(new_ref[:, g * HEAD_DIM:(g + 1) * HEAD_DIM], LANES)
        v_new = _pad_rows(new_ref[:, (NSA_KV_HEADS + g) * HEAD_DIM:(NSA_KV_HEADS + g + 1) * HEAD_DIM], LANES)
        k_all = jnp.concatenate([_page_rows(buf_ref, g, nbuf, stride), k_new], axis=0).astype(BF16)
        v_all = jnp.concatenate([_page_rows(buf_ref, NSA_KV_HEADS + g, nbuf, stride), v_new], axis=0).astype(BF16)
        _flash_step(_dot_nt(q4, k_all) * SCALE, mask, v_all, m_ref, l_ref, acc_ref)
        o = _flash_result(l_ref, acc_ref)
        for h in range(NSA_GROUP):
            c0 = (g * NSA_GROUP + h) * HEAD_DIM
            o_ref[:, c0:c0 + HEAD_DIM] = o[h * ds:(h + 1) * ds]


def win_sample(u_head, row0, buf, layer, batch, ds):
    nbuf = buf.shape[2] // (2 * NSA_KV_HEADS)
    rows = NSA_GROUP * ds
    return pl.pallas_call(
        functools.partial(_win_sample_kernel, ds=ds, nbuf=nbuf),
        grid=(batch,),
        in_specs=[pl.BlockSpec((ds, NSA_W), lambda b: (row0 // ds + b, 0)),
                  pl.BlockSpec((ds, KV_W), lambda b: (row0 // ds + b, OFF_WIN // KV_W)),
                  pl.BlockSpec((None, None, buf.shape[2], LANES), lambda b: (layer, b, 0, 0))],
        out_specs=pl.BlockSpec((ds, NSA_W), lambda b: (b, 0)),
        out_shape=jax.ShapeDtypeStruct((batch * ds, NSA_W), F32),
        scratch_shapes=[pltpu.VMEM((rows, 1), F32), pltpu.VMEM((rows, 1), F32), pltpu.VMEM((rows, HEAD_DIM), F32)],
        compiler_params=_cp(1),
        name="win_sample",
    )(u_head, u_head, buf)


SB_ROWS = 16


def _sb_sample_kernel(pt_ref, q_ref, kn_ref, vn_ref, pool_hbm, o_ref, acc_ref, carry_ref, kv_ref, sem, *,
                      pps, page, ds, n_pages, layer):
    b = pl.program_id(0)
    stride = 2 * SB_HEADS
    umat = _suffix_matrix()
    q_heads = [_pad_rows(q_ref[:, h * HEAD_DIM:(h + 1) * HEAD_DIM] * SCALE, SB_ROWS).astype(BF16)
               for h in range(SB_HEADS)]
    rows = SB_HEADS * SB_ROWS
    real_row = lax.broadcasted_iota(jnp.int32, (rows, LANES), 0) % SB_ROWS < ds

    def accumulate(k_heads, v_heads, mask):
        z = jnp.concatenate([_dot_nt(q_heads[h], k_heads[h]) for h in range(SB_HEADS)], axis=0)
        a, carry = _sb_weights(z, mask, carry_ref[...], umat)
        carry_ref[...] = carry
        a = a.astype(BF16)
        for h in range(SB_HEADS):
            sl = slice(h * SB_ROWS, (h + 1) * SB_ROWS)
            acc_ref[sl, :] += _dot(a[sl], v_heads[h])
        return jnp.max(jnp.where(real_row, carry, NEG_BIG))

    acc_ref[...] = jnp.zeros_like(acc_ref)
    carry_ref[...] = jnp.zeros_like(carry_ref)
    k_new = [_pad_rows(kn_ref[:, h * HEAD_DIM:(h + 1) * HEAD_DIM], LANES).astype(BF16) for h in range(SB_HEADS)]
    v_new = [_pad_rows(vn_ref[:, h * HEAD_DIM:(h + 1) * HEAD_DIM], LANES).astype(BF16) for h in range(SB_HEADS)]
    j = lax.broadcasted_iota(jnp.int32, (rows, LANES), 1)
    t = lax.broadcasted_iota(jnp.int32, (rows, LANES), 0) % SB_ROWS
    top = accumulate(k_new, v_new, j < t)

    def page_copy(g, k):
        pid = pt_ref[b * n_pages + n_pages - (g + 1) * pps + k]
        return pltpu.make_async_copy(pool_hbm.at[layer, pid], kv_ref.at[k], sem)

    def cond(st):
        return jnp.logical_and(st[0] < n_pages // pps, st[1] > SB_ZERO_LOG)

    def body(st):
        g = st[0]
        for k in range(pps):
            page_copy(g, k).start()
        for k in range(pps):
            page_copy(g, k).wait()
        k_heads = [jnp.concatenate([_page_rows(kv_ref.at[k], h, page, stride) for k in range(pps)],
                                   axis=0).astype(BF16) for h in range(SB_HEADS)]
        v_heads = [jnp.concatenate([_page_rows(kv_ref.at[k], SB_HEADS + h, page, stride) for k in range(pps)],
                                   axis=0).astype(BF16) for h in range(SB_HEADS)]
        return g + 1, accumulate(k_heads, v_heads, None)

    lax.while_loop(cond, body, (0, top))
    for h in range(SB_HEADS):
        o_ref[:, h * HEAD_DIM:(h + 1) * HEAD_DIM] = acc_ref[h * SB_ROWS:h * SB_ROWS + ds, :]


def sb_sample(u_rest, row0, pool, layer, pt_flat, batch, ds, n_pages, pps=2):
    page = pool.shape[2] // (2 * SB_HEADS)
    pps = min(pps, n_pages)
    assert n_pages % pps == 0 and row0 % ds == 0 and ds <= SB_ROWS
    rb = row0 // ds
    grid_spec = pltpu.PrefetchScalarGridSpec(
        num_scalar_prefetch=1,
        grid=(batch,),
        in_specs=[pl.BlockSpec((ds, SB_W), lambda b, pt: (rb + b, R_SB_Q // SB_W)),
                  pl.BlockSpec((ds, SB_W), lambda b, pt: (rb + b, R_SB_K // SB_W)),
                  pl.BlockSpec((ds, SB_W), lambda b, pt: (rb + b, R_SB_V // SB_W)),
                  pl.BlockSpec(memory_space=pl.ANY)],
        out_specs=pl.BlockSpec((ds, SB_W), lambda b, pt: (b, 0)),
        scratch_shapes=[pltpu.VMEM((SB_HEADS * SB_ROWS, HEAD_DIM), F32), pltpu.VMEM((SB_HEADS * SB_ROWS, LANES), F32),
                        pltpu.VMEM((pps, pool.shape[2], LANES), F32), pltpu.SemaphoreType.DMA],
    )
    return pl.pallas_call(
        functools.partial(_sb_sample_kernel, pps=pps, page=page, ds=ds, n_pages=n_pages, layer=layer),
        grid_spec=grid_spec,
        out_shape=jax.ShapeDtypeStruct((batch * ds, SB_W), F32),
        compiler_params=_cp(1),
        name="sb_sample",
    )(pt_flat, u_rest, u_rest, u_rest, pool)


def _router_kernel(x_ref, g_ref, w_ref, b_ref, h_ref, o_ref):
    x = x_ref[...]
    h = x * lax.rsqrt(jnp.mean(x * x, axis=-1, keepdims=True) + NORM_EPS) * g_ref[...]
    h_ref[...] = h
    hh, hm, hl = _split3(h)
    wh, wm, wl = _split3(w_ref[...])
    logits = (_dot(hh, wh) + _dot(hh, wm) + _dot(hm, wh) + _dot(hh, wl) + _dot(hl, wh) + _dot(hm, wm)) + b_ref[...]
    lane = lax.broadcasted_iota(jnp.int32, logits.shape, 1)
    logits = jnp.where(lane < N_EXPERTS, logits, NEG_BIG)
    m1 = jnp.max(logits, axis=1, keepdims=True)
    i1 = jnp.min(jnp.where(logits == m1, lane, LANES), axis=1, keepdims=True)
    rest = jnp.where(lane == i1, NEG_BIG, logits)
    m2 = jnp.max(rest, axis=1, keepdims=True)
    i2 = jnp.min(jnp.where(rest == m2, lane, LANES), axis=1, keepdims=True)
    e = jnp.exp(m2 - m1)
    g1 = 1.0 / (1.0 + e)
    g2 = e / (1.0 + e)
    o_ref[...] = jnp.where(lane == 0, i1.astype(F32), jnp.where(lane == 1, i2.astype(F32),
                           jnp.where(lane == 2, g1, jnp.where(lane == 3, g2, 0.0))))


def moe_router(x, g, rw, rb, tm):
    m, d = x.shape
    w = jnp.zeros((d, LANES), F32).at[:, :N_EXPERTS].set(rw)
    b = jnp.zeros((1, LANES), F32).at[0, :N_EXPERTS].set(rb.astype(F32))
    return pl.pallas_call(
        _router_kernel,
        grid=(m // tm,),
        in_specs=[pl.BlockSpec((tm, d), lambda i: (i, 0)), pl.BlockSpec((1, d), lambda i: (0, 0)),
                  pl.BlockSpec((d, LANES), lambda i: (0, 0)), pl.BlockSpec((1, LANES), lambda i: (0, 0))],
        out_specs=[pl.BlockSpec((tm, d), lambda i: (i, 0)), pl.BlockSpec((tm, LANES), lambda i: (i, 0))],
        out_shape=[jax.ShapeDtypeStruct((m, d), F32), jax.ShapeDtypeStruct((m, LANES), F32)],
        compiler_params=_cp(1),
        name="moe_router",
    )(x, g.reshape(1, d), w, b)


def _row_copy(src_hbm, row, dst_ref, r, sem):
    return pltpu.make_async_copy(src_hbm.at[pl.ds(row, 1), :], dst_ref.at[pl.ds(r, 1), :], sem)


def _gather_kernel(tok_ref, nu_ref, h_hbm, o_ref, buf_ref, sem):
    blk = pl.program_id(0)
    n = buf_ref.shape[0]

    @pl.when(blk < nu_ref[0])
    def _():
        def issue(r, c):
            _row_copy(h_hbm, tok_ref[blk * n + r], buf_ref, r, sem).start()
            return c

        lax.fori_loop(0, n, issue, 0, unroll=8)
        pltpu.make_async_copy(h_hbm.at[pl.ds(0, n), :], buf_ref, sem).wait()
        o_ref[...] = buf_ref[...].astype(o_ref.dtype)

    @pl.when(blk >= nu_ref[0])
    def _():
        o_ref[...] = jnp.zeros_like(o_ref)


def _used_block(i, nu):
    return jnp.minimum(i, nu[0] - 1)


def moe_gather(h, tok_buf, n_used):
    p = tok_buf.shape[0]
    d = h.shape[1]
    grid_spec = pltpu.PrefetchScalarGridSpec(
        num_scalar_prefetch=2,
        grid=(p // MOE_BLOCK,),
        in_specs=[pl.BlockSpec(memory_space=pl.ANY)],
        out_specs=pl.BlockSpec((MOE_BLOCK, d), lambda i, tok, nu: (i, 0)),
        scratch_shapes=[pltpu.VMEM((MOE_BLOCK, d), F32), pltpu.SemaphoreType.DMA],
    )
    return pl.pallas_call(
        _gather_kernel,
        grid_spec=grid_spec,
        out_shape=jax.ShapeDtypeStruct((p, d), BF16),
        compiler_params=_cp(1),
        name="moe_gather",
    )(tok_buf, n_used, h)


def _moe_mm_kernel(be_ref, nu_ref, *refs, n_w, mode):
    a_ref = refs[0]
    w_refs = refs[1:1 + n_w]
    o_ref = refs[1 + n_w]
    wbf_refs = refs[2 + n_w:]
    blk = pl.program_id(1)
    changed = jnp.logical_or(blk == 0, be_ref[blk] != be_ref[jnp.maximum(blk - 1, 0)])

    @pl.when(changed)
    def _():
        for w_ref, wbf_ref in zip(w_refs, wbf_refs):
            wbf_ref[...] = w_ref[...].astype(BF16)

    @pl.when(blk < nu_ref[0])
    def _():
        prods = [_dot(a_ref[...], r[...]) for r in wbf_refs]
        o_ref[...] = _mm_epilogue(mode, prods, ()).astype(o_ref.dtype)

    @pl.when(blk >= nu_ref[0])
    def _():
        o_ref[...] = jnp.zeros_like(o_ref)


def moe_matmul(a, w_list, moe_index, blk_e, n_used, mode, out_dtype, tn=512):
    p, k = a.shape
    n_out = w_list[0].shape[-1]
    tn = _tile(n_out, tn)
    grid_spec = pltpu.PrefetchScalarGridSpec(
        num_scalar_prefetch=2,
        grid=(n_out // tn, p // MOE_BLOCK),
        in_specs=[pl.BlockSpec((MOE_BLOCK, k), lambda j, i, be, nu: (_used_block(i, nu), 0))]
        + [pl.BlockSpec((None, None, k, tn), lambda j, i, be, nu: (moe_index, be[i], 0, j)) for _ in w_list],
        out_specs=pl.BlockSpec((MOE_BLOCK, tn), lambda j, i, be, nu: (i, j)),
        scratch_shapes=[pltpu.VMEM((k, tn), BF16) for _ in w_list],
    )
    return pl.pallas_call(
        functools.partial(_moe_mm_kernel, n_w=len(w_list), mode=mode),
        grid_spec=grid_spec,
        out_shape=jax.ShapeDtypeStruct((p, n_out), out_dtype),
        compiler_params=_cp(2),
        name="moe_mm_" + mode,
    )(blk_e, n_used, a, *w_list)


def _moe_combine_kernel(pos_ref, y_hbm, x_ref, gate_ref, gain_ref, o_ref, buf_ref, sem, *, tm, norm):
    i = pl.program_id(0)

    def issue(r, c):
        for k in range(TOP_K):
            _row_copy(y_hbm, pos_ref[(i * tm + r) * TOP_K + k], buf_ref.at[k], r, sem).start()
        return c

    lax.fori_loop(0, tm, issue, 0, unroll=8)
    for k in range(TOP_K):
        pltpu.make_async_copy(y_hbm.at[pl.ds(0, tm), :], buf_ref.at[k], sem).wait()
    gate = gate_ref[...]
    out = x_ref[...]
    y = gate[:, 2:3] * buf_ref[0]
    for k in range(1, TOP_K):
        y = y + gate[:, 2 + k:3 + k] * buf_ref[k]
    out = out + y
    if norm:
        out = out * lax.rsqrt(jnp.mean(out * out, axis=-1, keepdims=True) + NORM_EPS) * gain_ref[...]
    o_ref[...] = out


def moe_combine(x, yb, pos, gates, gain=None, tm=256):
    m, d = x.shape
    norm = gain is not None
    gain = jnp.ones((d,), F32) if gain is None else gain
    grid_spec = pltpu.PrefetchScalarGridSpec(
        num_scalar_prefetch=1,
        grid=(m // tm,),
        in_specs=[pl.BlockSpec(memory_space=pl.ANY),
                  pl.BlockSpec((tm, d), lambda i, pos: (i, 0)),
                  pl.BlockSpec((tm, LANES), lambda i, pos: (i, 0)),
                  pl.BlockSpec((1, d), lambda i, pos: (0, 0))],
        out_specs=pl.BlockSpec((tm, d), lambda i, pos: (i, 0)),
        scratch_shapes=[pltpu.VMEM((TOP_K, tm, d), F32), pltpu.SemaphoreType.DMA],
    )
    return pl.pallas_call(
        functools.partial(_moe_combine_kernel, tm=tm, norm=norm),
        grid_spec=grid_spec,
        out_shape=jax.ShapeDtypeStruct((m, d), F32),
        compiler_params=_cp(1),
        name="moe_combine",
    )(pos, yb, x, gates, gain.reshape(1, d).astype(F32))


def moe_layer(x, g, rw, rb, w1, w3, w2, moe_index, n_tok, tm, out_gain=None):
    m, d = x.shape
    h, route = moe_router(x, g, rw, rb, tm)
    top_i = route[:n_tok, :TOP_K].astype(jnp.int32)
    a = n_tok * TOP_K
    e_flat = top_i.reshape(a)
    order = jnp.argsort(e_flat)
    e_s = e_flat[order]
    tok_s = (order // TOP_K).astype(jnp.int32)
    counts = jnp.bincount(e_flat, length=N_EXPERTS)
    start = jnp.cumsum(counts) - counts
    padded = (counts + MOE_BLOCK - 1) // MOE_BLOCK * MOE_BLOCK
    pend = jnp.cumsum(padded)
    pstart = pend - padded
    dest = (pstart[e_s] + jnp.arange(a) - start[e_s]).astype(jnp.int32)
    nb = -(-a // MOE_BLOCK) + N_EXPERTS
    p = nb * MOE_BLOCK
    n_used = (pend[-1] // MOE_BLOCK).astype(jnp.int32)
    blk_e = jnp.minimum(jnp.searchsorted(pend, jnp.arange(nb) * MOE_BLOCK, side="right"), N_EXPERTS - 1)
    slot = jnp.arange(p)
    slot_e = blk_e[slot // MOE_BLOCK]
    local = slot - pstart[slot_e]
    tok_buf = jnp.where(local < counts[slot_e], tok_s[jnp.clip(start[slot_e] + local, 0, a - 1)], n_tok)
    tok_buf = tok_buf.astype(jnp.int32)
    blk_e = jnp.where(jnp.arange(nb) < n_used, blk_e, blk_e[n_used - 1]).astype(jnp.int32)
    pos = jnp.concatenate([dest[jnp.argsort(order)], jnp.zeros((m * TOP_K - a,), jnp.int32)])
    row_ok = (jnp.arange(m) < n_tok)[:, None]
    gates = jnp.where(row_ok, route, 0.0)
    n_used = n_used.reshape(1)
    xs = moe_gather(h, tok_buf, n_used)
    act = moe_matmul(xs, [w1, w3], moe_index, blk_e, n_used, "swiglu", BF16)
    yb = moe_matmul(act, [w2], moe_index, blk_e, n_used, "plain", F32)
    return moe_combine(x, yb, pos, gates, out_gain)


TM = 512


def _row_tile(m, pref=1152):
    t = (min(pref, m) // 16) * 16
    while m % t:
        t -= 16
    return t


def _slab(parts, m, dtype):
    rows = sum(p.shape[0] for p in parts)
    parts = [p.astype(dtype) for p in parts]
    return jnp.concatenate(parts + [jnp.zeros((m - rows, parts[0].shape[1]), dtype)], axis=0)


def kernel(x_prompt, x_sample, cache_nsa_cmp_kv, cache_nsa_sel_kv, cache_sb_kv, cache_nsa_win_kv, state_hgrn,
           page_table, attn_norm, w_in, cmp_pe_k, cmp_w1_k, cmp_w2_k, cmp_pe_v, cmp_w1_v, cmp_w2_v, hg_lb_logits,
           hg_norm, w_br_nsa, w_br_sb, w_br_hg, w_out, ffn_norm, ffn_w1, ffn_w3, ffn_w2, router_w, router_b,
           moe_w1, moe_w3, moe_w2, final_norm):
    bsz, seq, d = x_prompt.shape
    db, ds, _ = x_sample.shape
    depth = attn_norm.shape[0]
    n_p, n_s = bsz * seq, db * ds
    n_tok = n_p + n_s
    m = -(-(n_tok + 1) // TM) * TM
    n_pool, page = cache_nsa_cmp_kv.shape[1:3]
    n_pages = page_table.shape[1]
    past = n_pages * page
    assert past % CMP_STRIDE == 0 and ds < CMP_STRIDE and seq % CMP_STRIDE == 0

    x = _slab([x_prompt.reshape(n_p, d), x_sample.reshape(n_s, d)], m, F32)
    pt_flat = page_table.reshape(-1).astype(jnp.int32)
    cmp_pool = cache_nsa_cmp_kv.reshape(depth, n_pool, page * 2 * NSA_KV_HEADS, HEAD_DIM)
    sel_pool = cache_nsa_sel_kv.reshape(depth, n_pool, page * 2 * NSA_KV_HEADS, HEAD_DIM)
    sb_pool = cache_sb_kv.reshape(depth, n_pool, page * 2 * SB_HEADS, HEAD_DIM)
    nbuf = cache_nsa_win_kv.shape[2]
    win_buf = cache_nsa_win_kv.reshape(depth, db, nbuf * 2 * NSA_KV_HEADS, HEAD_DIM)
    rest_cols = w_in.shape[-1] - REST_START
    tmm = _row_tile(m)
    w_in_t = jnp.swapaxes(w_in, 1, 2)
    lb_all = jnp.cumsum(jax.nn.softmax(hg_lb_logits.astype(F32), axis=0), axis=0)
    kvs = (2, NSA_KV_HEADS, HEAD_DIM)

    states = []
    for l in range(depth):
        h = rmsnorm(x, attn_norm[l], BF16, TM)
        u_head = inproj_t(h, w_in_t, l, 0, HEAD_COLS, tmm, HEAD_COLS // 3)
        u_rest = inproj_t(h, w_in_t, l, REST_START, rest_cols, tmm, 1024)

        wk, pk = compress_params(cmp_pe_k[l], cmp_w1_k[l])
        wv, pv = compress_params(cmp_pe_v[l], cmp_w1_v[l])
        wcat, pecat = jnp.stack([wk, wv]), jnp.stack([pk, pv])
        w2s = jnp.stack([cmp_w2_k[l], cmp_w2_v[l]])
        lb = (lb_all[l] - lb_all[0]).reshape(HG_HEADS, HEAD_DIM)
        lbp = hgrn_lb_params(lb)

        kvc = compress_finish(compress_segments_prompt(u_head, bsz, seq, wcat, pecat), w2s)
        o_cmp, sel = cmp_select(u_head, 0, bsz, seq, kvc, seq // CMP_STRIDE - 1, -(-seq // SEL_BLOCK), 0)
        o_sel = sel_prompt(u_head, bsz, seq, sel)
        o_win = win_prompt(u_head, bsz, seq)
        nsa_p = nsa_combine(u_head, 0, o_cmp, o_sel, o_win)
        sb_p = sb_prompt(u_rest, bsz, seq)
        hg_p, s_p = hgrn(u_rest, 0, bsz, seq, lbp, hg_norm[l], jnp.zeros((bsz, HG_HEADS, HEAD_DIM, HEAD_DIM), F32))

        kvc_s = compress_finish(compress_segments_sample(cmp_pool, l, pt_flat, db, n_pages, wcat, pecat), w2s)
        o_cmp_s, sel_s = cmp_select(u_head, n_p, db, ds, kvc_s, past // CMP_STRIDE - 1,
                                    -(-(past + ds) // SEL_BLOCK), past, tq=ds)
        o_sel_s = sel_sample(u_head, n_p, sel_pool, l, pt_flat, sel_s, db, ds, n_pages)
        o_win_s = win_sample(u_head, n_p, win_buf, l, db, ds)
        nsa_s = nsa_combine(u_head, n_p, o_cmp_s, o_sel_s, o_win_s)
        sb_s = sb_sample(u_rest, n_p, sb_pool, l, pt_flat, db, ds, n_pages)
        hg_s, s_s = hgrn(u_rest, n_p, db, ds, lbp, hg_norm[l], state_hgrn[l])

        branches = [(nsa_p.astype(BF16), nsa_s.astype(BF16)), (sb_p.astype(BF16), sb_s.astype(BF16)),
                    (hg_p.astype(BF16), hg_s.astype(BF16))]
        merged = matmul(branches, [w_br_nsa, w_br_sb, w_br_hg], l, 0, d, "merge", BF16,
                        extras=[(u_rest, R_MERGE), (u_rest, R_MERGE + d), (u_rest, R_MERGE + 2 * d)], tm=TM, m=m)
        x = matmul([merged], [w_out], l, 0, d, "residual", F32, extras=[(x, 0)], tm=tmm)

        i = l // 2
        if l % 2 == 0:
            h2 = rmsnorm(x, ffn_norm[l], BF16, TM)
            act = matmul([h2], [ffn_w1, ffn_w3], i, 0, ffn_w1.shape[-1], "swiglu", BF16, tm=tmm)
            x = matmul([act], [ffn_w2], i, 0, d, "residual", F32, extras=[(x, 0)])
        else:
            x = moe_layer(x, ffn_norm[l], router_w[i], router_b[i], moe_w1, moe_w3, moe_w2, i, n_tok, TM,
                          out_gain=final_norm if l == depth - 1 else None)

        def head_cols(off, r0, r1, lead):
            return u_head[r0:r1, off:off + KV_W].reshape(lead + kvs)

        win_p = head_cols(OFF_WIN, 0, n_p, (bsz, seq))[:, seq - min(WINDOW, seq):]
        win_s = jnp.concatenate([cache_nsa_win_kv[l], head_cols(OFF_WIN, n_p, n_tok, (db, ds))], axis=1)[:, ds:]
        states.append((
            head_cols(OFF_CMP, 0, n_p, (bsz, seq)), head_cols(OFF_SEL, 0, n_p, (bsz, seq)), win_p,
            u_rest[:n_p, R_SB_K:R_SB_K + 2 * SB_W].reshape(bsz, seq, 2, SB_HEADS, HEAD_DIM), s_p,
            head_cols(OFF_CMP, n_p, n_tok, (db, ds)), head_cols(OFF_SEL, n_p, n_tok, (db, ds)), win_s,
            u_rest[n_p:n_tok, R_SB_K:R_SB_K + 2 * SB_W].reshape(db, ds, 2, SB_HEADS, HEAD_DIM), s_s))

    y = x if depth % 2 == 0 else rmsnorm(x, final_norm, F32, TM)
    stacked = [jnp.stack([st[i] for st in states]) for i in range(10)]
    return (y[:n_p].reshape(bsz, seq, d), y[n_p:n_tok].reshape(db, ds, d), *stacked)
```

```python
import functools

import jax
import jax.numpy as jnp
import numpy as np
from jax import lax
from jax.experimental import pallas as pl
from jax.experimental.pallas import tpu as pltpu

F32 = jnp.float32
BF16 = jnp.bfloat16

HEAD_DIM = 128
SCALE = HEAD_DIM ** -0.5
NSA_HEADS = 8
NSA_KV_HEADS = 2
NSA_GROUP = NSA_HEADS // NSA_KV_HEADS
CMP_LEN = 32
CMP_STRIDE = 16
SEL_BLOCK = 64
SEL_TOPK = 16
WINDOW = 512
SB_HEADS = 8
HG_HEADS = 8
HG_CHUNK = 64
HG_SUB = 16
HG_MIN_CHUNK = 16
N_EXPERTS = 8
TOP_K = 2
MOE_BLOCK = 512
NORM_EPS = 1e-6
NEG_BIG = -1e30

LANES = 128
SUBLANES = 8
VMEM_LIMIT = 56 * 1024 * 1024

NSA_W = NSA_HEADS * HEAD_DIM
KV_W = 2 * NSA_KV_HEADS * HEAD_DIM
OFF_CMP = NSA_W
OFF_SEL = OFF_CMP + KV_W
OFF_WIN = OFF_SEL + KV_W
OFF_NSA_GATE = OFF_WIN + KV_W
N_GATE = 3 * NSA_HEADS
HEAD_COLS = OFF_NSA_GATE + LANES
REST_START = OFF_NSA_GATE + N_GATE
SB_W = SB_HEADS * HEAD_DIM
HG_W = HG_HEADS * HEAD_DIM
R_SB_Q = 0
R_SB_K = R_SB_Q + SB_W
R_SB_V = R_SB_K + SB_W
R_HG_Q = R_SB_V + SB_W
R_HG_F = R_HG_Q + HG_W
R_HG_I = R_HG_F + HG_W
R_HG_G = R_HG_I + HG_W
R_MERGE = R_HG_G + HG_W


def _cp(n_axes, vmem=VMEM_LIMIT):
    return pltpu.CompilerParams(dimension_semantics=("arbitrary",) * n_axes, vmem_limit_bytes=vmem)


def _tile(n, pref, quantum=LANES):
    if n <= pref:
        return n
    t = (pref // quantum) * quantum
    while t > quantum and n % t:
        t -= quantum
    assert n % t == 0, (n, pref)
    return t


def _dot(a, b):
    return jnp.dot(a, b, preferred_element_type=F32)


def _dot_nt(a, b):
    return lax.dot_general(a, b, (((1,), (1,)), ((), ())), preferred_element_type=F32)


def _dot_tn(a, b):
    return lax.dot_general(a, b, (((0,), (0,)), ((), ())), preferred_element_type=F32)


def _split3(x):
    hi = x.astype(BF16)
    r = x - hi.astype(F32)
    mid = r.astype(BF16)
    lo = (r - mid.astype(F32)).astype(BF16)
    return hi, mid, lo


def _split2(x):
    hi = x.astype(BF16)
    return hi, (x - hi.astype(F32)).astype(BF16)


def _sigmoid(x):
    return 1.0 / (1.0 + jnp.exp(-x))


def _silu(x):
    return x * _sigmoid(x)


def _log_sigmoid(x):
    return jnp.minimum(x, 0.0) - jnp.log1p(jnp.exp(-jnp.abs(x)))


def _rmsnorm_kernel(x_ref, g_ref, o_ref):
    x = x_ref[...]
    y = x * lax.rsqrt(jnp.mean(x * x, axis=-1, keepdims=True) + NORM_EPS)
    o_ref[...] = (y * g_ref[...]).astype(o_ref.dtype)


def rmsnorm(x, g, out_dtype, tm):
    m, d = x.shape
    return pl.pallas_call(
        _rmsnorm_kernel,
        grid=(m // tm,),
        in_specs=[pl.BlockSpec((tm, d), lambda i: (i, 0)), pl.BlockSpec((1, d), lambda i: (0, 0))],
        out_specs=pl.BlockSpec((tm, d), lambda i: (i, 0)),
        out_shape=jax.ShapeDtypeStruct((m, d), out_dtype),
        compiler_params=_cp(1),
        name="rmsnorm",
    )(x, g.reshape(1, d))


def _mm_epilogue(mode, prods, x_refs):
    if mode == "plain":
        return prods[0]
    if mode == "merge":
        out = _sigmoid(x_refs[0][...]) * prods[0]
        for x_ref, p in zip(x_refs[1:], prods[1:]):
            out = out + _sigmoid(x_ref[...]) * p
        return out
    if mode == "residual":
        return x_refs[0][...] + prods[0]
    return _silu(prods[0]) * prods[1]


def _mm_kernel(*refs, n_a, n_w, n_extra, mode, cast, main_tiles):
    n_in = n_a if main_tiles is None else 2 * n_a
    a_in, rest = refs[:n_in], refs[n_in:]
    if main_tiles is None:
        a_vals = [r[...] for r in a_in]
    else:
        in_main = pl.program_id(1) < main_tiles
        a_vals = [jnp.where(in_main, a_in[2 * t][...], a_in[2 * t + 1][...]) for t in range(n_a)]
    w_refs = rest[:n_w]
    x_refs = rest[n_w:n_w + n_extra]
    o_ref = rest[n_w + n_extra]
    wbf_refs = rest[n_w + n_extra + 1:]

    if cast:
        @pl.when(pl.program_id(1) == 0)
        def _():
            for w_ref, wbf_ref in zip(w_refs, wbf_refs):
                wbf_ref[...] = w_ref[...].astype(BF16)
        ws = [r[...] for r in wbf_refs]
    else:
        ws = [r[...] for r in w_refs]

    prods = [_dot(a_vals[min(i, n_a - 1)], w) for i, w in enumerate(ws)]
    o_ref[...] = _mm_epilogue(mode, prods, x_refs).astype(o_ref.dtype)


def matmul(a_list, w_list, w_index, col0, n_out, mode, out_dtype, extras=(), tm=512, tn=512, m=None):
    split = isinstance(a_list[0], tuple)
    m = m or a_list[0].shape[0]
    tn = _tile(n_out, tn)
    assert col0 % tn == 0 and m % tm == 0
    cast = w_list[0].dtype != BF16
    in_specs, args, scratch = [], [], []
    main_tiles = None
    for a in a_list:
        if split:
            main, tail = a
            main_tiles = main.shape[0] // tm
            assert main.shape[0] % tm == 0 and m == (main_tiles + 1) * tm and tail.shape[0] <= tm
            tail = jnp.concatenate([tail, jnp.zeros((tm - tail.shape[0], tail.shape[1]), tail.dtype)], axis=0)
            in_specs.append(pl.BlockSpec((tm, main.shape[1]), lambda j, i, n=main_tiles: (jnp.minimum(i, n - 1), 0)))
            in_specs.append(pl.BlockSpec((tm, main.shape[1]), lambda j, i: (0, 0)))
            args += [main, tail]
            continue
        in_specs.append(pl.BlockSpec((tm, a.shape[1]), lambda j, i: (i, 0)))
        args.append(a)
    for w in w_list:
        k = w.shape[-2]
        if w.ndim == 3:
            in_specs.append(pl.BlockSpec((None, k, tn), lambda j, i: (w_index, 0, j + col0 // tn)))
        else:
            in_specs.append(pl.BlockSpec((k, tn), lambda j, i: (0, j + col0 // tn)))
        args.append(w)
        if cast:
            scratch.append(pltpu.VMEM((k, tn), BF16))
    for x, off in extras:
        assert off % tn == 0
        in_specs.append(pl.BlockSpec((tm, tn), lambda j, i, off=off: (i, j + off // tn)))
        args.append(x)
    return pl.pallas_call(
        functools.partial(_mm_kernel, n_a=len(a_list), n_w=len(w_list), n_extra=len(extras), mode=mode, cast=cast,
                          main_tiles=main_tiles),
        grid=(n_out // tn, m // tm),
        in_specs=in_specs,
        out_specs=pl.BlockSpec((tm, tn), lambda j, i: (i, j)),
        out_shape=jax.ShapeDtypeStruct((m, n_out), out_dtype),
        scratch_shapes=scratch,
        compiler_params=_cp(2),
        name="mm_" + mode,
    )(*args)


def _inproj_t_kernel(a_ref, wt_ref, o_ref, wbf_ref):
    @pl.when(pl.program_id(1) == 0)
    def _():
        wbf_ref[...] = wt_ref[0].T.astype(BF16)

    o_ref[...] = _dot(a_ref[...], wbf_ref[...])


def inproj_t(a, w_t, layer, row0, n_out, tm, tn):
    m, k = a.shape
    tn = _tile(n_out, tn)
    assert m % tm == 0 and row0 % SUBLANES == 0
    return pl.pallas_call(
        _inproj_t_kernel,
        grid=(n_out // tn, m // tm),
        in_specs=[pl.BlockSpec((tm, k), lambda j, i: (i, 0)),
                  pl.BlockSpec((pl.Element(1), pl.Element(tn), pl.Element(k)),
                               lambda j, i: (layer, pl.multiple_of(row0 + j * tn, SUBLANES), 0))],
        out_specs=pl.BlockSpec((tm, tn), lambda j, i: (i, j)),
        out_shape=jax.ShapeDtypeStruct((m, n_out), F32),
        scratch_shapes=[pltpu.VMEM((k, tn), BF16)],
        compiler_params=_cp(2),
        name="inproj_t",
    )(a, w_t)


def _inproj_rest_kernel(a_ref, wa_ref, wb_ref, o_ref, wbf_ref, *, shift, tn):
    @pl.when(pl.program_id(1) == 0)
    def _():
        w = jnp.concatenate([wa_ref[...], wb_ref[...]], axis=1)
        wbf_ref[...] = w[:, shift:shift + tn].astype(BF16)

    o_ref[...] = _dot(a_ref[...], wbf_ref[...])


def inproj_rest(a, w_in, layer, n_out, tm, tn=512):
    m, k = a.shape
    shift = REST_START % LANES
    base = REST_START - shift
    tn = _tile(n_out, tn)
    assert base % tn == 0 and m % tm == 0 and REST_START + n_out == w_in.shape[-1]
    return pl.pallas_call(
        functools.partial(_inproj_rest_kernel, shift=shift, tn=tn),
        grid=(n_out // tn, m // tm),
        in_specs=[pl.BlockSpec((tm, k), lambda j, i: (i, 0)),
                  pl.BlockSpec((None, k, tn), lambda j, i: (layer, 0, base // tn + j)),
                  pl.BlockSpec((None, k, LANES), lambda j, i: (layer, 0, (base + (j + 1) * tn) // LANES))],
        out_specs=pl.BlockSpec((tm, tn), lambda j, i: (i, j)),
        out_shape=jax.ShapeDtypeStruct((m, n_out), F32),
        scratch_shapes=[pltpu.VMEM((k, tn), BF16)],
        compiler_params=_cp(2),
        name="inproj_rest",
    )(a, w_in, w_in)


def _suffix_matrix():
    j = lax.broadcasted_iota(jnp.int32, (LANES, 2 * LANES), 0)
    s = lax.broadcasted_iota(jnp.int32, (LANES, 2 * LANES), 1)
    return jnp.where((j > s) | (s >= LANES), 1.0, 0.0).astype(BF16)


def _sb_weights(z, mask, carry, umat):
    t = jnp.log(1.0 + jnp.exp(-jnp.abs(z)))
    ls_pos = jnp.minimum(z, 0.0) - t
    c = ls_pos - z
    if mask is not None:
        c = jnp.where(mask, c, 0.0)
    n_sub = z.shape[1] // LANES
    pieces = [None] * n_sub
    for sb in reversed(range(n_sub)):
        sl = slice(sb * LANES, (sb + 1) * LANES)
        hi, lo = _split2(c[:, sl])
        r = _dot(hi, umat) + _dot(lo, umat)
        w = jnp.exp(ls_pos[:, sl] + (carry + r[:, :LANES]))
        pieces[sb] = w if mask is None else jnp.where(mask[:, sl], w, 0.0)
        carry = carry + r[:, LANES:]
    a = pieces[0] if n_sub == 1 else jnp.concatenate(pieces, axis=1)
    return a, carry


SB_ZERO_LOG = -104.0


def _sb_prompt_kernel(q_ref, k_ref, v_ref, o_ref, acc_ref, carry_ref, *, t, hb):
    qi = pl.program_id(2)
    umat = _suffix_matrix()
    qs = [(q_ref[:, j * HEAD_DIM:(j + 1) * HEAD_DIM] * SCALE).astype(BF16) for j in range(hb)]
    row = lax.broadcasted_iota(jnp.int32, (t, t), 0)
    col = lax.broadcasted_iota(jnp.int32, (t, t), 1)

    def tile(kb, mask):
        rows = pl.ds(pl.multiple_of(kb * t, t), t)
        top = None
        for j in range(hb):
            cols = slice(j * HEAD_DIM, (j + 1) * HEAD_DIM)
            z = _dot_nt(qs[j], k_ref[rows, cols].astype(BF16))
            a, carry = _sb_weights(z, mask, carry_ref[j], umat)
            acc_ref[j] += _dot(a.astype(BF16), v_ref[rows, cols].astype(BF16))
            carry_ref[j] = carry
            top = jnp.max(carry) if top is None else jnp.maximum(top, jnp.max(carry))
        return top

    acc_ref[...] = jnp.zeros_like(acc_ref)
    carry_ref[...] = jnp.zeros_like(carry_ref)
    top = tile(qi, col < row)

    def cond(st):
        return jnp.logical_and(st[0] >= 0, st[1] > SB_ZERO_LOG)

    def body(st):
        return st[0] - 1, tile(st[0], None)

    lax.while_loop(cond, body, (qi - 1, top))
    for j in range(hb):
        o_ref[:, j * HEAD_DIM:(j + 1) * HEAD_DIM] = acc_ref[j].astype(o_ref.dtype)


def sb_prompt(u_rest, batch, seq, t=256, hb=4):
    t = min(t, seq)
    nq = seq // t
    bw = hb * HEAD_DIM
    cq, ck, cv = R_SB_Q // bw, R_SB_K // bw, R_SB_V // bw
    assert SB_HEADS % hb == 0
    return pl.pallas_call(
        functools.partial(_sb_prompt_kernel, t=t, hb=hb),
        grid=(batch, SB_HEADS // hb, nq),
        in_specs=[pl.BlockSpec((t, bw), lambda b, h, i: (b * nq + i, cq + h)),
                  pl.BlockSpec((seq, bw), lambda b, h, i: (b, ck + h)),
                  pl.BlockSpec((seq, bw), lambda b, h, i: (b, cv + h))],
        out_specs=pl.BlockSpec((t, bw), lambda b, h, i: (b * nq + i, h)),
        out_shape=jax.ShapeDtypeStruct((batch * seq, SB_W), BF16),
        scratch_shapes=[pltpu.VMEM((hb, t, LANES), F32), pltpu.VMEM((hb, t, LANES), F32)],
        compiler_params=_cp(3),
        name="sb_prompt",
    )(u_rest, u_rest, u_rest)


def _hgrn_chunk(qr, fr, v, gr, lbp, norm_w, st, c, sub, c_real):
    log_lb, log_1m_lb, one_m_lb = lbp[0:1], lbp[1:2], lbp[2:3]
    q = _silu(qr)
    k = one_m_lb * _sigmoid(-fr)
    bb = log_1m_lb + _log_sigmoid(fr)
    mx = jnp.maximum(log_lb, bb)
    logf = mx + jnp.log1p(jnp.exp(-jnp.abs(log_lb - bb)))
    row = lax.broadcasted_iota(jnp.int32, (c, c), 0)
    col = lax.broadcasted_iota(jnp.int32, (c, c), 1)
    tri = jnp.where(row >= col, 1.0, 0.0).astype(BF16)
    hi, mid, lo = _split3(logf)
    b = _dot(tri, hi) + _dot(tri, mid) + _dot(tri, lo)
    o = _dot_nt((q * jnp.exp(b)).astype(BF16), st.astype(BF16))
    ridx = lax.broadcasted_iota(jnp.int32, (c, HEAD_DIM), 0)
    lane = lax.broadcasted_iota(jnp.int32, (sub, c), 1)
    att_rows = []
    for i in range(c // sub):
        r0 = i * sub
        b_i = b[r0:r0 + sub]
        q_i = q[r0:r0 + sub]
        att_i = jnp.zeros((sub, c), F32)
        if i > 0:
            rho = b_i[0:1]
            earlier = ridx < r0
            k_dec = jnp.where(earlier, k * jnp.exp(jnp.where(earlier, rho - b, 0.0)), 0.0)
            att_i = _dot_nt((q_i * jnp.exp(b_i - rho)).astype(BF16), k_dec.astype(BF16))
        trow = lax.broadcasted_iota(jnp.int32, (sub, 1), 0)
        for s in range(sub):
            d = q_i * jnp.exp(jnp.where(trow >= s, b_i - b_i[s:s + 1], 0.0)) * k[r0 + s:r0 + s + 1]
            colsum = jnp.sum(d, axis=1, keepdims=True)
            att_i = att_i + jnp.where((lane == r0 + s) & (trow >= s), colsum, 0.0)
        att_rows.append(att_i)
    att = att_rows[0] if len(att_rows) == 1 else jnp.concatenate(att_rows, axis=0)
    o = o + _dot(att.astype(BF16), v.astype(BF16))
    b_end = b[c_real - 1:c_real]
    real = ridx < c_real
    k_end = jnp.where(real, k * jnp.exp(jnp.where(real, b_end - b, 0.0)), 0.0)
    st = st * jnp.exp(b_end) + _dot_tn(v.astype(BF16), k_end.astype(BF16))
    o = o * lax.rsqrt(jnp.mean(o * o, axis=-1, keepdims=True) + NORM_EPS) * norm_w
    return o * _silu(gr), st


def _hgrn_kernel(q_ref, f_ref, i_ref, g_ref, lbp_ref, nw_ref, s0_ref, o_ref, s_out_ref, st_ref, *,
                 c, sub, n_chunks, hb):
    t = pl.program_id(2)

    @pl.when(t == 0)
    def _():
        for j in range(hb):
            st_ref[j] = s0_ref[j].T

    nw = nw_ref[...]
    in_refs = (q_ref, f_ref, i_ref, g_ref)

    if c < HG_MIN_CHUNK:
        pad = jnp.zeros((HG_MIN_CHUNK - c, HEAD_DIM), F32)
        for j in range(hb):
            cols = slice(j * HEAD_DIM, (j + 1) * HEAD_DIM)
            ins = [jnp.concatenate([r[:, cols], pad], axis=0) for r in in_refs]
            o, st = _hgrn_chunk(*ins, lbp_ref[j], nw, st_ref[j], HG_MIN_CHUNK, HG_SUB, c)
            st_ref[j] = st
            o_ref[:, cols] = o[:c].astype(o_ref.dtype)
    else:
        def body(ci, carry):
            rows = pl.ds(pl.multiple_of(ci * c, c), c)
            for j in range(hb):
                cols = slice(j * HEAD_DIM, (j + 1) * HEAD_DIM)
                o, st = _hgrn_chunk(*[r[rows, cols] for r in in_refs], lbp_ref[j], nw, st_ref[j], c, sub, c)
                st_ref[j] = st
                o_ref[rows, cols] = o.astype(o_ref.dtype)
            return carry

        lax.fori_loop(0, n_chunks, body, 0)

    @pl.when(t == pl.num_programs(2) - 1)
    def _():
        for j in range(hb):
            s_out_ref[j] = st_ref[j].T


def hgrn_lb_params(lb):
    rows = jnp.stack([jnp.log(lb), jnp.log1p(-lb), 1.0 - lb], axis=1)
    return jnp.concatenate([rows, jnp.zeros((lb.shape[0], SUBLANES - 3, lb.shape[1]), F32)], axis=1)


def hgrn(u_rest, row0, batch, seq, lbp, norm_w, s0, tl=512, hb=8):
    c = min(HG_CHUNK, seq)
    sub = min(HG_SUB, c)
    tl = min(tl, seq)
    nt = seq // tl
    rb0 = row0 // tl
    bw = hb * HEAD_DIM
    assert row0 % tl == 0 and seq % tl == 0 and tl % c == 0 and (c % HG_SUB == 0 or nt == 1) and HG_HEADS % hb == 0

    def col(off):
        return pl.BlockSpec((tl, bw), lambda b, h, t, off=off: (rb0 + b * nt + t, off // bw + h))

    o, s_out = pl.pallas_call(
        functools.partial(_hgrn_kernel, c=c, sub=sub, n_chunks=tl // c, hb=hb),
        grid=(batch, HG_HEADS // hb, nt),
        in_specs=[col(R_HG_Q), col(R_HG_F), col(R_HG_I), col(R_HG_G),
                  pl.BlockSpec((hb, SUBLANES, LANES), lambda b, h, t: (h, 0, 0)),
                  pl.BlockSpec((1, LANES), lambda b, h, t: (0, 0)),
                  pl.BlockSpec((None, hb, HEAD_DIM, HEAD_DIM), lambda b, h, t: (b, h, 0, 0))],
        out_specs=[pl.BlockSpec((tl, bw), lambda b, h, t: (b * nt + t, h)),
                   pl.BlockSpec((None, hb, HEAD_DIM, HEAD_DIM), lambda b, h, t: (b, h, 0, 0))],
        out_shape=[jax.ShapeDtypeStruct((batch * seq, HG_W), BF16 if tl % 16 == 0 else F32),
                   jax.ShapeDtypeStruct((batch, HG_HEADS, HEAD_DIM, HEAD_DIM), F32)],
        scratch_shapes=[pltpu.VMEM((hb, HEAD_DIM, HEAD_DIM), F32)],
        compiler_params=_cp(3),
        name="hgrn",
    )(u_rest, u_rest, u_rest, u_rest, lbp, norm_w.reshape(1, LANES), s0)
    return o, s_out


SEG_W = CMP_STRIDE * HEAD_DIM


def compress_params(pe, w1):
    w = jnp.concatenate([w1[:SEG_W], w1[SEG_W:]], axis=1).astype(BF16)
    rows = jnp.stack([pe[:CMP_STRIDE].reshape(SEG_W), pe[CMP_STRIDE:].reshape(SEG_W)])
    return w, jnp.concatenate([rows, jnp.zeros((SUBLANES - 2, SEG_W), F32)]).astype(BF16)


def _segment_products(r, w, pe):
    bias = _dot(pe, w)
    bias = jnp.concatenate([bias[0:1, :LANES], bias[1:2, LANES:]], axis=1)
    return _dot(r.astype(BF16), w) + bias


def _cmp1_prompt_kernel(x_ref, w_ref, pe_ref, o_ref, r_ref, *, n_seg):
    for l in range(CMP_STRIDE):
        r_ref[:, l * LANES:(l + 1) * LANES] = x_ref[pl.ds(l, n_seg, stride=CMP_STRIDE), :]
    o_ref[...] = _segment_products(r_ref[...], w_ref[...], pe_ref[...])


def compress_segments_prompt(u_head, batch, seq, wcat, pecat, tr=512):
    tr = min(tr, seq)
    n_seg = tr // CMP_STRIDE
    nt = seq // tr
    c0 = OFF_CMP // LANES
    return pl.pallas_call(
        functools.partial(_cmp1_prompt_kernel, n_seg=n_seg),
        grid=(batch, 2, NSA_KV_HEADS, nt),
        in_specs=[pl.BlockSpec((tr, LANES), lambda b, kv, g, t: (b * nt + t, c0 + kv * NSA_KV_HEADS + g)),
                  pl.BlockSpec((None, SEG_W, 2 * LANES), lambda b, kv, g, t: (kv, 0, 0)),
                  pl.BlockSpec((None, SUBLANES, SEG_W), lambda b, kv, g, t: (kv, 0, 0))],
        out_specs=pl.BlockSpec((None, None, None, n_seg, 2 * LANES), lambda b, kv, g, t: (b, kv, g, t, 0)),
        out_shape=jax.ShapeDtypeStruct((batch, 2, NSA_KV_HEADS, seq // CMP_STRIDE, 2 * LANES), F32),
        scratch_shapes=[pltpu.VMEM((n_seg, SEG_W), F32)],
        compiler_params=_cp(4),
        name="cmp_segments_prompt",
    )(u_head, wcat, pecat)


def _cmp2_kernel(pq_ref, w2_ref, o_ref, *, nc):
    pq = pq_ref[...]
    q_next = pltpu.roll(pq[:, LANES:], shift=nc - 1, axis=0)
    hid = _silu(pq[:, :LANES] + q_next)
    out = _dot(hid.astype(BF16), w2_ref[...].astype(BF16))
    row = lax.broadcasted_iota(jnp.int32, out.shape, 0)
    o_ref[...] = jnp.where(row < nc - 1, out, 0.0)


def compress_finish(pq, w2):
    batch, _, _, nc, _ = pq.shape
    return pl.pallas_call(
        functools.partial(_cmp2_kernel, nc=nc),
        grid=(batch, 2, NSA_KV_HEADS),
        in_specs=[pl.BlockSpec((None, None, None, nc, 2 * LANES), lambda b, kv, g: (b, kv, g, 0, 0)),
                  pl.BlockSpec((None, HEAD_DIM, HEAD_DIM), lambda b, kv, g: (kv, 0, 0))],
        out_specs=pl.BlockSpec((None, None, None, nc, LANES), lambda b, kv, g: (b, kv, g, 0, 0)),
        out_shape=jax.ShapeDtypeStruct((batch, 2, NSA_KV_HEADS, nc, LANES), F32),
        compiler_params=_cp(3),
        name="cmp_finish",
    )(pq, w2)


RANK_PAIRWISE_MAX_TOKENS = 16


def _cmp_select_kernel(q_ref, kc_ref, vc_ref, o_ref, sel_ref, score_ref, *, tq, tqp, nc, n_cmp, n_slc, nsp, pos0):
    qi = pl.program_id(2)
    qb = q_ref[...]
    parts = []
    for h in range(NSA_GROUP):
        qh = qb[:, h * HEAD_DIM:(h + 1) * HEAD_DIM]
        if tqp > tq:
            qh = jnp.concatenate([qh, jnp.zeros((tqp - tq, HEAD_DIM), F32)], axis=0)
        parts.append(qh)
    q4 = jnp.concatenate(parts, axis=0).astype(BF16)
    rows = NSA_GROUP * tqp
    st = _dot_nt(kc_ref[...].astype(BF16), q4) * SCALE
    ci = lax.broadcasted_iota(jnp.int32, (nc, rows), 0)
    tok = lax.broadcasted_iota(jnp.int32, (nc, rows), 1) & (tqp - 1)
    tpos = pos0 + qi * tq + tok
    valid = (ci < n_cmp) & (ci * CMP_STRIDE + CMP_LEN - 1 <= tpos)
    st = jnp.where(valid, st, NEG_BIG)
    m = jnp.max(st, axis=0, keepdims=True)
    e = jnp.where(valid, jnp.exp(st - m), 0.0)
    den = jnp.sum(e, axis=0, keepdims=True)
    pt = e / jnp.where(den > 0, den, 1.0)
    o = _dot_tn(pt.astype(BF16), vc_ref[...].astype(BF16))
    for h in range(NSA_GROUP):
        o_ref[:, h * HEAD_DIM:(h + 1) * HEAD_DIM] = o[h * tqp:h * tqp + tq]
    psum = pt[:, 0:tqp]
    for h in range(1, NSA_GROUP):
        psum = psum + pt[:, h * tqp:(h + 1) * tqp]
    nsr = score_ref.shape[0]
    jj = lax.broadcasted_iota(jnp.int32, (nsr, nc), 0)
    ii = lax.broadcasted_iota(jnp.int32, (nsr, nc), 1)
    cover = ((ii * CMP_STRIDE < jj * SEL_BLOCK + SEL_BLOCK) & (ii * CMP_STRIDE + CMP_LEN - 1 >= jj * SEL_BLOCK)
             & (ii < n_cmp))
    cover = jnp.where(cover, 1.0, 0.0).astype(BF16)
    hi, mid, lo = _split3(psum)
    imp = _dot(cover, hi) + _dot(cover, mid) + _dot(cover, lo)
    j = lax.broadcasted_iota(jnp.int32, (nsr, tqp), 0)
    tpos2 = pos0 + qi * tq + lax.broadcasted_iota(jnp.int32, (nsr, tqp), 1)
    cur = tpos2 // SEL_BLOCK
    forced = (j == 0) | (j == cur) | (j == cur - 1)
    ok = (j * SEL_BLOCK <= tpos2) & (j < n_slc)
    score = jnp.where(ok, jnp.where(forced, NSA_GROUP + 1.0, imp), -1.0)

    if tq <= RANK_PAIRWISE_MAX_TOKENS:
        if nsp > nsr:
            score = jnp.concatenate([score, jnp.full((nsp - nsr, tqp), -1.0, F32)], axis=0)
        score_rows = score.T
        jp_i = lax.broadcasted_iota(jnp.int32, (nsp, nsp), 0)
        j_i = lax.broadcasted_iota(jnp.int32, (nsp, nsp), 1)
        rows_out = []
        for t in range(tq):
            other = score[:, t:t + 1]
            mine = score_rows[t:t + 1, :]
            ahead = jnp.where(other > mine, 1.0, jnp.where((other == mine) & (jp_i < j_i), 1.0, 0.0))
            rank_t = jnp.sum(ahead, axis=0, keepdims=True)
            rows_out.append(jnp.where((rank_t < SEL_TOPK) & (mine >= 0), 1.0, 0.0))
        sel_ref[...] = jnp.concatenate(rows_out, axis=0)
        return

    score_ref[...] = score

    def body(jp, rank):
        other = score_ref[pl.ds(jp, 1), :]
        ahead = jnp.where(other > score, 1.0, jnp.where((other == score) & (jp < j), 1.0, 0.0))
        return rank + ahead

    n_valid = jnp.minimum(n_slc, (pos0 + qi * tq + tq - 1) // SEL_BLOCK + 1)
    rank = lax.fori_loop(0, n_valid, body, jnp.zeros((nsr, tqp), F32))
    sel_t = jnp.where((rank < SEL_TOPK) & (score >= 0), 1.0, 0.0)
    if nsp > nsr:
        sel_t = jnp.concatenate([sel_t, jnp.zeros((nsp - nsr, tqp), F32)], axis=0)
    sel_ref[...] = sel_t.T[:tq]


def cmp_select(u_head, row0, batch, seq, kv_cmp, n_cmp, n_slc, pos0, tq=256):
    tq = min(tq, seq)
    tqp = max(tq, LANES)
    nt = seq // tq
    rb0 = row0 // tq
    nc = kv_cmp.shape[3]
    nsp = -(-n_slc // LANES) * LANES
    gw = NSA_GROUP * HEAD_DIM
    assert row0 % tq == 0 and seq % tq == 0 and tqp & (tqp - 1) == 0
    return pl.pallas_call(
        functools.partial(_cmp_select_kernel, tq=tq, tqp=tqp, nc=nc, n_cmp=n_cmp, n_slc=n_slc, nsp=nsp, pos0=pos0),
        grid=(batch, NSA_KV_HEADS, nt),
        in_specs=[pl.BlockSpec((tq, gw), lambda b, g, t: (rb0 + b * nt + t, g)),
                  pl.BlockSpec((None, None, None, nc, LANES), lambda b, g, t: (b, 0, g, 0, 0)),
                  pl.BlockSpec((None, None, None, nc, LANES), lambda b, g, t: (b, 1, g, 0, 0))],
        out_specs=[pl.BlockSpec((tq, gw), lambda b, g, t: (b * nt + t, g)),
                   pl.BlockSpec((None, None, tq, nsp), lambda b, g, t: (b, g, t, 0))],
        out_shape=[jax.ShapeDtypeStruct((batch * seq, NSA_W), F32),
                   jax.ShapeDtypeStruct((batch, NSA_KV_HEADS, seq, nsp), F32)],
        scratch_shapes=[pltpu.VMEM((-(-n_slc // 16) * 16, tqp), F32)],
        compiler_params=_cp(3),
        name="cmp_select",
    )(u_head, kv_cmp, kv_cmp)


def _flash_step(s, mask, v, m_ref, l_ref, acc_ref):
    s = jnp.where(mask, s, NEG_BIG)
    m_prev = m_ref[...]
    m_new = jnp.maximum(m_prev, jnp.max(s, axis=1, keepdims=True))
    e = jnp.where(mask, jnp.exp(s - m_new), 0.0)
    alpha = jnp.exp(m_prev - m_new)
    l_ref[...] = alpha * l_ref[...] + jnp.sum(e, axis=1, keepdims=True)
    acc_ref[...] = alpha * acc_ref[...] + _dot(e.astype(BF16), v)
    m_ref[...] = m_new


def _flash_init(m_ref, l_ref, acc_ref):
    m_ref[...] = jnp.full_like(m_ref, NEG_BIG)
    l_ref[...] = jnp.zeros_like(l_ref)
    acc_ref[...] = jnp.zeros_like(acc_ref)


def _flash_result(l_ref, acc_ref):
    l = l_ref[...]
    return acc_ref[...] / jnp.where(l > 0, l, 1.0)


def _stack_heads(qb, pad_to=None):
    parts = []
    for h in range(NSA_GROUP):
        qh = qb[:, h * HEAD_DIM:(h + 1) * HEAD_DIM]
        if pad_to is not None and pad_to > qh.shape[0]:
            qh = jnp.concatenate([qh, jnp.zeros((pad_to - qh.shape[0], HEAD_DIM), qh.dtype)], axis=0)
        parts.append(qh)
    return jnp.concatenate(parts, axis=0)


def _flash_prompt_kernel(qi_tab, kb_tab, first_tab, last_tab, *refs, tq, tk, mode, nsp):
    if mode == "sel":
        q_ref, k_ref, v_ref, sel_ref, o_ref, m_ref, l_ref, acc_ref = refs
    else:
        q_ref, k_ref, v_ref, o_ref, m_ref, l_ref, acc_ref = refs
    p = pl.program_id(2)
    qi = qi_tab[p]
    kb = kb_tab[p]

    @pl.when(first_tab[p] == 1)
    def _():
        _flash_init(m_ref, l_ref, acc_ref)

    q4 = _stack_heads(q_ref[...]).astype(BF16)
    s = _dot_nt(q4, k_ref[...].astype(BF16)) * SCALE
    qpos = qi * tq + lax.broadcasted_iota(jnp.int32, (tq, tk), 0)
    kpos = kb * tk + lax.broadcasted_iota(jnp.int32, (tq, tk), 1)
    if mode == "win":
        d = qpos - kpos
        mask = (d >= 0) & (d < WINDOW)
    else:
        jj = lax.broadcasted_iota(jnp.int32, (nsp, tk), 0)
        kk = lax.broadcasted_iota(jnp.int32, (nsp, tk), 1)
        expand = jnp.where(jj == kb * (tk // SEL_BLOCK) + kk // SEL_BLOCK, 1.0, 0.0).astype(BF16)
        chosen = _dot(sel_ref[...].astype(BF16), expand)
        mask = (chosen > 0.5) & (kpos <= qpos)
    mask4 = jnp.concatenate([mask] * NSA_GROUP, axis=0)
    _flash_step(s, mask4, v_ref[...].astype(BF16), m_ref, l_ref, acc_ref)

    @pl.when(last_tab[p] == 1)
    def _():
        o = _flash_result(l_ref, acc_ref)
        for h in range(NSA_GROUP):
            o_ref[:, h * HEAD_DIM:(h + 1) * HEAD_DIM] = o[h * tq:(h + 1) * tq]


def flash_prompt(u_head, batch, seq, mode, sel=None, tq=256, tk=None):
    tq = min(tq, seq)
    tk = min(tk or (256 if mode == "win" else 512), seq)
    nq, nk = seq // tq, seq // tk
    pairs = []
    for qi in range(nq):
        if mode == "win":
            lo = max(0, (qi * tq - WINDOW + 1) // tk)
        else:
            lo = 0
        hi = (qi * tq + tq - 1) // tk
        kbs = list(range(lo, hi + 1))
        pairs += [(qi, kb, int(kb == kbs[0]), int(kb == kbs[-1])) for kb in kbs]
    tabs = [jnp.asarray([p[i] for p in pairs], jnp.int32) for i in range(4)]
    off = OFF_WIN if mode == "win" else OFF_SEL
    ck = off // LANES
    cv = ck + NSA_KV_HEADS
    gw = NSA_GROUP * HEAD_DIM
    in_specs = [
        pl.BlockSpec((tq, gw), lambda b, g, p, qt, kt, ft, lt: (b * nq + qt[p], g)),
        pl.BlockSpec((tk, LANES), lambda b, g, p, qt, kt, ft, lt: (b * nk + kt[p], ck + g)),
        pl.BlockSpec((tk, LANES), lambda b, g, p, qt, kt, ft, lt: (b * nk + kt[p], cv + g)),
    ]
    args = [u_head, u_head, u_head]
    nsp = 0
    if mode == "sel":
        nsp = sel.shape[-1]
        in_specs.append(pl.BlockSpec((None, None, tq, nsp), lambda b, g, p, qt, kt, ft, lt: (b, g, qt[p], 0)))
        args.append(sel)
    rows = NSA_GROUP * tq
    grid_spec = pltpu.PrefetchScalarGridSpec(
        num_scalar_prefetch=4,
        grid=(batch, NSA_KV_HEADS, len(pairs)),
        in_specs=in_specs,
        out_specs=pl.BlockSpec((tq, gw), lambda b, g, p, qt, kt, ft, lt: (b * nq + qt[p], g)),
        scratch_shapes=[pltpu.VMEM((rows, 1), F32), pltpu.VMEM((rows, 1), F32), pltpu.VMEM((rows, HEAD_DIM), F32)],
    )
    return pl.pallas_call(
        functools.partial(_flash_prompt_kernel, tq=tq, tk=tk, mode=mode, nsp=nsp),
        grid_spec=grid_spec,
        out_shape=jax.ShapeDtypeStruct((batch * seq, NSA_W), F32),
        compiler_params=_cp(3),
        name="flash_" + mode,
    )(*tabs, *args)


def _sel_prompt_kernel(q_ref, k_ref, v_ref, sel_ref, o_ref, m_ref, l_ref, acc_ref, *, tq, tk, nsp):
    qi = pl.program_id(2)
    q4 = (_stack_heads(q_ref[...]) * SCALE).astype(BF16)
    selb = sel_ref[...].astype(BF16)
    _flash_init(m_ref, l_ref, acc_ref)
    row = lax.broadcasted_iota(jnp.int32, (tq, tk), 0)
    col = lax.broadcasted_iota(jnp.int32, (tq, tk), 1)
    jj = lax.broadcasted_iota(jnp.int32, (nsp, tk), 0)
    kk = lax.broadcasted_iota(jnp.int32, (nsp, tk), 1) // SEL_BLOCK

    def body(kb, c):
        keys = pl.ds(pl.multiple_of(kb * tk, tk), tk)
        s = _dot_nt(q4, k_ref[keys, :].astype(BF16))
        expand = jnp.where(jj == kb * (tk // SEL_BLOCK) + kk, 1.0, 0.0).astype(BF16)
        chosen = _dot(selb, expand)
        ok = (chosen > 0.5) & (kb * tk + col <= qi * tq + row)
        bias = jnp.where(ok, 0.0, 2.0 * NEG_BIG)
        s = (s.reshape(NSA_GROUP, tq, tk) + bias[None]).reshape(NSA_GROUP * tq, tk)
        m_prev = m_ref[...]
        m_new = jnp.maximum(m_prev, jnp.max(s, axis=1, keepdims=True))
        e = jnp.exp(s - m_new)
        alpha = jnp.exp(m_prev - m_new)
        l_ref[...] = alpha * l_ref[...] + jnp.sum(e, axis=1, keepdims=True)
        acc_ref[...] = alpha * acc_ref[...] + _dot(e.astype(BF16), v_ref[keys, :].astype(BF16))
        m_ref[...] = m_new
        return c

    lax.fori_loop(0, (qi * tq + tq - 1) // tk + 1, body, 0)
    o = _flash_result(l_ref, acc_ref)
    for h in range(NSA_GROUP):
        o_ref[:, h * HEAD_DIM:(h + 1) * HEAD_DIM] = o[h * tq:(h + 1) * tq]


def sel_prompt(u_head, batch, seq, sel, tq=256, tk=512):
    tq = min(tq, seq)
    tk = min(tk, seq)
    nq = seq // tq
    nsp = sel.shape[-1]
    ck = OFF_SEL // LANES
    cv = ck + NSA_KV_HEADS
    gw = NSA_GROUP * HEAD_DIM
    rows = NSA_GROUP * tq
    assert seq % tq == 0 and seq % tk == 0 and tk % SEL_BLOCK == 0
    return pl.pallas_call(
        functools.partial(_sel_prompt_kernel, tq=tq, tk=tk, nsp=nsp),
        grid=(batch, NSA_KV_HEADS, nq),
        in_specs=[pl.BlockSpec((tq, gw), lambda b, g, i: (b * nq + i, g)),
                  pl.BlockSpec((seq, LANES), lambda b, g, i: (b, ck + g)),
                  pl.BlockSpec((seq, LANES), lambda b, g, i: (b, cv + g)),
                  pl.BlockSpec((None, None, tq, nsp), lambda b, g, i: (b, g, i, 0))],
        out_specs=pl.BlockSpec((tq, gw), lambda b, g, i: (b * nq + i, g)),
        out_shape=jax.ShapeDtypeStruct((batch * seq, NSA_W), F32),
        scratch_shapes=[pltpu.VMEM((rows, 1), F32), pltpu.VMEM((rows, 1), F32), pltpu.VMEM((rows, HEAD_DIM), F32)],
        compiler_params=_cp(3),
        name="sel_prompt",
    )(u_head, u_head, u_head, sel)


def _win_prompt_kernel(q_ref, k_ref, v_ref, o_ref, *, tq, nk, seq):
    q0 = pl.program_id(2) * tq
    start = pl.multiple_of(jnp.clip(q0 - WINDOW, 0, seq - nk), tq)
    keys = pl.ds(start, nk)
    q4 = (_stack_heads(q_ref[...]) * SCALE).astype(BF16)
    s = _dot_nt(q4, k_ref[keys, :].astype(BF16))
    d = (q0 + lax.broadcasted_iota(jnp.int32, (tq, nk), 0)) - (start + lax.broadcasted_iota(jnp.int32, (tq, nk), 1))
    bias = jnp.where((d >= 0) & (d < WINDOW), 0.0, NEG_BIG)
    s = s + jnp.concatenate([bias] * NSA_GROUP, axis=0)
    e = jnp.exp(s - jnp.max(s, axis=1, keepdims=True))
    o = _dot(e.astype(BF16), v_ref[keys, :].astype(BF16)) / jnp.sum(e, axis=1, keepdims=True)
    for h in range(NSA_GROUP):
        o_ref[:, h * HEAD_DIM:(h + 1) * HEAD_DIM] = o[h * tq:(h + 1) * tq]


def win_prompt(u_head, batch, seq, tq=256):
    tq = min(tq, seq)
    nk = min(seq, WINDOW + tq)
    nq = seq // tq
    ck = OFF_WIN // LANES
    cv = ck + NSA_KV_HEADS
    gw = NSA_GROUP * HEAD_DIM
    assert seq % tq == 0 and WINDOW % tq == 0
    return pl.pallas_call(
        functools.partial(_win_prompt_kernel, tq=tq, nk=nk, seq=seq),
        grid=(batch, NSA_KV_HEADS, nq),
        in_specs=[pl.BlockSpec((tq, gw), lambda b, g, i: (b * nq + i, g)),
                  pl.BlockSpec((seq, LANES), lambda b, g, i: (b, ck + g)),
                  pl.BlockSpec((seq, LANES), lambda b, g, i: (b, cv + g))],
        out_specs=pl.BlockSpec((tq, gw), lambda b, g, i: (b * nq + i, g)),
        out_shape=jax.ShapeDtypeStruct((batch * seq, NSA_W), F32),
        compiler_params=_cp(3),
        name="win_prompt",
    )(u_head, u_head, u_head)


def _nsa_combine_kernel(gate_ref, a_ref, b_ref, c_ref, o_ref):
    gate = _sigmoid(gate_ref[...])
    for h in range(NSA_HEADS):
        sl = slice(h * HEAD_DIM, (h + 1) * HEAD_DIM)
        out = (gate[:, h:h + 1] * a_ref[:, sl] + gate[:, NSA_HEADS + h:NSA_HEADS + h + 1] * b_ref[:, sl]
               + gate[:, 2 * NSA_HEADS + h:2 * NSA_HEADS + h + 1] * c_ref[:, sl])
        o_ref[:, sl] = out.astype(o_ref.dtype)


def nsa_combine(u_head, row0, o_cmp, o_sel, o_win, tm=256):
    n = o_cmp.shape[0]
    tm = min(tm, n)
    rb0 = row0 // tm
    assert row0 % tm == 0 and n % tm == 0
    spec = pl.BlockSpec((tm, NSA_W), lambda i: (i, 0))
    return pl.pallas_call(
        _nsa_combine_kernel,
        grid=(n // tm,),
        in_specs=[pl.BlockSpec((tm, LANES), lambda i: (rb0 + i, OFF_NSA_GATE // LANES)), spec, spec, spec],
        out_specs=spec,
        out_shape=jax.ShapeDtypeStruct((n, NSA_W), BF16 if tm % 16 == 0 else F32),
        compiler_params=_cp(1),
        name="nsa_combine",
    )(u_head, o_cmp, o_sel, o_win)


def _page_rows(page_ref, first, n_rows, stride):
    return page_ref[pl.ds(first, n_rows, stride=stride), :]


def _pad_rows(x, n):
    if x.shape[0] >= n:
        return x
    return jnp.concatenate([x, jnp.zeros((n - x.shape[0], x.shape[1]), x.dtype)], axis=0)


def _cmp1_sample_kernel(pt_ref, *refs, pps, page):
    page_refs = refs[:pps]
    w_ref, pe_ref, o_ref, r_ref = refs[pps:]
    segs = page // CMP_STRIDE
    stride = CMP_STRIDE * 2 * NSA_KV_HEADS
    for kv in range(2):
        for g in range(NSA_KV_HEADS):
            for k in range(pps):
                for l in range(CMP_STRIDE):
                    r_ref[k * segs:(k + 1) * segs, l * LANES:(l + 1) * LANES] = _page_rows(
                        page_refs[k], l * 2 * NSA_KV_HEADS + kv * NSA_KV_HEADS + g, segs, stride)
            o_ref[kv, g] = _segment_products(r_ref[...], w_ref[kv], pe_ref[kv])


def compress_segments_sample(pool, layer, pt_flat, batch, n_pages, wcat, pecat, pps=16):
    page = pool.shape[2] // (2 * NSA_KV_HEADS)
    pps = min(pps, n_pages)
    assert n_pages % pps == 0 and page % CMP_STRIDE == 0
    segs = page // CMP_STRIDE
    in_specs = [pl.BlockSpec((None, None, pool.shape[2], LANES),
                             lambda b, s, pt, k=k: (layer, pt[b * n_pages + s * pps + k], 0, 0)) for k in range(pps)]
    in_specs += [pl.BlockSpec((2, SEG_W, 2 * LANES), lambda b, s, pt: (0, 0, 0)),
                 pl.BlockSpec((2, SUBLANES, SEG_W), lambda b, s, pt: (0, 0, 0))]
    grid_spec = pltpu.PrefetchScalarGridSpec(
        num_scalar_prefetch=1,
        grid=(batch, n_pages // pps),
        in_specs=in_specs,
        out_specs=pl.BlockSpec((None, 2, NSA_KV_HEADS, pps * segs, 2 * LANES), lambda b, s, pt: (b, 0, 0, s, 0)),
        scratch_shapes=[pltpu.VMEM((pps * segs, SEG_W), F32)],
    )
    return pl.pallas_call(
        functools.partial(_cmp1_sample_kernel, pps=pps, page=page),
        grid_spec=grid_spec,
        out_shape=jax.ShapeDtypeStruct((batch, 2, NSA_KV_HEADS, n_pages * segs, 2 * LANES), F32),
        compiler_params=_cp(2),
        name="cmp_segments_sample",
    )(pt_flat, *([pool] * pps), wcat, pecat)


def _sel_sample_kernel(pt_ref, *refs, pps, page, ds, past, nsp):
    q_ref, new_ref, sel_ref, selstep_ref = refs[:4]
    page_refs = refs[4:4 + pps]
    o_ref, m_ref, l_ref, acc_ref = refs[4 + pps:]
    s = pl.program_id(1)
    rows = NSA_GROUP * ds
    keys = pps * page
    bps = keys // SEL_BLOCK
    stride = 2 * NSA_KV_HEADS

    @pl.when(s == 0)
    def _():
        _flash_init(m_ref, l_ref, acc_ref)

    jj = lax.broadcasted_iota(jnp.int32, (bps, keys), 0)
    kk = lax.broadcasted_iota(jnp.int32, (bps, keys), 1)
    expand = jnp.where(jj == kk // SEL_BLOCK, 1.0, 0.0).astype(BF16)
    for g in range(NSA_KV_HEADS):
        q4 = _stack_heads(q_ref[:, g * NSA_GROUP * HEAD_DIM:(g + 1) * NSA_GROUP * HEAD_DIM]).astype(BF16)
        sel4 = jnp.concatenate([sel_ref[g]] * NSA_GROUP, axis=0)
        sel_here = jnp.concatenate([selstep_ref[g]] * NSA_GROUP, axis=0)
        k_all = jnp.concatenate([_page_rows(r, g, page, stride) for r in page_refs], axis=0).astype(BF16)
        v_all = jnp.concatenate([_page_rows(r, NSA_KV_HEADS + g, page, stride) for r in page_refs],
                                axis=0).astype(BF16)
        sc = _dot_nt(q4, k_all) * SCALE
        chosen = _dot(sel_here.astype(BF16), expand)
        _flash_step(sc, chosen > 0.5, v_all, m_ref.at[g], l_ref.at[g], acc_ref.at[g])

        @pl.when(s == pl.num_programs(1) - 1)
        def _():
            k_new = _pad_rows(new_ref[:, g * HEAD_DIM:(g + 1) * HEAD_DIM], LANES).astype(BF16)
            v_new = _pad_rows(new_ref[:, (NSA_KV_HEADS + g) * HEAD_DIM:(NSA_KV_HEADS + g + 1) * HEAD_DIM],
                              LANES).astype(BF16)
            sn = _dot_nt(q4, k_new) * SCALE
            blk = past // SEL_BLOCK
            j = lax.broadcasted_iota(jnp.int32, (rows, LANES), 1)
            t = lax.broadcasted_iota(jnp.int32, (rows, LANES), 0) % ds
            mask = (sel4[:, blk:blk + 1] > 0.5) & (j <= t)
            _flash_step(sn, mask, v_new, m_ref.at[g], l_ref.at[g], acc_ref.at[g])
            o = _flash_result(l_ref.at[g], acc_ref.at[g])
            for h in range(NSA_GROUP):
                c0 = (g * NSA_GROUP + h) * HEAD_DIM
                o_ref[:, c0:c0 + HEAD_DIM] = o[h * ds:(h + 1) * ds]


def sel_sample(u_head, row0, pool, layer, pt_flat, sel, batch, ds, n_pages, pps=32):
    page = pool.shape[2] // (2 * NSA_KV_HEADS)
    pps = min(pps, n_pages)
    past = n_pages * page
    nsp = sel.shape[-1]
    rows = NSA_GROUP * ds
    ns = n_pages // pps
    bps = pps * page // SEL_BLOCK
    assert n_pages % pps == 0 and row0 % ds == 0 and past % SEL_BLOCK == 0 and ds <= SEL_BLOCK
    assert page % SEL_BLOCK == 0
    sel_steps = sel[..., :ns * bps].reshape(batch, NSA_KV_HEADS, ds, ns, bps).transpose(0, 1, 3, 2, 4)
    in_specs = [pl.BlockSpec((ds, NSA_W), lambda b, s, pt: (row0 // ds + b, 0)),
                pl.BlockSpec((ds, KV_W), lambda b, s, pt: (row0 // ds + b, OFF_SEL // KV_W)),
                pl.BlockSpec((None, NSA_KV_HEADS, ds, nsp), lambda b, s, pt: (b, 0, 0, 0)),
                pl.BlockSpec((None, NSA_KV_HEADS, None, ds, bps), lambda b, s, pt: (b, 0, s, 0, 0))]
    in_specs += [pl.BlockSpec((None, None, pool.shape[2], LANES),
                              lambda b, s, pt, k=k: (layer, pt[b * n_pages + s * pps + k], 0, 0)) for k in range(pps)]
    grid_spec = pltpu.PrefetchScalarGridSpec(
        num_scalar_prefetch=1,
        grid=(batch, n_pages // pps),
        in_specs=in_specs,
        out_specs=pl.BlockSpec((ds, NSA_W), lambda b, s, pt: (b, 0)),
        scratch_shapes=[pltpu.VMEM((NSA_KV_HEADS, rows, 1), F32), pltpu.VMEM((NSA_KV_HEADS, rows, 1), F32),
                        pltpu.VMEM((NSA_KV_HEADS, rows, HEAD_DIM), F32)],
    )
    return pl.pallas_call(
        functools.partial(_sel_sample_kernel, pps=pps, page=page, ds=ds, past=past, nsp=nsp),
        grid_spec=grid_spec,
        out_shape=jax.ShapeDtypeStruct((batch * ds, NSA_W), F32),
        compiler_params=_cp(2),
        name="sel_sample",
    )(pt_flat, u_head, u_head, sel, sel_steps, *([pool] * pps))


def _win_sample_kernel(q_ref, new_ref, buf_ref, o_ref, m_ref, l_ref, acc_ref, *, ds, nbuf):
    rows = NSA_GROUP * ds
    stride = 2 * NSA_KV_HEADS
    keys = nbuf + LANES
    i = lax.broadcasted_iota(jnp.int32, (rows, keys), 1)
    t = lax.broadcasted_iota(jnp.int32, (rows, keys), 0) % ds
    d = jnp.where(i < nbuf, t + nbuf - i, t - (i - nbuf))
    mask = (d >= 0) & (d < WINDOW) & (i < nbuf + ds)
    for g in range(NSA_KV_HEADS):
        _flash_init(m_ref, l_ref, acc_ref)
        q4 = _stack_heads(q_ref[:, g * NSA_GROUP * HEAD_DIM:(g + 1) * NSA_GROUP * HEAD_DIM]).astype(BF16)
        k_new = _pad_rows(new_ref[:, g * HEAD_DIM:(g + 1) * HEAD_DIM], LANES)
        v_new = _pad_rows(new_ref[:, (NSA_KV_HEADS + g) * HEAD_DIM:(NSA_KV_HEADS + g + 1) * HEAD_DIM], LANES)
        k_all = jnp.concatenate([_page_rows(buf_ref, g, nbuf, stride), k_new], axis=0).astype(BF16)
        v_all = jnp.concatenate([_page_rows(buf_ref, NSA_KV_HEADS + g, nbuf, stride), v_new], axis=0).astype(BF16)
        _flash_step(_dot_nt(q4, k_all) * SCALE, mask, v_all, m_ref, l_ref, acc_ref)
        o = _flash_result(l_ref, acc_ref)
        for h in range(NSA_GROUP):
            c0 = (g * NSA_GROUP + h) * HEAD_DIM
            o_ref[:, c0:c0 + HEAD_DIM] = o[h * ds:(h + 1) * ds]


def win_sample(u_head, row0, buf, layer, batch, ds):
    nbuf = buf.shape[2] // (2 * NSA_KV_HEADS)
    rows = NSA_GROUP * ds
    return pl.pallas_call(
        functools.partial(_win_sample_kernel, ds=ds, nbuf=nbuf),
        grid=(batch,),
        in_specs=[pl.BlockSpec((ds, NSA_W), lambda b: (row0 // ds + b, 0)),
                  pl.BlockSpec((ds, KV_W), lambda b: (row0 // ds + b, OFF_WIN // KV_W)),
                  pl.BlockSpec((None, None, buf.shape[2], LANES), lambda b: (layer, b, 0, 0))],
        out_specs=pl.BlockSpec((ds, NSA_W), lambda b: (b, 0)),
        out_shape=jax.ShapeDtypeStruct((batch * ds, NSA_W), F32),
        scratch_shapes=[pltpu.VMEM((rows, 1), F32), pltpu.VMEM((rows, 1), F32), pltpu.VMEM((rows, HEAD_DIM), F32)],
        compiler_params=_cp(1),
        name="win_sample",
    )(u_head, u_head, buf)


SB_ROWS = 16


def _sb_sample_kernel(pt_ref, q_ref, kn_ref, vn_ref, pool_hbm, o_ref, acc_ref, carry_ref, kv_ref, sem, *,
                      pps, page, ds, n_pages, layer):
    b = pl.program_id(0)
    stride = 2 * SB_HEADS
    umat = _suffix_matrix()
    q_heads = [_pad_rows(q_ref[:, h * HEAD_DIM:(h + 1) * HEAD_DIM] * SCALE, SB_ROWS).astype(BF16)
               for h in range(SB_HEADS)]
    rows = SB_HEADS * SB_ROWS
    real_row = lax.broadcasted_iota(jnp.int32, (rows, LANES), 0) % SB_ROWS < ds

    def accumulate(k_heads, v_heads, mask):
        z = jnp.concatenate([_dot_nt(q_heads[h], k_heads[h]) for h in range(SB_HEADS)], axis=0)
        a, carry = _sb_weights(z, mask, carry_ref[...], umat)
        carry_ref[...] = carry
        a = a.astype(BF16)
        for h in range(SB_HEADS):
            sl = slice(h * SB_ROWS, (h + 1) * SB_ROWS)
            acc_ref[sl, :] += _dot(a[sl], v_heads[h])
        return jnp.max(jnp.where(real_row, carry, NEG_BIG))

    acc_ref[...] = jnp.zeros_like(acc_ref)
    carry_ref[...] = jnp.zeros_like(carry_ref)
    k_new = [_pad_rows(kn_ref[:, h * HEAD_DIM:(h + 1) * HEAD_DIM], LANES).astype(BF16) for h in range(SB_HEADS)]
    v_new = [_pad_rows(vn_ref[:, h * HEAD_DIM:(h + 1) * HEAD_DIM], LANES).astype(BF16) for h in range(SB_HEADS)]
    j = lax.broadcasted_iota(jnp.int32, (rows, LANES), 1)
    t = lax.broadcasted_iota(jnp.int32, (rows, LANES), 0) % SB_ROWS
    top = accumulate(k_new, v_new, j < t)

    def page_copy(g, k):
        pid = pt_ref[b * n_pages + n_pages - (g + 1) * pps + k]
        return pltpu.make_async_copy(pool_hbm.at[layer, pid], kv_ref.at[k], sem)

    def cond(st):
        return jnp.logical_and(st[0] < n_pages // pps, st[1] > SB_ZERO_LOG)

    def body(st):
        g = st[0]
        for k in range(pps):
            page_copy(g, k).start()
        for k in range(pps):
            page_copy(g, k).wait()
        k_heads = [jnp.concatenate([_page_rows(kv_ref.at[k], h, page, stride) for k in range(pps)],
                                   axis=0).astype(BF16) for h in range(SB_HEADS)]
        v_heads = [jnp.concatenate([_page_rows(kv_ref.at[k], SB_HEADS + h, page, stride) for k in range(pps)],
                                   axis=0).astype(BF16) for h in range(SB_HEADS)]
        return g + 1, accumulate(k_heads, v_heads, None)

    lax.while_loop(cond, body, (0, top))
    for h in range(SB_HEADS):
        o_ref[:, h * HEAD_DIM:(h + 1) * HEAD_DIM] = acc_ref[h * SB_ROWS:h * SB_ROWS + ds, :]


def sb_sample(u_rest, row0, pool, layer, pt_flat, batch, ds, n_pages, pps=2):
    page = pool.shape[2] // (2 * SB_HEADS)
    pps = min(pps, n_pages)
    assert n_pages % pps == 0 and row0 % ds == 0 and ds <= SB_ROWS
    rb = row0 // ds
    grid_spec = pltpu.PrefetchScalarGridSpec(
        num_scalar_prefetch=1,
        grid=(batch,),
        in_specs=[pl.BlockSpec((ds, SB_W), lambda b, pt: (rb + b, R_SB_Q // SB_W)),
                  pl.BlockSpec((ds, SB_W), lambda b, pt: (rb + b, R_SB_K // SB_W)),
                  pl.BlockSpec((ds, SB_W), lambda b, pt: (rb + b, R_SB_V // SB_W)),
                  pl.BlockSpec(memory_space=pl.ANY)],
        out_specs=pl.BlockSpec((ds, SB_W), lambda b, pt: (b, 0)),
        scratch_shapes=[pltpu.VMEM((SB_HEADS * SB_ROWS, HEAD_DIM), F32), pltpu.VMEM((SB_HEADS * SB_ROWS, LANES), F32),
                        pltpu.VMEM((pps, pool.shape[2], LANES), F32), pltpu.SemaphoreType.DMA],
    )
    return pl.pallas_call(
        functools.partial(_sb_sample_kernel, pps=pps, page=page, ds=ds, n_pages=n_pages, layer=layer),
        grid_spec=grid_spec,
        out_shape=jax.ShapeDtypeStruct((batch * ds, SB_W), F32),
        compiler_params=_cp(1),
        name="sb_sample",
    )(pt_flat, u_rest, u_rest, u_rest, pool)


def _router_kernel(x_ref, g_ref, w_ref, b_ref, h_ref, o_ref):
    x = x_ref[...]
    h = x * lax.rsqrt(jnp.mean(x * x, axis=-1, keepdims=True) + NORM_EPS) * g_ref[...]
    h_ref[...] = h
    hh, hm, hl = _split3(h)
    wh, wm, wl = _split3(w_ref[...])
    logits = (_dot(hh, wh) + _dot(hh, wm) + _dot(hm, wh) + _dot(hh, wl) + _dot(hl, wh) + _dot(hm, wm)) + b_ref[...]
    lane = lax.broadcasted_iota(jnp.int32, logits.shape, 1)
    logits = jnp.where(lane < N_EXPERTS, logits, NEG_BIG)
    m1 = jnp.max(logits, axis=1, keepdims=True)
    i1 = jnp.min(jnp.where(logits == m1, lane, LANES), axis=1, keepdims=True)
    rest = jnp.where(lane == i1, NEG_BIG, logits)
    m2 = jnp.max(rest, axis=1, keepdims=True)
    i2 = jnp.min(jnp.where(rest == m2, lane, LANES), axis=1, keepdims=True)
    e = jnp.exp(m2 - m1)
    g1 = 1.0 / (1.0 + e)
    g2 = e / (1.0 + e)
    o_ref[...] = jnp.where(lane == 0, i1.astype(F32), jnp.where(lane == 1, i2.astype(F32),
                           jnp.where(lane == 2, g1, jnp.where(lane == 3, g2, 0.0))))


def moe_router(x, g, rw, rb, tm):
    m, d = x.shape
    w = jnp.zeros((d, LANES), F32).at[:, :N_EXPERTS].set(rw)
    b = jnp.zeros((1, LANES), F32).at[0, :N_EXPERTS].set(rb.astype(F32))
    return pl.pallas_call(
        _router_kernel,
        grid=(m // tm,),
        in_specs=[pl.BlockSpec((tm, d), lambda i: (i, 0)), pl.BlockSpec((1, d), lambda i: (0, 0)),
                  pl.BlockSpec((d, LANES), lambda i: (0, 0)), pl.BlockSpec((1, LANES), lambda i: (0, 0))],
        out_specs=[pl.BlockSpec((tm, d), lambda i: (i, 0)), pl.BlockSpec((tm, LANES), lambda i: (i, 0))],
        out_shape=[jax.ShapeDtypeStruct((m, d), F32), jax.ShapeDtypeStruct((m, LANES), F32)],
        compiler_params=_cp(1),
        name="moe_router",
    )(x, g.reshape(1, d), w, b)


ROW_DMA_UNROLL = 8


def _row_copy(src_hbm, row, dst_ref, r, sem):
    return pltpu.make_async_copy(src_hbm.at[pl.ds(row, 1), :], dst_ref.at[pl.ds(r, 1), :], sem)


def _gather_kernel(tok_ref, nu_ref, h_hbm, o_ref, buf_ref, sem):
    blk = pl.program_id(0)
    n = buf_ref.shape[0]

    @pl.when(blk < nu_ref[0])
    def _():
        def issue(g, c):
            for u in range(ROW_DMA_UNROLL):
                r = g * ROW_DMA_UNROLL + u
                _row_copy(h_hbm, tok_ref[blk * n + r], buf_ref, r, sem).start(priority=u % 2)
            return c

        lax.fori_loop(0, n // ROW_DMA_UNROLL, issue, 0)
        pltpu.make_async_copy(h_hbm.at[pl.ds(0, n), :], buf_ref, sem).wait()
        o_ref[...] = buf_ref[...].astype(o_ref.dtype)

    @pl.when(blk >= nu_ref[0])
    def _():
        o_ref[...] = jnp.zeros_like(o_ref)


def _used_block(i, nu):
    return jnp.minimum(i, nu[0] - 1)


def moe_gather(h, tok_buf, n_used):
    p = tok_buf.shape[0]
    d = h.shape[1]
    grid_spec = pltpu.PrefetchScalarGridSpec(
        num_scalar_prefetch=2,
        grid=(p // MOE_BLOCK,),
        in_specs=[pl.BlockSpec(memory_space=pl.ANY)],
        out_specs=pl.BlockSpec((MOE_BLOCK, d), lambda i, tok, nu: (i, 0)),
        scratch_shapes=[pltpu.VMEM((MOE_BLOCK, d), F32), pltpu.SemaphoreType.DMA],
    )
    return pl.pallas_call(
        _gather_kernel,
        grid_spec=grid_spec,
        out_shape=jax.ShapeDtypeStruct((p, d), BF16),
        compiler_params=_cp(1),
        name="moe_gather",
    )(tok_buf, n_used, h)


def _moe_mm_kernel(be_ref, nu_ref, *refs, n_w, mode):
    a_ref = refs[0]
    w_refs = refs[1:1 + n_w]
    o_ref = refs[1 + n_w]
    wbf_refs = refs[2 + n_w:]
    blk = pl.program_id(1)
    changed = jnp.logical_or(blk == 0, be_ref[blk] != be_ref[jnp.maximum(blk - 1, 0)])

    @pl.when(changed)
    def _():
        for w_ref, wbf_ref in zip(w_refs, wbf_refs):
            wbf_ref[...] = w_ref[...].astype(BF16)

    @pl.when(blk < nu_ref[0])
    def _():
        prods = [_dot(a_ref[...], r[...]) for r in wbf_refs]
        o_ref[...] = _mm_epilogue(mode, prods, ()).astype(o_ref.dtype)

    @pl.when(blk >= nu_ref[0])
    def _():
        o_ref[...] = jnp.zeros_like(o_ref)


def moe_matmul(a, w_list, moe_index, blk_e, n_used, mode, out_dtype, tn=512):
    p, k = a.shape
    n_out = w_list[0].shape[-1]
    tn = _tile(n_out, tn)
    grid_spec = pltpu.PrefetchScalarGridSpec(
        num_scalar_prefetch=2,
        grid=(n_out // tn, p // MOE_BLOCK),
        in_specs=[pl.BlockSpec((MOE_BLOCK, k), lambda j, i, be, nu: (_used_block(i, nu), 0))]
        + [pl.BlockSpec((None, None, k, tn), lambda j, i, be, nu: (moe_index, be[i], 0, j)) for _ in w_list],
        out_specs=pl.BlockSpec((MOE_BLOCK, tn), lambda j, i, be, nu: (i, j)),
        scratch_shapes=[pltpu.VMEM((k, tn), BF16) for _ in w_list],
    )
    return pl.pallas_call(
        functools.partial(_moe_mm_kernel, n_w=len(w_list), mode=mode),
        grid_spec=grid_spec,
        out_shape=jax.ShapeDtypeStruct((p, n_out), out_dtype),
        compiler_params=_cp(2),
        name="moe_mm_" + mode,
    )(blk_e, n_used, a, *w_list)


def _moe_combine_kernel(pos_ref, y_hbm, x_ref, gate_ref, gain_ref, o_ref, buf_ref, sem, *, tm, norm):
    i = pl.program_id(0)

    def issue(g, c):
        for u in range(ROW_DMA_UNROLL):
            r = g * ROW_DMA_UNROLL + u
            for k in range(TOP_K):
                _row_copy(y_hbm, pos_ref[(i * tm + r) * TOP_K + k], buf_ref.at[k], r, sem).start(priority=k % 2)
        return c

    lax.fori_loop(0, tm // ROW_DMA_UNROLL, issue, 0)
    for k in range(TOP_K):
        pltpu.make_async_copy(y_hbm.at[pl.ds(0, tm), :], buf_ref.at[k], sem).wait()
    gate = gate_ref[...]
    out = x_ref[...]
    y = gate[:, 2:3] * buf_ref[0]
    for k in range(1, TOP_K):
        y = y + gate[:, 2 + k:3 + k] * buf_ref[k]
    out = out + y
    if norm:
        out = out * lax.rsqrt(jnp.mean(out * out, axis=-1, keepdims=True) + NORM_EPS) * gain_ref[...]
    o_ref[...] = out


def moe_combine(x, yb, pos, gates, gain=None, tm=256):
    m, d = x.shape
    norm = gain is not None
    gain = jnp.ones((d,), F32) if gain is None else gain
    grid_spec = pltpu.PrefetchScalarGridSpec(
        num_scalar_prefetch=1,
        grid=(m // tm,),
        in_specs=[pl.BlockSpec(memory_space=pl.ANY),
                  pl.BlockSpec((tm, d), lambda i, pos: (i, 0)),
                  pl.BlockSpec((tm, LANES), lambda i, pos: (i, 0)),
                  pl.BlockSpec((1, d), lambda i, pos: (0, 0))],
        out_specs=pl.BlockSpec((tm, d), lambda i, pos: (i, 0)),
        scratch_shapes=[pltpu.VMEM((TOP_K, tm, d), F32), pltpu.SemaphoreType.DMA],
    )
    return pl.pallas_call(
        functools.partial(_moe_combine_kernel, tm=tm, norm=norm),
        grid_spec=grid_spec,
        out_shape=jax.ShapeDtypeStruct((m, d), F32),
        compiler_params=_cp(1),
        name="moe_combine",
    )(pos, yb, x, gates, gain.reshape(1, d).astype(F32))


def moe_layer(x, g, rw, rb, w1, w3, w2, moe_index, n_tok, tm, out_gain=None):
    m, d = x.shape
    h, route = moe_router(x, g, rw, rb, tm)
    top_i = route[:n_tok, :TOP_K].astype(jnp.int32)
    a = n_tok * TOP_K
    e_flat = top_i.reshape(a)
    order = jnp.argsort(e_flat)
    e_s = e_flat[order]
    tok_s = (order // TOP_K).astype(jnp.int32)
    counts = jnp.bincount(e_flat, length=N_EXPERTS)
    start = jnp.cumsum(counts) - counts
    padded = (counts + MOE_BLOCK - 1) // MOE_BLOCK * MOE_BLOCK
    pend = jnp.cumsum(padded)
    pstart = pend - padded
    dest = (pstart[e_s] + jnp.arange(a) - start[e_s]).astype(jnp.int32)
    nb = -(-a // MOE_BLOCK) + N_EXPERTS
    p = nb * MOE_BLOCK
    n_used = (pend[-1] // MOE_BLOCK).astype(jnp.int32)
    blk_e = jnp.minimum(jnp.searchsorted(pend, jnp.arange(nb) * MOE_BLOCK, side="right"), N_EXPERTS - 1)
    slot = jnp.arange(p)
    slot_e = blk_e[slot // MOE_BLOCK]
    local = slot - pstart[slot_e]
    tok_buf = jnp.where(local < counts[slot_e], tok_s[jnp.clip(start[slot_e] + local, 0, a - 1)], n_tok)
    tok_buf = tok_buf.astype(jnp.int32)
    blk_e = jnp.where(jnp.arange(nb) < n_used, blk_e, blk_e[n_used - 1]).astype(jnp.int32)
    pos = jnp.concatenate([dest[jnp.argsort(order)], jnp.zeros((m * TOP_K - a,), jnp.int32)])
    row_ok = (jnp.arange(m) < n_tok)[:, None]
    gates = jnp.where(row_ok, route, 0.0)
    n_used = n_used.reshape(1)
    xs = moe_gather(h, tok_buf, n_used)
    act = moe_matmul(xs, [w1, w3], moe_index, blk_e, n_used, "swiglu", BF16)
    yb = moe_matmul(act, [w2], moe_index, blk_e, n_used, "plain", F32)
    return moe_combine(x, yb, pos, gates, out_gain)


TM = 512


def _row_tile(m, pref=1152):
    t = (min(pref, m) // 16) * 16
    while m % t:
        t -= 16
    return t


def _slab(parts, m, dtype):
    rows = sum(p.shape[0] for p in parts)
    parts = [p.astype(dtype) for p in parts]
    return jnp.concatenate(parts + [jnp.zeros((m - rows, parts[0].shape[1]), dtype)], axis=0)


def kernel(x_prompt, x_sample, cache_nsa_cmp_kv, cache_nsa_sel_kv, cache_sb_kv, cache_nsa_win_kv, state_hgrn,
           page_table, attn_norm, w_in, cmp_pe_k, cmp_w1_k, cmp_w2_k, cmp_pe_v, cmp_w1_v, cmp_w2_v, hg_lb_logits,
           hg_norm, w_br_nsa, w_br_sb, w_br_hg, w_out, ffn_norm, ffn_w1, ffn_w3, ffn_w2, router_w, router_b,
           moe_w1, moe_w3, moe_w2, final_norm):
    bsz, seq, d = x_prompt.shape
    db, ds, _ = x_sample.shape
    depth = attn_norm.shape[0]
    n_p, n_s = bsz * seq, db * ds
    n_tok = n_p + n_s
    m = -(-(n_tok + 1) // TM) * TM
    n_pool, page = cache_nsa_cmp_kv.shape[1:3]
    n_pages = page_table.shape[1]
    past = n_pages * page
    assert past % CMP_STRIDE == 0 and ds < CMP_STRIDE and seq % CMP_STRIDE == 0

    x = _slab([x_prompt.reshape(n_p, d), x_sample.reshape(n_s, d)], m, F32)
    pt_flat = page_table.reshape(-1).astype(jnp.int32)
    cmp_pool = cache_nsa_cmp_kv.reshape(depth, n_pool, page * 2 * NSA_KV_HEADS, HEAD_DIM)
    sel_pool = cache_nsa_sel_kv.reshape(depth, n_pool, page * 2 * NSA_KV_HEADS, HEAD_DIM)
    sb_pool = cache_sb_kv.reshape(depth, n_pool, page * 2 * SB_HEADS, HEAD_DIM)
    nbuf = cache_nsa_win_kv.shape[2]
    win_buf = cache_nsa_win_kv.reshape(depth, db, nbuf * 2 * NSA_KV_HEADS, HEAD_DIM)
    rest_cols = w_in.shape[-1] - REST_START
    tmm = _row_tile(m)
    w_in_t = jnp.swapaxes(w_in, 1, 2)
    lb_all = jnp.cumsum(jax.nn.softmax(hg_lb_logits.astype(F32), axis=0), axis=0)
    kvs = (2, NSA_KV_HEADS, HEAD_DIM)

    states = []
    for l in range(depth):
        h = rmsnorm(x, attn_norm[l], BF16, TM)
        u_head = inproj_t(h, w_in_t, l, 0, HEAD_COLS, tmm, HEAD_COLS // 3)
        u_rest = inproj_t(h, w_in_t, l, REST_START, rest_cols, tmm, 1024)

        wk, pk = compress_params(cmp_pe_k[l], cmp_w1_k[l])
        wv, pv = compress_params(cmp_pe_v[l], cmp_w1_v[l])
        wcat, pecat = jnp.stack([wk, wv]), jnp.stack([pk, pv])
        w2s = jnp.stack([cmp_w2_k[l], cmp_w2_v[l]])
        lb = (lb_all[l] - lb_all[0]).reshape(HG_HEADS, HEAD_DIM)
        lbp = hgrn_lb_params(lb)

        kvc = compress_finish(compress_segments_prompt(u_head, bsz, seq, wcat, pecat), w2s)
        o_cmp, sel = cmp_select(u_head, 0, bsz, seq, kvc, seq // CMP_STRIDE - 1, -(-seq // SEL_BLOCK), 0)
        o_sel = sel_prompt(u_head, bsz, seq, sel)
        o_win = win_prompt(u_head, bsz, seq)
        nsa_p = nsa_combine(u_head, 0, o_cmp, o_sel, o_win)
        sb_p = sb_prompt(u_rest, bsz, seq)
        hg_p, s_p = hgrn(u_rest, 0, bsz, seq, lbp, hg_norm[l], jnp.zeros((bsz, HG_HEADS, HEAD_DIM, HEAD_DIM), F32))

        kvc_s = compress_finish(compress_segments_sample(cmp_pool, l, pt_flat, db, n_pages, wcat, pecat), w2s)
        o_cmp_s, sel_s = cmp_select(u_head, n_p, db, ds, kvc_s, past // CMP_STRIDE - 1,
                                    -(-(past + ds) // SEL_BLOCK), past, tq=ds)
        o_sel_s = sel_sample(u_head, n_p, sel_pool, l, pt_flat, sel_s, db, ds, n_pages)
        o_win_s = win_sample(u_head, n_p, win_buf, l, db, ds)
        nsa_s = nsa_combine(u_head, n_p, o_cmp_s, o_sel_s, o_win_s)
        sb_s = sb_sample(u_rest, n_p, sb_pool, l, pt_flat, db, ds, n_pages)
        hg_s, s_s = hgrn(u_rest, n_p, db, ds, lbp, hg_norm[l], state_hgrn[l])

        branches = [(nsa_p.astype(BF16), nsa_s.astype(BF16)), (sb_p.astype(BF16), sb_s.astype(BF16)),
                    (hg_p.astype(BF16), hg_s.astype(BF16))]
        merged = matmul(branches, [w_br_nsa, w_br_sb, w_br_hg], l, 0, d, "merge", BF16,
                        extras=[(u_rest, R_MERGE), (u_rest, R_MERGE + d), (u_rest, R_MERGE + 2 * d)], tm=TM, m=m)
        x = matmul([merged], [w_out], l, 0, d, "residual", F32, extras=[(x, 0)], tm=tmm)

        i = l // 2
        if l % 2 == 0:
            h2 = rmsnorm(x, ffn_norm[l], BF16, TM)
            act = matmul([h2], [ffn_w1, ffn_w3], i, 0, ffn_w1.shape[-1], "swiglu", BF16, tm=tmm)
            x = matmul([act], [ffn_w2], i, 0, d, "residual", F32, extras=[(x, 0)])
        else:
            x = moe_layer(x, ffn_norm[l], router_w[i], router_b[i], moe_w1, moe_w3, moe_w2, i, n_tok, TM,
                          out_gain=final_norm if l == depth - 1 else None)

        def head_cols(off, r0, r1, lead):
            return u_head[r0:r1, off:off + KV_W].reshape(lead + kvs)

        win_p = head_cols(OFF_WIN, 0, n_p, (bsz, seq))[:, seq - min(WINDOW, seq):]
        win_s = jnp.concatenate([cache_nsa_win_kv[l], head_cols(OFF_WIN, n_p, n_tok, (db, ds))], axis=1)[:, ds:]
        states.append((
            head_cols(OFF_CMP, 0, n_p, (bsz, seq)), head_cols(OFF_SEL, 0, n_p, (bsz, seq)), win_p,
            u_rest[:n_p, R_SB_K:R_SB_K + 2 * SB_W].reshape(bsz, seq, 2, SB_HEADS, HEAD_DIM), s_p,
            head_cols(OFF_CMP, n_p, n_tok, (db, ds)), head_cols(OFF_SEL, n_p, n_tok, (db, ds)), win_s,
            u_rest[n_p:n_tok, R_SB_K:R_SB_K + 2 * SB_W].reshape(db, ds, 2, SB_HEADS, HEAD_DIM), s_s))

    y = x if depth % 2 == 0 else rmsnorm(x, final_norm, F32, TM)
    stacked = [jnp.stack([st[i] for st in states]) for i in range(10)]
    return (y[:n_p].reshape(bsz, seq, d), y[n_p:n_tok].reshape(db, ds, d), *stacked)
```

```python
import functools

import jax
import jax.numpy as jnp
import numpy as np
from jax import lax
from jax.experimental import pallas as pl
from jax.experimental.pallas import tpu as pltpu

F32 = jnp.float32
BF16 = jnp.bfloat16

HEAD_DIM = 128
SCALE = HEAD_DIM ** -0.5
NSA_HEADS = 8
NSA_KV_HEADS = 2
NSA_GROUP = NSA_HEADS // NSA_KV_HEADS
CMP_LEN = 32
CMP_STRIDE = 16
SEL_BLOCK = 64
SEL_TOPK = 16
WINDOW = 512
SB_HEADS = 8
HG_HEADS = 8
HG_CHUNK = 64
HG_SUB = 16
HG_MIN_CHUNK = 16
N_EXPERTS = 8
TOP_K = 2
MOE_BLOCK = 512
NORM_EPS = 1e-6
NEG_BIG = -1e30

LANES = 128
SUBLANES = 8
VMEM_LIMIT = 56 * 1024 * 1024

NSA_W = NSA_HEADS * HEAD_DIM
KV_W = 2 * NSA_KV_HEADS * HEAD_DIM
OFF_CMP = NSA_W
OFF_SEL = OFF_CMP + KV_W
OFF_WIN = OFF_SEL + KV_W
OFF_NSA_GATE = OFF_WIN + KV_W
N_GATE = 3 * NSA_HEADS
HEAD_COLS = OFF_NSA_GATE + LANES
REST_START = OFF_NSA_GATE + N_GATE
SB_W = SB_HEADS * HEAD_DIM
HG_W = HG_HEADS * HEAD_DIM
R_SB_Q = 0
R_SB_K = R_SB_Q + SB_W
R_SB_V = R_SB_K + SB_W
R_HG_Q = R_SB_V + SB_W
R_HG_F = R_HG_Q + HG_W
R_HG_I = R_HG_F + HG_W
R_HG_G = R_HG_I + HG_W
R_MERGE = R_HG_G + HG_W


def _cp(n_axes, vmem=VMEM_LIMIT):
    return pltpu.CompilerParams(dimension_semantics=("arbitrary",) * n_axes, vmem_limit_bytes=vmem)


def _tile(n, pref, quantum=LANES):
    if n <= pref:
        return n
    t = (pref // quantum) * quantum
    while t > quantum and n % t:
        t -= quantum
    assert n % t == 0, (n, pref)
    return t


def _dot(a, b):
    return jnp.dot(a, b, preferred_element_type=F32)


def _dot_nt(a, b):
    return lax.dot_general(a, b, (((1,), (1,)), ((), ())), preferred_element_type=F32)


def _dot_tn(a, b):
    return lax.dot_general(a, b, (((0,), (0,)), ((), ())), preferred_element_type=F32)


def _split3(x):
    hi = x.astype(BF16)
    r = x - hi.astype(F32)
    mid = r.astype(BF16)
    lo = (r - mid.astype(F32)).astype(BF16)
    return hi, mid, lo


def _split2(x):
    hi = x.astype(BF16)
    return hi, (x - hi.astype(F32)).astype(BF16)


def _sigmoid(x):
    return 1.0 / (1.0 + jnp.exp(-x))


def _silu(x):
    return x * _sigmoid(x)


def _log_sigmoid(x):
    return jnp.minimum(x, 0.0) - jnp.log1p(jnp.exp(-jnp.abs(x)))


def _rmsnorm_kernel(x_ref, g_ref, o_ref):
    x = x_ref[...]
    y = x * lax.rsqrt(jnp.mean(x * x, axis=-1, keepdims=True) + NORM_EPS)
    o_ref[...] = (y * g_ref[...]).astype(o_ref.dtype)


def rmsnorm(x, g, out_dtype, tm):
    m, d = x.shape
    return pl.pallas_call(
        _rmsnorm_kernel,
        grid=(m // tm,),
        in_specs=[pl.BlockSpec((tm, d), lambda i: (i, 0)), pl.BlockSpec((1, d), lambda i: (0, 0))],
        out_specs=pl.BlockSpec((tm, d), lambda i: (i, 0)),
        out_shape=jax.ShapeDtypeStruct((m, d), out_dtype),
        compiler_params=_cp(1),
        name="rmsnorm",
    )(x, g.reshape(1, d))


def _mm_epilogue(mode, prods, x_refs):
    if mode == "plain":
        return prods[0]
    if mode == "merge":
        out = _sigmoid(x_refs[0][...]) * prods[0]
        for x_ref, p in zip(x_refs[1:], prods[1:]):
            out = out + _sigmoid(x_ref[...]) * p
        return out
    if mode == "residual":
        return x_refs[0][...] + prods[0]
    return _silu(prods[0]) * prods[1]


def _mm_kernel(*refs, n_a, n_w, n_extra, mode, cast, main_tiles):
    n_in = n_a if main_tiles is None else 2 * n_a
    a_in, rest = refs[:n_in], refs[n_in:]
    if main_tiles is None:
        a_vals = [r[...] for r in a_in]
    else:
        in_main = pl.program_id(1) < main_tiles
        a_vals = [jnp.where(in_main, a_in[2 * t][...], a_in[2 * t + 1][...]) for t in range(n_a)]
    w_refs = rest[:n_w]
    x_refs = rest[n_w:n_w + n_extra]
    o_ref = rest[n_w + n_extra]
    wbf_refs = rest[n_w + n_extra + 1:]

    if cast:
        @pl.when(pl.program_id(1) == 0)
        def _():
            for w_ref, wbf_ref in zip(w_refs, wbf_refs):
                wbf_ref[...] = w_ref[...].astype(BF16)
        ws = [r[...] for r in wbf_refs]
    else:
        ws = [r[...] for r in w_refs]

    prods = [_dot(a_vals[min(i, n_a - 1)], w) for i, w in enumerate(ws)]
    o_ref[...] = _mm_epilogue(mode, prods, x_refs).astype(o_ref.dtype)


def matmul(a_list, w_list, w_index, col0, n_out, mode, out_dtype, extras=(), tm=512, tn=512, m=None):
    split = isinstance(a_list[0], tuple)
    m = m or a_list[0].shape[0]
    tn = _tile(n_out, tn)
    assert col0 % tn == 0 and m % tm == 0
    cast = w_list[0].dtype != BF16
    in_specs, args, scratch = [], [], []
    main_tiles = None
    for a in a_list:
        if split:
            main, tail = a
            main_tiles = main.shape[0] // tm
            assert main.shape[0] % tm == 0 and m == (main_tiles + 1) * tm and tail.shape[0] <= tm
            tail = jnp.concatenate([tail, jnp.zeros((tm - tail.shape[0], tail.shape[1]), tail.dtype)], axis=0)
            in_specs.append(pl.BlockSpec((tm, main.shape[1]), lambda j, i, n=main_tiles: (jnp.minimum(i, n - 1), 0)))
            in_specs.append(pl.BlockSpec((tm, main.shape[1]), lambda j, i: (0, 0)))
            args += [main, tail]
            continue
        in_specs.append(pl.BlockSpec((tm, a.shape[1]), lambda j, i: (i, 0)))
        args.append(a)
    for w in w_list:
        k = w.shape[-2]
        if w.ndim == 3:
            in_specs.append(pl.BlockSpec((None, k, tn), lambda j, i: (w_index, 0, j + col0 // tn)))
        else:
            in_specs.append(pl.BlockSpec((k, tn), lambda j, i: (0, j + col0 // tn)))
        args.append(w)
        if cast:
            scratch.append(pltpu.VMEM((k, tn), BF16))
    for x, off in extras:
        assert off % tn == 0
        in_specs.append(pl.BlockSpec((tm, tn), lambda j, i, off=off: (i, j + off // tn)))
        args.append(x)
    return pl.pallas_call(
        functools.partial(_mm_kernel, n_a=len(a_list), n_w=len(w_list), n_extra=len(extras), mode=mode, cast=cast,
                          main_tiles=main_tiles),
        grid=(n_out // tn, m // tm),
        in_specs=in_specs,
        out_specs=pl.BlockSpec((tm, tn), lambda j, i: (i, j)),
        out_shape=jax.ShapeDtypeStruct((m, n_out), out_dtype),
        scratch_shapes=scratch,
        compiler_params=_cp(2),
        name="mm_" + mode,
    )(*args)


def _inproj_t_kernel(a_ref, wt_ref, o_ref, wbf_ref):
    @pl.when(pl.program_id(1) == 0)
    def _():
        wbf_ref[...] = wt_ref[0].T.astype(BF16)

    o_ref[...] = _dot(a_ref[...], wbf_ref[...])


def inproj_t(a, w_t, layer, row0, n_out, tm, tn):
    m, k = a.shape
    tn = _tile(n_out, tn)
    assert m % tm == 0 and row0 % SUBLANES == 0
    return pl.pallas_call(
        _inproj_t_kernel,
        grid=(n_out // tn, m // tm),
        in_specs=[pl.BlockSpec((tm, k), lambda j, i: (i, 0)),
                  pl.BlockSpec((pl.Element(1), pl.Element(tn), pl.Element(k)),
                               lambda j, i: (layer, pl.multiple_of(row0 + j * tn, SUBLANES), 0))],
        out_specs=pl.BlockSpec((tm, tn), lambda j, i: (i, j)),
        out_shape=jax.ShapeDtypeStruct((m, n_out), F32),
        scratch_shapes=[pltpu.VMEM((k, tn), BF16)],
        compiler_params=_cp(2),
        name="inproj_t",
    )(a, w_t)


def _suffix_matrix():
    j = lax.broadcasted_iota(jnp.int32, (LANES, 2 * LANES), 0)
    s = lax.broadcasted_iota(jnp.int32, (LANES, 2 * LANES), 1)
    return jnp.where((j > s) | (s >= LANES), 1.0, 0.0).astype(BF16)


def _sb_weights(z, mask, carry, umat):
    t = jnp.log(1.0 + jnp.exp(-jnp.abs(z)))
    ls_pos = jnp.minimum(z, 0.0) - t
    c = ls_pos - z
    if mask is not None:
        c = jnp.where(mask, c, 0.0)
    n_sub = z.shape[1] // LANES
    pieces = [None] * n_sub
    for sb in reversed(range(n_sub)):
        sl = slice(sb * LANES, (sb + 1) * LANES)
        hi, lo = _split2(c[:, sl])
        r = _dot(hi, umat) + _dot(lo, umat)
        w = jnp.exp(ls_pos[:, sl] + (carry + r[:, :LANES]))
        pieces[sb] = w if mask is None else jnp.where(mask[:, sl], w, 0.0)
        carry = carry + r[:, LANES:]
    a = pieces[0] if n_sub == 1 else jnp.concatenate(pieces, axis=1)
    return a, carry


SB_ZERO_LOG = -104.0


def _sb_prompt_kernel(q_ref, k_ref, v_ref, o_ref, acc_ref, carry_ref, *, t, hb):
    qi = pl.program_id(2)
    umat = _suffix_matrix()
    qs = [(q_ref[:, j * HEAD_DIM:(j + 1) * HEAD_DIM] * SCALE).astype(BF16) for j in range(hb)]
    row = lax.broadcasted_iota(jnp.int32, (t, t), 0)
    col = lax.broadcasted_iota(jnp.int32, (t, t), 1)

    def tile(kb, mask):
        rows = pl.ds(pl.multiple_of(kb * t, t), t)
        top = None
        for j in range(hb):
            cols = slice(j * HEAD_DIM, (j + 1) * HEAD_DIM)
            z = _dot_nt(qs[j], k_ref[rows, cols].astype(BF16))
            a, carry = _sb_weights(z, mask, carry_ref[j], umat)
            acc_ref[j] += _dot(a.astype(BF16), v_ref[rows, cols].astype(BF16))
            carry_ref[j] = carry
            top = jnp.max(carry) if top is None else jnp.maximum(top, jnp.max(carry))
        return top

    acc_ref[...] = jnp.zeros_like(acc_ref)
    carry_ref[...] = jnp.zeros_like(carry_ref)
    top = tile(qi, col < row)

    def cond(st):
        return jnp.logical_and(st[0] >= 0, st[1] > SB_ZERO_LOG)

    def body(st):
        return st[0] - 1, tile(st[0], None)

    lax.while_loop(cond, body, (qi - 1, top))
    for j in range(hb):
        o_ref[:, j * HEAD_DIM:(j + 1) * HEAD_DIM] = acc_ref[j].astype(o_ref.dtype)


def sb_prompt(u_rest, batch, seq, t=256, hb=4):
    t = min(t, seq)
    nq = seq // t
    bw = hb * HEAD_DIM
    cq, ck, cv = R_SB_Q // bw, R_SB_K // bw, R_SB_V // bw
    assert SB_HEADS % hb == 0
    return pl.pallas_call(
        functools.partial(_sb_prompt_kernel, t=t, hb=hb),
        grid=(batch, SB_HEADS // hb, nq),
        in_specs=[pl.BlockSpec((t, bw), lambda b, h, i: (b * nq + i, cq + h)),
                  pl.BlockSpec((seq, bw), lambda b, h, i: (b, ck + h)),
                  pl.BlockSpec((seq, bw), lambda b, h, i: (b, cv + h))],
        out_specs=pl.BlockSpec((t, bw), lambda b, h, i: (b * nq + i, h)),
        out_shape=jax.ShapeDtypeStruct((batch * seq, SB_W), BF16),
        scratch_shapes=[pltpu.VMEM((hb, t, LANES), F32), pltpu.VMEM((hb, t, LANES), F32)],
        compiler_params=_cp(3),
        name="sb_prompt",
    )(u_rest, u_rest, u_rest)


def _hgrn_chunk(qr, fr, v, gr, lbp, norm_w, st, c, sub, c_real):
    log_lb, log_1m_lb, one_m_lb = lbp[0:1], lbp[1:2], lbp[2:3]
    q = _silu(qr)
    k = one_m_lb * _sigmoid(-fr)
    bb = log_1m_lb + _log_sigmoid(fr)
    mx = jnp.maximum(log_lb, bb)
    logf = mx + jnp.log1p(jnp.exp(-jnp.abs(log_lb - bb)))
    row = lax.broadcasted_iota(jnp.int32, (c, c), 0)
    col = lax.broadcasted_iota(jnp.int32, (c, c), 1)
    tri = jnp.where(row >= col, 1.0, 0.0).astype(BF16)
    hi, mid, lo = _split3(logf)
    b = _dot(tri, hi) + _dot(tri, mid) + _dot(tri, lo)
    o = _dot_nt((q * jnp.exp(b)).astype(BF16), st.astype(BF16))
    ridx = lax.broadcasted_iota(jnp.int32, (c, HEAD_DIM), 0)
    lane = lax.broadcasted_iota(jnp.int32, (sub, c), 1)
    att_rows = []
    for i in range(c // sub):
        r0 = i * sub
        b_i = b[r0:r0 + sub]
        q_i = q[r0:r0 + sub]
        att_i = jnp.zeros((sub, c), F32)
        if i > 0:
            rho = b_i[0:1]
            earlier = ridx < r0
            k_dec = jnp.where(earlier, k * jnp.exp(jnp.where(earlier, rho - b, 0.0)), 0.0)
            att_i = _dot_nt((q_i * jnp.exp(b_i - rho)).astype(BF16), k_dec.astype(BF16))
        trow = lax.broadcasted_iota(jnp.int32, (sub, 1), 0)
        for s in range(sub):
            d = q_i * jnp.exp(jnp.where(trow >= s, b_i - b_i[s:s + 1], 0.0)) * k[r0 + s:r0 + s + 1]
            colsum = jnp.sum(d, axis=1, keepdims=True)
            att_i = att_i + jnp.where((lane == r0 + s) & (trow >= s), colsum, 0.0)
        att_rows.append(att_i)
    att = att_rows[0] if len(att_rows) == 1 else jnp.concatenate(att_rows, axis=0)
    o = o + _dot(att.astype(BF16), v.astype(BF16))
    b_end = b[c_real - 1:c_real]
    real = ridx < c_real
    k_end = jnp.where(real, k * jnp.exp(jnp.where(real, b_end - b, 0.0)), 0.0)
    st = st * jnp.exp(b_end) + _dot_tn(v.astype(BF16), k_end.astype(BF16))
    o = o * lax.rsqrt(jnp.mean(o * o, axis=-1, keepdims=True) + NORM_EPS) * norm_w
    return o * _silu(gr), st


def _hgrn_kernel(q_ref, f_ref, i_ref, g_ref, lbp_ref, nw_ref, s0_ref, o_ref, s_out_ref, st_ref, *,
                 c, sub, n_chunks, hb):
    t = pl.program_id(2)

    @pl.when(t == 0)
    def _():
        for j in range(hb):
            st_ref[j] = s0_ref[j].T

    nw = nw_ref[...]
    in_refs = (q_ref, f_ref, i_ref, g_ref)

    if c < HG_MIN_CHUNK:
        pad = jnp.zeros((HG_MIN_CHUNK - c, HEAD_DIM), F32)
        for j in range(hb):
            cols = slice(j * HEAD_DIM, (j + 1) * HEAD_DIM)
            ins = [jnp.concatenate([r[:, cols], pad], axis=0) for r in in_refs]
            o, st = _hgrn_chunk(*ins, lbp_ref[j], nw, st_ref[j], HG_MIN_CHUNK, HG_SUB, c)
            st_ref[j] = st
            o_ref[:, cols] = o[:c].astype(o_ref.dtype)
    else:
        def body(ci, carry):
            rows = pl.ds(pl.multiple_of(ci * c, c), c)
            for j in range(hb):
                cols = slice(j * HEAD_DIM, (j + 1) * HEAD_DIM)
                o, st = _hgrn_chunk(*[r[rows, cols] for r in in_refs], lbp_ref[j], nw, st_ref[j], c, sub, c)
                st_ref[j] = st
                o_ref[rows, cols] = o.astype(o_ref.dtype)
            return carry

        lax.fori_loop(0, n_chunks, body, 0)

    @pl.when(t == pl.num_programs(2) - 1)
    def _():
        for j in range(hb):
            s_out_ref[j] = st_ref[j].T


def hgrn_lb_params(lb):
    rows = jnp.stack([jnp.log(lb), jnp.log1p(-lb), 1.0 - lb], axis=1)
    return jnp.concatenate([rows, jnp.zeros((lb.shape[0], SUBLANES - 3, lb.shape[1]), F32)], axis=1)


def hgrn(u_rest, row0, batch, seq, lbp, norm_w, s0, tl=512, hb=8):
    c = min(HG_CHUNK, seq)
    sub = min(HG_SUB, c)
    tl = min(tl, seq)
    nt = seq // tl
    rb0 = row0 // tl
    bw = hb * HEAD_DIM
    assert row0 % tl == 0 and seq % tl == 0 and tl % c == 0 and (c % HG_SUB == 0 or nt == 1) and HG_HEADS % hb == 0

    def col(off):
        return pl.BlockSpec((tl, bw), lambda b, h, t, off=off: (rb0 + b * nt + t, off // bw + h))

    o, s_out = pl.pallas_call(
        functools.partial(_hgrn_kernel, c=c, sub=sub, n_chunks=tl // c, hb=hb),
        grid=(batch, HG_HEADS // hb, nt),
        in_specs=[col(R_HG_Q), col(R_HG_F), col(R_HG_I), col(R_HG_G),
                  pl.BlockSpec((hb, SUBLANES, LANES), lambda b, h, t: (h, 0, 0)),
                  pl.BlockSpec((1, LANES), lambda b, h, t: (0, 0)),
                  pl.BlockSpec((None, hb, HEAD_DIM, HEAD_DIM), lambda b, h, t: (b, h, 0, 0))],
        out_specs=[pl.BlockSpec((tl, bw), lambda b, h, t: (b * nt + t, h)),
                   pl.BlockSpec((None, hb, HEAD_DIM, HEAD_DIM), lambda b, h, t: (b, h, 0, 0))],
        out_shape=[jax.ShapeDtypeStruct((batch * seq, HG_W), BF16 if tl % 16 == 0 else F32),
                   jax.ShapeDtypeStruct((batch, HG_HEADS, HEAD_DIM, HEAD_DIM), F32)],
        scratch_shapes=[pltpu.VMEM((hb, HEAD_DIM, HEAD_DIM), F32)],
        compiler_params=_cp(3),
        name="hgrn",
    )(u_rest, u_rest, u_rest, u_rest, lbp, norm_w.reshape(1, LANES), s0)
    return o, s_out


SEG_W = CMP_STRIDE * HEAD_DIM


def compress_params(pe, w1):
    w = jnp.concatenate([w1[:SEG_W], w1[SEG_W:]], axis=1).astype(BF16)
    rows = jnp.stack([pe[:CMP_STRIDE].reshape(SEG_W), pe[CMP_STRIDE:].reshape(SEG_W)])
    return w, jnp.concatenate([rows, jnp.zeros((SUBLANES - 2, SEG_W), F32)]).astype(BF16)


def _segment_products(r, w, pe):
    bias = _dot(pe, w)
    bias = jnp.concatenate([bias[0:1, :LANES], bias[1:2, LANES:]], axis=1)
    return _dot(r.astype(BF16), w) + bias


def _cmp1_prompt_kernel(x_ref, w_ref, pe_ref, o_ref, r_ref, *, n_seg):
    for l in range(CMP_STRIDE):
        r_ref[:, l * LANES:(l + 1) * LANES] = x_ref[pl.ds(l, n_seg, stride=CMP_STRIDE), :]
    o_ref[...] = _segment_products(r_ref[...], w_ref[...], pe_ref[...])


def compress_segments_prompt(u_head, batch, seq, wcat, pecat, tr=512):
    tr = min(tr, seq)
    n_seg = tr // CMP_STRIDE
    nt = seq // tr
    c0 = OFF_CMP // LANES
    return pl.pallas_call(
        functools.partial(_cmp1_prompt_kernel, n_seg=n_seg),
        grid=(batch, 2, NSA_KV_HEADS, nt),
        in_specs=[pl.BlockSpec((tr, LANES), lambda b, kv, g, t: (b * nt + t, c0 + kv * NSA_KV_HEADS + g)),
                  pl.BlockSpec((None, SEG_W, 2 * LANES), lambda b, kv, g, t: (kv, 0, 0)),
                  pl.BlockSpec((None, SUBLANES, SEG_W), lambda b, kv, g, t: (kv, 0, 0))],
        out_specs=pl.BlockSpec((None, None, None, n_seg, 2 * LANES), lambda b, kv, g, t: (b, kv, g, t, 0)),
        out_shape=jax.ShapeDtypeStruct((batch, 2, NSA_KV_HEADS, seq // CMP_STRIDE, 2 * LANES), F32),
        scratch_shapes=[pltpu.VMEM((n_seg, SEG_W), F32)],
        compiler_params=_cp(4),
        name="cmp_segments_prompt",
    )(u_head, wcat, pecat)


def _cmp2_kernel(pq_ref, w2_ref, o_ref, *, nc):
    pq = pq_ref[...]
    q_next = pltpu.roll(pq[:, LANES:], shift=nc - 1, axis=0)
    hid = _silu(pq[:, :LANES] + q_next)
    out = _dot(hid.astype(BF16), w2_ref[...].astype(BF16))
    row = lax.broadcasted_iota(jnp.int32, out.shape, 0)
    o_ref[...] = jnp.where(row < nc - 1, out, 0.0)


def compress_finish(pq, w2):
    batch, _, _, nc, _ = pq.shape
    return pl.pallas_call(
        functools.partial(_cmp2_kernel, nc=nc),
        grid=(batch, 2, NSA_KV_HEADS),
        in_specs=[pl.BlockSpec((None, None, None, nc, 2 * LANES), lambda b, kv, g: (b, kv, g, 0, 0)),
                  pl.BlockSpec((None, HEAD_DIM, HEAD_DIM), lambda b, kv, g: (kv, 0, 0))],
        out_specs=pl.BlockSpec((None, None, None, nc, LANES), lambda b, kv, g: (b, kv, g, 0, 0)),
        out_shape=jax.ShapeDtypeStruct((batch, 2, NSA_KV_HEADS, nc, LANES), F32),
        compiler_params=_cp(3),
        name="cmp_finish",
    )(pq, w2)


RANK_PAIRWISE_MAX_TOKENS = 16


def _cmp_select_kernel(q_ref, kc_ref, vc_ref, o_ref, sel_ref, score_ref, *, tq, tqp, nc, n_cmp, n_slc, nsp, pos0):
    qi = pl.program_id(2)
    qb = q_ref[...]
    parts = []
    for h in range(NSA_GROUP):
        qh = qb[:, h * HEAD_DIM:(h + 1) * HEAD_DIM]
        if tqp > tq:
            qh = jnp.concatenate([qh, jnp.zeros((tqp - tq, HEAD_DIM), F32)], axis=0)
        parts.append(qh)
    q4 = jnp.concatenate(parts, axis=0).astype(BF16)
    rows = NSA_GROUP * tqp
    st = _dot_nt(kc_ref[...].astype(BF16), q4) * SCALE
    ci = lax.broadcasted_iota(jnp.int32, (nc, rows), 0)
    tok = lax.broadcasted_iota(jnp.int32, (nc, rows), 1) & (tqp - 1)
    tpos = pos0 + qi * tq + tok
    valid = (ci < n_cmp) & (ci * CMP_STRIDE + CMP_LEN - 1 <= tpos)
    st = jnp.where(valid, st, NEG_BIG)
    m = jnp.max(st, axis=0, keepdims=True)
    e = jnp.where(valid, jnp.exp(st - m), 0.0)
    den = jnp.sum(e, axis=0, keepdims=True)
    pt = e / jnp.where(den > 0, den, 1.0)
    o = _dot_tn(pt.astype(BF16), vc_ref[...].astype(BF16))
    for h in range(NSA_GROUP):
        o_ref[:, h * HEAD_DIM:(h + 1) * HEAD_DIM] = o[h * tqp:h * tqp + tq]
    psum = pt[:, 0:tqp]
    for h in range(1, NSA_GROUP):
        psum = psum + pt[:, h * tqp:(h + 1) * tqp]
    nsr = score_ref.shape[0]
    jj = lax.broadcasted_iota(jnp.int32, (nsr, nc), 0)
    ii = lax.broadcasted_iota(jnp.int32, (nsr, nc), 1)
    cover = ((ii * CMP_STRIDE < jj * SEL_BLOCK + SEL_BLOCK) & (ii * CMP_STRIDE + CMP_LEN - 1 >= jj * SEL_BLOCK)
             & (ii < n_cmp))
    cover = jnp.where(cover, 1.0, 0.0).astype(BF16)
    hi, mid, lo = _split3(psum)
    imp = _dot(cover, hi) + _dot(cover, mid) + _dot(cover, lo)
    j = lax.broadcasted_iota(jnp.int32, (nsr, tqp), 0)
    tpos2 = pos0 + qi * tq + lax.broadcasted_iota(jnp.int32, (nsr, tqp), 1)
    cur = tpos2 // SEL_BLOCK
    forced = (j == 0) | (j == cur) | (j == cur - 1)
    ok = (j * SEL_BLOCK <= tpos2) & (j < n_slc)
    score = jnp.where(ok, jnp.where(forced, NSA_GROUP + 1.0, imp), -1.0)

    if tq <= RANK_PAIRWISE_MAX_TOKENS:
        if nsp > nsr:
            score = jnp.concatenate([score, jnp.full((nsp - nsr, tqp), -1.0, F32)], axis=0)
        score_rows = score.T
        jp_i = lax.broadcasted_iota(jnp.int32, (nsp, nsp), 0)
        j_i = lax.broadcasted_iota(jnp.int32, (nsp, nsp), 1)
        rows_out = []
        for t in range(tq):
            other = score[:, t:t + 1]
            mine = score_rows[t:t + 1, :]
            ahead = jnp.where(other > mine, 1.0, jnp.where((other == mine) & (jp_i < j_i), 1.0, 0.0))
            rank_t = jnp.sum(ahead, axis=0, keepdims=True)
            rows_out.append(jnp.where((rank_t < SEL_TOPK) & (mine >= 0), 1.0, 0.0))
        sel_ref[...] = jnp.concatenate(rows_out, axis=0)
        return

    score_ref[...] = score

    def body(jp, rank):
        other = score_ref[pl.ds(jp, 1), :]
        ahead = jnp.where(other > score, 1.0, jnp.where((other == score) & (jp < j), 1.0, 0.0))
        return rank + ahead

    n_valid = jnp.minimum(n_slc, (pos0 + qi * tq + tq - 1) // SEL_BLOCK + 1)
    rank = lax.fori_loop(0, n_valid, body, jnp.zeros((nsr, tqp), F32))
    sel_t = jnp.where((rank < SEL_TOPK) & (score >= 0), 1.0, 0.0)
    if nsp > nsr:
        sel_t = jnp.concatenate([sel_t, jnp.zeros((nsp - nsr, tqp), F32)], axis=0)
    sel_ref[...] = sel_t.T[:tq]


def cmp_select(u_head, row0, batch, seq, kv_cmp, n_cmp, n_slc, pos0, tq=256):
    tq = min(tq, seq)
    tqp = max(tq, LANES)
    nt = seq // tq
    rb0 = row0 // tq
    nc = kv_cmp.shape[3]
    nsp = -(-n_slc // LANES) * LANES
    gw = NSA_GROUP * HEAD_DIM
    assert row0 % tq == 0 and seq % tq == 0 and tqp & (tqp - 1) == 0
    return pl.pallas_call(
        functools.partial(_cmp_select_kernel, tq=tq, tqp=tqp, nc=nc, n_cmp=n_cmp, n_slc=n_slc, nsp=nsp, pos0=pos0),
        grid=(batch, NSA_KV_HEADS, nt),
        in_specs=[pl.BlockSpec((tq, gw), lambda b, g, t: (rb0 + b * nt + t, g)),
                  pl.BlockSpec((None, None, None, nc, LANES), lambda b, g, t: (b, 0, g, 0, 0)),
                  pl.BlockSpec((None, None, None, nc, LANES), lambda b, g, t: (b, 1, g, 0, 0))],
        out_specs=[pl.BlockSpec((tq, gw), lambda b, g, t: (b * nt + t, g)),
                   pl.BlockSpec((None, None, tq, nsp), lambda b, g, t: (b, g, t, 0))],
        out_shape=[jax.ShapeDtypeStruct((batch * seq, NSA_W), F32),
                   jax.ShapeDtypeStruct((batch, NSA_KV_HEADS, seq, nsp), F32)],
        scratch_shapes=[pltpu.VMEM((-(-n_slc // 16) * 16, tqp), F32)],
        compiler_params=_cp(3),
        name="cmp_select",
    )(u_head, kv_cmp, kv_cmp)


def _flash_step(s, mask, v, m_ref, l_ref, acc_ref):
    s = jnp.where(mask, s, NEG_BIG)
    m_prev = m_ref[...]
    m_new = jnp.maximum(m_prev, jnp.max(s, axis=1, keepdims=True))
    e = jnp.where(mask, jnp.exp(s - m_new), 0.0)
    alpha = jnp.exp(m_prev - m_new)
    l_ref[...] = alpha * l_ref[...] + jnp.sum(e, axis=1, keepdims=True)
    acc_ref[...] = alpha * acc_ref[...] + _dot(e.astype(BF16), v)
    m_ref[...] = m_new


def _flash_init(m_ref, l_ref, acc_ref):
    m_ref[...] = jnp.full_like(m_ref, NEG_BIG)
    l_ref[...] = jnp.zeros_like(l_ref)
    acc_ref[...] = jnp.zeros_like(acc_ref)


def _flash_result(l_ref, acc_ref):
    l = l_ref[...]
    return acc_ref[...] / jnp.where(l > 0, l, 1.0)


def _stack_heads(qb, pad_to=None):
    parts = []
    for h in range(NSA_GROUP):
        qh = qb[:, h * HEAD_DIM:(h + 1) * HEAD_DIM]
        if pad_to is not None and pad_to > qh.shape[0]:
            qh = jnp.concatenate([qh, jnp.zeros((pad_to - qh.shape[0], HEAD_DIM), qh.dtype)], axis=0)
        parts.append(qh)
    return jnp.concatenate(parts, axis=0)


def _sel_prompt_kernel(q_ref, k_ref, v_ref, sel_ref, o_ref, m_ref, l_ref, acc_ref, *, tq, tk, nsp):
    qi = pl.program_id(2)
    q4 = (_stack_heads(q_ref[...]) * SCALE).astype(BF16)
    selb = sel_ref[...].astype(BF16)
    _flash_init(m_ref, l_ref, acc_ref)
    row = lax.broadcasted_iota(jnp.int32, (tq, tk), 0)
    col = lax.broadcasted_iota(jnp.int32, (tq, tk), 1)
    jj = lax.broadcasted_iota(jnp.int32, (nsp, tk), 0)
    kk = lax.broadcasted_iota(jnp.int32, (nsp, tk), 1) // SEL_BLOCK

    def body(kb, c):
        keys = pl.ds(pl.multiple_of(kb * tk, tk), tk)
        s = _dot_nt(q4, k_ref[keys, :].astype(BF16))
        expand = jnp.where(jj == kb * (tk // SEL_BLOCK) + kk, 1.0, 0.0).astype(BF16)
        chosen = _dot(selb, expand)
        ok = (chosen > 0.5) & (kb * tk + col <= qi * tq + row)
        bias = jnp.where(ok, 0.0, 2.0 * NEG_BIG)
        s = (s.reshape(NSA_GROUP, tq, tk) + bias[None]).reshape(NSA_GROUP * tq, tk)
        m_prev = m_ref[...]
        m_new = jnp.maximum(m_prev, jnp.max(s, axis=1, keepdims=True))
        e = jnp.exp(s - m_new)
        alpha = jnp.exp(m_prev - m_new)
        l_ref[...] = alpha * l_ref[...] + jnp.sum(e, axis=1, keepdims=True)
        acc_ref[...] = alpha * acc_ref[...] + _dot(e.astype(BF16), v_ref[keys, :].astype(BF16))
        m_ref[...] = m_new
        return c

    lax.fori_loop(0, (qi * tq + tq - 1) // tk + 1, body, 0)
    o = _flash_result(l_ref, acc_ref)
    for h in range(NSA_GROUP):
        o_ref[:, h * HEAD_DIM:(h + 1) * HEAD_DIM] = o[h * tq:(h + 1) * tq]


def sel_prompt(u_head, batch, seq, sel, tq=256, tk=512):
    tq = min(tq, seq)
    tk = min(tk, seq)
    nq = seq // tq
    nsp = sel.shape[-1]
    ck = OFF_SEL // LANES
    cv = ck + NSA_KV_HEADS
    gw = NSA_GROUP * HEAD_DIM
    rows = NSA_GROUP * tq
    assert seq % tq == 0 and seq % tk == 0 and tk % SEL_BLOCK == 0
    return pl.pallas_call(
        functools.partial(_sel_prompt_kernel, tq=tq, tk=tk, nsp=nsp),
        grid=(batch, NSA_KV_HEADS, nq),
        in_specs=[pl.BlockSpec((tq, gw), lambda b, g, i: (b * nq + i, g)),
                  pl.BlockSpec((seq, LANES), lambda b, g, i: (b, ck + g)),
                  pl.BlockSpec((seq, LANES), lambda b, g, i: (b, cv + g)),
                  pl.BlockSpec((None, None, tq, nsp), lambda b, g, i: (b, g, i, 0))],
        out_specs=pl.BlockSpec((tq, gw), lambda b, g, i: (b * nq + i, g)),
        out_shape=jax.ShapeDtypeStruct((batch * seq, NSA_W), F32),
        scratch_shapes=[pltpu.VMEM((rows, 1), F32), pltpu.VMEM((rows, 1), F32), pltpu.VMEM((rows, HEAD_DIM), F32)],
        compiler_params=_cp(3),
        name="sel_prompt",
    )(u_head, u_head, u_head, sel)


def _win_prompt_kernel(q_ref, k_ref, v_ref, o_ref, *, tq, nk, seq):
    q0 = pl.program_id(2) * tq
    start = pl.multiple_of(jnp.clip(q0 - WINDOW, 0, seq - nk), tq)
    keys = pl.ds(start, nk)
    q4 = (_stack_heads(q_ref[...]) * SCALE).astype(BF16)
    s = _dot_nt(q4, k_ref[keys, :].astype(BF16))
    d = (q0 + lax.broadcasted_iota(jnp.int32, (tq, nk), 0)) - (start + lax.broadcasted_iota(jnp.int32, (tq, nk), 1))
    bias = jnp.where((d >= 0) & (d < WINDOW), 0.0, NEG_BIG)
    s = s + jnp.concatenate([bias] * NSA_GROUP, axis=0)
    e = jnp.exp(s - jnp.max(s, axis=1, keepdims=True))
    o = _dot(e.astype(BF16), v_ref[keys, :].astype(BF16)) / jnp.sum(e, axis=1, keepdims=True)
    for h in range(NSA_GROUP):
        o_ref[:, h * HEAD_DIM:(h + 1) * HEAD_DIM] = o[h * tq:(h + 1) * tq]


def win_prompt(u_head, batch, seq, tq=256):
    tq = min(tq, seq)
    nk = min(seq, WINDOW + tq)
    nq = seq // tq
    ck = OFF_WIN // LANES
    cv = ck + NSA_KV_HEADS
    gw = NSA_GROUP * HEAD_DIM
    assert seq % tq == 0 and WINDOW % tq == 0
    return pl.pallas_call(
        functools.partial(_win_prompt_kernel, tq=tq, nk=nk, seq=seq),
        grid=(batch, NSA_KV_HEADS, nq),
        in_specs=[pl.BlockSpec((tq, gw), lambda b, g, i: (b * nq + i, g)),
                  pl.BlockSpec((seq, LANES), lambda b, g, i: (b, ck + g)),
                  pl.BlockSpec((seq, LANES), lambda b, g, i: (b, cv + g))],
        out_specs=pl.BlockSpec((tq, gw), lambda b, g, i: (b * nq + i, g)),
        out_shape=jax.ShapeDtypeStruct((batch * seq, NSA_W), F32),
        compiler_params=_cp(3),
        name="win_prompt",
    )(u_head, u_head, u_head)


def _nsa_combine_kernel(gate_ref, a_ref, b_ref, c_ref, o_ref):
    gate = _sigmoid(gate_ref[...])
    for h in range(NSA_HEADS):
        sl = slice(h * HEAD_DIM, (h + 1) * HEAD_DIM)
        out = (gate[:, h:h + 1] * a_ref[:, sl] + gate[:, NSA_HEADS + h:NSA_HEADS + h + 1] * b_ref[:, sl]
               + gate[:, 2 * NSA_HEADS + h:2 * NSA_HEADS + h + 1] * c_ref[:, sl])
        o_ref[:, sl] = out.astype(o_ref.dtype)


def nsa_combine(u_head, row0, o_cmp, o_sel, o_win, tm=256):
    n = o_cmp.shape[0]
    tm = min(tm, n)
    rb0 = row0 // tm
    assert row0 % tm == 0 and n % tm == 0
    spec = pl.BlockSpec((tm, NSA_W), lambda i: (i, 0))
    return pl.pallas_call(
        _nsa_combine_kernel,
        grid=(n // tm,),
        in_specs=[pl.BlockSpec((tm, LANES), lambda i: (rb0 + i, OFF_NSA_GATE // LANES)), spec, spec, spec],
        out_specs=spec,
        out_shape=jax.ShapeDtypeStruct((n, NSA_W), BF16 if tm % 16 == 0 else F32),
        compiler_params=_cp(1),
        name="nsa_combine",
    )(u_head, o_cmp, o_sel, o_win)


def _page_rows(page_ref, first, n_rows, stride):
    return page_ref[pl.ds(first, n_rows, stride=stride), :]


def _pad_rows(x, n):
    if x.shape[0] >= n:
        return x
    return jnp.concatenate([x, jnp.zeros((n - x.shape[0], x.shape[1]), x.dtype)], axis=0)


def _cmp1_sample_kernel(pt_ref, *refs, pps, page):
    page_refs = refs[:pps]
    w_ref, pe_ref, o_ref, r_ref = refs[pps:]
    segs = page // CMP_STRIDE
    stride = CMP_STRIDE * 2 * NSA_KV_HEADS
    for kv in range(2):
        for g in range(NSA_KV_HEADS):
            for k in range(pps):
                for l in range(CMP_STRIDE):
                    r_ref[k * segs:(k + 1) * segs, l * LANES:(l + 1) * LANES] = _page_rows(
                        page_refs[k], l * 2 * NSA_KV_HEADS + kv * NSA_KV_HEADS + g, segs, stride)
            o_ref[kv, g] = _segment_products(r_ref[...], w_ref[kv], pe_ref[kv])


def compress_segments_sample(pool, layer, pt_flat, batch, n_pages, wcat, pecat, pps=16):
    page = pool.shape[2] // (2 * NSA_KV_HEADS)
    pps = min(pps, n_pages)
    assert n_pages % pps == 0 and page % CMP_STRIDE == 0
    segs = page // CMP_STRIDE
    in_specs = [pl.BlockSpec((None, None, pool.shape[2], LANES),
                             lambda b, s, pt, k=k: (layer, pt[b * n_pages + s * pps + k], 0, 0)) for k in range(pps)]
    in_specs += [pl.BlockSpec((2, SEG_W, 2 * LANES), lambda b, s, pt: (0, 0, 0)),
                 pl.BlockSpec((2, SUBLANES, SEG_W), lambda b, s, pt: (0, 0, 0))]
    grid_spec = pltpu.PrefetchScalarGridSpec(
        num_scalar_prefetch=1,
        grid=(batch, n_pages // pps),
        in_specs=in_specs,
        out_specs=pl.BlockSpec((None, 2, NSA_KV_HEADS, pps * segs, 2 * LANES), lambda b, s, pt: (b, 0, 0, s, 0)),
        scratch_shapes=[pltpu.VMEM((pps * segs, SEG_W), F32)],
    )
    return pl.pallas_call(
        functools.partial(_cmp1_sample_kernel, pps=pps, page=page),
        grid_spec=grid_spec,
        out_shape=jax.ShapeDtypeStruct((batch, 2, NSA_KV_HEADS, n_pages * segs, 2 * LANES), F32),
        compiler_params=_cp(2),
        name="cmp_segments_sample",
    )(pt_flat, *([pool] * pps), wcat, pecat)


def _sel_sample_kernel(pt_ref, *refs, pps, page, ds, past, nsp):
    q_ref, new_ref, sel_ref, selstep_ref = refs[:4]
    page_refs = refs[4:4 + pps]
    o_ref, m_ref, l_ref, acc_ref = refs[4 + pps:]
    s = pl.program_id(1)
    rows = NSA_GROUP * ds
    keys = pps * page
    bps = keys // SEL_BLOCK
    stride = 2 * NSA_KV_HEADS

    @pl.when(s == 0)
    def _():
        _flash_init(m_ref, l_ref, acc_ref)

    jj = lax.broadcasted_iota(jnp.int32, (bps, keys), 0)
    kk = lax.broadcasted_iota(jnp.int32, (bps, keys), 1)
    expand = jnp.where(jj == kk // SEL_BLOCK, 1.0, 0.0).astype(BF16)
    for g in range(NSA_KV_HEADS):
        q4 = _stack_heads(q_ref[:, g * NSA_GROUP * HEAD_DIM:(g + 1) * NSA_GROUP * HEAD_DIM]).astype(BF16)
        sel4 = jnp.concatenate([sel_ref[g]] * NSA_GROUP, axis=0)
        sel_here = jnp.concatenate([selstep_ref[g]] * NSA_GROUP, axis=0)
        k_all = jnp.concatenate([_page_rows(r, g, page, stride) for r in page_refs], axis=0).astype(BF16)
        v_all = jnp.concatenate([_page_rows(r, NSA_KV_HEADS + g, page, stride) for r in page_refs],
                                axis=0).astype(BF16)
        sc = _dot_nt(q4, k_all) * SCALE
        chosen = _dot(sel_here.astype(BF16), expand)
        _flash_step(sc, chosen > 0.5, v_all, m_ref.at[g], l_ref.at[g], acc_ref.at[g])

        @pl.when(s == pl.num_programs(1) - 1)
        def _():
            k_new = _pad_rows(new_ref[:, g * HEAD_DIM:(g + 1) * HEAD_DIM], LANES).astype(BF16)
            v_new = _pad_rows(new_ref[:, (NSA_KV_HEADS + g) * HEAD_DIM:(NSA_KV_HEADS + g + 1) * HEAD_DIM],
                              LANES).astype(BF16)
            sn = _dot_nt(q4, k_new) * SCALE
            blk = past // SEL_BLOCK
            j = lax.broadcasted_iota(jnp.int32, (rows, LANES), 1)
            t = lax.broadcasted_iota(jnp.int32, (rows, LANES), 0) % ds
            mask = (sel4[:, blk:blk + 1] > 0.5) & (j <= t)
            _flash_step(sn, mask, v_new, m_ref.at[g], l_ref.at[g], acc_ref.at[g])
            o = _flash_result(l_ref.at[g], acc_ref.at[g])
            for h in range(NSA_GROUP):
                c0 = (g * NSA_GROUP + h) * HEAD_DIM
                o_ref[:, c0:c0 + HEAD_DIM] = o[h * ds:(h + 1) * ds]


def sel_sample(u_head, row0, pool, layer, pt_flat, sel, batch, ds, n_pages, pps=32):
    page = pool.shape[2] // (2 * NSA_KV_HEADS)
    pps = min(pps, n_pages)
    past = n_pages * page
    nsp = sel.shape[-1]
    rows = NSA_GROUP * ds
    ns = n_pages // pps
    bps = pps * page // SEL_BLOCK
    assert n_pages % pps == 0 and row0 % ds == 0 and past % SEL_BLOCK == 0 and ds <= SEL_BLOCK
    assert page % SEL_BLOCK == 0
    sel_steps = sel[..., :ns * bps].reshape(batch, NSA_KV_HEADS, ds, ns, bps).transpose(0, 1, 3, 2, 4)
    in_specs = [pl.BlockSpec((ds, NSA_W), lambda b, s, pt: (row0 // ds + b, 0)),
                pl.BlockSpec((ds, KV_W), lambda b, s, pt: (row0 // ds + b, OFF_SEL // KV_W)),
                pl.BlockSpec((None, NSA_KV_HEADS, ds, nsp), lambda b, s, pt: (b, 0, 0, 0)),
                pl.BlockSpec((None, NSA_KV_HEADS, None, ds, bps), lambda b, s, pt: (b, 0, s, 0, 0))]
    in_specs += [pl.BlockSpec((None, None, pool.shape[2], LANES),
                              lambda b, s, pt, k=k: (layer, pt[b * n_pages + s * pps + k], 0, 0)) for k in range(pps)]
    grid_spec = pltpu.PrefetchScalarGridSpec(
        num_scalar_prefetch=1,
        grid=(batch, n_pages // pps),
        in_specs=in_specs,
        out_specs=pl.BlockSpec((ds, NSA_W), lambda b, s, pt: (b, 0)),
        scratch_shapes=[pltpu.VMEM((NSA_KV_HEADS, rows, 1), F32), pltpu.VMEM((NSA_KV_HEADS, rows, 1), F32),
                        pltpu.VMEM((NSA_KV_HEADS, rows, HEAD_DIM), F32)],
    )
    return pl.pallas_call(
        functools.partial(_sel_sample_kernel, pps=pps, page=page, ds=ds, past=past, nsp=nsp),
        grid_spec=grid_spec,
        out_shape=jax.ShapeDtypeStruct((batch * ds, NSA_W), F32),
        compiler_params=_cp(2),
        name="sel_sample",
    )(pt_flat, u_head, u_head, sel, sel_steps, *([pool] * pps))


def _win_sample_kernel(q_ref, new_ref, buf_ref, o_ref, m_ref, l_ref, acc_ref, *, ds, nbuf):
    rows = NSA_GROUP * ds
    stride = 2 * NSA_KV_HEADS
    keys = nbuf + LANES
    i = lax.broadcasted_iota(jnp.int32, (rows, keys), 1)
    t = lax.broadcasted_iota(jnp.int32, (rows, keys), 0) % ds
    d = jnp.where(i < nbuf, t + nbuf - i, t - (i - nbuf))
    mask = (d >= 0) & (d < WINDOW) & (i < nbuf + ds)
    for g in range(NSA_KV_HEADS):
        _flash_init(m_ref, l_ref, acc_ref)
        q4 = _stack_heads(q_ref[:, g * NSA_GROUP * HEAD_DIM:(g + 1) * NSA_GROUP * HEAD_DIM]).astype(BF16)
        k_new = _pad_rows(new_ref[:, g * HEAD_DIM:(g + 1) * HEAD_DIM], LANES)
        v_new = _pad_rows(new_ref[:, (NSA_KV_HEADS + g) * HEAD_DIM:(NSA_KV_HEADS + g + 1) * HEAD_DIM], LANES)
        k_all = jnp.concatenate([_page_rows(buf_ref, g, nbuf, stride), k_new], axis=0).astype(BF16)
        v_all = jnp.concatenate([_page_rows(buf_ref, NSA_KV_HEADS + g, nbuf, stride), v_new], axis=0).astype(BF16)
        _flash_step(_dot_nt(q4, k_all) * SCALE, mask, v_all, m_ref, l_ref, acc_ref)
        o = _flash_result(l_ref, acc_ref)
        for h in range(NSA_GROUP):
            c0 = (g * NSA_GROUP + h) * HEAD_DIM
            o_ref[:, c0:c0 + HEAD_DIM] = o[h * ds:(h + 1) * ds]


def win_sample(u_head, row0, buf, layer, batch, ds):
    nbuf = buf.shape[2] // (2 * NSA_KV_HEADS)
    rows = NSA_GROUP * ds
    return pl.pallas_call(
        functools.partial(_win_sample_kernel, ds=ds, nbuf=nbuf),
        grid=(batch,),
        in_specs=[pl.BlockSpec((ds, NSA_W), lambda b: (row0 // ds + b, 0)),
                  pl.BlockSpec((ds, KV_W), lambda b: (row0 // ds + b, OFF_WIN // KV_W)),
                  pl.BlockSpec((None, None, buf.shape[2], LANES), lambda b: (layer, b, 0, 0))],
        out_specs=pl.BlockSpec((ds, NSA_W), lambda b: (b, 0)),
        out_shape=jax.ShapeDtypeStruct((batch * ds, NSA_W), F32),
        scratch_shapes=[pltpu.VMEM((rows, 1), F32), pltpu.VMEM((rows, 1), F32), pltpu.VMEM((rows, HEAD_DIM), F32)],
        compiler_params=_cp(1),
        name="win_sample",
    )(u_head, u_head, buf)


SB_ROWS = 16


def _sb_sample_kernel(pt_ref, q_ref, kn_ref, vn_ref, pool_hbm, o_ref, acc_ref, carry_ref, kv_ref, sem, *,
                      pps, page, ds, n_pages, layer):
    b = pl.program_id(0)
    stride = 2 * SB_HEADS
    umat = _suffix_matrix()
    q_heads = [_pad_rows(q_ref[:, h * HEAD_DIM:(h + 1) * HEAD_DIM] * SCALE, SB_ROWS).astype(BF16)
               for h in range(SB_HEADS)]
    rows = SB_HEADS * SB_ROWS
    real_row = lax.broadcasted_iota(jnp.int32, (rows, LANES), 0) % SB_ROWS < ds

    def accumulate(k_heads, v_heads, mask):
        z = jnp.concatenate([_dot_nt(q_heads[h], k_heads[h]) for h in range(SB_HEADS)], axis=0)
        a, carry = _sb_weights(z, mask, carry_ref[...], umat)
        carry_ref[...] = carry
        a = a.astype(BF16)
        for h in range(SB_HEADS):
            sl = slice(h * SB_ROWS, (h + 1) * SB_ROWS)
            acc_ref[sl, :] += _dot(a[sl], v_heads[h])
        return jnp.max(jnp.where(real_row, carry, NEG_BIG))

    acc_ref[...] = jnp.zeros_like(acc_ref)
    carry_ref[...] = jnp.zeros_like(carry_ref)
    k_new = [_pad_rows(kn_ref[:, h * HEAD_DIM:(h + 1) * HEAD_DIM], LANES).astype(BF16) for h in range(SB_HEADS)]
    v_new = [_pad_rows(vn_ref[:, h * HEAD_DIM:(h + 1) * HEAD_DIM], LANES).astype(BF16) for h in range(SB_HEADS)]
    j = lax.broadcasted_iota(jnp.int32, (rows, LANES), 1)
    t = lax.broadcasted_iota(jnp.int32, (rows, LANES), 0) % SB_ROWS
    top = accumulate(k_new, v_new, j < t)

    def page_copy(g, k):
        pid = pt_ref[b * n_pages + n_pages - (g + 1) * pps + k]
        return pltpu.make_async_copy(pool_hbm.at[layer, pid], kv_ref.at[k], sem)

    def cond(st):
        return jnp.logical_and(st[0] < n_pages // pps, st[1] > SB_ZERO_LOG)

    def body(st):
        g = st[0]
        for k in range(pps):
            page_copy(g, k).start()
        for k in range(pps):
            page_copy(g, k).wait()
        k_heads = [jnp.concatenate([_page_rows(kv_ref.at[k], h, page, stride) for k in range(pps)],
                                   axis=0).astype(BF16) for h in range(SB_HEADS)]
        v_heads = [jnp.concatenate([_page_rows(kv_ref.at[k], SB_HEADS + h, page, stride) for k in range(pps)],
                                   axis=0).astype(BF16) for h in range(SB_HEADS)]
        return g + 1, accumulate(k_heads, v_heads, None)

    lax.while_loop(cond, body, (0, top))
    for h in range(SB_HEADS):
        o_ref[:, h * HEAD_DIM:(h + 1) * HEAD_DIM] = acc_ref[h * SB_ROWS:h * SB_ROWS + ds, :]


def sb_sample(u_rest, row0, pool, layer, pt_flat, batch, ds, n_pages, pps=2):
    page = pool.shape[2] // (2 * SB_HEADS)
    pps = min(pps, n_pages)
    assert n_pages % pps == 0 and row0 % ds == 0 and ds <= SB_ROWS
    rb = row0 // ds
    grid_spec = pltpu.PrefetchScalarGridSpec(
        num_scalar_prefetch=1,
        grid=(batch,),
        in_specs=[pl.BlockSpec((ds, SB_W), lambda b, pt: (rb + b, R_SB_Q // SB_W)),
                  pl.BlockSpec((ds, SB_W), lambda b, pt: (rb + b, R_SB_K // SB_W)),
                  pl.BlockSpec((ds, SB_W), lambda b, pt: (rb + b, R_SB_V // SB_W)),
                  pl.BlockSpec(memory_space=pl.ANY)],
        out_specs=pl.BlockSpec((ds, SB_W), lambda b, pt: (b, 0)),
        scratch_shapes=[pltpu.VMEM((SB_HEADS * SB_ROWS, HEAD_DIM), F32), pltpu.VMEM((SB_HEADS * SB_ROWS, LANES), F32),
                        pltpu.VMEM((pps, pool.shape[2], LANES), F32), pltpu.SemaphoreType.DMA],
    )
    return pl.pallas_call(
        functools.partial(_sb_sample_kernel, pps=pps, page=page, ds=ds, n_pages=n_pages, layer=layer),
        grid_spec=grid_spec,
        out_shape=jax.ShapeDtypeStruct((batch * ds, SB_W), F32),
        compiler_params=_cp(1),
        name="sb_sample",
    )(pt_flat, u_rest, u_rest, u_rest, pool)


def _router_kernel(x_ref, g_ref, w_ref, b_ref, h_ref, o_ref):
    x = x_ref[...]
    h = x * lax.rsqrt(jnp.mean(x * x, axis=-1, keepdims=True) + NORM_EPS) * g_ref[...]
    h_ref[...] = h
    hh, hm, hl = _split3(h)
    wh, wm, wl = _split3(w_ref[...])
    logits = (_dot(hh, wh) + _dot(hh, wm) + _dot(hm, wh) + _dot(hh, wl) + _dot(hl, wh) + _dot(hm, wm)) + b_ref[...]
    lane = lax.broadcasted_iota(jnp.int32, logits.shape, 1)
    logits = jnp.where(lane < N_EXPERTS, logits, NEG_BIG)
    m1 = jnp.max(logits, axis=1, keepdims=True)
    i1 = jnp.min(jnp.where(logits == m1, lane, LANES), axis=1, keepdims=True)
    rest = jnp.where(lane == i1, NEG_BIG, logits)
    m2 = jnp.max(rest, axis=1, keepdims=True)
    i2 = jnp.min(jnp.where(rest == m2, lane, LANES), axis=1, keepdims=True)
    e = jnp.exp(m2 - m1)
    g1 = 1.0 / (1.0 + e)
    g2 = e / (1.0 + e)
    o_ref[...] = jnp.where(lane == 0, i1.astype(F32), jnp.where(lane == 1, i2.astype(F32),
                           jnp.where(lane == 2, g1, jnp.where(lane == 3, g2, 0.0))))


def moe_router(x, g, rw, rb, tm):
    m, d = x.shape
    w = jnp.zeros((d, LANES), F32).at[:, :N_EXPERTS].set(rw)
    b = jnp.zeros((1, LANES), F32).at[0, :N_EXPERTS].set(rb.astype(F32))
    return pl.pallas_call(
        _router_kernel,
        grid=(m // tm,),
        in_specs=[pl.BlockSpec((tm, d), lambda i: (i, 0)), pl.BlockSpec((1, d), lambda i: (0, 0)),
                  pl.BlockSpec((d, LANES), lambda i: (0, 0)), pl.BlockSpec((1, LANES), lambda i: (0, 0))],
        out_specs=[pl.BlockSpec((tm, d), lambda i: (i, 0)), pl.BlockSpec((tm, LANES), lambda i: (i, 0))],
        out_shape=[jax.ShapeDtypeStruct((m, d), F32), jax.ShapeDtypeStruct((m, LANES), F32)],
        compiler_params=_cp(1),
        name="moe_router",
    )(x, g.reshape(1, d), w, b)


ROW_DMA_UNROLL = 8


def _row_copy(src_hbm, row, dst_ref, r, sem):
    return pltpu.make_async_copy(src_hbm.at[pl.ds(row, 1), :], dst_ref.at[pl.ds(r, 1), :], sem)


def _gather_kernel(tok_ref, nu_ref, h_hbm, o_ref, buf_ref, sem):
    blk = pl.program_id(0)
    n = buf_ref.shape[0]

    @pl.when(blk < nu_ref[0])
    def _():
        def issue(g, c):
            for u in range(ROW_DMA_UNROLL):
                r = g * ROW_DMA_UNROLL + u
                _row_copy(h_hbm, tok_ref[blk * n + r], buf_ref, r, sem).start(priority=u % 2)
            return c

        lax.fori_loop(0, n // ROW_DMA_UNROLL, issue, 0)
        pltpu.make_async_copy(h_hbm.at[pl.ds(0, n), :], buf_ref, sem).wait()
        o_ref[...] = buf_ref[...].astype(o_ref.dtype)

    @pl.when(blk >= nu_ref[0])
    def _():
        o_ref[...] = jnp.zeros_like(o_ref)


def _used_block(i, nu):
    return jnp.minimum(i, nu[0] - 1)


def moe_gather(h, tok_buf, n_used):
    p = tok_buf.shape[0]
    d = h.shape[1]
    grid_spec = pltpu.PrefetchScalarGridSpec(
        num_scalar_prefetch=2,
        grid=(p // MOE_BLOCK,),
        in_specs=[pl.BlockSpec(memory_space=pl.ANY)],
        out_specs=pl.BlockSpec((MOE_BLOCK, d), lambda i, tok, nu: (i, 0)),
        scratch_shapes=[pltpu.VMEM((MOE_BLOCK, d), F32), pltpu.SemaphoreType.DMA],
    )
    return pl.pallas_call(
        _gather_kernel,
        grid_spec=grid_spec,
        out_shape=jax.ShapeDtypeStruct((p, d), BF16),
        compiler_params=_cp(1),
        name="moe_gather",
    )(tok_buf, n_used, h)


def _moe_mm_kernel(be_ref, nu_ref, *refs, n_w, mode):
    a_ref = refs[0]
    w_refs = refs[1:1 + n_w]
    o_ref = refs[1 + n_w]
    wbf_refs = refs[2 + n_w:]
    blk = pl.program_id(1)
    changed = jnp.logical_or(blk == 0, be_ref[blk] != be_ref[jnp.maximum(blk - 1, 0)])

    @pl.when(changed)
    def _():
        for w_ref, wbf_ref in zip(w_refs, wbf_refs):
            wbf_ref[...] = w_ref[...].astype(BF16)

    @pl.when(blk < nu_ref[0])
    def _():
        prods = [_dot(a_ref[...], r[...]) for r in wbf_refs]
        o_ref[...] = _mm_epilogue(mode, prods, ()).astype(o_ref.dtype)

    @pl.when(blk >= nu_ref[0])
    def _():
        o_ref[...] = jnp.zeros_like(o_ref)


def moe_matmul(a, w_list, moe_index, blk_e, n_used, mode, out_dtype, tn=512):
    p, k = a.shape
    n_out = w_list[0].shape[-1]
    tn = _tile(n_out, tn)
    grid_spec = pltpu.PrefetchScalarGridSpec(
        num_scalar_prefetch=2,
        grid=(n_out // tn, p // MOE_BLOCK),
        in_specs=[pl.BlockSpec((MOE_BLOCK, k), lambda j, i, be, nu: (_used_block(i, nu), 0))]
        + [pl.BlockSpec((None, None, k, tn), lambda j, i, be, nu: (moe_index, be[i], 0, j)) for _ in w_list],
        out_specs=pl.BlockSpec((MOE_BLOCK, tn), lambda j, i, be, nu: (i, j)),
        scratch_shapes=[pltpu.VMEM((k, tn), BF16) for _ in w_list],
    )
    return pl.pallas_call(
        functools.partial(_moe_mm_kernel, n_w=len(w_list), mode=mode),
        grid_spec=grid_spec,
        out_shape=jax.ShapeDtypeStruct((p, n_out), out_dtype),
        compiler_params=_cp(2),
        name="moe_mm_" + mode,
    )(blk_e, n_used, a, *w_list)


def _moe_combine_kernel(pos_ref, y_hbm, x_ref, gate_ref, gain_ref, o_ref, buf_ref, sem, *, tm, norm):
    i = pl.program_id(0)

    def issue(g, c):
        for u in range(ROW_DMA_UNROLL):
            r = g * ROW_DMA_UNROLL + u
            for k in range(TOP_K):
                _row_copy(y_hbm, pos_ref[(i * tm + r) * TOP_K + k], buf_ref.at[k], r, sem).start(priority=k % 2)
        return c

    lax.fori_loop(0, tm // ROW_DMA_UNROLL, issue, 0)
    for k in range(TOP_K):
        pltpu.make_async_copy(y_hbm.at[pl.ds(0, tm), :], buf_ref.at[k], sem).wait()
    gate = gate_ref[...]
    out = x_ref[...]
    y = gate[:, 2:3] * buf_ref[0]
    for k in range(1, TOP_K):
        y = y + gate[:, 2 + k:3 + k] * buf_ref[k]
    out = out + y
    if norm:
        out = out * lax.rsqrt(jnp.mean(out * out, axis=-1, keepdims=True) + NORM_EPS) * gain_ref[...]
    o_ref[...] = out


def moe_combine(x, yb, pos, gates, gain=None, tm=256):
    m, d = x.shape
    norm = gain is not None
    gain = jnp.ones((d,), F32) if gain is None else gain
    grid_spec = pltpu.PrefetchScalarGridSpec(
        num_scalar_prefetch=1,
        grid=(m // tm,),
        in_specs=[pl.BlockSpec(memory_space=pl.ANY),
                  pl.BlockSpec((tm, d), lambda i, pos: (i, 0)),
                  pl.BlockSpec((tm, LANES), lambda i, pos: (i, 0)),
                  pl.BlockSpec((1, d), lambda i, pos: (0, 0))],
        out_specs=pl.BlockSpec((tm, d), lambda i, pos: (i, 0)),
        scratch_shapes=[pltpu.VMEM((TOP_K, tm, d), F32), pltpu.SemaphoreType.DMA],
    )
    return pl.pallas_call(
        functools.partial(_moe_combine_kernel, tm=tm, norm=norm),
        grid_spec=grid_spec,
        out_shape=jax.ShapeDtypeStruct((m, d), F32),
        compiler_params=_cp(1),
        name="moe_combine",
    )(pos, yb, x, gates, gain.reshape(1, d).astype(F32))


def moe_layer(x, g, rw, rb, w1, w3, w2, moe_index, n_tok, tm, out_gain=None):
    m, d = x.shape
    h, route = moe_router(x, g, rw, rb, tm)
    top_i = route[:n_tok, :TOP_K].astype(jnp.int32)
    a = n_tok * TOP_K
    e_flat = top_i.reshape(a)
    order = jnp.argsort(e_flat)
    e_s = e_flat[order]
    tok_s = (order // TOP_K).astype(jnp.int32)
    counts = jnp.bincount(e_flat, length=N_EXPERTS)
    start = jnp.cumsum(counts) - counts
    padded = (counts + MOE_BLOCK - 1) // MOE_BLOCK * MOE_BLOCK
    pend = jnp.cumsum(padded)
    pstart = pend - padded
    dest = (pstart[e_s] + jnp.arange(a) - start[e_s]).astype(jnp.int32)
    nb = -(-a // MOE_BLOCK) + N_EXPERTS
    p = nb * MOE_BLOCK
    n_used = (pend[-1] // MOE_BLOCK).astype(jnp.int32)
    blk_e = jnp.minimum(jnp.searchsorted(pend, jnp.arange(nb) * MOE_BLOCK, side="right"), N_EXPERTS - 1)
    slot = jnp.arange(p)
    slot_e = blk_e[slot // MOE_BLOCK]
    local = slot - pstart[slot_e]
    tok_buf = jnp.where(local < counts[slot_e], tok_s[jnp.clip(start[slot_e] + local, 0, a - 1)], n_tok)
    tok_buf = tok_buf.astype(jnp.int32)
    blk_e = jnp.where(jnp.arange(nb) < n_used, blk_e, blk_e[n_used - 1]).astype(jnp.int32)
    pos = jnp.concatenate([dest[jnp.argsort(order)], jnp.zeros((m * TOP_K - a,), jnp.int32)])
    row_ok = (jnp.arange(m) < n_tok)[:, None]
    gates = jnp.where(row_ok, route, 0.0)
    n_used = n_used.reshape(1)
    xs = moe_gather(h, tok_buf, n_used)
    act = moe_matmul(xs, [w1, w3], moe_index, blk_e, n_used, "swiglu", BF16)
    yb = moe_matmul(act, [w2], moe_index, blk_e, n_used, "plain", F32)
    return moe_combine(x, yb, pos, gates, out_gain)


TM = 512


def _row_tile(m, pref=1152):
    t = (min(pref, m) // 16) * 16
    while m % t:
        t -= 16
    return t


def _slab(parts, m, dtype):
    rows = sum(p.shape[0] for p in parts)
    parts = [p.astype(dtype) for p in parts]
    return jnp.concatenate(parts + [jnp.zeros((m - rows, parts[0].shape[1]), dtype)], axis=0)


def kernel(x_prompt, x_sample, cache_nsa_cmp_kv, cache_nsa_sel_kv, cache_sb_kv, cache_nsa_win_kv, state_hgrn,
           page_table, attn_norm, w_in, cmp_pe_k, cmp_w1_k, cmp_w2_k, cmp_pe_v, cmp_w1_v, cmp_w2_v, hg_lb_logits,
           hg_norm, w_br_nsa, w_br_sb, w_br_hg, w_out, ffn_norm, ffn_w1, ffn_w3, ffn_w2, router_w, router_b,
           moe_w1, moe_w3, moe_w2, final_norm):
    bsz, seq, d = x_prompt.shape
    db, ds, _ = x_sample.shape
    depth = attn_norm.shape[0]
    n_p, n_s = bsz * seq, db * ds
    n_tok = n_p + n_s
    m = -(-(n_tok + 1) // TM) * TM
    n_pool, page = cache_nsa_cmp_kv.shape[1:3]
    n_pages = page_table.shape[1]
    past = n_pages * page
    assert past % CMP_STRIDE == 0 and ds < CMP_STRIDE and seq % CMP_STRIDE == 0

    x = _slab([x_prompt.reshape(n_p, d), x_sample.reshape(n_s, d)], m, F32)
    pt_flat = page_table.reshape(-1).astype(jnp.int32)
    cmp_pool = cache_nsa_cmp_kv.reshape(depth, n_pool, page * 2 * NSA_KV_HEADS, HEAD_DIM)
    sel_pool = cache_nsa_sel_kv.reshape(depth, n_pool, page * 2 * NSA_KV_HEADS, HEAD_DIM)
    sb_pool = cache_sb_kv.reshape(depth, n_pool, page * 2 * SB_HEADS, HEAD_DIM)
    nbuf = cache_nsa_win_kv.shape[2]
    win_buf = cache_nsa_win_kv.reshape(depth, db, nbuf * 2 * NSA_KV_HEADS, HEAD_DIM)
    rest_cols = w_in.shape[-1] - REST_START
    tmm = _row_tile(m)
    w_in_t = jnp.swapaxes(w_in, 1, 2)
    lb_all = jnp.cumsum(jax.nn.softmax(hg_lb_logits.astype(F32), axis=0), axis=0)
    kvs = (2, NSA_KV_HEADS, HEAD_DIM)

    states = []
    for l in range(depth):
        h = rmsnorm(x, attn_norm[l], BF16, TM)
        u_head = inproj_t(h, w_in_t, l, 0, HEAD_COLS, tmm, HEAD_COLS // 3)
        u_rest = inproj_t(h, w_in_t, l, REST_START, rest_cols, tmm, 1024)

        wk, pk = compress_params(cmp_pe_k[l], cmp_w1_k[l])
        wv, pv = compress_params(cmp_pe_v[l], cmp_w1_v[l])
        wcat, pecat = jnp.stack([wk, wv]), jnp.stack([pk, pv])
        w2s = jnp.stack([cmp_w2_k[l], cmp_w2_v[l]])
        lb = (lb_all[l] - lb_all[0]).reshape(HG_HEADS, HEAD_DIM)
        lbp = hgrn_lb_params(lb)

        kvc = compress_finish(compress_segments_prompt(u_head, bsz, seq, wcat, pecat), w2s)
        o_cmp, sel = cmp_select(u_head, 0, bsz, seq, kvc, seq // CMP_STRIDE - 1, -(-seq // SEL_BLOCK), 0)
        o_sel = sel_prompt(u_head, bsz, seq, sel)
        o_win = win_prompt(u_head, bsz, seq)
        nsa_p = nsa_combine(u_head, 0, o_cmp, o_sel, o_win)
        sb_p = sb_prompt(u_rest, bsz, seq)
        hg_p, s_p = hgrn(u_rest, 0, bsz, seq, lbp, hg_norm[l], jnp.zeros((bsz, HG_HEADS, HEAD_DIM, HEAD_DIM), F32))

        kvc_s = compress_finish(compress_segments_sample(cmp_pool, l, pt_flat, db, n_pages, wcat, pecat), w2s)
        o_cmp_s, sel_s = cmp_select(u_head, n_p, db, ds, kvc_s, past // CMP_STRIDE - 1,
                                    -(-(past + ds) // SEL_BLOCK), past, tq=ds)
        o_sel_s = sel_sample(u_head, n_p, sel_pool, l, pt_flat, sel_s, db, ds, n_pages)
        o_win_s = win_sample(u_head, n_p, win_buf, l, db, ds)
        nsa_s = nsa_combine(u_head, n_p, o_cmp_s, o_sel_s, o_win_s)
        sb_s = sb_sample(u_rest, n_p, sb_pool, l, pt_flat, db, ds, n_pages)
        hg_s, s_s = hgrn(u_rest, n_p, db, ds, lbp, hg_norm[l], state_hgrn[l])

        branches = [(nsa_p.astype(BF16), nsa_s.astype(BF16)), (sb_p.astype(BF16), sb_s.astype(BF16)),
                    (hg_p.astype(BF16), hg_s.astype(BF16))]
        merged = matmul(branches, [w_br_nsa, w_br_sb, w_br_hg], l, 0, d, "merge", BF16,
                        extras=[(u_rest, R_MERGE), (u_rest, R_MERGE + d), (u_rest, R_MERGE + 2 * d)], tm=TM, m=m)
        x = matmul([merged], [w_out], l, 0, d, "residual", F32, extras=[(x, 0)], tm=tmm)

        i = l // 2
        if l % 2 == 0:
            h2 = rmsnorm(x, ffn_norm[l], BF16, TM)
            act = matmul([h2], [ffn_w1, ffn_w3], i, 0, ffn_w1.shape[-1], "swiglu", BF16, tm=tmm)
            x = matmul([act], [ffn_w2], i, 0, d, "residual", F32, extras=[(x, 0)])
        else:
            x = moe_layer(x, ffn_norm[l], router_w[i], router_b[i], moe_w1, moe_w3, moe_w2, i, n_tok, TM,
                          out_gain=final_norm if l == depth - 1 else None)

        def head_cols(off, r0, r1, lead):
            return u_head[r0:r1, off:off + KV_W].reshape(lead + kvs)

        win_p = head_cols(OFF_WIN, 0, n_p, (bsz, seq))[:, seq - min(WINDOW, seq):]
        win_s = jnp.concatenate([cache_nsa_win_kv[l], head_cols(OFF_WIN, n_p, n_tok, (db, ds))], axis=1)[:, ds:]
        states.append((
            head_cols(OFF_CMP, 0, n_p, (bsz, seq)), head_cols(OFF_SEL, 0, n_p, (bsz, seq)), win_p,
            u_rest[:n_p, R_SB_K:R_SB_K + 2 * SB_W].reshape(bsz, seq, 2, SB_HEADS, HEAD_DIM), s_p,
            head_cols(OFF_CMP, n_p, n_tok, (db, ds)), head_cols(OFF_SEL, n_p, n_tok, (db, ds)), win_s,
            u_rest[n_p:n_tok, R_SB_K:R_SB_K + 2 * SB_W].reshape(db, ds, 2, SB_HEADS, HEAD_DIM), s_s))

    y = x if depth % 2 == 0 else rmsnorm(x, final_norm, F32, TM)
    stacked = [jnp.stack([st[i] for st in states]) for i in range(10)]
    return (y[:n_p].reshape(bsz, seq, d), y[n_p:n_tok].reshape(db, ds, d), *stacked)
```
